```python
import math
import jax, jax.numpy as jnp
from jax import lax
import numpy as np

D_MODEL = 1024
BATCH = 4
SEQ = 4096
DEPTH = 2

HEAD_DIM = 64
GDN_HEADS = 6
RET_HEADS = 4
MLSTM_HEADS = 6
GDN_WIDTH = GDN_HEADS * HEAD_DIM
RET_WIDTH = RET_HEADS * HEAD_DIM
MLSTM_WIDTH = MLSTM_HEADS * HEAD_DIM
D_MIX = GDN_WIDTH + RET_WIDTH + MLSTM_WIDTH
CONV_WIDTH = 4
CHUNK = 64
ROPE_BASE = 10000.0
N_GROUPS = 4
EXPERTS_PER_GROUP = 8
N_EXPERTS = N_GROUPS * EXPERTS_PER_GROUP
TOP_K = 2
D_EXPERT = D_MODEL // 4
EPS = 1e-6
IN_SPLITS = (3 * GDN_WIDTH, GDN_WIDTH, GDN_HEADS, GDN_HEADS,
             RET_WIDTH, RET_WIDTH, RET_WIDTH, RET_WIDTH,
             2 * MLSTM_WIDTH, MLSTM_WIDTH, MLSTM_WIDTH, MLSTM_HEADS, MLSTM_HEADS)
D_IN_PROJ = sum(IN_SPLITS)

kernel_name = 'hymba_style_gdn_retention_mlstm_hmoe'


def rms_norm(x, gain):
    xf = x.astype(jnp.float32)
    y = xf * lax.rsqrt(jnp.mean(xf * xf, axis=-1, keepdims=True) + EPS)
    return (y * gain.astype(jnp.float32)).astype(x.dtype)


def head_rms(x):
    return x * lax.rsqrt(jnp.mean(x * x, axis=-1, keepdims=True) + EPS)


def l2norm(x):
    return x * lax.rsqrt(jnp.sum(x * x, axis=-1, keepdims=True) + EPS)


def to_heads(x, n_heads):
    b, t, _ = x.shape
    return x.reshape(b, t, n_heads, -1).transpose(0, 2, 1, 3)


def from_heads(x):
    b, h, t, d = x.shape
    return x.transpose(0, 2, 1, 3).reshape(b, t, h * d)


def to_chunks(x):
    b, h, t = x.shape[:3]
    return x.reshape(b, h, t // CHUNK, CHUNK, *x.shape[3:])


def causal_conv(x, w):
    k, c = w.shape
    return lax.conv_general_dilated(x, w.astype(x.dtype)[:, None, :], window_strides=(1,),
                                    padding=[(k - 1, 0)], dimension_numbers=('NWC', 'WIO', 'NWC'),
                                    feature_group_count=c)


def rotary(x):
    t, d = x.shape[2], x.shape[3]
    inv_freq = ROPE_BASE ** (-jnp.arange(0, d, 2, dtype=jnp.float32) / d)
    ang = jnp.arange(t, dtype=jnp.float32)[:, None] * inv_freq[None, :]
    cos, sin = jnp.cos(ang), jnp.sin(ang)
    x1, x2 = x[..., : d // 2], x[..., d // 2:]
    return jnp.concatenate([x1 * cos - x2 * sin, x1 * sin + x2 * cos], axis=-1)


def chunk_gated_delta(q, k, v, g, beta):
    b, h, t, dk = q.shape
    dv = v.shape[-1]
    qc = to_chunks(q * dk ** -0.5)
    kc = to_chunks(k)
    kb = to_chunks(k * beta[..., None])
    vb = to_chunks(v * beta[..., None])
    gc = jnp.cumsum(to_chunks(g), axis=-1)
    tril = jnp.tril(jnp.ones((CHUNK, CHUNK), bool))
    strict = jnp.tril(jnp.ones((CHUNK, CHUNK), bool), -1)
    decay = jnp.exp(jnp.where(tril, gc[..., :, None] - gc[..., None, :], -jnp.inf))
    a_mat = jnp.where(strict, jnp.einsum('bhncd,bhnsd->bhncs', kb, kc) * decay, 0.0)
    rhs = jnp.concatenate([vb, kb * jnp.exp(gc)[..., None]], axis=-1)
    sol = lax.linalg.triangular_solve(a_mat, rhs, left_side=True, lower=True, unit_diagonal=True)
    u, w = sol[..., :dv], sol[..., dv:]
    qk = jnp.einsum('bhncd,bhnsd->bhncs', qc, kc) * decay

    def step(state, xs):
        q_c, k_c, u_c, w_c, g_c, qk_c = xs
        v_new = u_c - jnp.einsum('bhcd,bhde->bhce', w_c, state)
        o = (jnp.einsum('bhcd,bhde->bhce', q_c * jnp.exp(g_c)[..., None], state)
             + jnp.einsum('bhcs,bhse->bhce', qk_c, v_new))
        g_last = g_c[..., -1]
        state = (state * jnp.exp(g_last)[..., None, None]
                 + jnp.einsum('bhcd,bhce->bhde', k_c * jnp.exp(g_last[..., None] - g_c)[..., None], v_new))
        return state, o

    xs = tuple(jnp.moveaxis(a_, 2, 0) for a_ in (qc, kc, u, w, gc, qk))
    _, o = lax.scan(step, jnp.zeros((b, h, dk, dv), q.dtype), xs)
    return jnp.moveaxis(o, 0, 2).reshape(b, h, t, dv)


def chunk_retention(q, k, v):
    b, h, t, dk = q.shape
    dv = v.shape[-1]
    log_gamma = jnp.log(1.0 - 2.0 ** (-5.0 - jnp.arange(h, dtype=jnp.float32)))
    qc, kc, vc = to_chunks(q), to_chunks(k), to_chunks(v)
    pos = jnp.arange(CHUNK, dtype=jnp.float32)
    tril = jnp.tril(jnp.ones((CHUNK, CHUNK), jnp.float32))
    decay = jnp.exp((pos[:, None] - pos[None, :]) * log_gamma[:, None, None]) * tril
    scores = jnp.einsum('bhncd,bhnsd->bhncs', qc, kc) * decay[None, :, None]
    o_intra = jnp.einsum('bhncs,bhnse->bhnce', scores, vc)
    k_scale = jnp.exp(log_gamma[:, None] * (CHUNK - 1 - pos))
    q_scale = jnp.exp(log_gamma[:, None] * (pos + 1))
    chunk_decay = jnp.exp(log_gamma * CHUNK)
    kv = jnp.einsum('bhncd,bhnce->bhnde', kc * k_scale[None, :, None, :, None], vc)

    def step(state, kv_c):
        return chunk_decay[None, :, None, None] * state + kv_c, state

    _, prev = lax.scan(step, jnp.zeros((b, h, dk, dv), q.dtype), jnp.moveaxis(kv, 2, 0))
    prev = jnp.moveaxis(prev, 0, 2)
    o_inter = jnp.einsum('bhncd,bhnde->bhnce', qc * q_scale[None, :, None, :, None], prev)
    return (o_intra + o_inter).reshape(b, h, t, dv)


def chunk_mlstm(q, k, v, log_i, log_f):
    b, h, t, dk = q.shape
    dv = v.shape[-1]
    qc, kc, vc = to_chunks(q), to_chunks(k), to_chunks(v)
    bc = jnp.cumsum(to_chunks(log_f), axis=-1)
    ic = to_chunks(log_i)
    qk = jnp.einsum('bhncd,bhnsd->bhncs', qc, kc)
    tril = jnp.tril(jnp.ones((CHUNK, CHUNK), bool))

    def step(carry, xs):
        c_st, n_st, m_st = carry
        q_c, k_c, v_c, b_c, i_c, qk_c = xs
        log_d = jnp.where(tril, b_c[..., :, None] - b_c[..., None, :] + i_c[..., None, :], -jnp.inf)
        m_t = jnp.maximum(b_c + m_st[..., None], jnp.max(log_d, axis=-1))
        inter = jnp.exp(b_c + m_st[..., None] - m_t)
        wmat = jnp.exp(log_d - m_t[..., None]) * qk_c
        num = (inter[..., None] * jnp.einsum('bhcd,bhde->bhce', q_c, c_st)
               + jnp.einsum('bhcs,bhse->bhce', wmat, v_c))
        den = inter * jnp.einsum('bhcd,bhd->bhc', q_c, n_st) + jnp.sum(wmat, axis=-1)
        h_c = num / jnp.maximum(jnp.abs(den), jnp.exp(-m_t))[..., None]
        b_last = b_c[..., -1]
        a_c = b_last[..., None] - b_c + i_c
        m_new = jnp.maximum(b_last + m_st, jnp.max(a_c, axis=-1))
        dec = jnp.exp(b_last + m_st - m_new)
        wk = jnp.exp(a_c - m_new[..., None])
        c_st = dec[..., None, None] * c_st + jnp.einsum('bhcd,bhce->bhde', k_c * wk[..., None], v_c)
        n_st = dec[..., None] * n_st + jnp.einsum('bhc,bhcd->bhd', wk, k_c)
        return (c_st, n_st, m_new), h_c

    init = (jnp.zeros((b, h, dk, dv), q.dtype), jnp.zeros((b, h, dk), q.dtype), jnp.zeros((b, h), q.dtype))
    xs = tuple(jnp.moveaxis(a_, 2, 0) for a_ in (qc, kc, vc, bc, ic, qk))
    _, o = lax.scan(step, init, xs)
    return jnp.moveaxis(o, 0, 2).reshape(b, h, t, dv)


def gdn_mixer(qkv, z, b_logit, a_logit, conv_w, a_log, dt_bias, norm_gain):
    qkv = jax.nn.silu(causal_conv(qkv, conv_w))
    q, k, v = jnp.split(qkv, 3, axis=-1)
    q = l2norm(to_heads(q, GDN_HEADS))
    k = l2norm(to_heads(k, GDN_HEADS))
    v = to_heads(v, GDN_HEADS)
    beta = jax.nn.sigmoid(b_logit).transpose(0, 2, 1)
    g = (-jnp.exp(a_log) * jax.nn.softplus(a_logit + dt_bias)).transpose(0, 2, 1)
    o = chunk_gated_delta(q, k, v, g, beta)
    return from_heads(head_rms(o)) * norm_gain * jax.nn.silu(z)


def retention_mixer(q, k, v, gate, norm_gain):
    q = rotary(to_heads(q, RET_HEADS))
    k = rotary(to_heads(k, RET_HEADS)) * HEAD_DIM ** -0.5
    v = to_heads(v, RET_HEADS)
    o = chunk_retention(q, k, v)
    return from_heads(head_rms(o)) * norm_gain * jax.nn.silu(gate)


def mlstm_mixer(qk, v, o_logit, i_logit, f_logit, conv_w, i_bias, f_bias, norm_gain):
    qk = jax.nn.silu(causal_conv(qk, conv_w))
    q, k = jnp.split(qk, 2, axis=-1)
    q = to_heads(q, MLSTM_HEADS) * HEAD_DIM ** -0.5
    k = to_heads(k, MLSTM_HEADS)
    v = to_heads(v, MLSTM_HEADS)
    log_i = (i_logit + i_bias).transpose(0, 2, 1)
    log_f = jax.nn.log_sigmoid(f_logit + f_bias).transpose(0, 2, 1)
    h = chunk_mlstm(q, k, v, log_i, log_f)
    return jax.nn.sigmoid(o_logit) * (from_heads(head_rms(h)) * norm_gain)


def hierarchical_moe(h, router_group, router_expert, router_bias, expert_gate, expert_up, expert_down):
    b, t, d = h.shape
    tok = h.reshape(b * t, d)
    group_prob = jax.nn.softmax((tok @ router_group).astype(jnp.float32), axis=-1)
    group_p, group_idx = lax.top_k(group_prob, 1)
    expert_logits = (tok @ router_expert).astype(jnp.float32) + router_bias.astype(jnp.float32)
    expert_logits = expert_logits.reshape(-1, N_GROUPS, EXPERTS_PER_GROUP)
    in_group = jnp.take_along_axis(expert_logits, group_idx[:, :, None], axis=1)[:, 0]
    top_logit, top_idx = lax.top_k(in_group, TOP_K)
    weights = jax.nn.softmax(top_logit, axis=-1) * group_p
    expert_id = group_idx * EXPERTS_PER_GROUP + top_idx
    combine = jnp.einsum('nk,nke->ne', weights, jax.nn.one_hot(expert_id, N_EXPERTS, dtype=jnp.float32))
    out = jnp.zeros((b * t, d), jnp.float32)
    for e in range(N_EXPERTS):
        y = (jax.nn.silu(tok @ expert_gate[e]) * (tok @ expert_up[e])) @ expert_down[e]
        out = out + combine[:, e:e + 1] * y.astype(jnp.float32)
    return out.astype(h.dtype).reshape(b, t, d)


def hybrid_layer(x, norm_mix, w_in, gdn_conv, gdn_a_log, gdn_dt_bias, gdn_norm, ret_norm,
                 mlstm_conv, mlstm_i_bias, mlstm_f_bias, mlstm_norm, w_out, norm_ffn,
                 router_group, router_expert, router_bias, expert_gate, expert_up, expert_down):
    h = rms_norm(x, norm_mix)
    proj = jnp.einsum('btd,dp->btp', h, w_in).astype(jnp.float32)
    (g_qkv, g_z, g_b, g_a, r_q, r_k, r_v, r_g,
     m_qk, m_v, m_o, m_i, m_f) = jnp.split(proj, np.cumsum(IN_SPLITS)[:-1].tolist(), axis=-1)
    y_gdn = gdn_mixer(g_qkv, g_z, g_b, g_a, gdn_conv, gdn_a_log, gdn_dt_bias, gdn_norm)
    y_ret = retention_mixer(r_q, r_k, r_v, r_g, ret_norm)
    y_mlstm = mlstm_mixer(m_qk, m_v, m_o, m_i, m_f, mlstm_conv, mlstm_i_bias, mlstm_f_bias, mlstm_norm)
    mix = jnp.concatenate([y_gdn, y_ret, y_mlstm], axis=-1).astype(x.dtype)
    x = x + jnp.einsum('btm,md->btd', mix, w_out)
    x = x + hierarchical_moe(rms_norm(x, norm_ffn), router_group, router_expert, router_bias,
                             expert_gate, expert_up, expert_down)
    return x


def setup_inputs(seed: int = 0) -> dict:
    key = jax.random.key(seed)
    ks = jax.random.split(key, 24)
    f32 = jnp.float32
    L = DEPTH

    def normal(k, shape, scale):
        return scale * jax.random.normal(k, shape, f32)

    def gain(k, shape):
        return 1.0 + 0.02 * jax.random.normal(k, shape, f32)

    a_init = jax.random.uniform(ks[4], (L, GDN_HEADS), f32, 1.0, 16.0)
    dt = jnp.exp(jax.random.uniform(ks[5], (L, GDN_HEADS), f32, math.log(1e-3), math.log(1e-1)))
    dt_bias = dt + jnp.log(-jnp.expm1(-dt))
    f_bias = jnp.linspace(3.0, 6.0, MLSTM_HEADS, dtype=f32)[None, :] + normal(ks[10], (L, MLSTM_HEADS), 0.1)
    return {
        'x': normal(ks[0], (BATCH, SEQ, D_MODEL), 1.0),
        'norm_mix': gain(ks[1], (L, D_MODEL)),
        'w_in': normal(ks[2], (L, D_MODEL, D_IN_PROJ), D_MODEL ** -0.5),
        'gdn_conv': normal(ks[3], (L, CONV_WIDTH, 3 * GDN_WIDTH), CONV_WIDTH ** -0.5),
        'gdn_a_log': jnp.log(a_init),
        'gdn_dt_bias': dt_bias,
        'gdn_norm': gain(ks[6], (L, GDN_WIDTH)),
        'ret_norm': gain(ks[7], (L, RET_WIDTH)),
        'mlstm_conv': normal(ks[8], (L, CONV_WIDTH, 2 * MLSTM_WIDTH), CONV_WIDTH ** -0.5),
        'mlstm_i_bias': normal(ks[9], (L, MLSTM_HEADS), 0.1),
        'mlstm_f_bias': f_bias,
        'mlstm_norm': gain(ks[11], (L, MLSTM_WIDTH)),
        'w_out': normal(ks[12], (L, D_MIX, D_MODEL), D_MIX ** -0.5),
        'norm_ffn': gain(ks[13], (L, D_MODEL)),
        'router_group': normal(ks[14], (L, D_MODEL, N_GROUPS), D_MODEL ** -0.5),
        'router_expert': normal(ks[15], (L, D_MODEL, N_EXPERTS), D_MODEL ** -0.5),
        'router_bias': normal(ks[16], (L, N_EXPERTS), 0.01),
        'expert_gate': normal(ks[17], (L, N_EXPERTS, D_MODEL, D_EXPERT), D_MODEL ** -0.5),
        'expert_up': normal(ks[18], (L, N_EXPERTS, D_MODEL, D_EXPERT), D_MODEL ** -0.5),
        'expert_down': normal(ks[19], (L, N_EXPERTS, D_EXPERT, D_MODEL), D_EXPERT ** -0.5),
        'norm_final': gain(ks[20], (D_MODEL,)),
    }


def reference(x, norm_mix, w_in, gdn_conv, gdn_a_log, gdn_dt_bias, gdn_norm, ret_norm,
              mlstm_conv, mlstm_i_bias, mlstm_f_bias, mlstm_norm, w_out, norm_ffn,
              router_group, router_expert, router_bias, expert_gate, expert_up, expert_down, norm_final):
    for l in range(DEPTH):
        x = hybrid_layer(x, norm_mix[l], w_in[l], gdn_conv[l], gdn_a_log[l], gdn_dt_bias[l], gdn_norm[l],
                         ret_norm[l], mlstm_conv[l], mlstm_i_bias[l], mlstm_f_bias[l], mlstm_norm[l],
                         w_out[l], norm_ffn[l], router_group[l], router_expert[l], router_bias[l],
                         expert_gate[l], expert_up[l], expert_down[l])
    return rms_norm(x, norm_final)
```

```python
import functools
import math

import jax
import jax.numpy as jnp
from jax import lax
from jax.experimental import pallas as pl
from jax.experimental.pallas import tpu as pltpu

F32 = jnp.float32
BF16 = jnp.bfloat16

D_MODEL = 1024
HEAD_DIM = 64
CHUNK = 64
GDN_HEADS, RET_HEADS, MLSTM_HEADS = 6, 4, 6
GDN_W, RET_W, MLSTM_W = GDN_HEADS * HEAD_DIM, RET_HEADS * HEAD_DIM, MLSTM_HEADS * HEAD_DIM
CONV_K = 4
ROPE_BASE = 10000.0
N_GROUPS, EXPERTS_PER_GROUP = 4, 8
N_EXPERTS = N_GROUPS * EXPERTS_PER_GROUP
D_EXPERT = D_MODEL // 4
EPS = 1e-6
LANES = 128
SEC_M, SEC_G, SEC_R = 4 * MLSTM_W, 4 * GDN_W, 4 * RET_W
D_MAIN = SEC_M + SEC_G + SEC_R
G_BETA, G_A, M_I, M_F = 0, GDN_HEADS, 2 * GDN_HEADS, 2 * GDN_HEADS + MLSTM_HEADS

ROW_TILE = 256
TIME_BLOCK = 256
MOE_TILE = 128
VMEM_LIMIT = 56 * 1024 * 1024


def _cparams(sem):
    return pltpu.CompilerParams(dimension_semantics=sem, vmem_limit_bytes=VMEM_LIMIT)


def _silu(x):
    return x * (1.0 / (1.0 + jnp.exp(-x)))


def _sigmoid(x):
    return 1.0 / (1.0 + jnp.exp(-x))


def _softplus(x):
    return jnp.maximum(x, 0.0) + jnp.log(1.0 + jnp.exp(-jnp.abs(x)))


def _dot(a, b):
    return jnp.dot(a.astype(BF16), b.astype(BF16), preferred_element_type=F32)


def _dot_nt(a, b):
    return lax.dot_general(a.astype(BF16), b.astype(BF16), (((1,), (1,)), ((), ())),
                           preferred_element_type=F32)


def _lane_masks(rows):
    lane = lax.broadcasted_iota(jnp.int32, (rows, LANES), 1)
    row = lax.broadcasted_iota(jnp.int32, (rows, LANES), 0)
    return lane, row


def _half_sum(x, m_a):
    s_a = jnp.sum(jnp.where(m_a, x, 0.0), axis=-1, keepdims=True)
    s_b = jnp.sum(jnp.where(m_a, 0.0, x), axis=-1, keepdims=True)
    return jnp.where(m_a, s_a, s_b)


def _half_max(x, m_a):
    s_a = jnp.max(jnp.where(m_a, x, -jnp.inf), axis=-1, keepdims=True)
    s_b = jnp.max(jnp.where(m_a, -jnp.inf, x), axis=-1, keepdims=True)
    return jnp.where(m_a, s_a, s_b)


def _col_form(g, lane_a, m_a):
    rows = g.shape[0]
    ca = jnp.broadcast_to(g[:, lane_a:lane_a + 1], (rows, LANES))
    cb = jnp.broadcast_to(g[:, lane_a + 1:lane_a + 2], (rows, LANES))
    return jnp.where(m_a, ca, cb)


def _row_form(col, eye):
    return jnp.sum(jnp.where(eye, col, 0.0), axis=0, keepdims=True)


def _block_diag(y, m_a):
    return jnp.concatenate([jnp.where(m_a, y, 0.0), jnp.where(m_a, 0.0, y)], axis=0)


def _pmul(x, y, m_a):
    return _dot(x, _block_diag(y, m_a))


def _outer_state(k, v, bd_mask):
    zero = jnp.zeros_like(k)
    kt = jnp.concatenate([k, zero], axis=0).T
    vp = jnp.concatenate([v, zero], axis=0)
    return jnp.where(bd_mask, _dot(kt, vp), 0.0)


def _chunk_cumsum(x):
    rows = x.shape[0]
    r = lax.broadcasted_iota(jnp.int32, (rows, LANES), 0) % CHUNK
    s = 1
    while s < CHUNK:
        x = x + jnp.where(r >= s, pltpu.roll(x, s, axis=0), 0.0)
        s *= 2
    return x


def _causal_conv(x_ref, cbuf, w_ref, width, first):
    tb = x_ref.shape[0]

    @pl.when(first)
    def _():
        cbuf[0:8, :] = jnp.zeros((8, width), F32)

    cbuf[8:8 + tb, :] = x_ref[:, 0:width]
    acc = cbuf[8:8 + tb, :] * w_ref[CONV_K - 1:CONV_K, :]
    for j in range(CONV_K - 1):
        off = 8 - (CONV_K - 1) + j
        acc = acc + cbuf[off:off + tb, :] * w_ref[j:j + 1, :]
    cbuf[0:8, :] = x_ref[tb - 8:tb, 0:width]
    return acc


def _inproj_kernel(x_ref, gain_ref, w_ref, wg_ref, o_ref, og_ref):
    x = x_ref[...]
    ms = jnp.mean(x * x, axis=-1, keepdims=True)
    h = (x * lax.rsqrt(ms + EPS) * gain_ref[...]).astype(BF16)
    step = 512
    for c in range(D_MAIN // step):
        o_ref[:, c * step:(c + 1) * step] = jnp.dot(
            h, w_ref[:, c * step:(c + 1) * step], preferred_element_type=F32)
    og_ref[...] = jnp.dot(h, wg_ref[...], preferred_element_type=F32)


def _inproj(x2, gain, w_main, w_gate):
    n = x2.shape[0]
    return pl.pallas_call(
        _inproj_kernel,
        grid=(n // ROW_TILE,),
        in_specs=[
            pl.BlockSpec((ROW_TILE, D_MODEL), lambda i: (i, 0)),
            pl.BlockSpec((1, D_MODEL), lambda i: (0, 0)),
            pl.BlockSpec((D_MODEL, D_MAIN), lambda i: (0, 0)),
            pl.BlockSpec((D_MODEL, LANES), lambda i: (0, 0)),
        ],
        out_specs=[
            pl.BlockSpec((ROW_TILE, D_MAIN), lambda i: (i, 0)),
            pl.BlockSpec((ROW_TILE, LANES), lambda i: (i, 0)),
        ],
        out_shape=[jax.ShapeDtypeStruct((n, D_MAIN), F32), jax.ShapeDtypeStruct((n, LANES), F32)],
        compiler_params=_cparams(("parallel",)),
        name="norm_inproj",
    )(x2, gain, w_main, w_gate)


def _outproj_kernel(x_ref, ym_ref, yg_ref, yr_ref, wm_ref, wg_ref, wr_ref, gain_ref, wrt_ref, rb_ref,
                    xo_ref, hp_ref, lg_ref):
    acc = x_ref[...]
    acc = acc + jnp.dot(yg_ref[...], wg_ref[...], preferred_element_type=F32)
    acc = acc + jnp.dot(yr_ref[...], wr_ref[...], preferred_element_type=F32)
    acc = acc + jnp.dot(ym_ref[...], wm_ref[...], preferred_element_type=F32)
    xo_ref[...] = acc
    ms = jnp.mean(acc * acc, axis=-1, keepdims=True)
    hb = (acc * lax.rsqrt(ms + EPS) * gain_ref[...]).astype(BF16)
    lg_ref[...] = jnp.dot(hb, wrt_ref[...], preferred_element_type=F32) + rb_ref[...]
    bits = lax.bitcast_convert_type(hb.astype(F32), jnp.uint32)
    half = D_MODEL // 2
    hp_ref[...] = (bits[:, :half] >> 16) | (bits[:, half:] & jnp.uint32(0xFFFF0000))


def _outproj(x2, ym, yg, yr, wo_m, wo_g, wo_r, gain, w_router, router_bias):
    n = x2.shape[0]
    row = lambda w: pl.BlockSpec((ROW_TILE, w), lambda i: (i, 0))
    full = lambda a, b: pl.BlockSpec((a, b), lambda i: (0, 0))
    return pl.pallas_call(
        _outproj_kernel,
        grid=(n // ROW_TILE,),
        in_specs=[row(D_MODEL), row(MLSTM_W), row(GDN_W), row(RET_W),
                  full(MLSTM_W, D_MODEL), full(GDN_W, D_MODEL), full(RET_W, D_MODEL),
                  full(1, D_MODEL), full(D_MODEL, LANES), full(1, LANES)],
        out_specs=[row(D_MODEL), row(D_MODEL // 2), row(LANES)],
        out_shape=[jax.ShapeDtypeStruct((n, D_MODEL), F32),
                   jax.ShapeDtypeStruct((n, D_MODEL // 2), jnp.uint32),
                   jax.ShapeDtypeStruct((n, LANES), F32)],
        compiler_params=_cparams(("parallel",)),
        name="outproj_norm_router",
    )(x2, ym, yg, yr, wo_m, wo_g, wo_r, gain, w_router, router_bias)


def _gdn_kernel(x_ref, gt_ref, conv_ref, prm_ref, gain_ref, y_ref, state, cbuf):
    tb = x_ref.shape[0]
    first = pl.program_id(1) == 0

    @pl.when(first)
    def _():
        state[...] = jnp.zeros_like(state)

    qkv = _silu(_causal_conv(x_ref, cbuf, conv_ref, 3 * GDN_W, first))
    gt = gt_ref[...]
    beta_all = _sigmoid(gt)
    g_all = -jnp.exp(prm_ref[0:1, :]) * _softplus(gt + prm_ref[1:2, :])
    gc_all = _chunk_cumsum(g_all)

    lane, row = _lane_masks(CHUNK)
    m_a = lane < HEAD_DIM
    lane_h = lane % HEAD_DIM
    eye = lane_h == row
    tril = lane_h <= row
    strict = lane_h < row
    eye_f = eye.astype(F32)
    lane2 = lax.broadcasted_iota(jnp.int32, (LANES, LANES), 1)
    row2 = lax.broadcasted_iota(jnp.int32, (LANES, LANES), 0)
    bd_mask = (lane2 < HEAD_DIM) == (row2 < HEAD_DIM)
    lane_t = lax.broadcasted_iota(jnp.int32, (tb, LANES), 1)
    m_a_t = lane_t < HEAD_DIM
    scale = HEAD_DIM ** -0.5

    for p in range(GDN_HEADS // 2):
        cs = slice(p * LANES, (p + 1) * LANES)
        q_t = qkv[:, p * LANES:(p + 1) * LANES]
        k_t = qkv[:, GDN_W + p * LANES:GDN_W + (p + 1) * LANES]
        v_t = qkv[:, 2 * GDN_W + p * LANES:2 * GDN_W + (p + 1) * LANES]
        q_t = q_t * lax.rsqrt(_half_sum(q_t * q_t, m_a_t) + EPS) * scale
        k_t = k_t * lax.rsqrt(_half_sum(k_t * k_t, m_a_t) + EPS)
        beta_t = _col_form(beta_all, G_BETA + 2 * p, m_a_t)
        gc_t = _col_form(gc_all, G_A + 2 * p, m_a_t)
        s_bd = state[p]
        for c in range(tb // CHUNK):
            rs = slice(c * CHUNK, (c + 1) * CHUNK)
            q, k, v, beta, gc = q_t[rs], k_t[rs], v_t[rs], beta_t[rs], gc_t[rs]
            decay = jnp.where(tril, jnp.exp(gc - _row_form(gc, eye)), 0.0)
            kb = k * beta
            k_st = _block_diag(k, m_a)
            kkqk = _dot_nt(jnp.concatenate([kb, q], axis=0), k_st)
            a_mat = jnp.where(strict, kkqk[:CHUNK] * decay, 0.0)
            qk = kkqk[CHUNK:] * decay
            t_inv = eye_f - a_mat
            pw = a_mat
            for _ in range(5):
                pw = _pmul(pw, pw, m_a)
                t_inv = _pmul(t_inv, eye_f + pw, m_a)
            egc = jnp.exp(gc)
            u = _pmul(t_inv, v * beta, m_a)
            w = _pmul(t_inv, kb * egc, m_a)
            ws_qs = _dot(jnp.concatenate([w, q * egc], axis=0), s_bd)
            v_new = u - ws_qs[:CHUNK]
            o = ws_qs[CHUNK:] + _pmul(qk, v_new, m_a)
            g_last = gc[CHUNK - 1:CHUNK, :]
            s_bd = s_bd * jnp.exp(g_last) + _outer_state(k * jnp.exp(g_last - gc), v_new, bd_mask)
            ms = _half_sum(o * o, m_a) * (1.0 / HEAD_DIM)
            z = x_ref[rs, 3 * GDN_W + p * LANES:3 * GDN_W + (p + 1) * LANES]
            y = o * lax.rsqrt(ms + EPS) * gain_ref[:, cs] * _silu(z)
            y_ref[rs, cs] = y.astype(y_ref.dtype)
        state[p] = s_bd


def _ret_kernel(x_ref, cos_ref, sin_ref, gain_ref, y_ref, state):
    tb = x_ref.shape[0]

    @pl.when(pl.program_id(1) == 0)
    def _():
        state[...] = jnp.zeros_like(state)

    lane, row = _lane_masks(CHUNK)
    m_a = lane < HEAD_DIM
    lane_h = lane % HEAD_DIM
    tril = lane_h <= row
    lane2 = lax.broadcasted_iota(jnp.int32, (LANES, LANES), 1)
    row2 = lax.broadcasted_iota(jnp.int32, (LANES, LANES), 0)
    bd_mask = (lane2 < HEAD_DIM) == (row2 < HEAD_DIM)
    lane_t = lax.broadcasted_iota(jnp.int32, (tb, LANES), 1)
    first_half = (lane_t % HEAD_DIM) < (HEAD_DIM // 2)
    cos = cos_ref[...]
    sin = sin_ref[...]
    rowf = row.astype(F32)
    pos_diff = (row - lane_h).astype(F32)

    def rope(x):
        swapped = jnp.where(first_half, pltpu.roll(x, LANES - HEAD_DIM // 2, axis=1),
                            pltpu.roll(x, HEAD_DIM // 2, axis=1))
        return x * cos + swapped * sin

    for p in range(RET_HEADS // 2):
        cs = slice(p * LANES, (p + 1) * LANES)
        lg_a = math.log(1.0 - 2.0 ** (-5.0 - 2 * p))
        lg_b = math.log(1.0 - 2.0 ** (-5.0 - (2 * p + 1)))
        lg = jnp.where(m_a, lg_a, lg_b)
        decay = jnp.where(tril, jnp.exp(pos_diff * lg), 0.0)
        q_scale = jnp.exp(lg * (rowf + 1.0))
        k_scale = jnp.exp(lg * (CHUNK - 1.0 - rowf))
        chunk_decay = jnp.exp(lg[0:1, :] * CHUNK)
        q_t = rope(x_ref[:, p * LANES:(p + 1) * LANES])
        k_t = rope(x_ref[:, RET_W + p * LANES:RET_W + (p + 1) * LANES]) * (HEAD_DIM ** -0.5)
        s_bd = state[p]
        for c in range(tb // CHUNK):
            rs = slice(c * CHUNK, (c + 1) * CHUNK)
            q, k = q_t[rs], k_t[rs]
            v = x_ref[rs, 2 * RET_W + p * LANES:2 * RET_W + (p + 1) * LANES]
            scores = _dot_nt(q, _block_diag(k, m_a)) * decay
            o = _pmul(scores, v, m_a) + _dot(q * q_scale, s_bd)
            s_bd = chunk_decay * s_bd + _outer_state(k * k_scale, v, bd_mask)
            ms = _half_sum(o * o, m_a) * (1.0 / HEAD_DIM)
            gate = x_ref[rs, 3 * RET_W + p * LANES:3 * RET_W + (p + 1) * LANES]
            y = o * lax.rsqrt(ms + EPS) * gain_ref[:, cs] * _silu(gate)
            y_ref[rs, cs] = y.astype(y_ref.dtype)
        state[p] = s_bd


def _mlstm_kernel(x_ref, gt_ref, conv_ref, prm_ref, gain_ref, y_ref, c_state, nm_state, cbuf):
    tb = x_ref.shape[0]
    first = pl.program_id(1) == 0

    @pl.when(first)
    def _():
        c_state[...] = jnp.zeros_like(c_state)
        nm_state[...] = jnp.zeros_like(nm_state)

    qk_all = _silu(_causal_conv(x_ref, cbuf, conv_ref, 2 * MLSTM_W, first))
    gt = gt_ref[...]
    log_i_all = gt + prm_ref[0:1, :]
    f_pre = gt + prm_ref[1:2, :]
    log_f_all = jnp.minimum(f_pre, 0.0) - jnp.log(1.0 + jnp.exp(-jnp.abs(f_pre)))
    bc_all = _chunk_cumsum(log_f_all)

    lane, row = _lane_masks(CHUNK)
    m_a = lane < HEAD_DIM
    lane_h = lane % HEAD_DIM
    eye = lane_h == row
    tril = lane_h <= row
    lane2 = lax.broadcasted_iota(jnp.int32, (LANES, LANES), 1)
    row2 = lax.broadcasted_iota(jnp.int32, (LANES, LANES), 0)
    bd_mask = (lane2 < HEAD_DIM) == (row2 < HEAD_DIM)
    lane_t = lax.broadcasted_iota(jnp.int32, (tb, LANES), 1)
    m_a_t = lane_t < HEAD_DIM
    scale = HEAD_DIM ** -0.5

    for p in range(MLSTM_HEADS // 2):
        cs = slice(p * LANES, (p + 1) * LANES)
        q_t = qk_all[:, p * LANES:(p + 1) * LANES] * scale
        k_t = qk_all[:, MLSTM_W + p * LANES:MLSTM_W + (p + 1) * LANES]
        b_t = _col_form(bc_all, M_F + 2 * p, m_a_t)
        i_t = _col_form(log_i_all, M_I + 2 * p, m_a_t)
        c_bd = c_state[p]
        n_row = nm_state[p, 0:1, :]
        m_row = nm_state[p, 1:2, :]
        for c in range(tb // CHUNK):
            rs = slice(c * CHUNK, (c + 1) * CHUNK)
            q, k, b_c, i_c = q_t[rs], k_t[rs], b_t[rs], i_t[rs]
            v = x_ref[rs, 2 * MLSTM_W + p * LANES:2 * MLSTM_W + (p + 1) * LANES]
            qk = _dot_nt(q, _block_diag(k, m_a))
            log_d = jnp.where(tril, b_c - _row_form(b_c, eye) + _row_form(i_c, eye), -jnp.inf)
            m_t = jnp.maximum(b_c + m_row, _half_max(log_d, m_a))
            inter = jnp.exp(b_c + m_row - m_t)
            wmat = jnp.where(tril, jnp.exp(log_d - m_t), 0.0) * qk
            num = inter * _dot(q, c_bd) + _pmul(wmat, v, m_a)
            den = inter * _half_sum(q * n_row, m_a) + _half_sum(wmat, m_a)
            h = num / jnp.maximum(jnp.abs(den), jnp.exp(-m_t))
            b_last = b_c[CHUNK - 1:CHUNK, :]
            a_c = b_last - b_c + i_c
            m_new = jnp.maximum(b_last + m_row, jnp.max(a_c, axis=0, keepdims=True))
            dec = jnp.exp(b_last + m_row - m_new)
            kw = k * jnp.exp(a_c - m_new)
            c_bd = dec * c_bd + _outer_state(kw, v, bd_mask)
            n_row = dec * n_row + jnp.sum(kw, axis=0, keepdims=True)
            m_row = m_new
            ms = _half_sum(h * h, m_a) * (1.0 / HEAD_DIM)
            o_logit = x_ref[rs, 3 * MLSTM_W + p * LANES:3 * MLSTM_W + (p + 1) * LANES]
            y = _sigmoid(o_logit) * (h * lax.rsqrt(ms + EPS) * gain_ref[:, cs])
            y_ref[rs, cs] = y.astype(y_ref.dtype)
        c_state[p] = c_bd
        nm_state[p, 0:1, :] = n_row
        nm_state[p, 1:2, :] = m_row


def _mixer_call(kernel, name, proj3, sec_block, sec_width, out_width, extra_inputs, extra_specs, scratch):
    b, t, _ = proj3.shape
    tb = min(TIME_BLOCK, t)
    return pl.pallas_call(
        kernel,
        grid=(b, t // tb),
        in_specs=[pl.BlockSpec((None, tb, sec_width), lambda i, j: (i, j, sec_block))] + extra_specs(tb),
        out_specs=pl.BlockSpec((None, tb, out_width), lambda i, j: (i, j, 0)),
        out_shape=jax.ShapeDtypeStruct((b, t, out_width), BF16),
        scratch_shapes=scratch(tb),
        compiler_params=_cparams(("parallel", "arbitrary")),
        name=name,
    )(proj3, *extra_inputs)


def _full2(a, b):
    return pl.BlockSpec((a, b), lambda i, j: (0, 0))


def _gdn_mixer(proj3, gates3, conv_w, prm, gain):
    return _mixer_call(
        _gdn_kernel, "gdn_mixer", proj3, 1, SEC_G, GDN_W, (gates3, conv_w, prm, gain),
        lambda tb: [pl.BlockSpec((None, tb, LANES), lambda i, j: (i, j, 0)),
                    _full2(CONV_K, 3 * GDN_W), _full2(8, LANES), _full2(1, GDN_W)],
        lambda tb: [pltpu.VMEM((GDN_HEADS // 2, LANES, LANES), F32),
                    pltpu.VMEM((tb + 8, 3 * GDN_W), F32)])


def _mlstm_mixer(proj3, gates3, conv_w, prm, gain):
    return _mixer_call(
        _mlstm_kernel, "mlstm_mixer", proj3, 0, SEC_M, MLSTM_W, (gates3, conv_w, prm, gain),
        lambda tb: [pl.BlockSpec((None, tb, LANES), lambda i, j: (i, j, 0)),
                    _full2(CONV_K, 2 * MLSTM_W), _full2(8, LANES), _full2(1, MLSTM_W)],
        lambda tb: [pltpu.VMEM((MLSTM_HEADS // 2, LANES, LANES), F32),
                    pltpu.VMEM((MLSTM_HEADS // 2, 8, LANES), F32),
                    pltpu.VMEM((tb + 8, 2 * MLSTM_W), F32)])


def _ret_mixer(proj3, cos_tab, sin_tab, gain):
    return _mixer_call(
        _ret_kernel, "retention_mixer", proj3, 3, SEC_R, RET_W, (cos_tab, sin_tab, gain),
        lambda tb: [pl.BlockSpec((tb, LANES), lambda i, j: (j, 0)),
                    pl.BlockSpec((tb, LANES), lambda i, j: (j, 0)),
                    _full2(1, RET_W)],
        lambda tb: [pltpu.VMEM((RET_HEADS // 2, LANES, LANES), F32)])


def _gather_kernel(idx_ref, src_ref, out_ref, sem):
    rows = out_ref.shape[0]
    base = pl.program_id(0) * rows

    def issue(r, carry):
        pltpu.make_async_copy(src_ref.at[pl.ds(idx_ref[base + r], 1)], out_ref.at[pl.ds(r, 1)], sem).start()
        return carry

    lax.fori_loop(0, rows, issue, 0)
    pltpu.make_async_copy(src_ref.at[pl.ds(0, rows)], out_ref, sem).wait()


def _row_gather(src, idx, tile):
    r = idx.shape[0]
    width = src.shape[1]
    return pl.pallas_call(
        _gather_kernel,
        grid_spec=pltpu.PrefetchScalarGridSpec(
            num_scalar_prefetch=1,
            grid=(r // tile,),
            in_specs=[pl.BlockSpec(memory_space=pl.ANY)],
            out_specs=pl.BlockSpec((tile, width), lambda i, idx_ref: (i, 0)),
            scratch_shapes=[pltpu.SemaphoreType.DMA(())],
        ),
        out_shape=jax.ShapeDtypeStruct((r, width), src.dtype),
        compiler_params=_cparams(("arbitrary",)),
        name="row_gather",
    )(idx, src)


def _ffn_kernel(e1_ref, e2_ref, used_ref, x_ref, rw_ref, wg1, wu1, wd1, wg2, wu2, wd2, y_ref):
    used = pl.program_id(0) < used_ref[0]

    @pl.when(jnp.logical_not(used))
    def _():
        y_ref[...] = jnp.zeros_like(y_ref)

    @pl.when(used)
    def _():
        half = D_MODEL // 2
        xw = x_ref[...]
        x_lo = lax.bitcast_convert_type(xw << 16, F32).astype(BF16)
        x_hi = lax.bitcast_convert_type(xw & jnp.uint32(0xFFFF0000), F32).astype(BF16)

        def proj(w_ref):
            return (jnp.dot(x_lo, w_ref[0:half, :], preferred_element_type=F32)
                    + jnp.dot(x_hi, w_ref[half:, :], preferred_element_type=F32))

        def expert(wg, wu, wd):
            hid = _silu(proj(wg)) * proj(wu)
            return jnp.dot(hid.astype(BF16), wd[...], preferred_element_type=F32)

        rw = rw_ref[...]
        y_ref[...] = rw[:, 0:1] * expert(wg1, wu1, wd1) + rw[:, 1:2] * expert(wg2, wu2, wd2)


def _expert_ffn(xs, row_w, tile_e1, tile_e2, n_used, wg, wu, wd):
    r = xs.shape[0]
    first = lambda i, e1, e2, nu: (e1[i], 0, 0)
    second = lambda i, e1, e2, nu: (e2[i], 0, 0)
    up = lambda f: pl.BlockSpec((None, D_MODEL, D_EXPERT), f)
    down = lambda f: pl.BlockSpec((None, D_EXPERT, D_MODEL), f)
    return pl.pallas_call(
        _ffn_kernel,
        grid_spec=pltpu.PrefetchScalarGridSpec(
            num_scalar_prefetch=3,
            grid=(r // MOE_TILE,),
            in_specs=[pl.BlockSpec((MOE_TILE, D_MODEL // 2), lambda i, e1, e2, nu: (i, 0)),
                      pl.BlockSpec((MOE_TILE, 8), lambda i, e1, e2, nu: (i, 0)),
                      up(first), up(first), down(first), up(second), up(second), down(second)],
            out_specs=pl.BlockSpec((MOE_TILE, D_MODEL), lambda i, e1, e2, nu: (i, 0)),
        ),
        out_shape=jax.ShapeDtypeStruct((r, D_MODEL), F32),
        compiler_params=_cparams(("arbitrary",)),
        name="expert_pair_ffn",
    )(tile_e1, tile_e2, n_used, xs, row_w, wg, wu, wd, wg, wu, wd)


def _combine_kernel(final_norm, pos_ref, x_ref, gain_ref, ys_ref, o_ref, buf, sem):
    rows = o_ref.shape[0]
    base = pl.program_id(0) * rows

    def issue(r, carry):
        pltpu.make_async_copy(ys_ref.at[pl.ds(pos_ref[base + r], 1)], buf.at[pl.ds(r, 1)], sem).start()
        return carry

    lax.fori_loop(0, rows, issue, 0)
    pltpu.make_async_copy(ys_ref.at[pl.ds(0, rows)], buf, sem).wait()
    out = x_ref[...] + buf[...]
    if final_norm:
        ms = jnp.mean(out * out, axis=-1, keepdims=True)
        out = out * lax.rsqrt(ms + EPS) * gain_ref[...]
    o_ref[...] = out


def _combine(x2, ys, pos, gain, final_norm):
    n = x2.shape[0]
    return pl.pallas_call(
        functools.partial(_combine_kernel, final_norm),
        grid_spec=pltpu.PrefetchScalarGridSpec(
            num_scalar_prefetch=1,
            grid=(n // ROW_TILE,),
            in_specs=[pl.BlockSpec((ROW_TILE, D_MODEL), lambda i, p: (i, 0)),
                      pl.BlockSpec((1, D_MODEL), lambda i, p: (0, 0)),
                      pl.BlockSpec(memory_space=pl.ANY)],
            out_specs=pl.BlockSpec((ROW_TILE, D_MODEL), lambda i, p: (i, 0)),
            scratch_shapes=[pltpu.VMEM((ROW_TILE, D_MODEL), F32), pltpu.SemaphoreType.DMA(())],
        ),
        out_shape=jax.ShapeDtypeStruct((n, D_MODEL), F32),
        compiler_params=_cparams(("arbitrary",)),
        name="moe_combine",
    )(pos, x2, gain, ys)


def _route(logits):
    n = logits.shape[0]
    group_prob = jax.nn.softmax(logits[:, :N_GROUPS], axis=-1)
    group_p, group_idx = lax.top_k(group_prob, 1)
    expert_logits = logits[:, N_GROUPS:N_GROUPS + N_EXPERTS].reshape(n, N_GROUPS, EXPERTS_PER_GROUP)
    in_group = jnp.take_along_axis(expert_logits, group_idx[:, :, None], axis=1)[:, 0]
    top_logit, top_idx = lax.top_k(in_group, 2)
    weights = jax.nn.softmax(top_logit, axis=-1) * group_p
    swap = top_idx[:, 0] > top_idx[:, 1]
    e_lo = jnp.where(swap, top_idx[:, 1], top_idx[:, 0])
    e_hi = jnp.where(swap, top_idx[:, 0], top_idx[:, 1])
    w_lo = jnp.where(swap, weights[:, 1], weights[:, 0])
    w_hi = jnp.where(swap, weights[:, 0], weights[:, 1])
    g = group_idx[:, 0]
    n_cls = N_GROUPS * EXPERTS_PER_GROUP * EXPERTS_PER_GROUP
    cls = (g * EXPERTS_PER_GROUP + e_lo) * EXPERTS_PER_GROUP + e_hi

    counts = jnp.zeros((n_cls,), jnp.int32).at[cls].add(1)
    tiles_per = (counts + MOE_TILE - 1) // MOE_TILE
    tile_end = jnp.cumsum(tiles_per)
    tile_start = tile_end - tiles_per
    order = jnp.argsort(cls, stable=True).astype(jnp.int32)
    cls_sorted = cls[order]
    first_sorted = jnp.cumsum(counts) - counts
    rank = jnp.arange(n, dtype=jnp.int32) - first_sorted[cls_sorted]
    pos_sorted = tile_start[cls_sorted] * MOE_TILE + rank
    n_pairs = N_GROUPS * (EXPERTS_PER_GROUP * (EXPERTS_PER_GROUP - 1)) // 2
    n_tiles = n // MOE_TILE + n_pairs
    n_rows = n_tiles * MOE_TILE
    pos = jnp.zeros((n,), jnp.int32).at[order].set(pos_sorted)
    src = jnp.zeros((n_rows,), jnp.int32).at[pos_sorted].set(order)
    row_w = jnp.zeros((n_rows, 8), F32)
    row_w = row_w.at[pos, 0].set(w_lo).at[pos, 1].set(w_hi)
    n_used = tile_end[-1]
    tile_ids = jnp.minimum(jnp.arange(n_tiles, dtype=jnp.int32), n_used - 1)
    tile_cls = jnp.searchsorted(tile_end, tile_ids, side="right").astype(jnp.int32)
    tile_e1 = tile_cls // EXPERTS_PER_GROUP
    tile_e2 = (tile_cls // (EXPERTS_PER_GROUP * EXPERTS_PER_GROUP)) * EXPERTS_PER_GROUP + tile_cls % EXPERTS_PER_GROUP
    return pos, src, row_w, tile_e1.astype(jnp.int32), tile_e2.astype(jnp.int32), n_used.reshape(1).astype(jnp.int32)


def _rope_tables(t):
    inv_freq = ROPE_BASE ** (-jnp.arange(0, HEAD_DIM, 2, dtype=F32) / HEAD_DIM)
    ang = jnp.arange(t, dtype=F32)[:, None] * inv_freq[None, :]
    cos, sin = jnp.cos(ang), jnp.sin(ang)
    cos_tab = jnp.tile(cos, (1, 2 * LANES // HEAD_DIM))
    sin_tab = jnp.tile(jnp.concatenate([-sin, sin], axis=-1), (1, LANES // HEAD_DIM))
    return cos_tab, sin_tab


def _split_w_in(w):
    o = 0
    parts = {}
    for name, width in (("g_qkv", 3 * GDN_W), ("g_z", GDN_W), ("g_b", GDN_HEADS), ("g_a", GDN_HEADS),
                        ("r_q", RET_W), ("r_k", RET_W), ("r_v", RET_W), ("r_g", RET_W),
                        ("m_qk", 2 * MLSTM_W), ("m_v", MLSTM_W), ("m_o", MLSTM_W),
                        ("m_i", MLSTM_HEADS), ("m_f", MLSTM_HEADS)):
        parts[name] = w[:, o:o + width]
        o += width
    main = jnp.concatenate([parts[k] for k in ("m_qk", "m_v", "m_o", "g_qkv", "g_z", "r_q", "r_k", "r_v", "r_g")],
                           axis=1).astype(BF16)
    gate = jnp.concatenate([parts[k] for k in ("g_b", "g_a", "m_i", "m_f")], axis=1)
    gate = jnp.pad(gate, ((0, 0), (0, LANES - gate.shape[1]))).astype(BF16)
    return main, gate


def _lane_row(values, start):
    return jnp.zeros((LANES,), F32).at[start:start + values.shape[0]].set(values)


def kernel(x, norm_mix, w_in, gdn_conv, gdn_a_log, gdn_dt_bias, gdn_norm, ret_norm, mlstm_conv, mlstm_i_bias, mlstm_f_bias, mlstm_norm, w_out, norm_ffn, router_group, router_expert, router_bias, expert_gate, expert_up, expert_down, norm_final):
    b, t, d = x.shape
    n = b * t
    depth = w_in.shape[0]
    cos_tab, sin_tab = _rope_tables(t)
    x2 = x.reshape(n, d)
    for l in range(depth):
        w_main, w_gate = _split_w_in(w_in[l])
        proj, gates = _inproj(x2, norm_mix[l][None, :], w_main, w_gate)
        proj3 = proj.reshape(b, t, D_MAIN)
        gates3 = gates.reshape(b, t, LANES)
        gdn_prm = jnp.zeros((8, LANES), F32).at[0].set(_lane_row(gdn_a_log[l], G_A)).at[1].set(
            _lane_row(gdn_dt_bias[l], G_A))
        mlstm_prm = jnp.zeros((8, LANES), F32).at[0].set(_lane_row(mlstm_i_bias[l], M_I)).at[1].set(
            _lane_row(mlstm_f_bias[l], M_F))
        y_m = _mlstm_mixer(proj3, gates3, mlstm_conv[l], mlstm_prm, mlstm_norm[l][None, :])
        y_g = _gdn_mixer(proj3, gates3, gdn_conv[l], gdn_prm, gdn_norm[l][None, :])
        y_r = _ret_mixer(proj3, cos_tab, sin_tab, ret_norm[l][None, :])
        wo = w_out[l].astype(BF16)
        w_router = jnp.pad(jnp.concatenate([router_group[l], router_expert[l]], axis=1),
                           ((0, 0), (0, LANES - N_GROUPS - N_EXPERTS))).astype(BF16)
        rb = _lane_row(router_bias[l], N_GROUPS)[None, :]
        x2, h_packed, logits = _outproj(
            x2, y_m.reshape(n, MLSTM_W), y_g.reshape(n, GDN_W), y_r.reshape(n, RET_W),
            wo[GDN_W + RET_W:], wo[:GDN_W], wo[GDN_W:GDN_W + RET_W],
            norm_ffn[l][None, :], w_router, rb)
        pos, src, row_w, tile_e1, tile_e2, n_used = _route(logits)
        xs = _row_gather(h_packed, src, MOE_TILE)
        ys = _expert_ffn(xs, row_w, tile_e1, tile_e2, n_used,
                         expert_gate[l].astype(BF16), expert_up[l].astype(BF16), expert_down[l].astype(BF16))
        x2 = _combine(x2, ys, pos, norm_final[None, :], l == depth - 1)
    return x2.reshape(b, t, d)
```

```python
import functools
import math

import jax
import jax.numpy as jnp
from jax import lax
from jax.experimental import pallas as pl
from jax.experimental.pallas import tpu as pltpu

F32 = jnp.float32
BF16 = jnp.bfloat16

D_MODEL = 1024
HEAD_DIM = 64
CHUNK = 64
GDN_HEADS, RET_HEADS, MLSTM_HEADS = 6, 4, 6
GDN_W, RET_W, MLSTM_W = GDN_HEADS * HEAD_DIM, RET_HEADS * HEAD_DIM, MLSTM_HEADS * HEAD_DIM
CONV_K = 4
ROPE_BASE = 10000.0
N_GROUPS, EXPERTS_PER_GROUP = 4, 8
N_EXPERTS = N_GROUPS * EXPERTS_PER_GROUP
D_EXPERT = D_MODEL // 4
EPS = 1e-6
LANES = 128
SEC_M, SEC_G, SEC_R = 4 * MLSTM_W, 4 * GDN_W, 4 * RET_W
D_MAIN = SEC_M + SEC_G + SEC_R
G_BETA, G_A, M_I, M_F = 0, GDN_HEADS, 2 * GDN_HEADS, 2 * GDN_HEADS + MLSTM_HEADS

ROW_TILE = 256
TIME_BLOCK = 256
MOE_TILE = 128
VMEM_LIMIT = 56 * 1024 * 1024


def _cparams(sem):
    return pltpu.CompilerParams(dimension_semantics=sem, vmem_limit_bytes=VMEM_LIMIT)


def _silu(x):
    return x * (1.0 / (1.0 + jnp.exp(-x)))


def _sigmoid(x):
    return 1.0 / (1.0 + jnp.exp(-x))


def _softplus(x):
    return jnp.maximum(x, 0.0) + jnp.log(1.0 + jnp.exp(-jnp.abs(x)))


def _dot(a, b):
    return jnp.dot(a.astype(BF16), b.astype(BF16), preferred_element_type=F32)


def _dot_nt(a, b):
    return lax.dot_general(a.astype(BF16), b.astype(BF16), (((1,), (1,)), ((), ())),
                           preferred_element_type=F32)


def _lane_masks(rows):
    lane = lax.broadcasted_iota(jnp.int32, (rows, LANES), 1)
    row = lax.broadcasted_iota(jnp.int32, (rows, LANES), 0)
    return lane, row


def _half_sum(x, m_a):
    s_a = jnp.sum(jnp.where(m_a, x, 0.0), axis=-1, keepdims=True)
    s_b = jnp.sum(jnp.where(m_a, 0.0, x), axis=-1, keepdims=True)
    return jnp.where(m_a, s_a, s_b)


def _half_max(x, m_a):
    s_a = jnp.max(jnp.where(m_a, x, -jnp.inf), axis=-1, keepdims=True)
    s_b = jnp.max(jnp.where(m_a, -jnp.inf, x), axis=-1, keepdims=True)
    return jnp.where(m_a, s_a, s_b)


def _col_form(g, lane_a, m_a):
    rows = g.shape[0]
    ca = jnp.broadcast_to(g[:, lane_a:lane_a + 1], (rows, LANES))
    cb = jnp.broadcast_to(g[:, lane_a + 1:lane_a + 2], (rows, LANES))
    return jnp.where(m_a, ca, cb)


def _row_form(col, eye):
    return jnp.sum(jnp.where(eye, col, 0.0), axis=0, keepdims=True)


def _block_diag(y, m_a):
    return jnp.concatenate([jnp.where(m_a, y, 0.0), jnp.where(m_a, 0.0, y)], axis=0)


def _pmul(x, y, m_a):
    return _dot(x, _block_diag(y, m_a))


def _outer_state(k, v, bd_mask):
    zero = jnp.zeros_like(k)
    kt = jnp.concatenate([k, zero], axis=0).T
    vp = jnp.concatenate([v, zero], axis=0)
    return jnp.where(bd_mask, _dot(kt, vp), 0.0)


def _chunk_cumsum(x):
    rows = x.shape[0]
    r = lax.broadcasted_iota(jnp.int32, (rows, LANES), 0) % CHUNK
    s = 1
    while s < CHUNK:
        x = x + jnp.where(r >= s, pltpu.roll(x, s, axis=0), 0.0)
        s *= 2
    return x


def _causal_conv(x_ref, cbuf, w_ref, width, first):
    tb = x_ref.shape[0]

    @pl.when(first)
    def _():
        cbuf[0:8, :] = jnp.zeros((8, width), F32)

    cbuf[8:8 + tb, :] = x_ref[:, 0:width]
    acc = cbuf[8:8 + tb, :] * w_ref[CONV_K - 1:CONV_K, :]
    for j in range(CONV_K - 1):
        off = 8 - (CONV_K - 1) + j
        acc = acc + cbuf[off:off + tb, :] * w_ref[j:j + 1, :]
    cbuf[0:8, :] = x_ref[tb - 8:tb, 0:width]
    return acc


def _inproj_kernel(x_ref, gain_ref, w_ref, wg_ref, o_ref, og_ref):
    x = x_ref[...]
    ms = jnp.mean(x * x, axis=-1, keepdims=True)
    h = (x * lax.rsqrt(ms + EPS) * gain_ref[...]).astype(BF16)
    step = 512
    for c in range(D_MAIN // step):
        o_ref[:, c * step:(c + 1) * step] = jnp.dot(
            h, w_ref[:, c * step:(c + 1) * step], preferred_element_type=F32)
    og_ref[...] = jnp.dot(h, wg_ref[...], preferred_element_type=F32)


def _inproj(x2, gain, w_main, w_gate):
    n = x2.shape[0]
    return pl.pallas_call(
        _inproj_kernel,
        grid=(n // ROW_TILE,),
        in_specs=[
            pl.BlockSpec((ROW_TILE, D_MODEL), lambda i: (i, 0)),
            pl.BlockSpec((1, D_MODEL), lambda i: (0, 0)),
            pl.BlockSpec((D_MODEL, D_MAIN), lambda i: (0, 0)),
            pl.BlockSpec((D_MODEL, LANES), lambda i: (0, 0)),
        ],
        out_specs=[
            pl.BlockSpec((ROW_TILE, D_MAIN), lambda i: (i, 0)),
            pl.BlockSpec((ROW_TILE, LANES), lambda i: (i, 0)),
        ],
        out_shape=[jax.ShapeDtypeStruct((n, D_MAIN), F32), jax.ShapeDtypeStruct((n, LANES), F32)],
        compiler_params=_cparams(("parallel",)),
        name="norm_inproj",
    )(x2, gain, w_main, w_gate)


N_CLASSES = N_GROUPS * EXPERTS_PER_GROUP * EXPERTS_PER_GROUP
HP_WIDTH = D_MODEL + LANES


def _route_rows(lg):
    rows = lg.shape[0]
    lane = lax.broadcasted_iota(jnp.int32, (rows, LANES), 1)
    neg = -jnp.inf
    gl = jnp.where(lane < N_GROUPS, lg, neg)
    gm = jnp.max(gl, axis=-1, keepdims=True)
    gsum = jnp.sum(jnp.where(lane < N_GROUPS, jnp.exp(gl - gm), 0.0), axis=-1, keepdims=True)
    gidx = jnp.min(jnp.where(gl == gm, lane, LANES), axis=-1, keepdims=True)
    group_p = 1.0 / gsum
    in_group = (lane >= N_GROUPS) & (lane < N_GROUPS + N_EXPERTS) & (
        lax.shift_right_arithmetic(lane - N_GROUPS, 3) == gidx)
    el = jnp.where(in_group, lg, neg)
    m1 = jnp.max(el, axis=-1, keepdims=True)
    i1 = jnp.min(jnp.where(el == m1, lane, LANES), axis=-1, keepdims=True)
    el2 = jnp.where(lane == i1, neg, el)
    m2 = jnp.max(el2, axis=-1, keepdims=True)
    i2 = jnp.min(jnp.where(el2 == m2, lane, LANES), axis=-1, keepdims=True)
    e2 = jnp.exp(m2 - m1)
    w1 = group_p * (1.0 / (1.0 + e2))
    w2 = group_p * (e2 / (1.0 + e2))
    first_lower = i1 < i2
    base = N_GROUPS + gidx * EXPERTS_PER_GROUP
    lo = jnp.minimum(i1, i2) - base
    hi = jnp.maximum(i1, i2) - base
    cls = (gidx * EXPERTS_PER_GROUP + lo) * EXPERTS_PER_GROUP + hi
    return cls, jnp.where(first_lower, w1, w2), jnp.where(first_lower, w2, w1)


def _outproj_kernel(x_ref, ym_ref, yg_ref, yr_ref, wm_ref, wg_ref, wr_ref, gain_ref, wrt_ref, rb_ref,
                    xo_ref, hp_ref, rt_ref, cnt_ref, running):
    @pl.when(pl.program_id(0) == 0)
    def _():
        running[...] = jnp.zeros_like(running)

    acc = x_ref[...]
    acc = acc + jnp.dot(yg_ref[...], wg_ref[...], preferred_element_type=F32)
    acc = acc + jnp.dot(yr_ref[...], wr_ref[...], preferred_element_type=F32)
    acc = acc + jnp.dot(ym_ref[...], wm_ref[...], preferred_element_type=F32)
    xo_ref[...] = acc
    ms = jnp.mean(acc * acc, axis=-1, keepdims=True)
    hn = acc * lax.rsqrt(ms + EPS) * gain_ref[...]
    logits = jnp.dot(hn.astype(BF16), wrt_ref[...], preferred_element_type=F32) + rb_ref[...]
    cls, w_lo, w_hi = _route_rows(logits)

    rows = acc.shape[0]
    cls_lane = lax.broadcasted_iota(jnp.int32, (rows, N_CLASSES), 1)
    onehot = cls_lane == cls
    before = lax.broadcasted_iota(jnp.int32, (rows, rows), 1) < lax.broadcasted_iota(jnp.int32, (rows, rows), 0)
    prefix = jnp.dot(before.astype(BF16), onehot.astype(BF16), preferred_element_type=F32)
    rank = jnp.sum(jnp.where(onehot, prefix + running[0:1, :], 0.0), axis=-1, keepdims=True)
    running[0:1, :] = running[0:1, :] + jnp.sum(onehot.astype(F32), axis=0, keepdims=True)
    cnt_ref[...] = jnp.broadcast_to(running[0:1, :], cnt_ref.shape)

    lane = lax.broadcasted_iota(jnp.int32, (rows, LANES), 1)
    rt_ref[...] = jnp.where(lane == 0, cls, jnp.where(lane == 1, rank.astype(jnp.int32), 0))
    hp_ref[:, 0:D_MODEL] = hn
    hp_ref[:, D_MODEL:] = jnp.where(lane == 0, w_lo, jnp.where(lane == 1, w_hi, 0.0))


def _outproj(x2, ym, yg, yr, wo_m, wo_g, wo_r, gain, w_router, router_bias):
    n = x2.shape[0]
    row = lambda w: pl.BlockSpec((ROW_TILE, w), lambda i: (i, 0))
    full = lambda a, b: pl.BlockSpec((a, b), lambda i: (0, 0))
    return pl.pallas_call(
        _outproj_kernel,
        grid=(n // ROW_TILE,),
        in_specs=[row(D_MODEL), row(MLSTM_W), row(GDN_W), row(RET_W),
                  full(MLSTM_W, D_MODEL), full(GDN_W, D_MODEL), full(RET_W, D_MODEL),
                  full(1, D_MODEL), full(D_MODEL, LANES), full(1, LANES)],
        out_specs=[row(D_MODEL), row(HP_WIDTH), row(LANES), full(8, N_CLASSES)],
        out_shape=[jax.ShapeDtypeStruct((n, D_MODEL), F32),
                   jax.ShapeDtypeStruct((n, HP_WIDTH), F32),
                   jax.ShapeDtypeStruct((n, LANES), jnp.int32),
                   jax.ShapeDtypeStruct((8, N_CLASSES), F32)],
        scratch_shapes=[pltpu.VMEM((8, N_CLASSES), F32)],
        compiler_params=_cparams(("arbitrary",)),
        name="outproj_norm_router",
    )(x2, ym, yg, yr, wo_m, wo_g, wo_r, gain, w_router, router_bias)


def _gdn_kernel(x_ref, gt_ref, conv_ref, prm_ref, gain_ref, y_ref, state, cbuf):
    tb = x_ref.shape[0]
    first = pl.program_id(1) == 0

    @pl.when(first)
    def _():
        state[...] = jnp.zeros_like(state)

    qkv = _silu(_causal_conv(x_ref, cbuf, conv_ref, 3 * GDN_W, first))
    gt = gt_ref[...]
    beta_all = _sigmoid(gt)
    g_all = -jnp.exp(prm_ref[0:1, :]) * _softplus(gt + prm_ref[1:2, :])
    gc_all = _chunk_cumsum(g_all)

    lane, row = _lane_masks(CHUNK)
    m_a = lane < HEAD_DIM
    lane_h = lane % HEAD_DIM
    eye = lane_h == row
    tril = lane_h <= row
    strict = lane_h < row
    eye_f = eye.astype(F32)
    lane2 = lax.broadcasted_iota(jnp.int32, (LANES, LANES), 1)
    row2 = lax.broadcasted_iota(jnp.int32, (LANES, LANES), 0)
    bd_mask = (lane2 < HEAD_DIM) == (row2 < HEAD_DIM)
    lane_t = lax.broadcasted_iota(jnp.int32, (tb, LANES), 1)
    m_a_t = lane_t < HEAD_DIM
    scale = HEAD_DIM ** -0.5

    n_pairs = GDN_HEADS // 2
    n_chunks = tb // CHUNK
    units = [(p, c) for p in range(n_pairs) for c in range(n_chunks)]

    pre = {}
    for p in range(n_pairs):
        q_t = qkv[:, p * LANES:(p + 1) * LANES]
        k_t = qkv[:, GDN_W + p * LANES:GDN_W + (p + 1) * LANES]
        v_t = qkv[:, 2 * GDN_W + p * LANES:2 * GDN_W + (p + 1) * LANES]
        q_t = q_t * lax.rsqrt(_half_sum(q_t * q_t, m_a_t) + EPS) * scale
        k_t = k_t * lax.rsqrt(_half_sum(k_t * k_t, m_a_t) + EPS)
        beta_t = _col_form(beta_all, G_BETA + 2 * p, m_a_t)
        gc_t = _col_form(gc_all, G_A + 2 * p, m_a_t)
        for c in range(n_chunks):
            rs = slice(c * CHUNK, (c + 1) * CHUNK)
            q, k, v, beta, gc = q_t[rs], k_t[rs], v_t[rs], beta_t[rs], gc_t[rs]
            decay = jnp.where(tril, jnp.exp(gc - _row_form(gc, eye)), 0.0)
            kb = k * beta
            kkqk = _dot_nt(jnp.concatenate([kb, q], axis=0), _block_diag(k, m_a))
            egc = jnp.exp(gc)
            g_last = gc[CHUNK - 1:CHUNK, :]
            k_dec = k * jnp.exp(g_last - gc)
            pre[p, c] = dict(
                a=jnp.where(strict, kkqk[:CHUNK] * decay, 0.0), qk=kkqk[CHUNK:] * decay,
                vb=v * beta, kbe=kb * egc, qe=q * egc, s_decay=jnp.exp(g_last),
                k_dec_t=jnp.concatenate([k_dec, jnp.zeros_like(k_dec)], axis=0).T)

    t_inv = {u: eye_f - pre[u]["a"] for u in units}
    pw = {u: pre[u]["a"] for u in units}
    for _ in range(5):
        pw = {u: _pmul(pw[u], pw[u], m_a) for u in units}
        t_inv = {u: _pmul(t_inv[u], eye_f + pw[u], m_a) for u in units}
    u_mat = {u: _pmul(t_inv[u], pre[u]["vb"], m_a) for u in units}
    w_mat = {u: _pmul(t_inv[u], pre[u]["kbe"], m_a) for u in units}

    s_bd = [state[p] for p in range(n_pairs)]
    for c in range(n_chunks):
        rs = slice(c * CHUNK, (c + 1) * CHUNK)
        for p in range(n_pairs):
            cs = slice(p * LANES, (p + 1) * LANES)
            d = pre[p, c]
            ws_qs = _dot(jnp.concatenate([w_mat[p, c], d["qe"]], axis=0), s_bd[p])
            v_new = u_mat[p, c] - ws_qs[:CHUNK]
            o = ws_qs[CHUNK:] + _pmul(d["qk"], v_new, m_a)
            v_pad = jnp.concatenate([v_new, jnp.zeros_like(v_new)], axis=0)
            s_bd[p] = s_bd[p] * d["s_decay"] + jnp.where(bd_mask, _dot(d["k_dec_t"], v_pad), 0.0)
            ms = _half_sum(o * o, m_a) * (1.0 / HEAD_DIM)
            z = x_ref[rs, 3 * GDN_W + p * LANES:3 * GDN_W + (p + 1) * LANES]
            y = o * lax.rsqrt(ms + EPS) * gain_ref[:, cs] * _silu(z)
            y_ref[rs, cs] = y.astype(y_ref.dtype)
    for p in range(n_pairs):
        state[p] = s_bd[p]


def _ret_kernel(x_ref, cos_ref, sin_ref, gain_ref, y_ref, state):
    tb = x_ref.shape[0]

    @pl.when(pl.program_id(1) == 0)
    def _():
        state[...] = jnp.zeros_like(state)

    lane, row = _lane_masks(CHUNK)
    m_a = lane < HEAD_DIM
    lane_h = lane % HEAD_DIM
    tril = lane_h <= row
    lane2 = lax.broadcasted_iota(jnp.int32, (LANES, LANES), 1)
    row2 = lax.broadcasted_iota(jnp.int32, (LANES, LANES), 0)
    bd_mask = (lane2 < HEAD_DIM) == (row2 < HEAD_DIM)
    lane_t = lax.broadcasted_iota(jnp.int32, (tb, LANES), 1)
    first_half = (lane_t % HEAD_DIM) < (HEAD_DIM // 2)
    cos = cos_ref[...]
    sin = sin_ref[...]
    rowf = row.astype(F32)
    pos_diff = (row - lane_h).astype(F32)

    def rope(x):
        swapped = jnp.where(first_half, pltpu.roll(x, LANES - HEAD_DIM // 2, axis=1),
                            pltpu.roll(x, HEAD_DIM // 2, axis=1))
        return x * cos + swapped * sin

    n_pairs = RET_HEADS // 2
    n_chunks = tb // CHUNK
    pre = {}
    chunk_decay = []
    for p in range(n_pairs):
        lg_a = math.log(1.0 - 2.0 ** (-5.0 - 2 * p))
        lg_b = math.log(1.0 - 2.0 ** (-5.0 - (2 * p + 1)))
        lg = jnp.where(m_a, lg_a, lg_b)
        decay = jnp.where(tril, jnp.exp(pos_diff * lg), 0.0)
        q_scale = jnp.exp(lg * (rowf + 1.0))
        k_scale = jnp.exp(lg * (CHUNK - 1.0 - rowf))
        chunk_decay.append(jnp.exp(lg[0:1, :] * CHUNK))
        q_t = rope(x_ref[:, p * LANES:(p + 1) * LANES])
        k_t = rope(x_ref[:, RET_W + p * LANES:RET_W + (p + 1) * LANES]) * (HEAD_DIM ** -0.5)
        for c in range(n_chunks):
            rs = slice(c * CHUNK, (c + 1) * CHUNK)
            q, k = q_t[rs], k_t[rs]
            v = x_ref[rs, 2 * RET_W + p * LANES:2 * RET_W + (p + 1) * LANES]
            scores = _dot_nt(q, _block_diag(k, m_a)) * decay
            pre[p, c] = (_pmul(scores, v, m_a), q * q_scale, _outer_state(k * k_scale, v, bd_mask))

    s_bd = [state[p] for p in range(n_pairs)]
    for c in range(n_chunks):
        rs = slice(c * CHUNK, (c + 1) * CHUNK)
        for p in range(n_pairs):
            cs = slice(p * LANES, (p + 1) * LANES)
            o_intra, q_dec, kv = pre[p, c]
            o = o_intra + _dot(q_dec, s_bd[p])
            s_bd[p] = chunk_decay[p] * s_bd[p] + kv
            ms = _half_sum(o * o, m_a) * (1.0 / HEAD_DIM)
            gate = x_ref[rs, 3 * RET_W + p * LANES:3 * RET_W + (p + 1) * LANES]
            y = o * lax.rsqrt(ms + EPS) * gain_ref[:, cs] * _silu(gate)
            y_ref[rs, cs] = y.astype(y_ref.dtype)
    for p in range(n_pairs):
        state[p] = s_bd[p]


def _mlstm_kernel(x_ref, gt_ref, conv_ref, prm_ref, gain_ref, y_ref, c_state, nm_state, cbuf):
    tb = x_ref.shape[0]
    first = pl.program_id(1) == 0

    @pl.when(first)
    def _():
        c_state[...] = jnp.zeros_like(c_state)
        nm_state[...] = jnp.zeros_like(nm_state)

    qk_all = _silu(_causal_conv(x_ref, cbuf, conv_ref, 2 * MLSTM_W, first))
    gt = gt_ref[...]
    log_i_all = gt + prm_ref[0:1, :]
    f_pre = gt + prm_ref[1:2, :]
    log_f_all = jnp.minimum(f_pre, 0.0) - jnp.log(1.0 + jnp.exp(-jnp.abs(f_pre)))
    bc_all = _chunk_cumsum(log_f_all)

    lane, row = _lane_masks(CHUNK)
    m_a = lane < HEAD_DIM
    lane_h = lane % HEAD_DIM
    eye = lane_h == row
    tril = lane_h <= row
    lane2 = lax.broadcasted_iota(jnp.int32, (LANES, LANES), 1)
    row2 = lax.broadcasted_iota(jnp.int32, (LANES, LANES), 0)
    bd_mask = (lane2 < HEAD_DIM) == (row2 < HEAD_DIM)
    lane_t = lax.broadcasted_iota(jnp.int32, (tb, LANES), 1)
    m_a_t = lane_t < HEAD_DIM
    scale = HEAD_DIM ** -0.5

    n_pairs = MLSTM_HEADS // 2
    n_chunks = tb // CHUNK
    pre = {}
    m_final = []
    for p in range(n_pairs):
        q_t = qk_all[:, p * LANES:(p + 1) * LANES] * scale
        k_t = qk_all[:, MLSTM_W + p * LANES:MLSTM_W + (p + 1) * LANES]
        b_t = _col_form(bc_all, M_F + 2 * p, m_a_t)
        i_t = _col_form(log_i_all, M_I + 2 * p, m_a_t)
        m_row = nm_state[p, 1:2, :]
        for c in range(n_chunks):
            rs = slice(c * CHUNK, (c + 1) * CHUNK)
            q, k, b_c, i_c = q_t[rs], k_t[rs], b_t[rs], i_t[rs]
            v = x_ref[rs, 2 * MLSTM_W + p * LANES:2 * MLSTM_W + (p + 1) * LANES]
            qk = _dot_nt(q, _block_diag(k, m_a))
            log_d = jnp.where(tril, b_c - _row_form(b_c, eye) + _row_form(i_c, eye), -jnp.inf)
            m_t = jnp.maximum(b_c + m_row, _half_max(log_d, m_a))
            inter = jnp.exp(b_c + m_row - m_t)
            wmat = jnp.where(tril, jnp.exp(log_d - m_t), 0.0) * qk
            b_last = b_c[CHUNK - 1:CHUNK, :]
            a_c = b_last - b_c + i_c
            m_new = jnp.maximum(b_last + m_row, jnp.max(a_c, axis=0, keepdims=True))
            kw = k * jnp.exp(a_c - m_new)
            pre[p, c] = dict(
                q=q, inter=inter, num=_pmul(wmat, v, m_a), den=_half_sum(wmat, m_a), floor=jnp.exp(-m_t),
                dec=jnp.exp(b_last + m_row - m_new), kv=_outer_state(kw, v, bd_mask),
                kn=jnp.sum(kw, axis=0, keepdims=True))
            m_row = m_new
        m_final.append(m_row)

    c_bd = [c_state[p] for p in range(n_pairs)]
    n_row = [nm_state[p, 0:1, :] for p in range(n_pairs)]
    for c in range(n_chunks):
        rs = slice(c * CHUNK, (c + 1) * CHUNK)
        for p in range(n_pairs):
            cs = slice(p * LANES, (p + 1) * LANES)
            d = pre[p, c]
            num = d["inter"] * _dot(d["q"], c_bd[p]) + d["num"]
            den = d["inter"] * _half_sum(d["q"] * n_row[p], m_a) + d["den"]
            h = num / jnp.maximum(jnp.abs(den), d["floor"])
            c_bd[p] = d["dec"] * c_bd[p] + d["kv"]
            n_row[p] = d["dec"] * n_row[p] + d["kn"]
            ms = _half_sum(h * h, m_a) * (1.0 / HEAD_DIM)
            o_logit = x_ref[rs, 3 * MLSTM_W + p * LANES:3 * MLSTM_W + (p + 1) * LANES]
            y = _sigmoid(o_logit) * (h * lax.rsqrt(ms + EPS) * gain_ref[:, cs])
            y_ref[rs, cs] = y.astype(y_ref.dtype)
    for p in range(n_pairs):
        c_state[p] = c_bd[p]
        nm_state[p, 0:1, :] = n_row[p]
        nm_state[p, 1:2, :] = m_final[p]


def _mixer_call(kernel, name, proj3, sec_block, sec_width, out_width, extra_inputs, extra_specs, scratch):
    b, t, _ = proj3.shape
    tb = min(TIME_BLOCK, t)
    return pl.pallas_call(
        kernel,
        grid=(b, t // tb),
        in_specs=[pl.BlockSpec((None, tb, sec_width), lambda i, j: (i, j, sec_block))] + extra_specs(tb),
        out_specs=pl.BlockSpec((None, tb, out_width), lambda i, j: (i, j, 0)),
        out_shape=jax.ShapeDtypeStruct((b, t, out_width), BF16),
        scratch_shapes=scratch(tb),
        compiler_params=_cparams(("parallel", "arbitrary")),
        name=name,
    )(proj3, *extra_inputs)


def _full2(a, b):
    return pl.BlockSpec((a, b), lambda i, j: (0, 0))


def _gdn_mixer(proj3, gates3, conv_w, prm, gain):
    return _mixer_call(
        _gdn_kernel, "gdn_mixer", proj3, 1, SEC_G, GDN_W, (gates3, conv_w, prm, gain),
        lambda tb: [pl.BlockSpec((None, tb, LANES), lambda i, j: (i, j, 0)),
                    _full2(CONV_K, 3 * GDN_W), _full2(8, LANES), _full2(1, GDN_W)],
        lambda tb: [pltpu.VMEM((GDN_HEADS // 2, LANES, LANES), F32),
                    pltpu.VMEM((tb + 8, 3 * GDN_W), F32)])


def _mlstm_mixer(proj3, gates3, conv_w, prm, gain):
    return _mixer_call(
        _mlstm_kernel, "mlstm_mixer", proj3, 0, SEC_M, MLSTM_W, (gates3, conv_w, prm, gain),
        lambda tb: [pl.BlockSpec((None, tb, LANES), lambda i, j: (i, j, 0)),
                    _full2(CONV_K, 2 * MLSTM_W), _full2(8, LANES), _full2(1, MLSTM_W)],
        lambda tb: [pltpu.VMEM((MLSTM_HEADS // 2, LANES, LANES), F32),
                    pltpu.VMEM((MLSTM_HEADS // 2, 8, LANES), F32),
                    pltpu.VMEM((tb + 8, 2 * MLSTM_W), F32)])


def _ret_mixer(proj3, cos_tab, sin_tab, gain):
    return _mixer_call(
        _ret_kernel, "retention_mixer", proj3, 3, SEC_R, RET_W, (cos_tab, sin_tab, gain),
        lambda tb: [pl.BlockSpec((tb, LANES), lambda i, j: (j, 0)),
                    pl.BlockSpec((tb, LANES), lambda i, j: (j, 0)),
                    _full2(1, RET_W)],
        lambda tb: [pltpu.VMEM((RET_HEADS // 2, LANES, LANES), F32)])


DISPATCH_TILE = 512


def _dispatch_kernel(pos_ref, src_ref, zero_ref, out_ref, sem):
    del zero_ref
    rows = src_ref.shape[0]
    base = pl.program_id(0) * rows

    def issue(r, carry):
        pltpu.make_async_copy(src_ref.at[pl.ds(r, 1)], out_ref.at[pl.ds(pos_ref[base + r], 1)], sem).start()
        return carry

    lax.fori_loop(0, rows, issue, 0, unroll=8)
    pltpu.make_async_copy(src_ref, out_ref.at[pl.ds(0, rows)], sem).wait()


def _dispatch(src, pos, n_rows):
    n, width = src.shape
    tile = min(DISPATCH_TILE, n)
    zeros = jnp.zeros((n_rows, width), src.dtype)
    return pl.pallas_call(
        _dispatch_kernel,
        grid_spec=pltpu.PrefetchScalarGridSpec(
            num_scalar_prefetch=1,
            grid=(n // tile,),
            in_specs=[pl.BlockSpec((tile, width), lambda i, p: (i, 0)), pl.BlockSpec(memory_space=pl.ANY)],
            out_specs=pl.BlockSpec(memory_space=pl.ANY),
            scratch_shapes=[pltpu.SemaphoreType.DMA(())],
        ),
        out_shape=jax.ShapeDtypeStruct((n_rows, width), src.dtype),
        input_output_aliases={2: 0},
        compiler_params=_cparams(("arbitrary",)),
        name="moe_dispatch",
    )(pos, src, zeros)


def _ffn_kernel(e1_ref, e2_ref, used_ref, x_ref, wg1, wu1, wd1, wg2, wu2, wd2, y_ref):
    used = pl.program_id(0) < used_ref[0]

    @pl.when(jnp.logical_not(used))
    def _():
        y_ref[...] = jnp.zeros_like(y_ref)

    @pl.when(used)
    def _():
        x = x_ref[:, 0:D_MODEL].astype(BF16)

        def expert(wg, wu, wd):
            gate = jnp.dot(x, wg[...], preferred_element_type=F32)
            hid = _silu(gate) * jnp.dot(x, wu[...], preferred_element_type=F32)
            return jnp.dot(hid.astype(BF16), wd[...], preferred_element_type=F32)

        rw = x_ref[:, D_MODEL:]
        y_ref[...] = rw[:, 0:1] * expert(wg1, wu1, wd1) + rw[:, 1:2] * expert(wg2, wu2, wd2)


def _expert_ffn(xs, tile_e1, tile_e2, n_used, wg, wu, wd):
    r = xs.shape[0]
    first = lambda i, e1, e2, nu: (e1[i], 0, 0)
    second = lambda i, e1, e2, nu: (e2[i], 0, 0)
    up = lambda f: pl.BlockSpec((None, D_MODEL, D_EXPERT), f)
    down = lambda f: pl.BlockSpec((None, D_EXPERT, D_MODEL), f)
    return pl.pallas_call(
        _ffn_kernel,
        grid_spec=pltpu.PrefetchScalarGridSpec(
            num_scalar_prefetch=3,
            grid=(r // MOE_TILE,),
            in_specs=[pl.BlockSpec((MOE_TILE, HP_WIDTH), lambda i, e1, e2, nu: (jnp.minimum(i, nu[0] - 1), 0)),
                      up(first), up(first), down(first), up(second), up(second), down(second)],
            out_specs=pl.BlockSpec((MOE_TILE, D_MODEL), lambda i, e1, e2, nu: (i, 0)),
        ),
        out_shape=jax.ShapeDtypeStruct((r, D_MODEL), F32),
        compiler_params=_cparams(("arbitrary",)),
        name="expert_pair_ffn",
    )(tile_e1, tile_e2, n_used, xs, wg, wu, wd, wg, wu, wd)


def _combine_kernel(final_norm, pos_ref, x_ref, gain_ref, ys_ref, o_ref, buf, sems):
    rows = o_ref.shape[0]
    i = pl.program_id(0)
    slot = i & 1

    def gather_tile(tile, to_slot):
        def issue(r, carry):
            pltpu.make_async_copy(ys_ref.at[pl.ds(pos_ref[tile * rows + r], 1)],
                                  buf.at[to_slot, pl.ds(r, 1)], sems.at[to_slot]).start()
            return carry

        lax.fori_loop(0, rows, issue, 0, unroll=8)

    @pl.when(i == 0)
    def _():
        gather_tile(0, 0)

    @pl.when(i + 1 < pl.num_programs(0))
    def _():
        gather_tile(i + 1, 1 - slot)

    pltpu.make_async_copy(ys_ref.at[pl.ds(0, rows)], buf.at[slot], sems.at[slot]).wait()
    out = x_ref[...] + buf[slot]
    if final_norm:
        ms = jnp.mean(out * out, axis=-1, keepdims=True)
        out = out * lax.rsqrt(ms + EPS) * gain_ref[...]
    o_ref[...] = out


def _combine(x2, ys, pos, gain, final_norm):
    n = x2.shape[0]
    return pl.pallas_call(
        functools.partial(_combine_kernel, final_norm),
        grid_spec=pltpu.PrefetchScalarGridSpec(
            num_scalar_prefetch=1,
            grid=(n // ROW_TILE,),
            in_specs=[pl.BlockSpec((ROW_TILE, D_MODEL), lambda i, p: (i, 0)),
                      pl.BlockSpec((1, D_MODEL), lambda i, p: (0, 0)),
                      pl.BlockSpec(memory_space=pl.ANY)],
            out_specs=pl.BlockSpec((ROW_TILE, D_MODEL), lambda i, p: (i, 0)),
            scratch_shapes=[pltpu.VMEM((2, ROW_TILE, D_MODEL), F32), pltpu.SemaphoreType.DMA((2,))],
        ),
        out_shape=jax.ShapeDtypeStruct((n, D_MODEL), F32),
        compiler_params=_cparams(("arbitrary",)),
        name="moe_combine",
    )(pos, x2, gain, ys)


def _plan(route, counts, n):
    cls, rank = route[:, 0], route[:, 1]
    counts = counts.astype(jnp.int32)
    tiles_per = (counts + MOE_TILE - 1) // MOE_TILE
    tile_end = jnp.cumsum(tiles_per)
    tile_start = tile_end - tiles_per
    class_ids = jnp.arange(N_CLASSES, dtype=jnp.int32)
    start_of = jnp.sum(jnp.where(cls[:, None] == class_ids[None, :], tile_start[None, :], 0), axis=1)
    pos = start_of * MOE_TILE + rank
    n_pairs = N_GROUPS * (EXPERTS_PER_GROUP * (EXPERTS_PER_GROUP - 1)) // 2
    n_tiles = n // MOE_TILE + n_pairs
    n_used = tile_end[-1]
    tile_ids = jnp.minimum(jnp.arange(n_tiles, dtype=jnp.int32), n_used - 1)
    tile_cls = jnp.sum((tile_end[None, :] <= tile_ids[:, None]).astype(jnp.int32), axis=1)
    tile_e1 = tile_cls // EXPERTS_PER_GROUP
    tile_e2 = (tile_cls // (EXPERTS_PER_GROUP * EXPERTS_PER_GROUP)) * EXPERTS_PER_GROUP + tile_cls % EXPERTS_PER_GROUP
    return pos, tile_e1, tile_e2, n_used.reshape(1), n_tiles * MOE_TILE


def _rope_tables(t):
    inv_freq = ROPE_BASE ** (-jnp.arange(0, HEAD_DIM, 2, dtype=F32) / HEAD_DIM)
    ang = jnp.arange(t, dtype=F32)[:, None] * inv_freq[None, :]
    cos, sin = jnp.cos(ang), jnp.sin(ang)
    cos_tab = jnp.tile(cos, (1, 2 * LANES // HEAD_DIM))
    sin_tab = jnp.tile(jnp.concatenate([-sin, sin], axis=-1), (1, LANES // HEAD_DIM))
    return cos_tab, sin_tab


def _split_w_in(w):
    o = 0
    parts = {}
    for name, width in (("g_qkv", 3 * GDN_W), ("g_z", GDN_W), ("g_b", GDN_HEADS), ("g_a", GDN_HEADS),
                        ("r_q", RET_W), ("r_k", RET_W), ("r_v", RET_W), ("r_g", RET_W),
                        ("m_qk", 2 * MLSTM_W), ("m_v", MLSTM_W), ("m_o", MLSTM_W),
                        ("m_i", MLSTM_HEADS), ("m_f", MLSTM_HEADS)):
        parts[name] = w[:, o:o + width]
        o += width
    main = jnp.concatenate([parts[k] for k in ("m_qk", "m_v", "m_o", "g_qkv", "g_z", "r_q", "r_k", "r_v", "r_g")],
                           axis=1).astype(BF16)
    gate = jnp.concatenate([parts[k] for k in ("g_b", "g_a", "m_i", "m_f")], axis=1)
    gate = jnp.pad(gate, ((0, 0), (0, LANES - gate.shape[1]))).astype(BF16)
    return main, gate


def _lane_row(values, start):
    return jnp.zeros((LANES,), F32).at[start:start + values.shape[0]].set(values)


def kernel(x, norm_mix, w_in, gdn_conv, gdn_a_log, gdn_dt_bias, gdn_norm, ret_norm, mlstm_conv, mlstm_i_bias, mlstm_f_bias, mlstm_norm, w_out, norm_ffn, router_group, router_expert, router_bias, expert_gate, expert_up, expert_down, norm_final):
    b, t, d = x.shape
    n = b * t
    depth = w_in.shape[0]
    cos_tab, sin_tab = _rope_tables(t)
    x2 = x.reshape(n, d)
    for l in range(depth):
        w_main, w_gate = _split_w_in(w_in[l])
        proj, gates = _inproj(x2, norm_mix[l][None, :], w_main, w_gate)
        proj3 = proj.reshape(b, t, D_MAIN)
        gates3 = gates.reshape(b, t, LANES)
        gdn_prm = jnp.zeros((8, LANES), F32).at[0].set(_lane_row(gdn_a_log[l], G_A)).at[1].set(
            _lane_row(gdn_dt_bias[l], G_A))
        mlstm_prm = jnp.zeros((8, LANES), F32).at[0].set(_lane_row(mlstm_i_bias[l], M_I)).at[1].set(
            _lane_row(mlstm_f_bias[l], M_F))
        y_m = _mlstm_mixer(proj3, gates3, mlstm_conv[l], mlstm_prm, mlstm_norm[l][None, :])
        y_g = _gdn_mixer(proj3, gates3, gdn_conv[l], gdn_prm, gdn_norm[l][None, :])
        y_r = _ret_mixer(proj3, cos_tab, sin_tab, ret_norm[l][None, :])
        wo = w_out[l].astype(BF16)
        w_router = jnp.pad(jnp.concatenate([router_group[l], router_expert[l]], axis=1),
                           ((0, 0), (0, LANES - N_GROUPS - N_EXPERTS))).astype(BF16)
        rb = _lane_row(router_bias[l], N_GROUPS)[None, :]
        x2, h_packed, route, counts = _outproj(
            x2, y_m.reshape(n, MLSTM_W), y_g.reshape(n, GDN_W), y_r.reshape(n, RET_W),
            wo[GDN_W + RET_W:], wo[:GDN_W], wo[GDN_W:GDN_W + RET_W],
            norm_ffn[l][None, :], w_router, rb)
        pos, tile_e1, tile_e2, n_used, n_rows = _plan(route, counts[0], n)
        xs = _dispatch(h_packed, pos, n_rows)
        ys = _expert_ffn(xs, tile_e1, tile_e2, n_used,
                         expert_gate[l].astype(BF16), expert_up[l].astype(BF16), expert_down[l].astype(BF16))
        x2 = _combine(x2, ys, pos, norm_final[None, :], l == depth - 1)
    return x2.reshape(b, t, d)
```

```python
import functools
import math

import jax
import jax.numpy as jnp
from jax import lax
from jax.experimental import pallas as pl
from jax.experimental.pallas import tpu as pltpu

F32 = jnp.float32
BF16 = jnp.bfloat16

D_MODEL = 1024
HEAD_DIM = 64
CHUNK = 64
GDN_HEADS, RET_HEADS, MLSTM_HEADS = 6, 4, 6
GDN_W, RET_W, MLSTM_W = GDN_HEADS * HEAD_DIM, RET_HEADS * HEAD_DIM, MLSTM_HEADS * HEAD_DIM
CONV_K = 4
ROPE_BASE = 10000.0
N_GROUPS, EXPERTS_PER_GROUP = 4, 8
N_EXPERTS = N_GROUPS * EXPERTS_PER_GROUP
D_EXPERT = D_MODEL // 4
EPS = 1e-6
LANES = 128
SEC_M, SEC_G, SEC_R = 4 * MLSTM_W, 4 * GDN_W, 4 * RET_W
D_MAIN = SEC_M + SEC_G + SEC_R
G_BETA, G_A, M_I, M_F = 0, GDN_HEADS, 2 * GDN_HEADS, 2 * GDN_HEADS + MLSTM_HEADS

ROW_TILE = 256
OUT_TILE = 512
TIME_BLOCK = 256
MOE_TILE = 128
VMEM_LIMIT = 56 * 1024 * 1024


def _cparams(sem):
    return pltpu.CompilerParams(dimension_semantics=sem, vmem_limit_bytes=VMEM_LIMIT)


def _silu(x):
    return x * (1.0 / (1.0 + jnp.exp(-x)))


def _sigmoid(x):
    return 1.0 / (1.0 + jnp.exp(-x))


def _softplus(x):
    return jnp.maximum(x, 0.0) + jnp.log(1.0 + jnp.exp(-jnp.abs(x)))


def _dot(a, b):
    return jnp.dot(a.astype(BF16), b.astype(BF16), preferred_element_type=F32)


def _dot_nt(a, b):
    return lax.dot_general(a.astype(BF16), b.astype(BF16), (((1,), (1,)), ((), ())),
                           preferred_element_type=F32)


def _lane_masks(rows):
    lane = lax.broadcasted_iota(jnp.int32, (rows, LANES), 1)
    row = lax.broadcasted_iota(jnp.int32, (rows, LANES), 0)
    return lane, row


def _half_sum(x, m_a):
    s_a = jnp.sum(jnp.where(m_a, x, 0.0), axis=-1, keepdims=True)
    s_b = jnp.sum(jnp.where(m_a, 0.0, x), axis=-1, keepdims=True)
    return jnp.where(m_a, s_a, s_b)


def _half_max(x, m_a):
    s_a = jnp.max(jnp.where(m_a, x, -jnp.inf), axis=-1, keepdims=True)
    s_b = jnp.max(jnp.where(m_a, -jnp.inf, x), axis=-1, keepdims=True)
    return jnp.where(m_a, s_a, s_b)


def _col_form(g, lane_a, m_a):
    rows = g.shape[0]
    ca = jnp.broadcast_to(g[:, lane_a:lane_a + 1], (rows, LANES))
    cb = jnp.broadcast_to(g[:, lane_a + 1:lane_a + 2], (rows, LANES))
    return jnp.where(m_a, ca, cb)


def _row_form(col, eye):
    return jnp.sum(jnp.where(eye, col, 0.0), axis=0, keepdims=True)


def _block_diag(y, m_a):
    return jnp.concatenate([jnp.where(m_a, y, 0.0), jnp.where(m_a, 0.0, y)], axis=0)


def _pmul(x, y, m_a):
    return _dot(x, _block_diag(y, m_a))


def _outer_state(k, v, bd_mask):
    zero = jnp.zeros_like(k)
    kt = jnp.concatenate([k, zero], axis=0).T
    vp = jnp.concatenate([v, zero], axis=0)
    return jnp.where(bd_mask, _dot(kt, vp), 0.0)


def _chunk_cumsum(x):
    rows = x.shape[0]
    r = lax.broadcasted_iota(jnp.int32, (rows, LANES), 0) % CHUNK
    s = 1
    while s < CHUNK:
        x = x + jnp.where(r >= s, pltpu.roll(x, s, axis=0), 0.0)
        s *= 2
    return x


def _causal_conv(raw, cbuf, w_ref):
    rows = raw.shape[0]
    cbuf[8:8 + rows, :] = raw
    acc = raw * w_ref[CONV_K - 1:CONV_K, :]
    for j in range(CONV_K - 1):
        off = 8 - (CONV_K - 1) + j
        acc = acc + cbuf[off:off + rows, :] * w_ref[j:j + 1, :]
    cbuf[0:8, :] = raw[rows - 8:rows, :]
    return acc


def _inproj_kernel(x_ref, gain_ref, w_ref, wg_ref, o_ref, og_ref):
    x = x_ref[...]
    ms = jnp.mean(x * x, axis=-1, keepdims=True)
    h = (x * lax.rsqrt(ms + EPS) * gain_ref[...]).astype(BF16)

    def proj(lo, hi):
        return jnp.dot(h, w_ref[:, lo:hi], preferred_element_type=F32)

    step = 512
    for c in range(D_MAIN // step):
        o_ref[:, c * step:(c + 1) * step] = proj(c * step, (c + 1) * step)
    og_ref[...] = jnp.dot(h, wg_ref[...], preferred_element_type=F32)


def _inproj(x2, gain, w_main, w_gate):
    n = x2.shape[0]
    full = lambda a, b: pl.BlockSpec((a, b), lambda i: (0, 0))
    return pl.pallas_call(
        _inproj_kernel,
        grid=(n // ROW_TILE,),
        in_specs=[
            pl.BlockSpec((ROW_TILE, D_MODEL), lambda i: (i, 0)),
            full(1, D_MODEL), full(D_MODEL, D_MAIN), full(D_MODEL, LANES),
        ],
        out_specs=[
            pl.BlockSpec((ROW_TILE, D_MAIN), lambda i: (i, 0)),
            pl.BlockSpec((ROW_TILE, LANES), lambda i: (i, 0)),
        ],
        out_shape=[jax.ShapeDtypeStruct((n, D_MAIN), F32), jax.ShapeDtypeStruct((n, LANES), F32)],
        compiler_params=_cparams(("parallel",)),
        name="norm_inproj",
    )(x2, gain, w_main, w_gate)


N_CLASSES = N_GROUPS * EXPERTS_PER_GROUP * EXPERTS_PER_GROUP
HP_WIDTH = D_MODEL + LANES


def _route_rows(lg):
    rows = lg.shape[0]
    lane = lax.broadcasted_iota(jnp.int32, (rows, LANES), 1)
    neg = -jnp.inf
    gl = jnp.where(lane < N_GROUPS, lg, neg)
    gm = jnp.max(gl, axis=-1, keepdims=True)
    gsum = jnp.sum(jnp.where(lane < N_GROUPS, jnp.exp(gl - gm), 0.0), axis=-1, keepdims=True)
    gidx = jnp.min(jnp.where(gl == gm, lane, LANES), axis=-1, keepdims=True)
    group_p = 1.0 / gsum
    in_group = (lane >= N_GROUPS) & (lane < N_GROUPS + N_EXPERTS) & (
        lax.shift_right_arithmetic(lane - N_GROUPS, 3) == gidx)
    el = jnp.where(in_group, lg, neg)
    m1 = jnp.max(el, axis=-1, keepdims=True)
    i1 = jnp.min(jnp.where(el == m1, lane, LANES), axis=-1, keepdims=True)
    el2 = jnp.where(lane == i1, neg, el)
    m2 = jnp.max(el2, axis=-1, keepdims=True)
    i2 = jnp.min(jnp.where(el2 == m2, lane, LANES), axis=-1, keepdims=True)
    e2 = jnp.exp(m2 - m1)
    w1 = group_p * (1.0 / (1.0 + e2))
    w2 = group_p * (e2 / (1.0 + e2))
    first_lower = i1 < i2
    base = N_GROUPS + gidx * EXPERTS_PER_GROUP
    lo = jnp.minimum(i1, i2) - base
    hi = jnp.maximum(i1, i2) - base
    cls = (gidx * EXPERTS_PER_GROUP + lo) * EXPERTS_PER_GROUP + hi
    return cls, jnp.where(first_lower, w1, w2), jnp.where(first_lower, w2, w1)


def _outproj_kernel(x_ref, ym_ref, yg_ref, yr_ref, wm_ref, wg_ref, wr_ref, gain_ref, wrt_ref, rb_ref,
                    xo_ref, hp_ref, rt_ref, cnt_ref, running):
    @pl.when(pl.program_id(0) == 0)
    def _():
        running[...] = jnp.zeros_like(running)

    acc = x_ref[...]
    acc = acc + jnp.dot(yg_ref[...], wg_ref[...], preferred_element_type=F32)
    acc = acc + jnp.dot(yr_ref[...], wr_ref[...], preferred_element_type=F32)
    acc = acc + jnp.dot(ym_ref[...], wm_ref[...], preferred_element_type=F32)
    xo_ref[...] = acc
    ms = jnp.mean(acc * acc, axis=-1, keepdims=True)
    hn = acc * lax.rsqrt(ms + EPS) * gain_ref[...]
    logits = jnp.dot(hn.astype(BF16), wrt_ref[...], preferred_element_type=F32) + rb_ref[...]
    cls, w_lo, w_hi = _route_rows(logits)

    rows = acc.shape[0]
    cls_lane = lax.broadcasted_iota(jnp.int32, (rows, N_CLASSES), 1)
    onehot = cls_lane == cls
    before = lax.broadcasted_iota(jnp.int32, (rows, rows), 1) < lax.broadcasted_iota(jnp.int32, (rows, rows), 0)
    prefix = jnp.dot(before.astype(BF16), onehot.astype(BF16), preferred_element_type=F32)
    rank = jnp.sum(jnp.where(onehot, prefix + running[0:1, :], 0.0), axis=-1, keepdims=True)
    running[0:1, :] = running[0:1, :] + jnp.sum(onehot.astype(F32), axis=0, keepdims=True)
    cnt_ref[...] = jnp.broadcast_to(running[0:1, :], cnt_ref.shape)

    lane = lax.broadcasted_iota(jnp.int32, (rows, LANES), 1)
    rt_ref[...] = jnp.where(lane == 0, cls, jnp.where(lane == 1, rank.astype(jnp.int32), 0))
    hp_ref[:, 0:D_MODEL] = hn
    hp_ref[:, D_MODEL:] = jnp.where(lane == 0, w_lo, jnp.where(lane == 1, w_hi, 0.0))


def _outproj(x2, ym, yg, yr, wo_m, wo_g, wo_r, gain, w_router, router_bias):
    n = x2.shape[0]
    row = lambda w: pl.BlockSpec((OUT_TILE, w), lambda i: (i, 0))
    full = lambda a, b: pl.BlockSpec((a, b), lambda i: (0, 0))
    return pl.pallas_call(
        _outproj_kernel,
        grid=(n // OUT_TILE,),
        in_specs=[row(D_MODEL), row(MLSTM_W), row(GDN_W), row(RET_W),
                  full(MLSTM_W, D_MODEL), full(GDN_W, D_MODEL), full(RET_W, D_MODEL),
                  full(1, D_MODEL), full(D_MODEL, LANES), full(1, LANES)],
        out_specs=[row(D_MODEL), row(HP_WIDTH), row(LANES), full(8, N_CLASSES)],
        out_shape=[jax.ShapeDtypeStruct((n, D_MODEL), F32),
                   jax.ShapeDtypeStruct((n, HP_WIDTH), F32),
                   jax.ShapeDtypeStruct((n, LANES), jnp.int32),
                   jax.ShapeDtypeStruct((8, N_CLASSES), F32)],
        scratch_shapes=[pltpu.VMEM((8, N_CLASSES), F32)],
        compiler_params=_cparams(("arbitrary",)),
        name="outproj_norm_router",
    )(x2, ym, yg, yr, wo_m, wo_g, wo_r, gain, w_router, router_bias)


def _gdn_kernel(x_ref, gt_ref, z_ref, conv_ref, prm_ref, gain_ref, y_ref,
                state, u_s, wq_s, qk_s, kt_s, sd_s, cbuf):
    tb = x_ref.shape[0]
    first = pl.program_id(1) == 0

    @pl.when(first)
    def _():
        for ref in (state, u_s, wq_s, qk_s, kt_s, sd_s):
            ref[...] = jnp.zeros_like(ref)
        cbuf[0:8, :] = jnp.zeros((8, cbuf.shape[1]), F32)

    gt = gt_ref[...]
    beta_all = _sigmoid(gt)
    g_all = -jnp.exp(prm_ref[0:1, :]) * _softplus(gt + prm_ref[1:2, :])
    gc_all = _chunk_cumsum(g_all)

    lane, row = _lane_masks(CHUNK)
    m_a = lane < HEAD_DIM
    lane_h = lane % HEAD_DIM
    eye = lane_h == row
    tril = lane_h <= row
    strict = lane_h < row
    eye_f = eye.astype(F32)
    lane2 = lax.broadcasted_iota(jnp.int32, (LANES, LANES), 1)
    row2 = lax.broadcasted_iota(jnp.int32, (LANES, LANES), 0)
    bd_mask = (lane2 < HEAD_DIM) == (row2 < HEAD_DIM)
    lane_t = lax.broadcasted_iota(jnp.int32, (tb, LANES), 1)
    m_a_t = lane_t < HEAD_DIM

    n_pairs = GDN_HEADS // 2
    n_chunks = tb // CHUNK
    units = [(p, c) for p in range(n_pairs) for c in range(n_chunks)]

    def unit_index(p, c):
        return p * n_chunks + c

    prev = {}
    for p, c in units:
        i = unit_index(p, c)
        prev[p, c] = (u_s[i], wq_s[i], qk_s[i], kt_s[i], sd_s[i, 0:1, :])
    s_bd = [state[p] for p in range(n_pairs)]

    def recurrence(c):
        rs = slice(c * CHUNK, (c + 1) * CHUNK)
        for p in range(n_pairs):
            cs = slice(p * LANES, (p + 1) * LANES)
            u_mat, wq, qk, k_dec_t, s_decay = prev[p, c]
            ws_qs = jnp.dot(wq, s_bd[p].astype(BF16), preferred_element_type=F32)
            v_new = u_mat - ws_qs[:CHUNK]
            o = ws_qs[CHUNK:] + jnp.dot(qk, _block_diag(v_new, m_a).astype(BF16), preferred_element_type=F32)
            v_pad = jnp.concatenate([v_new, jnp.zeros_like(v_new)], axis=0).astype(BF16)
            s_bd[p] = s_bd[p] * s_decay + jnp.where(
                bd_mask, jnp.dot(k_dec_t, v_pad, preferred_element_type=F32), 0.0)
            ms = _half_sum(o * o, m_a) * (1.0 / HEAD_DIM)
            y = o * lax.rsqrt(ms + EPS) * gain_ref[:, cs] * _silu(z_ref[rs, cs])
            y_ref[rs, cs] = y.astype(y_ref.dtype)

    recurrence(0)
    qkv = _silu(_causal_conv(x_ref[:, 0:3 * GDN_W], cbuf, conv_ref))
    pre = {}
    for p in range(n_pairs):
        q_t = qkv[:, p * LANES:(p + 1) * LANES]
        k_t = qkv[:, GDN_W + p * LANES:GDN_W + (p + 1) * LANES]
        v_t = qkv[:, 2 * GDN_W + p * LANES:2 * GDN_W + (p + 1) * LANES]
        q_t = q_t * lax.rsqrt(_half_sum(q_t * q_t, m_a_t) + EPS) * (HEAD_DIM ** -0.5)
        k_t = k_t * lax.rsqrt(_half_sum(k_t * k_t, m_a_t) + EPS)
        beta_t = _col_form(beta_all, G_BETA + 2 * p, m_a_t)
        gc_t = _col_form(gc_all, G_A + 2 * p, m_a_t)
        for c in range(n_chunks):
            rs = slice(c * CHUNK, (c + 1) * CHUNK)
            q, k, v, beta, gc = q_t[rs], k_t[rs], v_t[rs], beta_t[rs], gc_t[rs]
            decay = jnp.where(tril, jnp.exp(gc - _row_form(gc, eye)), 0.0)
            kb = k * beta
            kkqk = _dot_nt(jnp.concatenate([kb, q], axis=0), _block_diag(k, m_a))
            egc = jnp.exp(gc)
            g_last = gc[CHUNK - 1:CHUNK, :]
            k_dec = k * jnp.exp(g_last - gc)
            pre[p, c] = dict(
                a=jnp.where(strict, kkqk[:CHUNK] * decay, 0.0), qk=kkqk[CHUNK:] * decay,
                vb=v * beta, kbe=kb * egc, qe=q * egc, s_decay=jnp.exp(g_last),
                k_dec_t=jnp.concatenate([k_dec, jnp.zeros_like(k_dec)], axis=0).T)

    pw = {u: _pmul(pre[u]["a"], pre[u]["a"], m_a) for u in units}
    t_inv = {u: eye_f - pre[u]["a"] for u in units}
    later_chunks = list(range(1, n_chunks))
    for step in range(4):
        if step % 2 == 0 and later_chunks:
            recurrence(later_chunks.pop(0))
        both = {u: _dot(jnp.concatenate([pw[u], t_inv[u]], axis=0), _block_diag(pw[u], m_a)) for u in units}
        pw = {u: both[u][:CHUNK] for u in units}
        t_inv = {u: t_inv[u] + both[u][CHUNK:] for u in units}
    for c in later_chunks:
        recurrence(c)
    t_inv = {u: t_inv[u] + _pmul(t_inv[u], pw[u], m_a) for u in units}
    for p, c in units:
        d = pre[p, c]
        uw = _dot(t_inv[p, c], jnp.concatenate([_block_diag(d["vb"], m_a), _block_diag(d["kbe"], m_a)], axis=1))
        i = unit_index(p, c)
        u_s[i] = uw[:, :LANES]
        wq_s[i] = jnp.concatenate([uw[:, LANES:], d["qe"]], axis=0).astype(BF16)
        qk_s[i] = d["qk"].astype(BF16)
        kt_s[i] = d["k_dec_t"].astype(BF16)
        sd_s[i] = jnp.broadcast_to(d["s_decay"], (8, LANES))
    for p in range(n_pairs):
        state[p] = s_bd[p]


def _ret_kernel(x_ref, cos_ref, sin_ref, gain_ref, y_ref, state):
    tb = x_ref.shape[0]

    @pl.when(pl.program_id(1) == 0)
    def _():
        state[...] = jnp.zeros_like(state)

    lane, row = _lane_masks(CHUNK)
    m_a = lane < HEAD_DIM
    lane_h = lane % HEAD_DIM
    tril = lane_h <= row
    lane2 = lax.broadcasted_iota(jnp.int32, (LANES, LANES), 1)
    row2 = lax.broadcasted_iota(jnp.int32, (LANES, LANES), 0)
    bd_mask = (lane2 < HEAD_DIM) == (row2 < HEAD_DIM)
    lane_t = lax.broadcasted_iota(jnp.int32, (tb, LANES), 1)
    first_half = (lane_t % HEAD_DIM) < (HEAD_DIM // 2)
    cos = cos_ref[...]
    sin = sin_ref[...]
    rowf = row.astype(F32)
    pos_diff = (row - lane_h).astype(F32)

    def rope(x):
        swapped = jnp.where(first_half, pltpu.roll(x, LANES - HEAD_DIM // 2, axis=1),
                            pltpu.roll(x, HEAD_DIM // 2, axis=1))
        return x * cos + swapped * sin

    n_pairs = RET_HEADS // 2
    n_chunks = tb // CHUNK
    pre = {}
    chunk_decay = []
    for p in range(n_pairs):
        lg_a = math.log(1.0 - 2.0 ** (-5.0 - 2 * p))
        lg_b = math.log(1.0 - 2.0 ** (-5.0 - (2 * p + 1)))
        lg = jnp.where(m_a, lg_a, lg_b)
        decay = jnp.where(tril, jnp.exp(pos_diff * lg), 0.0)
        q_scale = jnp.exp(lg * (rowf + 1.0))
        k_scale = jnp.exp(lg * (CHUNK - 1.0 - rowf))
        chunk_decay.append(jnp.exp(lg[0:1, :] * CHUNK))
        q_t = rope(x_ref[:, p * LANES:(p + 1) * LANES])
        k_t = rope(x_ref[:, RET_W + p * LANES:RET_W + (p + 1) * LANES]) * (HEAD_DIM ** -0.5)
        for c in range(n_chunks):
            rs = slice(c * CHUNK, (c + 1) * CHUNK)
            q, k = q_t[rs], k_t[rs]
            v = x_ref[rs, 2 * RET_W + p * LANES:2 * RET_W + (p + 1) * LANES]
            scores = _dot_nt(q, _block_diag(k, m_a)) * decay
            pre[p, c] = (_pmul(scores, v, m_a), q * q_scale, _outer_state(k * k_scale, v, bd_mask))

    s_bd = [state[p] for p in range(n_pairs)]
    for c in range(n_chunks):
        rs = slice(c * CHUNK, (c + 1) * CHUNK)
        for p in range(n_pairs):
            cs = slice(p * LANES, (p + 1) * LANES)
            o_intra, q_dec, kv = pre[p, c]
            o = o_intra + _dot(q_dec, s_bd[p])
            s_bd[p] = chunk_decay[p] * s_bd[p] + kv
            ms = _half_sum(o * o, m_a) * (1.0 / HEAD_DIM)
            gate = x_ref[rs, 3 * RET_W + p * LANES:3 * RET_W + (p + 1) * LANES]
            y = o * lax.rsqrt(ms + EPS) * gain_ref[:, cs] * _silu(gate)
            y_ref[rs, cs] = y.astype(y_ref.dtype)
    for p in range(n_pairs):
        state[p] = s_bd[p]


def _mlstm_kernel(x_ref, gt_ref, conv_ref, prm_ref, gain_ref, y_ref, c_state, nm_state, cbuf):
    tb = x_ref.shape[0]
    first = pl.program_id(1) == 0

    @pl.when(first)
    def _():
        c_state[...] = jnp.zeros_like(c_state)
        nm_state[...] = jnp.zeros_like(nm_state)
        cbuf[0:8, :] = jnp.zeros((8, cbuf.shape[1]), F32)

    qk_all = _silu(_causal_conv(x_ref[:, 0:2 * MLSTM_W], cbuf, conv_ref))
    gt = gt_ref[...]
    log_i_all = gt + prm_ref[0:1, :]
    f_pre = gt + prm_ref[1:2, :]
    log_f_all = jnp.minimum(f_pre, 0.0) - jnp.log(1.0 + jnp.exp(-jnp.abs(f_pre)))
    bc_all = _chunk_cumsum(log_f_all)

    lane, row = _lane_masks(CHUNK)
    m_a = lane < HEAD_DIM
    lane_h = lane % HEAD_DIM
    eye = lane_h == row
    tril = lane_h <= row
    lane2 = lax.broadcasted_iota(jnp.int32, (LANES, LANES), 1)
    row2 = lax.broadcasted_iota(jnp.int32, (LANES, LANES), 0)
    bd_mask = (lane2 < HEAD_DIM) == (row2 < HEAD_DIM)
    lane_t = lax.broadcasted_iota(jnp.int32, (tb, LANES), 1)
    m_a_t = lane_t < HEAD_DIM

    n_pairs = MLSTM_HEADS // 2
    n_chunks = tb // CHUNK
    units = [(p, c) for p in range(n_pairs) for c in range(n_chunks)]
    gates = {}
    for p in range(n_pairs):
        b_t = _col_form(bc_all, M_F + 2 * p, m_a_t)
        i_t = _col_form(log_i_all, M_I + 2 * p, m_a_t)
        for c in range(n_chunks):
            rs = slice(c * CHUNK, (c + 1) * CHUNK)
            gates[p, c] = (b_t[rs], i_t[rs])
    qkv = {(p, c): (qk_all[c * CHUNK:(c + 1) * CHUNK, p * LANES:(p + 1) * LANES] * (HEAD_DIM ** -0.5),
                    qk_all[c * CHUNK:(c + 1) * CHUNK, MLSTM_W + p * LANES:MLSTM_W + (p + 1) * LANES],
                    x_ref[c * CHUNK:(c + 1) * CHUNK, 2 * MLSTM_W + p * LANES:2 * MLSTM_W + (p + 1) * LANES])
           for p, c in units}
    qk = {u: _dot_nt(qkv[u][0], _block_diag(qkv[u][1], m_a)) for u in units}
    log_d = {u: jnp.where(tril, gates[u][0] - _row_form(gates[u][0], eye) + _row_form(gates[u][1], eye), -jnp.inf)
             for u in units}
    m_intra = {u: _half_max(log_d[u], m_a) for u in units}
    a_c = {u: gates[u][0][CHUNK - 1:CHUNK, :] - gates[u][0] + gates[u][1] for u in units}
    a_max = {u: jnp.max(a_c[u], axis=0, keepdims=True) for u in units}
    m_in, m_out = {}, {}
    for p in range(n_pairs):
        m_row = nm_state[p, 1:2, :]
        for c in range(n_chunks):
            m_in[p, c] = m_row
            m_row = jnp.maximum(gates[p, c][0][CHUNK - 1:CHUNK, :] + m_row, a_max[p, c])
            m_out[p, c] = m_row
    m_t = {u: jnp.maximum(gates[u][0] + m_in[u], m_intra[u]) for u in units}
    inter = {u: jnp.exp(gates[u][0] + m_in[u] - m_t[u]) for u in units}
    wmat = {u: jnp.where(tril, jnp.exp(log_d[u] - m_t[u]), 0.0) * qk[u] for u in units}
    kw = {u: qkv[u][1] * jnp.exp(a_c[u] - m_out[u]) for u in units}
    dec = {u: jnp.exp(gates[u][0][CHUNK - 1:CHUNK, :] + m_in[u] - m_out[u]) for u in units}
    num_intra = {u: _pmul(wmat[u], qkv[u][2], m_a) for u in units}
    den_intra = {u: _half_sum(wmat[u], m_a) for u in units}
    kv = {u: _outer_state(kw[u], qkv[u][2], bd_mask) for u in units}
    kn = {u: jnp.sum(kw[u], axis=0, keepdims=True) for u in units}
    c_in, n_in = {}, {}
    for p in range(n_pairs):
        c_bd, n_row = c_state[p], nm_state[p, 0:1, :]
        for c in range(n_chunks):
            c_in[p, c], n_in[p, c] = c_bd, n_row
            c_bd = dec[p, c] * c_bd + kv[p, c]
            n_row = dec[p, c] * n_row + kn[p, c]
        c_state[p] = c_bd
        nm_state[p, 0:1, :] = n_row
        nm_state[p, 1:2, :] = m_out[p, n_chunks - 1]
    num = {u: inter[u] * _dot(qkv[u][0], c_in[u]) + num_intra[u] for u in units}
    den = {u: inter[u] * _half_sum(qkv[u][0] * n_in[u], m_a) + den_intra[u] for u in units}
    h = {u: num[u] / jnp.maximum(jnp.abs(den[u]), jnp.exp(-m_t[u])) for u in units}
    ms = {u: _half_sum(h[u] * h[u], m_a) * (1.0 / HEAD_DIM) for u in units}
    for p, c in units:
        rs = slice(c * CHUNK, (c + 1) * CHUNK)
        cs = slice(p * LANES, (p + 1) * LANES)
        o_logit = x_ref[rs, 3 * MLSTM_W + p * LANES:3 * MLSTM_W + (p + 1) * LANES]
        y = _sigmoid(o_logit) * (h[p, c] * lax.rsqrt(ms[p, c] + EPS) * gain_ref[:, cs])
        y_ref[rs, cs] = y.astype(y_ref.dtype)


def _mixer_call(kernel, name, proj3, sec_block, sec_width, out_width, extra_inputs, extra_specs, scratch):
    b, t, _ = proj3.shape
    tb = min(TIME_BLOCK, t)
    return pl.pallas_call(
        kernel,
        grid=(b, t // tb),
        in_specs=[pl.BlockSpec((None, tb, sec_width), lambda i, j: (i, j, sec_block))] + extra_specs(tb),
        out_specs=pl.BlockSpec((None, tb, out_width), lambda i, j: (i, j, 0)),
        out_shape=jax.ShapeDtypeStruct((b, t, out_width), BF16),
        scratch_shapes=scratch(tb),
        compiler_params=_cparams(("parallel", "arbitrary")),
        name=name,
    )(proj3, *extra_inputs)


def _full2(a, b):
    return pl.BlockSpec((a, b), lambda i, j: (0, 0))


def _gdn_mixer(proj3, gates3, conv_w, prm, gain):
    b, t, _ = proj3.shape
    tb = min(TIME_BLOCK, t)
    n_blocks = t // tb
    n_units = (GDN_HEADS // 2) * (tb // CHUNK)
    cur = lambda i, j: (i, jnp.minimum(j, n_blocks - 1))
    prv = lambda i, j: (i, jnp.maximum(j - 1, 0))
    z_block = (SEC_M + 3 * GDN_W) // GDN_W
    assert z_block * GDN_W == SEC_M + 3 * GDN_W
    return pl.pallas_call(
        _gdn_kernel,
        grid=(b, n_blocks + 1),
        in_specs=[pl.BlockSpec((None, tb, SEC_G), lambda i, j: cur(i, j) + (1,)),
                  pl.BlockSpec((None, tb, LANES), lambda i, j: cur(i, j) + (0,)),
                  pl.BlockSpec((None, tb, GDN_W), lambda i, j: prv(i, j) + (z_block,)),
                  _full2(CONV_K, 3 * GDN_W), _full2(8, LANES), _full2(1, GDN_W)],
        out_specs=pl.BlockSpec((None, tb, GDN_W), lambda i, j: prv(i, j) + (0,)),
        out_shape=jax.ShapeDtypeStruct((b, t, GDN_W), BF16),
        scratch_shapes=[pltpu.VMEM((GDN_HEADS // 2, LANES, LANES), F32),
                        pltpu.VMEM((n_units, CHUNK, LANES), F32),
                        pltpu.VMEM((n_units, 2 * CHUNK, LANES), BF16),
                        pltpu.VMEM((n_units, CHUNK, LANES), BF16),
                        pltpu.VMEM((n_units, LANES, LANES), BF16),
                        pltpu.VMEM((n_units, 8, LANES), F32),
                        pltpu.VMEM((tb + 8, 3 * GDN_W), F32)],
        compiler_params=_cparams(("parallel", "arbitrary")),
        name="gdn_mixer",
    )(proj3, gates3, proj3, conv_w, prm, gain)


def _mlstm_mixer(proj3, gates3, conv_w, prm, gain):
    return _mixer_call(
        _mlstm_kernel, "mlstm_mixer", proj3, 0, SEC_M, MLSTM_W, (gates3, conv_w, prm, gain),
        lambda tb: [pl.BlockSpec((None, tb, LANES), lambda i, j: (i, j, 0)),
                    _full2(CONV_K, 2 * MLSTM_W), _full2(8, LANES), _full2(1, MLSTM_W)],
        lambda tb: [pltpu.VMEM((MLSTM_HEADS // 2, LANES, LANES), F32),
                    pltpu.VMEM((MLSTM_HEADS // 2, 8, LANES), F32),
                    pltpu.VMEM((tb + 8, 2 * MLSTM_W), F32)])


def _ret_mixer(proj3, cos_tab, sin_tab, gain):
    return _mixer_call(
        _ret_kernel, "retention_mixer", proj3, 3, SEC_R, RET_W, (cos_tab, sin_tab, gain),
        lambda tb: [pl.BlockSpec((tb, LANES), lambda i, j: (j, 0)),
                    pl.BlockSpec((tb, LANES), lambda i, j: (j, 0)),
                    _full2(1, RET_W)],
        lambda tb: [pltpu.VMEM((RET_HEADS // 2, LANES, LANES), F32)])


DISPATCH_TILE = 512


def _dispatch_kernel(pos_ref, src_ref, zero_ref, out_ref, sem):
    del zero_ref
    rows = src_ref.shape[0]
    base = pl.program_id(0) * rows

    def issue(r, carry):
        pltpu.make_async_copy(src_ref.at[pl.ds(r, 1)], out_ref.at[pl.ds(pos_ref[base + r], 1)], sem).start()
        return carry

    lax.fori_loop(0, rows, issue, 0, unroll=8)
    pltpu.make_async_copy(src_ref, out_ref.at[pl.ds(0, rows)], sem).wait()


def _dispatch(src, pos, n_rows):
    n, width = src.shape
    tile = min(DISPATCH_TILE, n)
    zeros = jnp.zeros((n_rows, width), src.dtype)
    return pl.pallas_call(
        _dispatch_kernel,
        grid_spec=pltpu.PrefetchScalarGridSpec(
            num_scalar_prefetch=1,
            grid=(n // tile,),
            in_specs=[pl.BlockSpec((tile, width), lambda i, p: (i, 0)), pl.BlockSpec(memory_space=pl.ANY)],
            out_specs=pl.BlockSpec(memory_space=pl.ANY),
            scratch_shapes=[pltpu.SemaphoreType.DMA(())],
        ),
        out_shape=jax.ShapeDtypeStruct((n_rows, width), src.dtype),
        input_output_aliases={2: 0},
        compiler_params=_cparams(("arbitrary",)),
        name="moe_dispatch",
    )(pos, src, zeros)


def _ffn_kernel(e1_ref, e2_ref, used_ref, x_ref, wg1, wu1, wd1, wg2, wu2, wd2, y_ref):
    used = pl.program_id(0) < used_ref[0]

    @pl.when(jnp.logical_not(used))
    def _():
        y_ref[...] = jnp.zeros_like(y_ref)

    @pl.when(used)
    def _():
        x = x_ref[:, 0:D_MODEL].astype(BF16)
        gate1 = jnp.dot(x, wg1[...], preferred_element_type=F32)
        gate2 = jnp.dot(x, wg2[...], preferred_element_type=F32)
        up1 = jnp.dot(x, wu1[...], preferred_element_type=F32)
        up2 = jnp.dot(x, wu2[...], preferred_element_type=F32)
        hid1 = (_silu(gate1) * up1).astype(BF16)
        hid2 = (_silu(gate2) * up2).astype(BF16)
        y1 = jnp.dot(hid1, wd1[...], preferred_element_type=F32)
        y2 = jnp.dot(hid2, wd2[...], preferred_element_type=F32)
        rw = x_ref[:, D_MODEL:]
        y_ref[...] = rw[:, 0:1] * y1 + rw[:, 1:2] * y2


def _expert_ffn(xs, tile_e1, tile_e2, n_used, wg, wu, wd, layer):
    r = xs.shape[0]
    first = lambda i, e1, e2, nu: (layer, e1[i], 0, 0)
    second = lambda i, e1, e2, nu: (layer, e2[i], 0, 0)
    up = lambda f: pl.BlockSpec((None, None, D_MODEL, D_EXPERT), f)
    down = lambda f: pl.BlockSpec((None, None, D_EXPERT, D_MODEL), f)
    return pl.pallas_call(
        _ffn_kernel,
        grid_spec=pltpu.PrefetchScalarGridSpec(
            num_scalar_prefetch=3,
            grid=(r // MOE_TILE,),
            in_specs=[pl.BlockSpec((MOE_TILE, HP_WIDTH), lambda i, e1, e2, nu: (jnp.minimum(i, nu[0] - 1), 0)),
                      up(first), up(first), down(first), up(second), up(second), down(second)],
            out_specs=pl.BlockSpec((MOE_TILE, D_MODEL), lambda i, e1, e2, nu: (i, 0)),
        ),
        out_shape=jax.ShapeDtypeStruct((r, D_MODEL), F32),
        compiler_params=_cparams(("arbitrary",)),
        name="expert_pair_ffn",
    )(tile_e1, tile_e2, n_used, xs, wg, wu, wd, wg, wu, wd)


def _combine_kernel(final_norm, pos_ref, x_ref, gain_ref, ys_ref, o_ref, buf, sems):
    rows = o_ref.shape[0]
    i = pl.program_id(0)
    slot = i & 1

    def gather_tile(tile, to_slot):
        def issue(r, carry):
            pltpu.make_async_copy(ys_ref.at[pl.ds(pos_ref[tile * rows + r], 1)],
                                  buf.at[to_slot, pl.ds(r, 1)], sems.at[to_slot]).start()
            return carry

        lax.fori_loop(0, rows, issue, 0, unroll=8)

    @pl.when(i == 0)
    def _():
        gather_tile(0, 0)

    @pl.when(i + 1 < pl.num_programs(0))
    def _():
        gather_tile(i + 1, 1 - slot)

    pltpu.make_async_copy(ys_ref.at[pl.ds(0, rows)], buf.at[slot], sems.at[slot]).wait()
    out = x_ref[...] + buf[slot]
    if final_norm:
        ms = jnp.mean(out * out, axis=-1, keepdims=True)
        out = out * lax.rsqrt(ms + EPS) * gain_ref[...]
    o_ref[...] = out


def _combine(x2, ys, pos, gain, final_norm):
    n = x2.shape[0]
    return pl.pallas_call(
        functools.partial(_combine_kernel, final_norm),
        grid_spec=pltpu.PrefetchScalarGridSpec(
            num_scalar_prefetch=1,
            grid=(n // ROW_TILE,),
            in_specs=[pl.BlockSpec((ROW_TILE, D_MODEL), lambda i, p: (i, 0)),
                      pl.BlockSpec((1, D_MODEL), lambda i, p: (0, 0)),
                      pl.BlockSpec(memory_space=pl.ANY)],
            out_specs=pl.BlockSpec((ROW_TILE, D_MODEL), lambda i, p: (i, 0)),
            scratch_shapes=[pltpu.VMEM((2, ROW_TILE, D_MODEL), F32), pltpu.SemaphoreType.DMA((2,))],
        ),
        out_shape=jax.ShapeDtypeStruct((n, D_MODEL), F32),
        compiler_params=_cparams(("arbitrary",)),
        name="moe_combine",
    )(pos, x2, gain, ys)


def _plan(route, counts, n):
    cls, rank = route[:, 0], route[:, 1]
    counts = counts.astype(jnp.int32)
    tiles_per = (counts + MOE_TILE - 1) // MOE_TILE
    tile_end = jnp.cumsum(tiles_per)
    tile_start = tile_end - tiles_per
    class_ids = jnp.arange(N_CLASSES, dtype=jnp.int32)
    start_of = jnp.sum(jnp.where(cls[:, None] == class_ids[None, :], tile_start[None, :], 0), axis=1)
    pos = start_of * MOE_TILE + rank
    n_pairs = N_GROUPS * (EXPERTS_PER_GROUP * (EXPERTS_PER_GROUP - 1)) // 2
    n_tiles = n // MOE_TILE + n_pairs
    n_used = tile_end[-1]
    tile_ids = jnp.minimum(jnp.arange(n_tiles, dtype=jnp.int32), n_used - 1)
    tile_cls = jnp.sum((tile_end[None, :] <= tile_ids[:, None]).astype(jnp.int32), axis=1)
    tile_e1 = tile_cls // EXPERTS_PER_GROUP
    tile_e2 = (tile_cls // (EXPERTS_PER_GROUP * EXPERTS_PER_GROUP)) * EXPERTS_PER_GROUP + tile_cls % EXPERTS_PER_GROUP
    return pos, tile_e1, tile_e2, n_used.reshape(1), n_tiles * MOE_TILE


def _rope_tables(t):
    inv_freq = ROPE_BASE ** (-jnp.arange(0, HEAD_DIM, 2, dtype=F32) / HEAD_DIM)
    ang = jnp.arange(t, dtype=F32)[:, None] * inv_freq[None, :]
    cos, sin = jnp.cos(ang), jnp.sin(ang)
    cos_tab = jnp.tile(cos, (1, 2 * LANES // HEAD_DIM))
    sin_tab = jnp.tile(jnp.concatenate([-sin, sin], axis=-1), (1, LANES // HEAD_DIM))
    return cos_tab, sin_tab


def _split_w_in(w):
    o = 0
    parts = {}
    for name, width in (("g_qkv", 3 * GDN_W), ("g_z", GDN_W), ("g_b", GDN_HEADS), ("g_a", GDN_HEADS),
                        ("r_q", RET_W), ("r_k", RET_W), ("r_v", RET_W), ("r_g", RET_W),
                        ("m_qk", 2 * MLSTM_W), ("m_v", MLSTM_W), ("m_o", MLSTM_W),
                        ("m_i", MLSTM_HEADS), ("m_f", MLSTM_HEADS)):
        parts[name] = w[:, o:o + width]
        o += width
    main = jnp.concatenate([parts[k] for k in ("m_qk", "m_v", "m_o", "g_qkv", "g_z", "r_q", "r_k", "r_v", "r_g")],
                           axis=1).astype(BF16)
    gate = jnp.concatenate([parts[k] for k in ("g_b", "g_a", "m_i", "m_f")], axis=1)
    gate = jnp.pad(gate, ((0, 0), (0, LANES - gate.shape[1]))).astype(BF16)
    return main, gate


def _lane_row(values, start):
    return jnp.zeros((LANES,), F32).at[start:start + values.shape[0]].set(values)


def kernel(x, norm_mix, w_in, gdn_conv, gdn_a_log, gdn_dt_bias, gdn_norm, ret_norm, mlstm_conv, mlstm_i_bias, mlstm_f_bias, mlstm_norm, w_out, norm_ffn, router_group, router_expert, router_bias, expert_gate, expert_up, expert_down, norm_final):
    b, t, d = x.shape
    n = b * t
    depth = w_in.shape[0]
    cos_tab, sin_tab = _rope_tables(t)
    wg_all, wu_all, wd_all = expert_gate.astype(BF16), expert_up.astype(BF16), expert_down.astype(BF16)
    x2 = x.reshape(n, d)
    for l in range(depth):
        w_main, w_gate = _split_w_in(w_in[l])
        proj, gates = _inproj(x2, norm_mix[l][None, :], w_main, w_gate)
        proj3 = proj.reshape(b, t, D_MAIN)
        gates3 = gates.reshape(b, t, LANES)
        gdn_prm = jnp.zeros((8, LANES), F32).at[0].set(_lane_row(gdn_a_log[l], G_A)).at[1].set(
            _lane_row(gdn_dt_bias[l], G_A))
        mlstm_prm = jnp.zeros((8, LANES), F32).at[0].set(_lane_row(mlstm_i_bias[l], M_I)).at[1].set(
            _lane_row(mlstm_f_bias[l], M_F))
        y_m = _mlstm_mixer(proj3, gates3, mlstm_conv[l], mlstm_prm, mlstm_norm[l][None, :])
        y_g = _gdn_mixer(proj3, gates3, gdn_conv[l], gdn_prm, gdn_norm[l][None, :])
        y_r = _ret_mixer(proj3, cos_tab, sin_tab, ret_norm[l][None, :])
        wo = w_out[l].astype(BF16)
        w_router = jnp.pad(jnp.concatenate([router_group[l], router_expert[l]], axis=1),
                           ((0, 0), (0, LANES - N_GROUPS - N_EXPERTS))).astype(BF16)
        rb = _lane_row(router_bias[l], N_GROUPS)[None, :]
        x2, h_packed, route, counts = _outproj(
            x2, y_m.reshape(n, MLSTM_W), y_g.reshape(n, GDN_W), y_r.reshape(n, RET_W),
            wo[GDN_W + RET_W:], wo[:GDN_W], wo[GDN_W:GDN_W + RET_W],
            norm_ffn[l][None, :], w_router, rb)
        pos, tile_e1, tile_e2, n_used, n_rows = _plan(route, counts[0], n)
        xs = _dispatch(h_packed, pos, n_rows)
        ys = _expert_ffn(xs, tile_e1, tile_e2, n_used, wg_all, wu_all, wd_all, l)
        x2 = _combine(x2, ys, pos, norm_final[None, :], l == depth - 1)
    return x2.reshape(b, t, d)
```

```python
import functools
import math

import jax
import jax.numpy as jnp
from jax import lax
from jax.experimental import pallas as pl
from jax.experimental.pallas import tpu as pltpu

F32 = jnp.float32
BF16 = jnp.bfloat16

D_MODEL = 1024
HEAD_DIM = 64
CHUNK = 64
GDN_HEADS, RET_HEADS, MLSTM_HEADS = 6, 4, 6
GDN_W, RET_W, MLSTM_W = GDN_HEADS * HEAD_DIM, RET_HEADS * HEAD_DIM, MLSTM_HEADS * HEAD_DIM
CONV_K = 4
ROPE_BASE = 10000.0
N_GROUPS, EXPERTS_PER_GROUP = 4, 8
N_EXPERTS = N_GROUPS * EXPERTS_PER_GROUP
D_EXPERT = D_MODEL // 4
EPS = 1e-6
LANES = 128
SEC_M, SEC_G, SEC_R = 4 * MLSTM_W, 4 * GDN_W, 4 * RET_W
D_MAIN = SEC_M + SEC_G + SEC_R
G_BETA, G_A, M_I, M_F = 0, GDN_HEADS, 2 * GDN_HEADS, 2 * GDN_HEADS + MLSTM_HEADS

ROW_TILE = 256
OUT_TILE = 512
TIME_BLOCK = 256
MOE_TILE = 128
VMEM_LIMIT = 56 * 1024 * 1024


def _cparams(sem):
    return pltpu.CompilerParams(dimension_semantics=sem, vmem_limit_bytes=VMEM_LIMIT)


def _silu(x):
    return x * (1.0 / (1.0 + jnp.exp(-x)))


def _sigmoid(x):
    return 1.0 / (1.0 + jnp.exp(-x))


def _softplus(x):
    return jnp.maximum(x, 0.0) + jnp.log(1.0 + jnp.exp(-jnp.abs(x)))


def _dot(a, b):
    return jnp.dot(a.astype(BF16), b.astype(BF16), preferred_element_type=F32)


def _dot_nt(a, b):
    return lax.dot_general(a.astype(BF16), b.astype(BF16), (((1,), (1,)), ((), ())),
                           preferred_element_type=F32)


def _lane_masks(rows):
    lane = lax.broadcasted_iota(jnp.int32, (rows, LANES), 1)
    row = lax.broadcasted_iota(jnp.int32, (rows, LANES), 0)
    return lane, row


def _half_sum(x, m_a):
    s_a = jnp.sum(jnp.where(m_a, x, 0.0), axis=-1, keepdims=True)
    s_b = jnp.sum(jnp.where(m_a, 0.0, x), axis=-1, keepdims=True)
    return jnp.where(m_a, s_a, s_b)


def _half_max(x, m_a):
    s_a = jnp.max(jnp.where(m_a, x, -jnp.inf), axis=-1, keepdims=True)
    s_b = jnp.max(jnp.where(m_a, -jnp.inf, x), axis=-1, keepdims=True)
    return jnp.where(m_a, s_a, s_b)


def _col_form(g, lane_a, m_a):
    rows = g.shape[0]
    ca = jnp.broadcast_to(g[:, lane_a:lane_a + 1], (rows, LANES))
    cb = jnp.broadcast_to(g[:, lane_a + 1:lane_a + 2], (rows, LANES))
    return jnp.where(m_a, ca, cb)


def _row_form(col, eye):
    return jnp.sum(jnp.where(eye, col, 0.0), axis=0, keepdims=True)


def _block_diag(y, m_a):
    return jnp.concatenate([jnp.where(m_a, y, 0.0), jnp.where(m_a, 0.0, y)], axis=0)


def _pmul(x, y, m_a):
    return _dot(x, _block_diag(y, m_a))


def _outer_state(k, v, bd_mask):
    zero = jnp.zeros_like(k)
    kt = jnp.concatenate([k, zero], axis=0).T
    vp = jnp.concatenate([v, zero], axis=0)
    return jnp.where(bd_mask, _dot(kt, vp), 0.0)


def _chunk_cumsum(x):
    rows = x.shape[0]
    r = lax.broadcasted_iota(jnp.int32, (rows, LANES), 0) % CHUNK
    s = 1
    while s < CHUNK:
        x = x + jnp.where(r >= s, pltpu.roll(x, s, axis=0), 0.0)
        s *= 2
    return x


def _causal_conv(raw, cbuf, w_ref):
    rows = raw.shape[0]
    cbuf[8:8 + rows, :] = raw
    acc = raw * w_ref[CONV_K - 1:CONV_K, :]
    for j in range(CONV_K - 1):
        off = 8 - (CONV_K - 1) + j
        acc = acc + cbuf[off:off + rows, :] * w_ref[j:j + 1, :]
    cbuf[0:8, :] = raw[rows - 8:rows, :]
    return acc


def _inproj_kernel(x_ref, gain_ref, w_ref, wg_ref, o_ref, og_ref):
    x = x_ref[...]
    ms = jnp.mean(x * x, axis=-1, keepdims=True)
    h = (x * lax.rsqrt(ms + EPS) * gain_ref[...]).astype(BF16)

    def proj(lo, hi):
        return jnp.dot(h, w_ref[:, lo:hi], preferred_element_type=F32)

    step = 512
    for c in range(D_MAIN // step):
        o_ref[:, c * step:(c + 1) * step] = proj(c * step, (c + 1) * step)
    og_ref[...] = jnp.dot(h, wg_ref[...], preferred_element_type=F32)


def _inproj(x2, gain, w_main, w_gate):
    n = x2.shape[0]
    full = lambda a, b: pl.BlockSpec((a, b), lambda i: (0, 0))
    return pl.pallas_call(
        _inproj_kernel,
        grid=(n // ROW_TILE,),
        in_specs=[
            pl.BlockSpec((ROW_TILE, D_MODEL), lambda i: (i, 0)),
            full(1, D_MODEL), full(D_MODEL, D_MAIN), full(D_MODEL, LANES),
        ],
        out_specs=[
            pl.BlockSpec((ROW_TILE, D_MAIN), lambda i: (i, 0)),
            pl.BlockSpec((ROW_TILE, LANES), lambda i: (i, 0)),
        ],
        out_shape=[jax.ShapeDtypeStruct((n, D_MAIN), F32), jax.ShapeDtypeStruct((n, LANES), F32)],
        compiler_params=_cparams(("parallel",)),
        name="norm_inproj",
    )(x2, gain, w_main, w_gate)


N_CLASSES = N_GROUPS * EXPERTS_PER_GROUP * EXPERTS_PER_GROUP
HP_WIDTH = D_MODEL + LANES


def _route_rows(lg):
    rows = lg.shape[0]
    lane = lax.broadcasted_iota(jnp.int32, (rows, LANES), 1)
    neg = -jnp.inf
    gl = jnp.where(lane < N_GROUPS, lg, neg)
    gm = jnp.max(gl, axis=-1, keepdims=True)
    gsum = jnp.sum(jnp.where(lane < N_GROUPS, jnp.exp(gl - gm), 0.0), axis=-1, keepdims=True)
    gidx = jnp.min(jnp.where(gl == gm, lane, LANES), axis=-1, keepdims=True)
    group_p = 1.0 / gsum
    in_group = (lane >= N_GROUPS) & (lane < N_GROUPS + N_EXPERTS) & (
        lax.shift_right_arithmetic(lane - N_GROUPS, 3) == gidx)
    el = jnp.where(in_group, lg, neg)
    m1 = jnp.max(el, axis=-1, keepdims=True)
    i1 = jnp.min(jnp.where(el == m1, lane, LANES), axis=-1, keepdims=True)
    el2 = jnp.where(lane == i1, neg, el)
    m2 = jnp.max(el2, axis=-1, keepdims=True)
    i2 = jnp.min(jnp.where(el2 == m2, lane, LANES), axis=-1, keepdims=True)
    e2 = jnp.exp(m2 - m1)
    w1 = group_p * (1.0 / (1.0 + e2))
    w2 = group_p * (e2 / (1.0 + e2))
    first_lower = i1 < i2
    base = N_GROUPS + gidx * EXPERTS_PER_GROUP
    lo = jnp.minimum(i1, i2) - base
    hi = jnp.maximum(i1, i2) - base
    cls = (gidx * EXPERTS_PER_GROUP + lo) * EXPERTS_PER_GROUP + hi
    return cls, jnp.where(first_lower, w1, w2), jnp.where(first_lower, w2, w1)


def _outproj_kernel(x_ref, ym_ref, yg_ref, yr_ref, wm_ref, wg_ref, wr_ref, gain_ref, wrt_ref, rb_ref,
                    xo_ref, hp_ref, rt_ref, cnt_ref, running):
    @pl.when(pl.program_id(0) == 0)
    def _():
        running[...] = jnp.zeros_like(running)

    acc = x_ref[...]
    acc = acc + jnp.dot(yg_ref[...], wg_ref[...], preferred_element_type=F32)
    acc = acc + jnp.dot(yr_ref[...], wr_ref[...], preferred_element_type=F32)
    acc = acc + jnp.dot(ym_ref[...], wm_ref[...], preferred_element_type=F32)
    xo_ref[...] = acc
    ms = jnp.mean(acc * acc, axis=-1, keepdims=True)
    hn = acc * lax.rsqrt(ms + EPS) * gain_ref[...]
    logits = jnp.dot(hn.astype(BF16), wrt_ref[...], preferred_element_type=F32) + rb_ref[...]
    cls, w_lo, w_hi = _route_rows(logits)

    rows = acc.shape[0]
    cls_lane = lax.broadcasted_iota(jnp.int32, (rows, N_CLASSES), 1)
    onehot = cls_lane == cls
    before = lax.broadcasted_iota(jnp.int32, (rows, rows), 1) < lax.broadcasted_iota(jnp.int32, (rows, rows), 0)
    prefix = jnp.dot(before.astype(BF16), onehot.astype(BF16), preferred_element_type=F32)
    rank = jnp.sum(jnp.where(onehot, prefix + running[0:1, :], 0.0), axis=-1, keepdims=True)
    running[0:1, :] = running[0:1, :] + jnp.sum(onehot.astype(F32), axis=0, keepdims=True)
    cnt_ref[...] = jnp.broadcast_to(running[0:1, :], cnt_ref.shape)

    lane = lax.broadcasted_iota(jnp.int32, (rows, LANES), 1)
    rt_ref[...] = jnp.where(lane == 0, cls, jnp.where(lane == 1, rank.astype(jnp.int32), 0))
    hp_ref[:, 0:D_MODEL] = hn
    hp_ref[:, D_MODEL:] = jnp.where(lane == 0, w_lo, jnp.where(lane == 1, w_hi, 0.0))


def _outproj(x2, ym, yg, yr, wo_m, wo_g, wo_r, gain, w_router, router_bias):
    n = x2.shape[0]
    row = lambda w: pl.BlockSpec((OUT_TILE, w), lambda i: (i, 0))
    full = lambda a, b: pl.BlockSpec((a, b), lambda i: (0, 0))
    return pl.pallas_call(
        _outproj_kernel,
        grid=(n // OUT_TILE,),
        in_specs=[row(D_MODEL), row(MLSTM_W), row(GDN_W), row(RET_W),
                  full(MLSTM_W, D_MODEL), full(GDN_W, D_MODEL), full(RET_W, D_MODEL),
                  full(1, D_MODEL), full(D_MODEL, LANES), full(1, LANES)],
        out_specs=[row(D_MODEL), row(HP_WIDTH), row(LANES), full(8, N_CLASSES)],
        out_shape=[jax.ShapeDtypeStruct((n, D_MODEL), F32),
                   jax.ShapeDtypeStruct((n, HP_WIDTH), F32),
                   jax.ShapeDtypeStruct((n, LANES), jnp.int32),
                   jax.ShapeDtypeStruct((8, N_CLASSES), F32)],
        scratch_shapes=[pltpu.VMEM((8, N_CLASSES), F32)],
        compiler_params=_cparams(("arbitrary",)),
        name="outproj_norm_router",
    )(x2, ym, yg, yr, wo_m, wo_g, wo_r, gain, w_router, router_bias)


def _gdn_kernel(x_ref, gt_ref, z_ref, conv_ref, prm_ref, gain_ref, y_ref,
                state, u_s, wq_s, qk_s, kt_s, sd_s, cbuf):
    tb = x_ref.shape[0]
    first = pl.program_id(1) == 0

    @pl.when(first)
    def _():
        for ref in (state, u_s, wq_s, qk_s, kt_s, sd_s):
            ref[...] = jnp.zeros_like(ref)
        cbuf[0:8, :] = jnp.zeros((8, cbuf.shape[1]), F32)

    gt = gt_ref[...]
    beta_all = _sigmoid(gt)
    g_all = -jnp.exp(prm_ref[0:1, :]) * _softplus(gt + prm_ref[1:2, :])
    gc_all = _chunk_cumsum(g_all)

    lane, row = _lane_masks(CHUNK)
    m_a = lane < HEAD_DIM
    lane_h = lane % HEAD_DIM
    eye = lane_h == row
    tril = lane_h <= row
    strict = lane_h < row
    eye_f = eye.astype(F32)
    lane2 = lax.broadcasted_iota(jnp.int32, (LANES, LANES), 1)
    row2 = lax.broadcasted_iota(jnp.int32, (LANES, LANES), 0)
    bd_mask = (lane2 < HEAD_DIM) == (row2 < HEAD_DIM)
    lane_t = lax.broadcasted_iota(jnp.int32, (tb, LANES), 1)
    m_a_t = lane_t < HEAD_DIM

    n_pairs = GDN_HEADS // 2
    n_chunks = tb // CHUNK
    units = [(p, c) for p in range(n_pairs) for c in range(n_chunks)]

    def unit_index(p, c):
        return p * n_chunks + c

    prev = {}
    for p, c in units:
        i = unit_index(p, c)
        prev[p, c] = (u_s[i], wq_s[i], qk_s[i], kt_s[i], sd_s[i, 0:1, :])
    s_bd = [state[p] for p in range(n_pairs)]

    def recurrence(c):
        rs = slice(c * CHUNK, (c + 1) * CHUNK)
        for p in range(n_pairs):
            cs = slice(p * LANES, (p + 1) * LANES)
            u_mat, wq, qk, k_dec_t, s_decay = prev[p, c]
            ws_qs = jnp.dot(wq, s_bd[p].astype(BF16), preferred_element_type=F32)
            v_new = u_mat - ws_qs[:CHUNK]
            o = ws_qs[CHUNK:] + jnp.dot(qk, _block_diag(v_new, m_a).astype(BF16), preferred_element_type=F32)
            v_pad = jnp.concatenate([v_new, jnp.zeros_like(v_new)], axis=0).astype(BF16)
            s_bd[p] = s_bd[p] * s_decay + jnp.where(
                bd_mask, jnp.dot(k_dec_t, v_pad, preferred_element_type=F32), 0.0)
            ms = _half_sum(o * o, m_a) * (1.0 / HEAD_DIM)
            y = o * lax.rsqrt(ms + EPS) * gain_ref[:, cs] * _silu(z_ref[rs, cs])
            y_ref[rs, cs] = y.astype(y_ref.dtype)

    recurrence(0)
    qkv = _silu(_causal_conv(x_ref[:, 0:3 * GDN_W], cbuf, conv_ref))
    pre = {}
    for p in range(n_pairs):
        q_t = qkv[:, p * LANES:(p + 1) * LANES]
        k_t = qkv[:, GDN_W + p * LANES:GDN_W + (p + 1) * LANES]
        v_t = qkv[:, 2 * GDN_W + p * LANES:2 * GDN_W + (p + 1) * LANES]
        q_t = q_t * lax.rsqrt(_half_sum(q_t * q_t, m_a_t) + EPS) * (HEAD_DIM ** -0.5)
        k_t = k_t * lax.rsqrt(_half_sum(k_t * k_t, m_a_t) + EPS)
        beta_t = _col_form(beta_all, G_BETA + 2 * p, m_a_t)
        gc_t = _col_form(gc_all, G_A + 2 * p, m_a_t)
        for c in range(n_chunks):
            rs = slice(c * CHUNK, (c + 1) * CHUNK)
            q, k, v, beta, gc = q_t[rs], k_t[rs], v_t[rs], beta_t[rs], gc_t[rs]
            decay = jnp.where(tril, jnp.exp(gc - _row_form(gc, eye)), 0.0)
            kb = k * beta
            kkqk = _dot_nt(jnp.concatenate([kb, q], axis=0), _block_diag(k, m_a))
            egc = jnp.exp(gc)
            g_last = gc[CHUNK - 1:CHUNK, :]
            k_dec = k * jnp.exp(g_last - gc)
            pre[p, c] = dict(
                a=jnp.where(strict, kkqk[:CHUNK] * decay, 0.0), qk=kkqk[CHUNK:] * decay,
                vb=v * beta, kbe=kb * egc, qe=q * egc, s_decay=jnp.exp(g_last),
                k_dec_t=jnp.concatenate([k_dec, jnp.zeros_like(k_dec)], axis=0).T)

    pw = {u: _pmul(pre[u]["a"], pre[u]["a"], m_a) for u in units}
    t_inv = {u: eye_f - pre[u]["a"] for u in units}
    later_chunks = list(range(1, n_chunks))
    for step in range(4):
        if step % 2 == 0 and later_chunks:
            recurrence(later_chunks.pop(0))
        both = {u: _dot(jnp.concatenate([pw[u], t_inv[u]], axis=0), _block_diag(pw[u], m_a)) for u in units}
        pw = {u: both[u][:CHUNK] for u in units}
        t_inv = {u: t_inv[u] + both[u][CHUNK:] for u in units}
    for c in later_chunks:
        recurrence(c)
    t_inv = {u: t_inv[u] + _pmul(t_inv[u], pw[u], m_a) for u in units}
    for p, c in units:
        d = pre[p, c]
        uw = _dot(t_inv[p, c], jnp.concatenate([_block_diag(d["vb"], m_a), _block_diag(d["kbe"], m_a)], axis=1))
        i = unit_index(p, c)
        u_s[i] = uw[:, :LANES]
        wq_s[i] = jnp.concatenate([uw[:, LANES:], d["qe"]], axis=0).astype(BF16)
        qk_s[i] = d["qk"].astype(BF16)
        kt_s[i] = d["k_dec_t"].astype(BF16)
        sd_s[i] = jnp.broadcast_to(d["s_decay"], (8, LANES))
    for p in range(n_pairs):
        state[p] = s_bd[p]


def _ret_kernel(x_ref, cos_ref, sin_ref, gain_ref, y_ref, state):
    tb = x_ref.shape[0]

    @pl.when(pl.program_id(1) == 0)
    def _():
        state[...] = jnp.zeros_like(state)

    lane, row = _lane_masks(CHUNK)
    m_a = lane < HEAD_DIM
    lane_h = lane % HEAD_DIM
    tril = lane_h <= row
    lane2 = lax.broadcasted_iota(jnp.int32, (LANES, LANES), 1)
    row2 = lax.broadcasted_iota(jnp.int32, (LANES, LANES), 0)
    bd_mask = (lane2 < HEAD_DIM) == (row2 < HEAD_DIM)
    lane_t = lax.broadcasted_iota(jnp.int32, (tb, LANES), 1)
    first_half = (lane_t % HEAD_DIM) < (HEAD_DIM // 2)
    cos = cos_ref[...]
    sin = sin_ref[...]
    rowf = row.astype(F32)
    pos_diff = (row - lane_h).astype(F32)

    def rope(x):
        swapped = jnp.where(first_half, pltpu.roll(x, LANES - HEAD_DIM // 2, axis=1),
                            pltpu.roll(x, HEAD_DIM // 2, axis=1))
        return x * cos + swapped * sin

    n_pairs = RET_HEADS // 2
    n_chunks = tb // CHUNK
    pre = {}
    chunk_decay = []
    for p in range(n_pairs):
        lg_a = math.log(1.0 - 2.0 ** (-5.0 - 2 * p))
        lg_b = math.log(1.0 - 2.0 ** (-5.0 - (2 * p + 1)))
        lg = jnp.where(m_a, lg_a, lg_b)
        decay = jnp.where(tril, jnp.exp(pos_diff * lg), 0.0)
        q_scale = jnp.exp(lg * (rowf + 1.0))
        k_scale = jnp.exp(lg * (CHUNK - 1.0 - rowf))
        chunk_decay.append(jnp.exp(lg[0:1, :] * CHUNK))
        q_t = rope(x_ref[:, p * LANES:(p + 1) * LANES])
        k_t = rope(x_ref[:, RET_W + p * LANES:RET_W + (p + 1) * LANES]) * (HEAD_DIM ** -0.5)
        for c in range(n_chunks):
            rs = slice(c * CHUNK, (c + 1) * CHUNK)
            q, k = q_t[rs], k_t[rs]
            v = x_ref[rs, 2 * RET_W + p * LANES:2 * RET_W + (p + 1) * LANES]
            scores = _dot_nt(q, _block_diag(k, m_a)) * decay
            pre[p, c] = (_pmul(scores, v, m_a), q * q_scale, _outer_state(k * k_scale, v, bd_mask))

    s_bd = [state[p] for p in range(n_pairs)]
    for c in range(n_chunks):
        rs = slice(c * CHUNK, (c + 1) * CHUNK)
        for p in range(n_pairs):
            cs = slice(p * LANES, (p + 1) * LANES)
            o_intra, q_dec, kv = pre[p, c]
            o = o_intra + _dot(q_dec, s_bd[p])
            s_bd[p] = chunk_decay[p] * s_bd[p] + kv
            ms = _half_sum(o * o, m_a) * (1.0 / HEAD_DIM)
            gate = x_ref[rs, 3 * RET_W + p * LANES:3 * RET_W + (p + 1) * LANES]
            y = o * lax.rsqrt(ms + EPS) * gain_ref[:, cs] * _silu(gate)
            y_ref[rs, cs] = y.astype(y_ref.dtype)
    for p in range(n_pairs):
        state[p] = s_bd[p]


def _mlstm_kernel(x_ref, gt_ref, conv_ref, prm_ref, gain_ref, y_ref, c_state, nm_state, cbuf):
    tb = x_ref.shape[0]
    first = pl.program_id(1) == 0

    @pl.when(first)
    def _():
        c_state[...] = jnp.zeros_like(c_state)
        nm_state[...] = jnp.zeros_like(nm_state)
        cbuf[0:8, :] = jnp.zeros((8, cbuf.shape[1]), F32)

    qk_all = _silu(_causal_conv(x_ref[:, 0:2 * MLSTM_W], cbuf, conv_ref))
    gt = gt_ref[...]
    log_i_all = gt + prm_ref[0:1, :]
    f_pre = gt + prm_ref[1:2, :]
    log_f_all = jnp.minimum(f_pre, 0.0) - jnp.log(1.0 + jnp.exp(-jnp.abs(f_pre)))
    bc_all = _chunk_cumsum(log_f_all)

    lane, row = _lane_masks(CHUNK)
    m_a = lane < HEAD_DIM
    lane_h = lane % HEAD_DIM
    eye = lane_h == row
    tril = lane_h <= row
    lane2 = lax.broadcasted_iota(jnp.int32, (LANES, LANES), 1)
    row2 = lax.broadcasted_iota(jnp.int32, (LANES, LANES), 0)
    bd_mask = (lane2 < HEAD_DIM) == (row2 < HEAD_DIM)
    lane_t = lax.broadcasted_iota(jnp.int32, (tb, LANES), 1)
    m_a_t = lane_t < HEAD_DIM

    n_pairs = MLSTM_HEADS // 2
    n_chunks = tb // CHUNK
    units = [(p, c) for p in range(n_pairs) for c in range(n_chunks)]
    gates = {}
    for p in range(n_pairs):
        b_t = _col_form(bc_all, M_F + 2 * p, m_a_t)
        i_t = _col_form(log_i_all, M_I + 2 * p, m_a_t)
        for c in range(n_chunks):
            rs = slice(c * CHUNK, (c + 1) * CHUNK)
            gates[p, c] = (b_t[rs], i_t[rs])
    qkv = {(p, c): (qk_all[c * CHUNK:(c + 1) * CHUNK, p * LANES:(p + 1) * LANES] * (HEAD_DIM ** -0.5),
                    qk_all[c * CHUNK:(c + 1) * CHUNK, MLSTM_W + p * LANES:MLSTM_W + (p + 1) * LANES],
                    x_ref[c * CHUNK:(c + 1) * CHUNK, 2 * MLSTM_W + p * LANES:2 * MLSTM_W + (p + 1) * LANES])
           for p, c in units}
    qk = {u: _dot_nt(qkv[u][0], _block_diag(qkv[u][1], m_a)) for u in units}
    log_d = {u: jnp.where(tril, gates[u][0] - _row_form(gates[u][0], eye) + _row_form(gates[u][1], eye), -jnp.inf)
             for u in units}
    m_intra = {u: _half_max(log_d[u], m_a) for u in units}
    a_c = {u: gates[u][0][CHUNK - 1:CHUNK, :] - gates[u][0] + gates[u][1] for u in units}
    a_max = {u: jnp.max(a_c[u], axis=0, keepdims=True) for u in units}
    m_in, m_out = {}, {}
    for p in range(n_pairs):
        m_row = nm_state[p, 1:2, :]
        for c in range(n_chunks):
            m_in[p, c] = m_row
            m_row = jnp.maximum(gates[p, c][0][CHUNK - 1:CHUNK, :] + m_row, a_max[p, c])
            m_out[p, c] = m_row
    m_t = {u: jnp.maximum(gates[u][0] + m_in[u], m_intra[u]) for u in units}
    inter = {u: jnp.exp(gates[u][0] + m_in[u] - m_t[u]) for u in units}
    wmat = {u: jnp.where(tril, jnp.exp(log_d[u] - m_t[u]), 0.0) * qk[u] for u in units}
    kw = {u: qkv[u][1] * jnp.exp(a_c[u] - m_out[u]) for u in units}
    dec = {u: jnp.exp(gates[u][0][CHUNK - 1:CHUNK, :] + m_in[u] - m_out[u]) for u in units}
    num_intra = {u: _pmul(wmat[u], qkv[u][2], m_a) for u in units}
    den_intra = {u: _half_sum(wmat[u], m_a) for u in units}
    kv = {u: _outer_state(kw[u], qkv[u][2], bd_mask) for u in units}
    kn = {u: jnp.sum(kw[u], axis=0, keepdims=True) for u in units}
    c_in, n_in = {}, {}
    for p in range(n_pairs):
        c_bd, n_row = c_state[p], nm_state[p, 0:1, :]
        for c in range(n_chunks):
            c_in[p, c], n_in[p, c] = c_bd, n_row
            c_bd = dec[p, c] * c_bd + kv[p, c]
            n_row = dec[p, c] * n_row + kn[p, c]
        c_state[p] = c_bd
        nm_state[p, 0:1, :] = n_row
        nm_state[p, 1:2, :] = m_out[p, n_chunks - 1]
    num = {u: inter[u] * _dot(qkv[u][0], c_in[u]) + num_intra[u] for u in units}
    den = {u: inter[u] * _half_sum(qkv[u][0] * n_in[u], m_a) + den_intra[u] for u in units}
    h = {u: num[u] / jnp.maximum(jnp.abs(den[u]), jnp.exp(-m_t[u])) for u in units}
    ms = {u: _half_sum(h[u] * h[u], m_a) * (1.0 / HEAD_DIM) for u in units}
    for p, c in units:
        rs = slice(c * CHUNK, (c + 1) * CHUNK)
        cs = slice(p * LANES, (p + 1) * LANES)
        o_logit = x_ref[rs, 3 * MLSTM_W + p * LANES:3 * MLSTM_W + (p + 1) * LANES]
        y = _sigmoid(o_logit) * (h[p, c] * lax.rsqrt(ms[p, c] + EPS) * gain_ref[:, cs])
        y_ref[rs, cs] = y.astype(y_ref.dtype)


def _mixer_call(kernel, name, proj3, sec_block, sec_width, out_width, extra_inputs, extra_specs, scratch):
    b, t, _ = proj3.shape
    tb = min(TIME_BLOCK, t)
    return pl.pallas_call(
        kernel,
        grid=(b, t // tb),
        in_specs=[pl.BlockSpec((None, tb, sec_width), lambda i, j: (i, j, sec_block))] + extra_specs(tb),
        out_specs=pl.BlockSpec((None, tb, out_width), lambda i, j: (i, j, 0)),
        out_shape=jax.ShapeDtypeStruct((b, t, out_width), BF16),
        scratch_shapes=scratch(tb),
        compiler_params=_cparams(("parallel", "arbitrary")),
        name=name,
    )(proj3, *extra_inputs)


def _full2(a, b):
    return pl.BlockSpec((a, b), lambda i, j: (0, 0))


def _gdn_mixer(proj3, gates3, conv_w, prm, gain):
    b, t, _ = proj3.shape
    tb = min(TIME_BLOCK, t)
    n_blocks = t // tb
    n_units = (GDN_HEADS // 2) * (tb // CHUNK)
    cur = lambda i, j: (i, jnp.minimum(j, n_blocks - 1))
    prv = lambda i, j: (i, jnp.maximum(j - 1, 0))
    z_block = (SEC_M + 3 * GDN_W) // GDN_W
    assert z_block * GDN_W == SEC_M + 3 * GDN_W
    return pl.pallas_call(
        _gdn_kernel,
        grid=(b, n_blocks + 1),
        in_specs=[pl.BlockSpec((None, tb, SEC_G), lambda i, j: cur(i, j) + (1,)),
                  pl.BlockSpec((None, tb, LANES), lambda i, j: cur(i, j) + (0,)),
                  pl.BlockSpec((None, tb, GDN_W), lambda i, j: prv(i, j) + (z_block,)),
                  _full2(CONV_K, 3 * GDN_W), _full2(8, LANES), _full2(1, GDN_W)],
        out_specs=pl.BlockSpec((None, tb, GDN_W), lambda i, j: prv(i, j) + (0,)),
        out_shape=jax.ShapeDtypeStruct((b, t, GDN_W), BF16),
        scratch_shapes=[pltpu.VMEM((GDN_HEADS // 2, LANES, LANES), F32),
                        pltpu.VMEM((n_units, CHUNK, LANES), F32),
                        pltpu.VMEM((n_units, 2 * CHUNK, LANES), BF16),
                        pltpu.VMEM((n_units, CHUNK, LANES), BF16),
                        pltpu.VMEM((n_units, LANES, LANES), BF16),
                        pltpu.VMEM((n_units, 8, LANES), F32),
                        pltpu.VMEM((tb + 8, 3 * GDN_W), F32)],
        compiler_params=_cparams(("parallel", "arbitrary")),
        name="gdn_mixer",
    )(proj3, gates3, proj3, conv_w, prm, gain)


def _mlstm_mixer(proj3, gates3, conv_w, prm, gain):
    return _mixer_call(
        _mlstm_kernel, "mlstm_mixer", proj3, 0, SEC_M, MLSTM_W, (gates3, conv_w, prm, gain),
        lambda tb: [pl.BlockSpec((None, tb, LANES), lambda i, j: (i, j, 0)),
                    _full2(CONV_K, 2 * MLSTM_W), _full2(8, LANES), _full2(1, MLSTM_W)],
        lambda tb: [pltpu.VMEM((MLSTM_HEADS // 2, LANES, LANES), F32),
                    pltpu.VMEM((MLSTM_HEADS // 2, 8, LANES), F32),
                    pltpu.VMEM((tb + 8, 2 * MLSTM_W), F32)])


def _ret_mixer(proj3, cos_tab, sin_tab, gain):
    return _mixer_call(
        _ret_kernel, "retention_mixer", proj3, 3, SEC_R, RET_W, (cos_tab, sin_tab, gain),
        lambda tb: [pl.BlockSpec((tb, LANES), lambda i, j: (j, 0)),
                    pl.BlockSpec((tb, LANES), lambda i, j: (j, 0)),
                    _full2(1, RET_W)],
        lambda tb: [pltpu.VMEM((RET_HEADS // 2, LANES, LANES), F32)])


DISPATCH_TILE = 512


def _dispatch_kernel(pos_ref, src_ref, zero_ref, out_ref, sem):
    del zero_ref
    rows = src_ref.shape[0]
    base = pl.program_id(0) * rows

    def issue(pair, carry):
        for k in range(2):
            r = 2 * pair + k
            pltpu.make_async_copy(src_ref.at[pl.ds(r, 1)], out_ref.at[pl.ds(pos_ref[base + r], 1)],
                                  sem).start(priority=k)
        return carry

    lax.fori_loop(0, rows // 2, issue, 0, unroll=4)
    pltpu.make_async_copy(src_ref, out_ref.at[pl.ds(0, rows)], sem).wait()


def _dispatch(src, pos, n_rows):
    n, width = src.shape
    tile = min(DISPATCH_TILE, n)
    zeros = jnp.zeros((n_rows, width), src.dtype)
    return pl.pallas_call(
        _dispatch_kernel,
        grid_spec=pltpu.PrefetchScalarGridSpec(
            num_scalar_prefetch=1,
            grid=(n // tile,),
            in_specs=[pl.BlockSpec((tile, width), lambda i, p: (i, 0)), pl.BlockSpec(memory_space=pl.ANY)],
            out_specs=pl.BlockSpec(memory_space=pl.ANY),
            scratch_shapes=[pltpu.SemaphoreType.DMA(())],
        ),
        out_shape=jax.ShapeDtypeStruct((n_rows, width), src.dtype),
        input_output_aliases={2: 0},
        compiler_params=_cparams(("arbitrary",)),
        name="moe_dispatch",
    )(pos, src, zeros)


def _ffn_kernel(layer, plan_ref, used_ref, x_ref, wg_hbm, wu_hbm, wd_hbm, y_ref, wg_buf, wu_buf, wd_buf, sems):
    i = pl.program_id(0)
    used = i < used_ref[0]
    group, slot, starts_group, next_group = plan_ref[0, i], plan_ref[1, i], plan_ref[2, i], plan_ref[3, i]

    def group_copies(g, to_slot):
        lo = pl.multiple_of(g * EXPERTS_PER_GROUP, EXPERTS_PER_GROUP)
        experts = pl.ds(lo, EXPERTS_PER_GROUP)
        return [pltpu.make_async_copy(hbm.at[layer, experts], buf.at[to_slot], sems.at[to_slot])
                for hbm, buf in ((wg_hbm, wg_buf), (wu_hbm, wu_buf), (wd_hbm, wd_buf))]

    @pl.when(used & (i == 0))
    def _():
        for cp in group_copies(group, slot):
            cp.start()

    @pl.when(used & (starts_group == 1) & (next_group >= 0))
    def _():
        for cp in group_copies(next_group, 1 - slot):
            cp.start()

    @pl.when(used & (starts_group == 1))
    def _():
        for cp in group_copies(group, slot):
            cp.wait()

    @pl.when(jnp.logical_not(used))
    def _():
        y_ref[...] = jnp.zeros_like(y_ref)

    @pl.when(used)
    def _():
        e1, e2 = plan_ref[4, i], plan_ref[5, i]
        x = x_ref[:, 0:D_MODEL].astype(BF16)
        gate1 = jnp.dot(x, wg_buf[slot, e1], preferred_element_type=F32)
        gate2 = jnp.dot(x, wg_buf[slot, e2], preferred_element_type=F32)
        up1 = jnp.dot(x, wu_buf[slot, e1], preferred_element_type=F32)
        up2 = jnp.dot(x, wu_buf[slot, e2], preferred_element_type=F32)
        hid1 = (_silu(gate1) * up1).astype(BF16)
        hid2 = (_silu(gate2) * up2).astype(BF16)
        y1 = jnp.dot(hid1, wd_buf[slot, e1], preferred_element_type=F32)
        y2 = jnp.dot(hid2, wd_buf[slot, e2], preferred_element_type=F32)
        rw = x_ref[:, D_MODEL:]
        y_ref[...] = rw[:, 0:1] * y1 + rw[:, 1:2] * y2


def _expert_ffn(xs, tile_plan, n_used, wg, wu, wd, layer):
    r = xs.shape[0]
    return pl.pallas_call(
        functools.partial(_ffn_kernel, layer),
        grid_spec=pltpu.PrefetchScalarGridSpec(
            num_scalar_prefetch=2,
            grid=(r // MOE_TILE,),
            in_specs=[pl.BlockSpec((MOE_TILE, HP_WIDTH), lambda i, plan, nu: (jnp.minimum(i, nu[0] - 1), 0)),
                      pl.BlockSpec(memory_space=pl.ANY), pl.BlockSpec(memory_space=pl.ANY),
                      pl.BlockSpec(memory_space=pl.ANY)],
            out_specs=pl.BlockSpec((MOE_TILE, D_MODEL), lambda i, plan, nu: (i, 0)),
            scratch_shapes=[pltpu.VMEM((2, EXPERTS_PER_GROUP, D_MODEL, D_EXPERT), BF16),
                            pltpu.VMEM((2, EXPERTS_PER_GROUP, D_MODEL, D_EXPERT), BF16),
                            pltpu.VMEM((2, EXPERTS_PER_GROUP, D_EXPERT, D_MODEL), BF16),
                            pltpu.SemaphoreType.DMA((2,))],
        ),
        out_shape=jax.ShapeDtypeStruct((r, D_MODEL), F32),
        compiler_params=_cparams(("arbitrary",)),
        name="expert_pair_ffn",
    )(tile_plan, n_used, xs, wg, wu, wd)


def _combine_kernel(final_norm, pos_ref, x_ref, gain_ref, ys_ref, o_ref, buf, sems):
    rows = o_ref.shape[0]
    i = pl.program_id(0)
    slot = i & 1

    def gather_tile(tile, to_slot):
        def issue(pair, carry):
            for k in range(2):
                r = 2 * pair + k
                pltpu.make_async_copy(ys_ref.at[pl.ds(pos_ref[tile * rows + r], 1)],
                                      buf.at[to_slot, pl.ds(r, 1)], sems.at[to_slot]).start(priority=k)
            return carry

        lax.fori_loop(0, rows // 2, issue, 0, unroll=4)

    @pl.when(i == 0)
    def _():
        gather_tile(0, 0)

    @pl.when(i + 1 < pl.num_programs(0))
    def _():
        gather_tile(i + 1, 1 - slot)

    pltpu.make_async_copy(ys_ref.at[pl.ds(0, rows)], buf.at[slot], sems.at[slot]).wait()
    out = x_ref[...] + buf[slot]
    if final_norm:
        ms = jnp.mean(out * out, axis=-1, keepdims=True)
        out = out * lax.rsqrt(ms + EPS) * gain_ref[...]
    o_ref[...] = out


def _combine(x2, ys, pos, gain, final_norm):
    n = x2.shape[0]
    return pl.pallas_call(
        functools.partial(_combine_kernel, final_norm),
        grid_spec=pltpu.PrefetchScalarGridSpec(
            num_scalar_prefetch=1,
            grid=(n // ROW_TILE,),
            in_specs=[pl.BlockSpec((ROW_TILE, D_MODEL), lambda i, p: (i, 0)),
                      pl.BlockSpec((1, D_MODEL), lambda i, p: (0, 0)),
                      pl.BlockSpec(memory_space=pl.ANY)],
            out_specs=pl.BlockSpec((ROW_TILE, D_MODEL), lambda i, p: (i, 0)),
            scratch_shapes=[pltpu.VMEM((2, ROW_TILE, D_MODEL), F32), pltpu.SemaphoreType.DMA((2,))],
        ),
        out_shape=jax.ShapeDtypeStruct((n, D_MODEL), F32),
        compiler_params=_cparams(("arbitrary",)),
        name="moe_combine",
    )(pos, x2, gain, ys)


def _plan(route, counts, n):
    cls, rank = route[:, 0], route[:, 1]
    counts = counts.astype(jnp.int32)
    tiles_per = (counts + MOE_TILE - 1) // MOE_TILE
    tile_end = jnp.cumsum(tiles_per)
    tile_start = tile_end - tiles_per
    class_ids = jnp.arange(N_CLASSES, dtype=jnp.int32)
    start_of = jnp.sum(jnp.where(cls[:, None] == class_ids[None, :], tile_start[None, :], 0), axis=1)
    pos = start_of * MOE_TILE + rank
    n_pairs = N_GROUPS * (EXPERTS_PER_GROUP * (EXPERTS_PER_GROUP - 1)) // 2
    n_tiles = n // MOE_TILE + n_pairs
    n_used = tile_end[-1]
    tile_ids = jnp.minimum(jnp.arange(n_tiles, dtype=jnp.int32), n_used - 1)
    tile_cls = jnp.sum((tile_end[None, :] <= tile_ids[:, None]).astype(jnp.int32), axis=1)
    per_group = EXPERTS_PER_GROUP * EXPERTS_PER_GROUP
    tile_group = tile_cls // per_group
    starts_group = jnp.concatenate([jnp.ones((1,), jnp.int32),
                                    (tile_group[1:] != tile_group[:-1]).astype(jnp.int32)])
    slot = (jnp.cumsum(starts_group) - 1) % 2
    present = jnp.sum(tiles_per.reshape(N_GROUPS, per_group), axis=1) > 0
    nxt = [jnp.int32(-1)] * N_GROUPS
    for g in range(N_GROUPS - 2, -1, -1):
        nxt[g] = jnp.where(present[g + 1], g + 1, nxt[g + 1])
    next_group = jnp.stack(nxt)[tile_group]
    tile_plan = jnp.stack([tile_group, slot, starts_group, next_group,
                           (tile_cls // EXPERTS_PER_GROUP) % EXPERTS_PER_GROUP, tile_cls % EXPERTS_PER_GROUP])
    return pos, tile_plan.astype(jnp.int32), n_used.reshape(1), n_tiles * MOE_TILE


def _rope_tables(t):
    inv_freq = ROPE_BASE ** (-jnp.arange(0, HEAD_DIM, 2, dtype=F32) / HEAD_DIM)
    ang = jnp.arange(t, dtype=F32)[:, None] * inv_freq[None, :]
    cos, sin = jnp.cos(ang), jnp.sin(ang)
    cos_tab = jnp.tile(cos, (1, 2 * LANES // HEAD_DIM))
    sin_tab = jnp.tile(jnp.concatenate([-sin, sin], axis=-1), (1, LANES // HEAD_DIM))
    return cos_tab, sin_tab


def _split_w_in(w):
    o = 0
    parts = {}
    for name, width in (("g_qkv", 3 * GDN_W), ("g_z", GDN_W), ("g_b", GDN_HEADS), ("g_a", GDN_HEADS),
                        ("r_q", RET_W), ("r_k", RET_W), ("r_v", RET_W), ("r_g", RET_W),
                        ("m_qk", 2 * MLSTM_W), ("m_v", MLSTM_W), ("m_o", MLSTM_W),
                        ("m_i", MLSTM_HEADS), ("m_f", MLSTM_HEADS)):
        parts[name] = w[:, o:o + width]
        o += width
    main = jnp.concatenate([parts[k] for k in ("m_qk", "m_v", "m_o", "g_qkv", "g_z", "r_q", "r_k", "r_v", "r_g")],
                           axis=1).astype(BF16)
    gate = jnp.concatenate([parts[k] for k in ("g_b", "g_a", "m_i", "m_f")], axis=1)
    gate = jnp.pad(gate, ((0, 0), (0, LANES - gate.shape[1]))).astype(BF16)
    return main, gate


def _lane_row(values, start):
    return jnp.zeros((LANES,), F32).at[start:start + values.shape[0]].set(values)


def kernel(x, norm_mix, w_in, gdn_conv, gdn_a_log, gdn_dt_bias, gdn_norm, ret_norm, mlstm_conv, mlstm_i_bias, mlstm_f_bias, mlstm_norm, w_out, norm_ffn, router_group, router_expert, router_bias, expert_gate, expert_up, expert_down, norm_final):
    b, t, d = x.shape
    n = b * t
    depth = w_in.shape[0]
    cos_tab, sin_tab = _rope_tables(t)
    wg_all, wu_all, wd_all = expert_gate.astype(BF16), expert_up.astype(BF16), expert_down.astype(BF16)
    x2 = x.reshape(n, d)
    for l in range(depth):
        w_main, w_gate = _split_w_in(w_in[l])
        proj, gates = _inproj(x2, norm_mix[l][None, :], w_main, w_gate)
        proj3 = proj.reshape(b, t, D_MAIN)
        gates3 = gates.reshape(b, t, LANES)
        gdn_prm = jnp.zeros((8, LANES), F32).at[0].set(_lane_row(gdn_a_log[l], G_A)).at[1].set(
            _lane_row(gdn_dt_bias[l], G_A))
        mlstm_prm = jnp.zeros((8, LANES), F32).at[0].set(_lane_row(mlstm_i_bias[l], M_I)).at[1].set(
            _lane_row(mlstm_f_bias[l], M_F))
        y_m = _mlstm_mixer(proj3, gates3, mlstm_conv[l], mlstm_prm, mlstm_norm[l][None, :])
        y_g = _gdn_mixer(proj3, gates3, gdn_conv[l], gdn_prm, gdn_norm[l][None, :])
        y_r = _ret_mixer(proj3, cos_tab, sin_tab, ret_norm[l][None, :])
        wo = w_out[l].astype(BF16)
        w_router = jnp.pad(jnp.concatenate([router_group[l], router_expert[l]], axis=1),
                           ((0, 0), (0, LANES - N_GROUPS - N_EXPERTS))).astype(BF16)
        rb = _lane_row(router_bias[l], N_GROUPS)[None, :]
        x2, h_packed, route, counts = _outproj(
            x2, y_m.reshape(n, MLSTM_W), y_g.reshape(n, GDN_W), y_r.reshape(n, RET_W),
            wo[GDN_W + RET_W:], wo[:GDN_W], wo[GDN_W:GDN_W + RET_W],
            norm_ffn[l][None, :], w_router, rb)
        pos, tile_plan, n_used, n_rows = _plan(route, counts[0], n)
        xs = _dispatch(h_packed, pos, n_rows)
        ys = _expert_ffn(xs, tile_plan, n_used, wg_all, wu_all, wd_all, l)
        x2 = _combine(x2, ys, pos, norm_final[None, :], l == depth - 1)
    return x2.reshape(b, t, d)
```

```python
import functools
import math

import jax
import jax.numpy as jnp
from jax import lax
from jax.experimental import pallas as pl
from jax.experimental.pallas import tpu as pltpu

F32 = jnp.float32
BF16 = jnp.bfloat16

D_MODEL = 1024
HEAD_DIM = 64
CHUNK = 64
GDN_HEADS, RET_HEADS, MLSTM_HEADS = 6, 4, 6
GDN_W, RET_W, MLSTM_W = GDN_HEADS * HEAD_DIM, RET_HEADS * HEAD_DIM, MLSTM_HEADS * HEAD_DIM
CONV_K = 4
ROPE_BASE = 10000.0
N_GROUPS, EXPERTS_PER_GROUP = 4, 8
N_EXPERTS = N_GROUPS * EXPERTS_PER_GROUP
D_EXPERT = D_MODEL // 4
EPS = 1e-6
LANES = 128
SEC_M, SEC_G, SEC_R = 4 * MLSTM_W, 4 * GDN_W, 4 * RET_W
D_MAIN = SEC_M + SEC_G + SEC_R
G_BETA, G_A, M_I, M_F = 0, GDN_HEADS, 2 * GDN_HEADS, 2 * GDN_HEADS + MLSTM_HEADS

ROW_TILE = 256
OUT_TILE = 512
TIME_BLOCK = 256
MOE_TILE = 128
VMEM_LIMIT = 56 * 1024 * 1024


def _cparams(sem):
    return pltpu.CompilerParams(dimension_semantics=sem, vmem_limit_bytes=VMEM_LIMIT)


def _silu(x):
    return x * (1.0 / (1.0 + jnp.exp(-x)))


def _sigmoid(x):
    return 1.0 / (1.0 + jnp.exp(-x))


def _softplus(x):
    return jnp.maximum(x, 0.0) + jnp.log(1.0 + jnp.exp(-jnp.abs(x)))


def _dot(a, b):
    return jnp.dot(a.astype(BF16), b.astype(BF16), preferred_element_type=F32)


def _dot_nt(a, b):
    return lax.dot_general(a.astype(BF16), b.astype(BF16), (((1,), (1,)), ((), ())),
                           preferred_element_type=F32)


def _lane_masks(rows):
    lane = lax.broadcasted_iota(jnp.int32, (rows, LANES), 1)
    row = lax.broadcasted_iota(jnp.int32, (rows, LANES), 0)
    return lane, row


def _half_sum(x, m_a):
    s_a = jnp.sum(jnp.where(m_a, x, 0.0), axis=-1, keepdims=True)
    s_b = jnp.sum(jnp.where(m_a, 0.0, x), axis=-1, keepdims=True)
    return jnp.where(m_a, s_a, s_b)


def _half_max(x, m_a):
    s_a = jnp.max(jnp.where(m_a, x, -jnp.inf), axis=-1, keepdims=True)
    s_b = jnp.max(jnp.where(m_a, -jnp.inf, x), axis=-1, keepdims=True)
    return jnp.where(m_a, s_a, s_b)


def _col_form(g, lane_a, m_a):
    rows = g.shape[0]
    ca = jnp.broadcast_to(g[:, lane_a:lane_a + 1], (rows, LANES))
    cb = jnp.broadcast_to(g[:, lane_a + 1:lane_a + 2], (rows, LANES))
    return jnp.where(m_a, ca, cb)


def _row_form(col, eye):
    return jnp.sum(jnp.where(eye, col, 0.0), axis=0, keepdims=True)


def _block_diag(y, m_a):
    return jnp.concatenate([jnp.where(m_a, y, 0.0), jnp.where(m_a, 0.0, y)], axis=0)


def _pmul(x, y, m_a):
    return _dot(x, _block_diag(y, m_a))


def _outer_state(k, v, bd_mask):
    zero = jnp.zeros_like(k)
    kt = jnp.concatenate([k, zero], axis=0).T
    vp = jnp.concatenate([v, zero], axis=0)
    return jnp.where(bd_mask, _dot(kt, vp), 0.0)


def _chunk_cumsum(x):
    rows = x.shape[0]
    r = lax.broadcasted_iota(jnp.int32, (rows, LANES), 0) % CHUNK
    s = 1
    while s < CHUNK:
        x = x + jnp.where(r >= s, pltpu.roll(x, s, axis=0), 0.0)
        s *= 2
    return x


def _causal_conv(raw, cbuf, w_ref):
    rows = raw.shape[0]
    cbuf[8:8 + rows, :] = raw
    acc = raw * w_ref[CONV_K - 1:CONV_K, :]
    for j in range(CONV_K - 1):
        off = 8 - (CONV_K - 1) + j
        acc = acc + cbuf[off:off + rows, :] * w_ref[j:j + 1, :]
    cbuf[0:8, :] = raw[rows - 8:rows, :]
    return acc


def _inproj_kernel(add_moe, *refs):
    if add_moe:
        x_ref, y_ref, gain_ref, w_ref, wg_ref, xo_ref, o_ref, og_ref = refs
        x = x_ref[...] + y_ref[...]
        xo_ref[...] = x
    else:
        x_ref, gain_ref, w_ref, wg_ref, o_ref, og_ref = refs
        x = x_ref[...]
    ms = jnp.mean(x * x, axis=-1, keepdims=True)
    h = (x * lax.rsqrt(ms + EPS) * gain_ref[...]).astype(BF16)

    def proj(lo, hi):
        return jnp.dot(h, w_ref[:, lo:hi], preferred_element_type=F32)

    step = 512
    for c in range(D_MAIN // step):
        o_ref[:, c * step:(c + 1) * step] = proj(c * step, (c + 1) * step)
    og_ref[...] = jnp.dot(h, wg_ref[...], preferred_element_type=F32)


def _inproj(x2, y_tok, gain, w_main, w_gate):
    n = x2.shape[0]
    full = lambda a, b: pl.BlockSpec((a, b), lambda i: (0, 0))
    row = lambda w: pl.BlockSpec((ROW_TILE, w), lambda i: (i, 0))
    add_moe = y_tok is not None
    outs = pl.pallas_call(
        functools.partial(_inproj_kernel, add_moe),
        grid=(n // ROW_TILE,),
        in_specs=[row(D_MODEL)] * (2 if add_moe else 1) + [full(1, D_MODEL), full(D_MODEL, D_MAIN),
                                                            full(D_MODEL, LANES)],
        out_specs=[row(D_MODEL)] * add_moe + [row(D_MAIN), row(LANES)],
        out_shape=[jax.ShapeDtypeStruct((n, D_MODEL), F32)] * add_moe + [
            jax.ShapeDtypeStruct((n, D_MAIN), F32), jax.ShapeDtypeStruct((n, LANES), F32)],
        compiler_params=_cparams(("parallel",)),
        name="norm_inproj",
    )(*((x2, y_tok) if add_moe else (x2,)), gain, w_main, w_gate)
    return tuple(outs) if add_moe else (x2, *outs)


N_CLASSES = N_GROUPS * EXPERTS_PER_GROUP * EXPERTS_PER_GROUP
HP_WIDTH = D_MODEL + LANES


def _route_rows(lg):
    rows = lg.shape[0]
    lane = lax.broadcasted_iota(jnp.int32, (rows, LANES), 1)
    neg = -jnp.inf
    gl = jnp.where(lane < N_GROUPS, lg, neg)
    gm = jnp.max(gl, axis=-1, keepdims=True)
    gsum = jnp.sum(jnp.where(lane < N_GROUPS, jnp.exp(gl - gm), 0.0), axis=-1, keepdims=True)
    gidx = jnp.min(jnp.where(gl == gm, lane, LANES), axis=-1, keepdims=True)
    group_p = 1.0 / gsum
    in_group = (lane >= N_GROUPS) & (lane < N_GROUPS + N_EXPERTS) & (
        lax.shift_right_arithmetic(lane - N_GROUPS, 3) == gidx)
    el = jnp.where(in_group, lg, neg)
    m1 = jnp.max(el, axis=-1, keepdims=True)
    i1 = jnp.min(jnp.where(el == m1, lane, LANES), axis=-1, keepdims=True)
    el2 = jnp.where(lane == i1, neg, el)
    m2 = jnp.max(el2, axis=-1, keepdims=True)
    i2 = jnp.min(jnp.where(el2 == m2, lane, LANES), axis=-1, keepdims=True)
    e2 = jnp.exp(m2 - m1)
    w1 = group_p * (1.0 / (1.0 + e2))
    w2 = group_p * (e2 / (1.0 + e2))
    first_lower = i1 < i2
    base = N_GROUPS + gidx * EXPERTS_PER_GROUP
    lo = jnp.minimum(i1, i2) - base
    hi = jnp.maximum(i1, i2) - base
    cls = (gidx * EXPERTS_PER_GROUP + lo) * EXPERTS_PER_GROUP + hi
    return cls, jnp.where(first_lower, w1, w2), jnp.where(first_lower, w2, w1)


def _outproj_kernel(x_ref, ym_ref, yg_ref, yr_ref, wm_ref, wg_ref, wr_ref, gain_ref, wrt_ref, rb_ref,
                    xo_ref, hp_ref, rt_ref, cnt_ref, running):
    @pl.when(pl.program_id(0) == 0)
    def _():
        running[...] = jnp.zeros_like(running)

    acc = x_ref[...]
    acc = acc + jnp.dot(yg_ref[...], wg_ref[...], preferred_element_type=F32)
    acc = acc + jnp.dot(yr_ref[...], wr_ref[...], preferred_element_type=F32)
    acc = acc + jnp.dot(ym_ref[...], wm_ref[...], preferred_element_type=F32)
    xo_ref[...] = acc
    ms = jnp.mean(acc * acc, axis=-1, keepdims=True)
    hn = acc * lax.rsqrt(ms + EPS) * gain_ref[...]
    logits = jnp.dot(hn.astype(BF16), wrt_ref[...], preferred_element_type=F32) + rb_ref[...]
    cls, w_lo, w_hi = _route_rows(logits)

    rows = acc.shape[0]
    cls_lane = lax.broadcasted_iota(jnp.int32, (rows, N_CLASSES), 1)
    onehot = cls_lane == cls
    before = lax.broadcasted_iota(jnp.int32, (rows, rows), 1) < lax.broadcasted_iota(jnp.int32, (rows, rows), 0)
    prefix = jnp.dot(before.astype(BF16), onehot.astype(BF16), preferred_element_type=F32)
    rank = jnp.sum(jnp.where(onehot, prefix + running[0:1, :], 0.0), axis=-1, keepdims=True)
    running[0:1, :] = running[0:1, :] + jnp.sum(onehot.astype(F32), axis=0, keepdims=True)
    cnt_ref[...] = jnp.broadcast_to(running[0:1, :], cnt_ref.shape)

    lane = lax.broadcasted_iota(jnp.int32, (rows, LANES), 1)
    rt_ref[...] = jnp.where(lane == 0, cls, jnp.where(lane == 1, rank.astype(jnp.int32), 0))
    hp_ref[:, 0:D_MODEL] = hn
    hp_ref[:, D_MODEL:] = jnp.where(lane == 0, w_lo, jnp.where(lane == 1, w_hi, 0.0))


def _outproj(x2, ym, yg, yr, wo_m, wo_g, wo_r, gain, w_router, router_bias):
    n = x2.shape[0]
    row = lambda w: pl.BlockSpec((OUT_TILE, w), lambda i: (i, 0))
    full = lambda a, b: pl.BlockSpec((a, b), lambda i: (0, 0))
    return pl.pallas_call(
        _outproj_kernel,
        grid=(n // OUT_TILE,),
        in_specs=[row(D_MODEL), row(MLSTM_W), row(GDN_W), row(RET_W),
                  full(MLSTM_W, D_MODEL), full(GDN_W, D_MODEL), full(RET_W, D_MODEL),
                  full(1, D_MODEL), full(D_MODEL, LANES), full(1, LANES)],
        out_specs=[row(D_MODEL), row(HP_WIDTH), row(LANES), full(8, N_CLASSES)],
        out_shape=[jax.ShapeDtypeStruct((n, D_MODEL), F32),
                   jax.ShapeDtypeStruct((n, HP_WIDTH), F32),
                   jax.ShapeDtypeStruct((n, LANES), jnp.int32),
                   jax.ShapeDtypeStruct((8, N_CLASSES), F32)],
        scratch_shapes=[pltpu.VMEM((8, N_CLASSES), F32)],
        compiler_params=_cparams(("arbitrary",)),
        name="outproj_norm_router",
    )(x2, ym, yg, yr, wo_m, wo_g, wo_r, gain, w_router, router_bias)


def _gdn_kernel(x_ref, gt_ref, z_ref, conv_ref, prm_ref, gain_ref, y_ref,
                state, u_s, wq_s, qk_s, kt_s, sd_s, cbuf):
    tb = x_ref.shape[0]
    first = pl.program_id(1) == 0

    @pl.when(first)
    def _():
        for ref in (state, u_s, wq_s, qk_s, kt_s, sd_s):
            ref[...] = jnp.zeros_like(ref)
        cbuf[0:8, :] = jnp.zeros((8, cbuf.shape[1]), F32)

    gt = gt_ref[...]
    beta_all = _sigmoid(gt)
    g_all = -jnp.exp(prm_ref[0:1, :]) * _softplus(gt + prm_ref[1:2, :])
    gc_all = _chunk_cumsum(g_all)

    lane, row = _lane_masks(CHUNK)
    m_a = lane < HEAD_DIM
    lane_h = lane % HEAD_DIM
    eye = lane_h == row
    tril = lane_h <= row
    strict = lane_h < row
    eye_f = eye.astype(F32)
    lane2 = lax.broadcasted_iota(jnp.int32, (LANES, LANES), 1)
    row2 = lax.broadcasted_iota(jnp.int32, (LANES, LANES), 0)
    bd_mask = (lane2 < HEAD_DIM) == (row2 < HEAD_DIM)
    lane_t = lax.broadcasted_iota(jnp.int32, (tb, LANES), 1)
    m_a_t = lane_t < HEAD_DIM

    n_pairs = GDN_HEADS // 2
    n_chunks = tb // CHUNK
    units = [(p, c) for p in range(n_pairs) for c in range(n_chunks)]

    def unit_index(p, c):
        return p * n_chunks + c

    prev = {}
    for p, c in units:
        i = unit_index(p, c)
        prev[p, c] = (u_s[i], wq_s[i], qk_s[i], kt_s[i], sd_s[i, 0:1, :])
    s_bd = [state[p] for p in range(n_pairs)]

    def recurrence(c):
        rs = slice(c * CHUNK, (c + 1) * CHUNK)
        for p in range(n_pairs):
            cs = slice(p * LANES, (p + 1) * LANES)
            u_mat, wq, qk, k_dec_t, s_decay = prev[p, c]
            ws_qs = jnp.dot(wq, s_bd[p].astype(BF16), preferred_element_type=F32)
            v_new = u_mat - ws_qs[:CHUNK]
            o = ws_qs[CHUNK:] + jnp.dot(qk, _block_diag(v_new, m_a).astype(BF16), preferred_element_type=F32)
            v_pad = jnp.concatenate([v_new, jnp.zeros_like(v_new)], axis=0).astype(BF16)
            s_bd[p] = s_bd[p] * s_decay + jnp.where(
                bd_mask, jnp.dot(k_dec_t, v_pad, preferred_element_type=F32), 0.0)
            ms = _half_sum(o * o, m_a) * (1.0 / HEAD_DIM)
            y = o * lax.rsqrt(ms + EPS) * gain_ref[:, cs] * _silu(z_ref[rs, cs])
            y_ref[rs, cs] = y.astype(y_ref.dtype)

    recurrence(0)
    qkv = _silu(_causal_conv(x_ref[:, 0:3 * GDN_W], cbuf, conv_ref))
    pre = {}
    for p in range(n_pairs):
        q_t = qkv[:, p * LANES:(p + 1) * LANES]
        k_t = qkv[:, GDN_W + p * LANES:GDN_W + (p + 1) * LANES]
        v_t = qkv[:, 2 * GDN_W + p * LANES:2 * GDN_W + (p + 1) * LANES]
        q_t = q_t * lax.rsqrt(_half_sum(q_t * q_t, m_a_t) + EPS) * (HEAD_DIM ** -0.5)
        k_t = k_t * lax.rsqrt(_half_sum(k_t * k_t, m_a_t) + EPS)
        beta_t = _col_form(beta_all, G_BETA + 2 * p, m_a_t)
        gc_t = _col_form(gc_all, G_A + 2 * p, m_a_t)
        for c in range(n_chunks):
            rs = slice(c * CHUNK, (c + 1) * CHUNK)
            q, k, v, beta, gc = q_t[rs], k_t[rs], v_t[rs], beta_t[rs], gc_t[rs]
            decay = jnp.where(tril, jnp.exp(gc - _row_form(gc, eye)), 0.0)
            kb = k * beta
            kkqk = _dot_nt(jnp.concatenate([kb, q], axis=0), _block_diag(k, m_a))
            egc = jnp.exp(gc)
            g_last = gc[CHUNK - 1:CHUNK, :]
            k_dec = k * jnp.exp(g_last - gc)
            pre[p, c] = dict(
                a=jnp.where(strict, kkqk[:CHUNK] * decay, 0.0), qk=kkqk[CHUNK:] * decay,
                vb=v * beta, kbe=kb * egc, qe=q * egc, s_decay=jnp.exp(g_last),
                k_dec_t=jnp.concatenate([k_dec, jnp.zeros_like(k_dec)], axis=0).T)

    pw = {u: _pmul(pre[u]["a"], pre[u]["a"], m_a) for u in units}
    t_inv = {u: eye_f - pre[u]["a"] for u in units}
    later_chunks = list(range(1, n_chunks))
    for step in range(4):
        if step % 2 == 0 and later_chunks:
            recurrence(later_chunks.pop(0))
        both = {u: _dot(jnp.concatenate([pw[u], t_inv[u]], axis=0), _block_diag(pw[u], m_a)) for u in units}
        pw = {u: both[u][:CHUNK] for u in units}
        t_inv = {u: t_inv[u] + both[u][CHUNK:] for u in units}
    for c in later_chunks:
        recurrence(c)
    t_inv = {u: t_inv[u] + _pmul(t_inv[u], pw[u], m_a) for u in units}
    for p, c in units:
        d = pre[p, c]
        uw = _dot(t_inv[p, c], jnp.concatenate([_block_diag(d["vb"], m_a), _block_diag(d["kbe"], m_a)], axis=1))
        i = unit_index(p, c)
        u_s[i] = uw[:, :LANES]
        wq_s[i] = jnp.concatenate([uw[:, LANES:], d["qe"]], axis=0).astype(BF16)
        qk_s[i] = d["qk"].astype(BF16)
        kt_s[i] = d["k_dec_t"].astype(BF16)
        sd_s[i] = jnp.broadcast_to(d["s_decay"], (8, LANES))
    for p in range(n_pairs):
        state[p] = s_bd[p]


def _ret_kernel(x_ref, cos_ref, sin_ref, gain_ref, y_ref, state):
    tb = x_ref.shape[0]

    @pl.when(pl.program_id(1) == 0)
    def _():
        state[...] = jnp.zeros_like(state)

    lane, row = _lane_masks(CHUNK)
    m_a = lane < HEAD_DIM
    lane_h = lane % HEAD_DIM
    tril = lane_h <= row
    lane2 = lax.broadcasted_iota(jnp.int32, (LANES, LANES), 1)
    row2 = lax.broadcasted_iota(jnp.int32, (LANES, LANES), 0)
    bd_mask = (lane2 < HEAD_DIM) == (row2 < HEAD_DIM)
    lane_t = lax.broadcasted_iota(jnp.int32, (tb, LANES), 1)
    first_half = (lane_t % HEAD_DIM) < (HEAD_DIM // 2)
    cos = cos_ref[...]
    sin = sin_ref[...]
    rowf = row.astype(F32)
    pos_diff = (row - lane_h).astype(F32)

    def rope(x):
        swapped = jnp.where(first_half, pltpu.roll(x, LANES - HEAD_DIM // 2, axis=1),
                            pltpu.roll(x, HEAD_DIM // 2, axis=1))
        return x * cos + swapped * sin

    n_pairs = RET_HEADS // 2
    n_chunks = tb // CHUNK
    pre = {}
    chunk_decay = []
    for p in range(n_pairs):
        lg_a = math.log(1.0 - 2.0 ** (-5.0 - 2 * p))
        lg_b = math.log(1.0 - 2.0 ** (-5.0 - (2 * p + 1)))
        lg = jnp.where(m_a, lg_a, lg_b)
        decay = jnp.where(tril, jnp.exp(pos_diff * lg), 0.0)
        q_scale = jnp.exp(lg * (rowf + 1.0))
        k_scale = jnp.exp(lg * (CHUNK - 1.0 - rowf))
        chunk_decay.append(jnp.exp(lg[0:1, :] * CHUNK))
        q_t = rope(x_ref[:, p * LANES:(p + 1) * LANES])
        k_t = rope(x_ref[:, RET_W + p * LANES:RET_W + (p + 1) * LANES]) * (HEAD_DIM ** -0.5)
        for c in range(n_chunks):
            rs = slice(c * CHUNK, (c + 1) * CHUNK)
            q, k = q_t[rs], k_t[rs]
            v = x_ref[rs, 2 * RET_W + p * LANES:2 * RET_W + (p + 1) * LANES]
            scores = _dot_nt(q, _block_diag(k, m_a)) * decay
            pre[p, c] = (_pmul(scores, v, m_a), q * q_scale, _outer_state(k * k_scale, v, bd_mask))

    s_bd = [state[p] for p in range(n_pairs)]
    for c in range(n_chunks):
        rs = slice(c * CHUNK, (c + 1) * CHUNK)
        for p in range(n_pairs):
            cs = slice(p * LANES, (p + 1) * LANES)
            o_intra, q_dec, kv = pre[p, c]
            o = o_intra + _dot(q_dec, s_bd[p])
            s_bd[p] = chunk_decay[p] * s_bd[p] + kv
            ms = _half_sum(o * o, m_a) * (1.0 / HEAD_DIM)
            gate = x_ref[rs, 3 * RET_W + p * LANES:3 * RET_W + (p + 1) * LANES]
            y = o * lax.rsqrt(ms + EPS) * gain_ref[:, cs] * _silu(gate)
            y_ref[rs, cs] = y.astype(y_ref.dtype)
    for p in range(n_pairs):
        state[p] = s_bd[p]


def _mlstm_kernel(x_ref, gt_ref, conv_ref, prm_ref, gain_ref, y_ref, c_state, nm_state, cbuf):
    tb = x_ref.shape[0]
    first = pl.program_id(1) == 0

    @pl.when(first)
    def _():
        c_state[...] = jnp.zeros_like(c_state)
        nm_state[...] = jnp.zeros_like(nm_state)
        cbuf[0:8, :] = jnp.zeros((8, cbuf.shape[1]), F32)

    qk_all = _silu(_causal_conv(x_ref[:, 0:2 * MLSTM_W], cbuf, conv_ref))
    gt = gt_ref[...]
    log_i_all = gt + prm_ref[0:1, :]
    f_pre = gt + prm_ref[1:2, :]
    log_f_all = jnp.minimum(f_pre, 0.0) - jnp.log(1.0 + jnp.exp(-jnp.abs(f_pre)))
    bc_all = _chunk_cumsum(log_f_all)

    lane, row = _lane_masks(CHUNK)
    m_a = lane < HEAD_DIM
    lane_h = lane % HEAD_DIM
    eye = lane_h == row
    tril = lane_h <= row
    lane2 = lax.broadcasted_iota(jnp.int32, (LANES, LANES), 1)
    row2 = lax.broadcasted_iota(jnp.int32, (LANES, LANES), 0)
    bd_mask = (lane2 < HEAD_DIM) == (row2 < HEAD_DIM)
    lane_t = lax.broadcasted_iota(jnp.int32, (tb, LANES), 1)
    m_a_t = lane_t < HEAD_DIM

    n_pairs = MLSTM_HEADS // 2
    n_chunks = tb // CHUNK
    units = [(p, c) for p in range(n_pairs) for c in range(n_chunks)]
    gates = {}
    for p in range(n_pairs):
        b_t = _col_form(bc_all, M_F + 2 * p, m_a_t)
        i_t = _col_form(log_i_all, M_I + 2 * p, m_a_t)
        for c in range(n_chunks):
            rs = slice(c * CHUNK, (c + 1) * CHUNK)
            gates[p, c] = (b_t[rs], i_t[rs])
    qkv = {(p, c): (qk_all[c * CHUNK:(c + 1) * CHUNK, p * LANES:(p + 1) * LANES] * (HEAD_DIM ** -0.5),
                    qk_all[c * CHUNK:(c + 1) * CHUNK, MLSTM_W + p * LANES:MLSTM_W + (p + 1) * LANES],
                    x_ref[c * CHUNK:(c + 1) * CHUNK, 2 * MLSTM_W + p * LANES:2 * MLSTM_W + (p + 1) * LANES])
           for p, c in units}
    qk = {u: _dot_nt(qkv[u][0], _block_diag(qkv[u][1], m_a)) for u in units}
    log_d = {u: jnp.where(tril, gates[u][0] - _row_form(gates[u][0], eye) + _row_form(gates[u][1], eye), -jnp.inf)
             for u in units}
    m_intra = {u: _half_max(log_d[u], m_a) for u in units}
    a_c = {u: gates[u][0][CHUNK - 1:CHUNK, :] - gates[u][0] + gates[u][1] for u in units}
    a_max = {u: jnp.max(a_c[u], axis=0, keepdims=True) for u in units}
    m_in, m_out = {}, {}
    for p in range(n_pairs):
        m_row = nm_state[p, 1:2, :]
        for c in range(n_chunks):
            m_in[p, c] = m_row
            m_row = jnp.maximum(gates[p, c][0][CHUNK - 1:CHUNK, :] + m_row, a_max[p, c])
            m_out[p, c] = m_row
    m_t = {u: jnp.maximum(gates[u][0] + m_in[u], m_intra[u]) for u in units}
    inter = {u: jnp.exp(gates[u][0] + m_in[u] - m_t[u]) for u in units}
    wmat = {u: jnp.where(tril, jnp.exp(log_d[u] - m_t[u]), 0.0) * qk[u] for u in units}
    kw = {u: qkv[u][1] * jnp.exp(a_c[u] - m_out[u]) for u in units}
    dec = {u: jnp.exp(gates[u][0][CHUNK - 1:CHUNK, :] + m_in[u] - m_out[u]) for u in units}
    num_intra = {u: _pmul(wmat[u], qkv[u][2], m_a) for u in units}
    den_intra = {u: _half_sum(wmat[u], m_a) for u in units}
    kv = {u: _outer_state(kw[u], qkv[u][2], bd_mask) for u in units}
    kn = {u: jnp.sum(kw[u], axis=0, keepdims=True) for u in units}
    c_in, n_in = {}, {}
    for p in range(n_pairs):
        c_bd, n_row = c_state[p], nm_state[p, 0:1, :]
        for c in range(n_chunks):
            c_in[p, c], n_in[p, c] = c_bd, n_row
            c_bd = dec[p, c] * c_bd + kv[p, c]
            n_row = dec[p, c] * n_row + kn[p, c]
        c_state[p] = c_bd
        nm_state[p, 0:1, :] = n_row
        nm_state[p, 1:2, :] = m_out[p, n_chunks - 1]
    num = {u: inter[u] * _dot(qkv[u][0], c_in[u]) + num_intra[u] for u in units}
    den = {u: inter[u] * _half_sum(qkv[u][0] * n_in[u], m_a) + den_intra[u] for u in units}
    h = {u: num[u] / jnp.maximum(jnp.abs(den[u]), jnp.exp(-m_t[u])) for u in units}
    ms = {u: _half_sum(h[u] * h[u], m_a) * (1.0 / HEAD_DIM) for u in units}
    for p, c in units:
        rs = slice(c * CHUNK, (c + 1) * CHUNK)
        cs = slice(p * LANES, (p + 1) * LANES)
        o_logit = x_ref[rs, 3 * MLSTM_W + p * LANES:3 * MLSTM_W + (p + 1) * LANES]
        y = _sigmoid(o_logit) * (h[p, c] * lax.rsqrt(ms[p, c] + EPS) * gain_ref[:, cs])
        y_ref[rs, cs] = y.astype(y_ref.dtype)


def _mixer_call(kernel, name, proj3, sec_block, sec_width, out_width, extra_inputs, extra_specs, scratch):
    b, t, _ = proj3.shape
    tb = min(TIME_BLOCK, t)
    return pl.pallas_call(
        kernel,
        grid=(b, t // tb),
        in_specs=[pl.BlockSpec((None, tb, sec_width), lambda i, j: (i, j, sec_block))] + extra_specs(tb),
        out_specs=pl.BlockSpec((None, tb, out_width), lambda i, j: (i, j, 0)),
        out_shape=jax.ShapeDtypeStruct((b, t, out_width), BF16),
        scratch_shapes=scratch(tb),
        compiler_params=_cparams(("parallel", "arbitrary")),
        name=name,
    )(proj3, *extra_inputs)


def _full2(a, b):
    return pl.BlockSpec((a, b), lambda i, j: (0, 0))


def _gdn_mixer(proj3, gates3, conv_w, prm, gain):
    b, t, _ = proj3.shape
    tb = min(TIME_BLOCK, t)
    n_blocks = t // tb
    n_units = (GDN_HEADS // 2) * (tb // CHUNK)
    cur = lambda i, j: (i, jnp.minimum(j, n_blocks - 1))
    prv = lambda i, j: (i, jnp.maximum(j - 1, 0))
    z_block = (SEC_M + 3 * GDN_W) // GDN_W
    assert z_block * GDN_W == SEC_M + 3 * GDN_W
    return pl.pallas_call(
        _gdn_kernel,
        grid=(b, n_blocks + 1),
        in_specs=[pl.BlockSpec((None, tb, SEC_G), lambda i, j: cur(i, j) + (1,)),
                  pl.BlockSpec((None, tb, LANES), lambda i, j: cur(i, j) + (0,)),
                  pl.BlockSpec((None, tb, GDN_W), lambda i, j: prv(i, j) + (z_block,)),
                  _full2(CONV_K, 3 * GDN_W), _full2(8, LANES), _full2(1, GDN_W)],
        out_specs=pl.BlockSpec((None, tb, GDN_W), lambda i, j: prv(i, j) + (0,)),
        out_shape=jax.ShapeDtypeStruct((b, t, GDN_W), BF16),
        scratch_shapes=[pltpu.VMEM((GDN_HEADS // 2, LANES, LANES), F32),
                        pltpu.VMEM((n_units, CHUNK, LANES), F32),
                        pltpu.VMEM((n_units, 2 * CHUNK, LANES), BF16),
                        pltpu.VMEM((n_units, CHUNK, LANES), BF16),
                        pltpu.VMEM((n_units, LANES, LANES), BF16),
                        pltpu.VMEM((n_units, 8, LANES), F32),
                        pltpu.VMEM((tb + 8, 3 * GDN_W), F32)],
        compiler_params=_cparams(("parallel", "arbitrary")),
        name="gdn_mixer",
    )(proj3, gates3, proj3, conv_w, prm, gain)


def _mlstm_mixer(proj3, gates3, conv_w, prm, gain):
    return _mixer_call(
        _mlstm_kernel, "mlstm_mixer", proj3, 0, SEC_M, MLSTM_W, (gates3, conv_w, prm, gain),
        lambda tb: [pl.BlockSpec((None, tb, LANES), lambda i, j: (i, j, 0)),
                    _full2(CONV_K, 2 * MLSTM_W), _full2(8, LANES), _full2(1, MLSTM_W)],
        lambda tb: [pltpu.VMEM((MLSTM_HEADS // 2, LANES, LANES), F32),
                    pltpu.VMEM((MLSTM_HEADS // 2, 8, LANES), F32),
                    pltpu.VMEM((tb + 8, 2 * MLSTM_W), F32)])


def _ret_mixer(proj3, cos_tab, sin_tab, gain):
    return _mixer_call(
        _ret_kernel, "retention_mixer", proj3, 3, SEC_R, RET_W, (cos_tab, sin_tab, gain),
        lambda tb: [pl.BlockSpec((tb, LANES), lambda i, j: (j, 0)),
                    pl.BlockSpec((tb, LANES), lambda i, j: (j, 0)),
                    _full2(1, RET_W)],
        lambda tb: [pltpu.VMEM((RET_HEADS // 2, LANES, LANES), F32)])


DISPATCH_TILE = 512


def _dispatch_kernel(pos_ref, src_ref, zero_ref, out_ref, sem):
    del zero_ref
    rows = src_ref.shape[0]
    base = pl.program_id(0) * rows

    def issue(pair, carry):
        for k in range(2):
            r = 2 * pair + k
            pltpu.make_async_copy(src_ref.at[pl.ds(r, 1)], out_ref.at[pl.ds(pos_ref[base + r], 1)],
                                  sem).start(priority=k)
        return carry

    lax.fori_loop(0, rows // 2, issue, 0, unroll=4)
    pltpu.make_async_copy(src_ref, out_ref.at[pl.ds(0, rows)], sem).wait()


def _dispatch(src, pos, n_rows):
    n, width = src.shape
    tile = min(DISPATCH_TILE, n)
    zeros = jnp.zeros((n_rows, width), src.dtype)
    return pl.pallas_call(
        _dispatch_kernel,
        grid_spec=pltpu.PrefetchScalarGridSpec(
            num_scalar_prefetch=1,
            grid=(n // tile,),
            in_specs=[pl.BlockSpec((tile, width), lambda i, p: (i, 0)), pl.BlockSpec(memory_space=pl.ANY)],
            out_specs=pl.BlockSpec(memory_space=pl.ANY),
            scratch_shapes=[pltpu.SemaphoreType.DMA(())],
        ),
        out_shape=jax.ShapeDtypeStruct((n_rows, width), src.dtype),
        input_output_aliases={2: 0},
        compiler_params=_cparams(("arbitrary",)),
        name="moe_dispatch",
    )(pos, src, zeros)


def _ffn_kernel(layer, plan_ref, used_ref, x_ref, wg_hbm, wu_hbm, wd_hbm, y_ref, wg_buf, wu_buf, wd_buf, sems):
    i = pl.program_id(0)
    used = i < used_ref[0]
    group, slot, starts_group, next_group = plan_ref[0, i], plan_ref[1, i], plan_ref[2, i], plan_ref[3, i]

    def group_copies(g, to_slot):
        lo = pl.multiple_of(g * EXPERTS_PER_GROUP, EXPERTS_PER_GROUP)
        experts = pl.ds(lo, EXPERTS_PER_GROUP)
        return [pltpu.make_async_copy(hbm.at[layer, experts], buf.at[to_slot], sems.at[to_slot])
                for hbm, buf in ((wg_hbm, wg_buf), (wu_hbm, wu_buf), (wd_hbm, wd_buf))]

    @pl.when(used & (i == 0))
    def _():
        for cp in group_copies(group, slot):
            cp.start()

    @pl.when(used & (starts_group == 1) & (next_group >= 0))
    def _():
        for cp in group_copies(next_group, 1 - slot):
            cp.start()

    @pl.when(used & (starts_group == 1))
    def _():
        for cp in group_copies(group, slot):
            cp.wait()

    @pl.when(jnp.logical_not(used))
    def _():
        y_ref[...] = jnp.zeros_like(y_ref)

    @pl.when(used)
    def _():
        e1, e2 = plan_ref[4, i], plan_ref[5, i]
        x = x_ref[:, 0:D_MODEL].astype(BF16)
        gate1 = jnp.dot(x, wg_buf[slot, e1], preferred_element_type=F32)
        gate2 = jnp.dot(x, wg_buf[slot, e2], preferred_element_type=F32)
        up1 = jnp.dot(x, wu_buf[slot, e1], preferred_element_type=F32)
        up2 = jnp.dot(x, wu_buf[slot, e2], preferred_element_type=F32)
        hid1 = (_silu(gate1) * up1).astype(BF16)
        hid2 = (_silu(gate2) * up2).astype(BF16)
        y1 = jnp.dot(hid1, wd_buf[slot, e1], preferred_element_type=F32)
        y2 = jnp.dot(hid2, wd_buf[slot, e2], preferred_element_type=F32)
        rw = x_ref[:, D_MODEL:]
        y_ref[...] = rw[:, 0:1] * y1 + rw[:, 1:2] * y2


def _expert_ffn(xs, tile_plan, n_used, wg, wu, wd, layer):
    r = xs.shape[0]
    return pl.pallas_call(
        functools.partial(_ffn_kernel, layer),
        grid_spec=pltpu.PrefetchScalarGridSpec(
            num_scalar_prefetch=2,
            grid=(r // MOE_TILE,),
            in_specs=[pl.BlockSpec((MOE_TILE, HP_WIDTH), lambda i, plan, nu: (jnp.minimum(i, nu[0] - 1), 0)),
                      pl.BlockSpec(memory_space=pl.ANY), pl.BlockSpec(memory_space=pl.ANY),
                      pl.BlockSpec(memory_space=pl.ANY)],
            out_specs=pl.BlockSpec((MOE_TILE, D_MODEL), lambda i, plan, nu: (i, 0)),
            scratch_shapes=[pltpu.VMEM((2, EXPERTS_PER_GROUP, D_MODEL, D_EXPERT), BF16),
                            pltpu.VMEM((2, EXPERTS_PER_GROUP, D_MODEL, D_EXPERT), BF16),
                            pltpu.VMEM((2, EXPERTS_PER_GROUP, D_EXPERT, D_MODEL), BF16),
                            pltpu.SemaphoreType.DMA((2,))],
        ),
        out_shape=jax.ShapeDtypeStruct((r, D_MODEL), F32),
        compiler_params=_cparams(("arbitrary",)),
        name="expert_pair_ffn",
    )(tile_plan, n_used, xs, wg, wu, wd)


def _combine_kernel(final_norm, pos_ref, x_ref, gain_ref, ys_ref, o_ref, buf, sems):
    rows = o_ref.shape[0]
    i = pl.program_id(0)
    slot = i & 1

    def gather_tile(tile, to_slot):
        def issue(pair, carry):
            for k in range(2):
                r = 2 * pair + k
                pltpu.make_async_copy(ys_ref.at[pl.ds(pos_ref[tile * rows + r], 1)],
                                      buf.at[to_slot, pl.ds(r, 1)], sems.at[to_slot]).start(priority=k)
            return carry

        lax.fori_loop(0, rows // 2, issue, 0, unroll=4)

    @pl.when(i == 0)
    def _():
        gather_tile(0, 0)

    @pl.when(i + 1 < pl.num_programs(0))
    def _():
        gather_tile(i + 1, 1 - slot)

    pltpu.make_async_copy(ys_ref.at[pl.ds(0, rows)], buf.at[slot], sems.at[slot]).wait()
    out = x_ref[...] + buf[slot]
    if final_norm:
        ms = jnp.mean(out * out, axis=-1, keepdims=True)
        out = out * lax.rsqrt(ms + EPS) * gain_ref[...]
    o_ref[...] = out


def _combine(x2, ys, pos, gain, final_norm):
    n = x2.shape[0]
    return pl.pallas_call(
        functools.partial(_combine_kernel, final_norm),
        grid_spec=pltpu.PrefetchScalarGridSpec(
            num_scalar_prefetch=1,
            grid=(n // ROW_TILE,),
            in_specs=[pl.BlockSpec((ROW_TILE, D_MODEL), lambda i, p: (i, 0)),
                      pl.BlockSpec((1, D_MODEL), lambda i, p: (0, 0)),
                      pl.BlockSpec(memory_space=pl.ANY)],
            out_specs=pl.BlockSpec((ROW_TILE, D_MODEL), lambda i, p: (i, 0)),
            scratch_shapes=[pltpu.VMEM((2, ROW_TILE, D_MODEL), F32), pltpu.SemaphoreType.DMA((2,))],
        ),
        out_shape=jax.ShapeDtypeStruct((n, D_MODEL), F32),
        compiler_params=_cparams(("arbitrary",)),
        name="moe_combine",
    )(pos, x2, gain, ys)


def _moe_kernel(layer, n_tok, plan_ref, used_ref, src_ref, dst_ref, hn_hbm, wg_hbm, wu_hbm, wd_hbm, y_hbm,
                xbuf, ybuf, wg_buf, wu_buf, wd_buf, gsem, ssem, wsem):
    i = pl.program_id(0)
    n_used = used_ref[0]
    used = i < n_used
    slot = i & 1
    group, wslot, starts_group, next_group = plan_ref[0, i], plan_ref[1, i], plan_ref[2, i], plan_ref[3, i]

    quarter = MOE_TILE // 4

    def issue_gather(tile, to_slot, part=None):
        base = tile * MOE_TILE
        for j in range(MOE_TILE) if part is None else range(part * quarter, (part + 1) * quarter):
            pltpu.make_async_copy(hn_hbm.at[pl.ds(src_ref[base + j], 1)], xbuf.at[to_slot, pl.ds(j, 1)],
                                  gsem.at[to_slot]).start()

    def issue_scatter(tile, from_slot, part=None):
        base = tile * MOE_TILE
        for j in range(MOE_TILE) if part is None else range(part * quarter, (part + 1) * quarter):
            pltpu.make_async_copy(ybuf.at[from_slot, pl.ds(j, 1)], y_hbm.at[pl.ds(dst_ref[base + j], 1)],
                                  ssem.at[from_slot]).start()

    def wait_gather(at_slot):
        pltpu.make_async_copy(hn_hbm.at[pl.ds(0, MOE_TILE)], xbuf.at[at_slot], gsem.at[at_slot]).wait()

    def wait_scatter(at_slot):
        pltpu.make_async_copy(ybuf.at[at_slot], y_hbm.at[pl.ds(0, MOE_TILE)], ssem.at[at_slot]).wait()

    def group_copies(g, to_slot):
        lo = pl.multiple_of(g * EXPERTS_PER_GROUP, EXPERTS_PER_GROUP)
        experts = pl.ds(lo, EXPERTS_PER_GROUP)
        return [pltpu.make_async_copy(hbm.at[layer, experts], buf.at[to_slot], wsem.at[to_slot])
                for hbm, buf in ((wg_hbm, wg_buf), (wu_hbm, wu_buf), (wd_hbm, wd_buf))]

    def compute(copy_work):
        pending = list(copy_work)

        def spread():
            if pending:
                pending.pop(0)()

        e1, e2 = plan_ref[4, i], plan_ref[5, i]
        x = xbuf[slot, :, 0:D_MODEL].astype(BF16)
        spread()
        gate1 = jnp.dot(x, wg_buf[wslot, e1], preferred_element_type=F32)
        spread()
        gate2 = jnp.dot(x, wg_buf[wslot, e2], preferred_element_type=F32)
        spread()
        up1 = jnp.dot(x, wu_buf[wslot, e1], preferred_element_type=F32)
        spread()
        up2 = jnp.dot(x, wu_buf[wslot, e2], preferred_element_type=F32)
        spread()
        hid1 = (_silu(gate1) * up1).astype(BF16)
        hid2 = (_silu(gate2) * up2).astype(BF16)
        spread()
        y1 = jnp.dot(hid1, wd_buf[wslot, e1], preferred_element_type=F32)
        spread()
        y2 = jnp.dot(hid2, wd_buf[wslot, e2], preferred_element_type=F32)
        spread()
        rw = xbuf[slot, :, D_MODEL:]
        ybuf[slot] = rw[:, 0:1] * y1 + rw[:, 1:2] * y2
        while pending:
            pending.pop(0)()

    @pl.when(i == 0)
    def _():
        ybuf[...] = jnp.zeros_like(ybuf)
        spare = [pltpu.make_async_copy(ybuf.at[s], y_hbm.at[pl.ds(n_tok + s * MOE_TILE, MOE_TILE)], ssem.at[s])
                 for s in range(2)]
        for cp in spare:
            cp.start()
        for cp in spare:
            cp.wait()
        issue_gather(0, 0)
        for cp in group_copies(group, wslot):
            cp.start()

    @pl.when(used & (starts_group == 1) & (next_group >= 0))
    def _():
        for cp in group_copies(next_group, 1 - wslot):
            cp.start()

    @pl.when(used & (starts_group == 1))
    def _():
        for cp in group_copies(group, wslot):
            cp.wait()

    @pl.when(used & (i >= 2))
    def _():
        wait_scatter(slot)

    @pl.when(used & (i == 0))
    def _():
        wait_gather(0)
        compute([functools.partial(issue_gather, 1, 1, part) for part in range(4)])

    @pl.when(used & (i >= 1))
    def _():
        wait_gather(slot)
        work = []
        for part in range(4):
            work.append(functools.partial(issue_gather, i + 1, 1 - slot, part))
            work.append(functools.partial(issue_scatter, i - 1, 1 - slot, part))
        compute(work)

    @pl.when(i == n_used)
    def _():
        wait_gather(slot)
        issue_scatter(i - 1, 1 - slot)
        wait_scatter(1 - slot)

    @pl.when((i == n_used) & (n_used >= 2))
    def _():
        wait_scatter(slot)


def _moe_ffn(hn, src_rows, dst_rows, tile_plan, n_used, wg, wu, wd, layer):
    n = hn.shape[0]
    n_tiles = tile_plan.shape[1]
    anywhere = pl.BlockSpec(memory_space=pl.ANY)
    return pl.pallas_call(
        functools.partial(_moe_kernel, layer, n),
        grid_spec=pltpu.PrefetchScalarGridSpec(
            num_scalar_prefetch=4,
            grid=(n_tiles,),
            in_specs=[anywhere, anywhere, anywhere, anywhere],
            out_specs=anywhere,
            scratch_shapes=[pltpu.VMEM((2, MOE_TILE, HP_WIDTH), F32),
                            pltpu.VMEM((2, MOE_TILE, D_MODEL), F32),
                            pltpu.VMEM((2, EXPERTS_PER_GROUP, D_MODEL, D_EXPERT), BF16),
                            pltpu.VMEM((2, EXPERTS_PER_GROUP, D_MODEL, D_EXPERT), BF16),
                            pltpu.VMEM((2, EXPERTS_PER_GROUP, D_EXPERT, D_MODEL), BF16),
                            pltpu.SemaphoreType.DMA((2,)), pltpu.SemaphoreType.DMA((2,)),
                            pltpu.SemaphoreType.DMA((2,))],
        ),
        out_shape=jax.ShapeDtypeStruct((n + 2 * MOE_TILE, D_MODEL), F32),
        compiler_params=_cparams(("arbitrary",)),
        name="moe_gather_ffn_scatter",
    )(tile_plan, n_used, src_rows, dst_rows, hn, wg, wu, wd)


def _add_norm_kernel(x_ref, y_ref, gain_ref, o_ref):
    out = x_ref[...] + y_ref[...]
    ms = jnp.mean(out * out, axis=-1, keepdims=True)
    o_ref[...] = out * lax.rsqrt(ms + EPS) * gain_ref[...]


def _add_norm(x2, y_tok, gain):
    n = x2.shape[0]
    return pl.pallas_call(
        _add_norm_kernel,
        grid=(n // OUT_TILE,),
        in_specs=[pl.BlockSpec((OUT_TILE, D_MODEL), lambda i: (i, 0)),
                  pl.BlockSpec((OUT_TILE, D_MODEL), lambda i: (i, 0)),
                  pl.BlockSpec((1, D_MODEL), lambda i: (0, 0))],
        out_specs=pl.BlockSpec((OUT_TILE, D_MODEL), lambda i: (i, 0)),
        out_shape=jax.ShapeDtypeStruct((n, D_MODEL), F32),
        compiler_params=_cparams(("parallel",)),
        name="residual_final_norm",
    )(x2, y_tok, gain)


def _plan(route, counts, n):
    cls, rank = route[:, 0], route[:, 1]
    counts = counts.astype(jnp.int32)
    tiles_per = (counts + MOE_TILE - 1) // MOE_TILE
    tile_end = jnp.cumsum(tiles_per)
    tile_start = tile_end - tiles_per
    class_ids = jnp.arange(N_CLASSES, dtype=jnp.int32)
    start_of = jnp.sum(jnp.where(cls[:, None] == class_ids[None, :], tile_start[None, :], 0), axis=1)
    pos = start_of * MOE_TILE + rank
    n_pairs = N_GROUPS * (EXPERTS_PER_GROUP * (EXPERTS_PER_GROUP - 1)) // 2
    n_tiles = n // MOE_TILE + n_pairs
    n_used = tile_end[-1]
    tile_ids = jnp.minimum(jnp.arange(n_tiles, dtype=jnp.int32), n_used - 1)
    tile_cls = jnp.sum((tile_end[None, :] <= tile_ids[:, None]).astype(jnp.int32), axis=1)
    per_group = EXPERTS_PER_GROUP * EXPERTS_PER_GROUP
    tile_group = tile_cls // per_group
    starts_group = jnp.concatenate([jnp.ones((1,), jnp.int32),
                                    (tile_group[1:] != tile_group[:-1]).astype(jnp.int32)])
    slot = (jnp.cumsum(starts_group) - 1) % 2
    present = jnp.sum(tiles_per.reshape(N_GROUPS, per_group), axis=1) > 0
    nxt = [jnp.int32(-1)] * N_GROUPS
    for g in range(N_GROUPS - 2, -1, -1):
        nxt[g] = jnp.where(present[g + 1], g + 1, nxt[g + 1])
    next_group = jnp.stack(nxt)[tile_group]
    tile_plan = jnp.stack([tile_group, slot, starts_group, next_group,
                           (tile_cls // EXPERTS_PER_GROUP) % EXPERTS_PER_GROUP, tile_cls % EXPERTS_PER_GROUP])
    rows = jnp.arange(n_tiles * MOE_TILE, dtype=jnp.int32)
    tok = jnp.full((n_tiles * MOE_TILE,), -1, jnp.int32).at[pos].set(jnp.arange(n, dtype=jnp.int32))
    src_rows = jnp.where(tok >= 0, tok, n - 1)
    dst_rows = jnp.where(tok >= 0, tok, n + rows % (2 * MOE_TILE))
    return src_rows, dst_rows, tile_plan.astype(jnp.int32), n_used.reshape(1)


def _rope_tables(t):
    inv_freq = ROPE_BASE ** (-jnp.arange(0, HEAD_DIM, 2, dtype=F32) / HEAD_DIM)
    ang = jnp.arange(t, dtype=F32)[:, None] * inv_freq[None, :]
    cos, sin = jnp.cos(ang), jnp.sin(ang)
    cos_tab = jnp.tile(cos, (1, 2 * LANES // HEAD_DIM))
    sin_tab = jnp.tile(jnp.concatenate([-sin, sin], axis=-1), (1, LANES // HEAD_DIM))
    return cos_tab, sin_tab


def _split_w_in(w):
    o = 0
    parts = {}
    for name, width in (("g_qkv", 3 * GDN_W), ("g_z", GDN_W), ("g_b", GDN_HEADS), ("g_a", GDN_HEADS),
                        ("r_q", RET_W), ("r_k", RET_W), ("r_v", RET_W), ("r_g", RET_W),
                        ("m_qk", 2 * MLSTM_W), ("m_v", MLSTM_W), ("m_o", MLSTM_W),
                        ("m_i", MLSTM_HEADS), ("m_f", MLSTM_HEADS)):
        parts[name] = w[:, o:o + width]
        o += width
    main = jnp.concatenate([parts[k] for k in ("m_qk", "m_v", "m_o", "g_qkv", "g_z", "r_q", "r_k", "r_v", "r_g")],
                           axis=1).astype(BF16)
    gate = jnp.concatenate([parts[k] for k in ("g_b", "g_a", "m_i", "m_f")], axis=1)
    gate = jnp.pad(gate, ((0, 0), (0, LANES - gate.shape[1]))).astype(BF16)
    return main, gate


def _lane_row(values, start):
    return jnp.zeros((LANES,), F32).at[start:start + values.shape[0]].set(values)


def kernel(x, norm_mix, w_in, gdn_conv, gdn_a_log, gdn_dt_bias, gdn_norm, ret_norm, mlstm_conv, mlstm_i_bias, mlstm_f_bias, mlstm_norm, w_out, norm_ffn, router_group, router_expert, router_bias, expert_gate, expert_up, expert_down, norm_final):
    b, t, d = x.shape
    n = b * t
    depth = w_in.shape[0]
    cos_tab, sin_tab = _rope_tables(t)
    wg_all, wu_all, wd_all = expert_gate.astype(BF16), expert_up.astype(BF16), expert_down.astype(BF16)
    x2 = x.reshape(n, d)
    y_tok = None
    for l in range(depth):
        w_main, w_gate = _split_w_in(w_in[l])
        x2, proj, gates = _inproj(x2, y_tok, norm_mix[l][None, :], w_main, w_gate)
        proj3 = proj.reshape(b, t, D_MAIN)
        gates3 = gates.reshape(b, t, LANES)
        gdn_prm = jnp.zeros((8, LANES), F32).at[0].set(_lane_row(gdn_a_log[l], G_A)).at[1].set(
            _lane_row(gdn_dt_bias[l], G_A))
        mlstm_prm = jnp.zeros((8, LANES), F32).at[0].set(_lane_row(mlstm_i_bias[l], M_I)).at[1].set(
            _lane_row(mlstm_f_bias[l], M_F))
        y_m = _mlstm_mixer(proj3, gates3, mlstm_conv[l], mlstm_prm, mlstm_norm[l][None, :])
        y_g = _gdn_mixer(proj3, gates3, gdn_conv[l], gdn_prm, gdn_norm[l][None, :])
        y_r = _ret_mixer(proj3, cos_tab, sin_tab, ret_norm[l][None, :])
        wo = w_out[l].astype(BF16)
        w_router = jnp.pad(jnp.concatenate([router_group[l], router_expert[l]], axis=1),
                           ((0, 0), (0, LANES - N_GROUPS - N_EXPERTS))).astype(BF16)
        rb = _lane_row(router_bias[l], N_GROUPS)[None, :]
        x2, h_packed, route, counts = _outproj(
            x2, y_m.reshape(n, MLSTM_W), y_g.reshape(n, GDN_W), y_r.reshape(n, RET_W),
            wo[GDN_W + RET_W:], wo[:GDN_W], wo[GDN_W:GDN_W + RET_W],
            norm_ffn[l][None, :], w_router, rb)
        src_rows, dst_rows, tile_plan, n_used = _plan(route, counts[0], n)
        y_tok = _moe_ffn(h_packed, src_rows, dst_rows, tile_plan, n_used, wg_all, wu_all, wd_all, l)
    return _add_norm(x2, y_tok, norm_final[None, :]).reshape(b, t, d)
```

```python
import functools
import math

import jax
import jax.numpy as jnp
from jax import lax
from jax.experimental import pallas as pl
from jax.experimental.pallas import tpu as pltpu

F32 = jnp.float32
BF16 = jnp.bfloat16

D_MODEL = 1024
HEAD_DIM = 64
CHUNK = 64
GDN_HEADS, RET_HEADS, MLSTM_HEADS = 6, 4, 6
GDN_W, RET_W, MLSTM_W = GDN_HEADS * HEAD_DIM, RET_HEADS * HEAD_DIM, MLSTM_HEADS * HEAD_DIM
CONV_K = 4
ROPE_BASE = 10000.0
N_GROUPS, EXPERTS_PER_GROUP = 4, 8
N_EXPERTS = N_GROUPS * EXPERTS_PER_GROUP
D_EXPERT = D_MODEL // 4
EPS = 1e-6
LANES = 128
SEC_M, SEC_G, SEC_R = 4 * MLSTM_W, 4 * GDN_W, 4 * RET_W
D_MAIN = SEC_M + SEC_G + SEC_R
G_BETA, G_A, M_I, M_F = 0, GDN_HEADS, 2 * GDN_HEADS, 2 * GDN_HEADS + MLSTM_HEADS

ROW_TILE = 256
OUT_TILE = 512
TIME_BLOCK = 256
MOE_TILE = 128
FFN_STEP_TILES = 2
VMEM_LIMIT = 56 * 1024 * 1024


def _cparams(sem):
    return pltpu.CompilerParams(dimension_semantics=sem, vmem_limit_bytes=VMEM_LIMIT)


def _silu(x):
    return x * (1.0 / (1.0 + jnp.exp(-x)))


def _sigmoid(x):
    return 1.0 / (1.0 + jnp.exp(-x))


def _softplus(x):
    return jnp.maximum(x, 0.0) + jnp.log(1.0 + jnp.exp(-jnp.abs(x)))


def _dot(a, b):
    return jnp.dot(a.astype(BF16), b.astype(BF16), preferred_element_type=F32)


def _dot_nt(a, b):
    return lax.dot_general(a.astype(BF16), b.astype(BF16), (((1,), (1,)), ((), ())),
                           preferred_element_type=F32)


def _lane_masks(rows):
    lane = lax.broadcasted_iota(jnp.int32, (rows, LANES), 1)
    row = lax.broadcasted_iota(jnp.int32, (rows, LANES), 0)
    return lane, row


def _half_sum(x, m_a):
    s_a = jnp.sum(jnp.where(m_a, x, 0.0), axis=-1, keepdims=True)
    s_b = jnp.sum(jnp.where(m_a, 0.0, x), axis=-1, keepdims=True)
    return jnp.where(m_a, s_a, s_b)


def _half_max(x, m_a):
    s_a = jnp.max(jnp.where(m_a, x, -jnp.inf), axis=-1, keepdims=True)
    s_b = jnp.max(jnp.where(m_a, -jnp.inf, x), axis=-1, keepdims=True)
    return jnp.where(m_a, s_a, s_b)


def _col_form(g, lane_a, m_a):
    rows = g.shape[0]
    ca = jnp.broadcast_to(g[:, lane_a:lane_a + 1], (rows, LANES))
    cb = jnp.broadcast_to(g[:, lane_a + 1:lane_a + 2], (rows, LANES))
    return jnp.where(m_a, ca, cb)


def _row_form(col, eye):
    return jnp.sum(jnp.where(eye, col, 0.0), axis=0, keepdims=True)


def _block_diag(y, m_a):
    return jnp.concatenate([jnp.where(m_a, y, 0.0), jnp.where(m_a, 0.0, y)], axis=0)


def _pmul(x, y, m_a):
    return _dot(x, _block_diag(y, m_a))


def _outer_state(k, v, bd_mask):
    zero = jnp.zeros_like(k)
    kt = jnp.concatenate([k, zero], axis=0).T
    vp = jnp.concatenate([v, zero], axis=0)
    return jnp.where(bd_mask, _dot(kt, vp), 0.0)


def _chunk_cumsum(x):
    rows = x.shape[0]
    r = lax.broadcasted_iota(jnp.int32, (rows, LANES), 0) % CHUNK
    s = 1
    while s < CHUNK:
        x = x + jnp.where(r >= s, pltpu.roll(x, s, axis=0), 0.0)
        s *= 2
    return x


def _causal_conv(raw, cbuf, w_ref):
    rows = raw.shape[0]
    cbuf[8:8 + rows, :] = raw
    acc = raw * w_ref[CONV_K - 1:CONV_K, :]
    for j in range(CONV_K - 1):
        off = 8 - (CONV_K - 1) + j
        acc = acc + cbuf[off:off + rows, :] * w_ref[j:j + 1, :]
    cbuf[0:8, :] = raw[rows - 8:rows, :]
    return acc


def _inproj_kernel(x_ref, gain_ref, w_ref, wg_ref, o_ref, og_ref):
    x = x_ref[...]
    ms = jnp.mean(x * x, axis=-1, keepdims=True)
    h = (x * lax.rsqrt(ms + EPS) * gain_ref[...]).astype(BF16)

    def proj(lo, hi):
        return jnp.dot(h, w_ref[:, lo:hi], preferred_element_type=F32)

    step = 512
    for c in range(D_MAIN // step):
        o_ref[:, c * step:(c + 1) * step] = proj(c * step, (c + 1) * step)
    og_ref[...] = jnp.dot(h, wg_ref[...], preferred_element_type=F32)


def _inproj(x2, gain, w_main, w_gate):
    n = x2.shape[0]
    full = lambda a, b: pl.BlockSpec((a, b), lambda i: (0, 0))
    return pl.pallas_call(
        _inproj_kernel,
        grid=(n // OUT_TILE,),
        in_specs=[
            pl.BlockSpec((OUT_TILE, D_MODEL), lambda i: (i, 0)),
            full(1, D_MODEL), full(D_MODEL, D_MAIN), full(D_MODEL, LANES),
        ],
        out_specs=[
            pl.BlockSpec((OUT_TILE, D_MAIN), lambda i: (i, 0)),
            pl.BlockSpec((OUT_TILE, LANES), lambda i: (i, 0)),
        ],
        out_shape=[jax.ShapeDtypeStruct((n, D_MAIN), F32), jax.ShapeDtypeStruct((n, LANES), F32)],
        compiler_params=_cparams(("parallel",)),
        name="norm_inproj",
    )(x2, gain, w_main, w_gate)


N_CLASSES = N_GROUPS * EXPERTS_PER_GROUP * EXPERTS_PER_GROUP
HP_WIDTH = D_MODEL + LANES


def _route_rows(lg):
    rows = lg.shape[0]
    lane = lax.broadcasted_iota(jnp.int32, (rows, LANES), 1)
    neg = -jnp.inf
    gl = jnp.where(lane < N_GROUPS, lg, neg)
    gm = jnp.max(gl, axis=-1, keepdims=True)
    gsum = jnp.sum(jnp.where(lane < N_GROUPS, jnp.exp(gl - gm), 0.0), axis=-1, keepdims=True)
    gidx = jnp.min(jnp.where(gl == gm, lane, LANES), axis=-1, keepdims=True)
    group_p = 1.0 / gsum
    in_group = (lane >= N_GROUPS) & (lane < N_GROUPS + N_EXPERTS) & (
        lax.shift_right_arithmetic(lane - N_GROUPS, 3) == gidx)
    el = jnp.where(in_group, lg, neg)
    m1 = jnp.max(el, axis=-1, keepdims=True)
    i1 = jnp.min(jnp.where(el == m1, lane, LANES), axis=-1, keepdims=True)
    el2 = jnp.where(lane == i1, neg, el)
    m2 = jnp.max(el2, axis=-1, keepdims=True)
    i2 = jnp.min(jnp.where(el2 == m2, lane, LANES), axis=-1, keepdims=True)
    e2 = jnp.exp(m2 - m1)
    w1 = group_p * (1.0 / (1.0 + e2))
    w2 = group_p * (e2 / (1.0 + e2))
    first_lower = i1 < i2
    base = N_GROUPS + gidx * EXPERTS_PER_GROUP
    lo = jnp.minimum(i1, i2) - base
    hi = jnp.maximum(i1, i2) - base
    cls = (gidx * EXPERTS_PER_GROUP + lo) * EXPERTS_PER_GROUP + hi
    return cls, jnp.where(first_lower, w1, w2), jnp.where(first_lower, w2, w1)


def _outproj_kernel(x_ref, ym_ref, yg_ref, yr_ref, wm_ref, wg_ref, wr_ref, gain_ref, wrt_ref, rb_ref,
                    xo_ref, hp_ref, rt_ref, cnt_ref, running):
    @pl.when(pl.program_id(0) == 0)
    def _():
        running[...] = jnp.zeros_like(running)

    acc = x_ref[...]
    acc = acc + jnp.dot(yg_ref[...], wg_ref[...], preferred_element_type=F32)
    acc = acc + jnp.dot(yr_ref[...], wr_ref[...], preferred_element_type=F32)
    acc = acc + jnp.dot(ym_ref[...], wm_ref[...], preferred_element_type=F32)
    xo_ref[...] = acc
    ms = jnp.mean(acc * acc, axis=-1, keepdims=True)
    hn = acc * lax.rsqrt(ms + EPS) * gain_ref[...]
    logits = jnp.dot(hn.astype(BF16), wrt_ref[...], preferred_element_type=F32) + rb_ref[...]
    cls, w_lo, w_hi = _route_rows(logits)

    rows = acc.shape[0]
    cls_lane = lax.broadcasted_iota(jnp.int32, (rows, N_CLASSES), 1)
    onehot = cls_lane == cls
    before = lax.broadcasted_iota(jnp.int32, (rows, rows), 1) < lax.broadcasted_iota(jnp.int32, (rows, rows), 0)
    prefix = jnp.dot(before.astype(BF16), onehot.astype(BF16), preferred_element_type=F32)
    rank = jnp.sum(jnp.where(onehot, prefix + running[0:1, :], 0.0), axis=-1, keepdims=True)
    running[0:1, :] = running[0:1, :] + jnp.sum(onehot.astype(F32), axis=0, keepdims=True)
    cnt_ref[...] = jnp.broadcast_to(running[0:1, :], cnt_ref.shape)

    lane = lax.broadcasted_iota(jnp.int32, (rows, LANES), 1)
    rt_ref[...] = jnp.where(lane == 0, cls, jnp.where(lane == 1, rank.astype(jnp.int32), 0))
    hp_ref[:, 0:D_MODEL] = hn
    hp_ref[:, D_MODEL:] = jnp.where(lane == 0, w_lo, jnp.where(lane == 1, w_hi, 0.0))


def _outproj(x2, ym, yg, yr, wo_m, wo_g, wo_r, gain, w_router, router_bias):
    n = x2.shape[0]
    row = lambda w: pl.BlockSpec((OUT_TILE, w), lambda i: (i, 0))
    full = lambda a, b: pl.BlockSpec((a, b), lambda i: (0, 0))
    return pl.pallas_call(
        _outproj_kernel,
        grid=(n // OUT_TILE,),
        in_specs=[row(D_MODEL), row(MLSTM_W), row(GDN_W), row(RET_W),
                  full(MLSTM_W, D_MODEL), full(GDN_W, D_MODEL), full(RET_W, D_MODEL),
                  full(1, D_MODEL), full(D_MODEL, LANES), full(1, LANES)],
        out_specs=[row(D_MODEL), row(HP_WIDTH), row(LANES), full(8, N_CLASSES)],
        out_shape=[jax.ShapeDtypeStruct((n, D_MODEL), F32),
                   jax.ShapeDtypeStruct((n, HP_WIDTH), F32),
                   jax.ShapeDtypeStruct((n, LANES), jnp.int32),
                   jax.ShapeDtypeStruct((8, N_CLASSES), F32)],
        scratch_shapes=[pltpu.VMEM((8, N_CLASSES), F32)],
        compiler_params=_cparams(("arbitrary",)),
        name="outproj_norm_router",
    )(x2, ym, yg, yr, wo_m, wo_g, wo_r, gain, w_router, router_bias)


def _gdn_kernel(x_ref, gt_ref, z_ref, conv_ref, prm_ref, gain_ref, y_ref,
                state, u_s, wq_s, qk_s, kt_s, sd_s, cbuf):
    tb = x_ref.shape[0]
    first = pl.program_id(1) == 0

    @pl.when(first)
    def _():
        for ref in (state, u_s, wq_s, qk_s, kt_s, sd_s):
            ref[...] = jnp.zeros_like(ref)
        cbuf[0:8, :] = jnp.zeros((8, cbuf.shape[1]), F32)

    gt = gt_ref[...]
    beta_all = _sigmoid(gt)
    g_all = -jnp.exp(prm_ref[0:1, :]) * _softplus(gt + prm_ref[1:2, :])
    gc_all = _chunk_cumsum(g_all)

    lane, row = _lane_masks(CHUNK)
    m_a = lane < HEAD_DIM
    lane_h = lane % HEAD_DIM
    eye = lane_h == row
    tril = lane_h <= row
    strict = lane_h < row
    eye_f = eye.astype(F32)
    lane2 = lax.broadcasted_iota(jnp.int32, (LANES, LANES), 1)
    row2 = lax.broadcasted_iota(jnp.int32, (LANES, LANES), 0)
    bd_mask = (lane2 < HEAD_DIM) == (row2 < HEAD_DIM)
    lane_t = lax.broadcasted_iota(jnp.int32, (tb, LANES), 1)
    m_a_t = lane_t < HEAD_DIM

    n_pairs = GDN_HEADS // 2
    n_chunks = tb // CHUNK
    units = [(p, c) for p in range(n_pairs) for c in range(n_chunks)]

    def unit_index(p, c):
        return p * n_chunks + c

    prev = {}
    for p, c in units:
        i = unit_index(p, c)
        prev[p, c] = (u_s[i], wq_s[i], qk_s[i], kt_s[i], sd_s[i, 0:1, :])
    s_bd = [state[p] for p in range(n_pairs)]

    def recurrence(c):
        rs = slice(c * CHUNK, (c + 1) * CHUNK)
        for p in range(n_pairs):
            cs = slice(p * LANES, (p + 1) * LANES)
            u_mat, wq, qk, k_dec_t, s_decay = prev[p, c]
            ws_qs = jnp.dot(wq, s_bd[p].astype(BF16), preferred_element_type=F32)
            v_new = u_mat - ws_qs[:CHUNK]
            o = ws_qs[CHUNK:] + jnp.dot(qk, _block_diag(v_new, m_a).astype(BF16), preferred_element_type=F32)
            v_pad = jnp.concatenate([v_new, jnp.zeros_like(v_new)], axis=0).astype(BF16)
            s_bd[p] = s_bd[p] * s_decay + jnp.where(
                bd_mask, jnp.dot(k_dec_t, v_pad, preferred_element_type=F32), 0.0)
            ms = _half_sum(o * o, m_a) * (1.0 / HEAD_DIM)
            y = o * lax.rsqrt(ms + EPS) * gain_ref[:, cs] * _silu(z_ref[rs, cs])
            y_ref[rs, cs] = y.astype(y_ref.dtype)

    recurrence(0)
    qkv = _silu(_causal_conv(x_ref[:, 0:3 * GDN_W], cbuf, conv_ref))
    ins = {}
    for p in range(n_pairs):
        q_t = qkv[:, p * LANES:(p + 1) * LANES]
        k_t = qkv[:, GDN_W + p * LANES:GDN_W + (p + 1) * LANES]
        v_t = qkv[:, 2 * GDN_W + p * LANES:2 * GDN_W + (p + 1) * LANES]
        q_t = q_t * lax.rsqrt(_half_sum(q_t * q_t, m_a_t) + EPS) * (HEAD_DIM ** -0.5)
        k_t = k_t * lax.rsqrt(_half_sum(k_t * k_t, m_a_t) + EPS)
        beta_t = _col_form(beta_all, G_BETA + 2 * p, m_a_t)
        gc_t = _col_form(gc_all, G_A + 2 * p, m_a_t)
        for c in range(n_chunks):
            rs = slice(c * CHUNK, (c + 1) * CHUNK)
            ins[p, c] = (q_t[rs], k_t[rs], v_t[rs], beta_t[rs], gc_t[rs])
    kkqk = {u: _dot_nt(jnp.concatenate([ins[u][1] * ins[u][3], ins[u][0]], axis=0), _block_diag(ins[u][1], m_a))
            for u in units}
    pre = {}
    for u in units:
        q, k, v, beta, gc = ins[u]
        decay = jnp.where(tril, jnp.exp(gc - _row_form(gc, eye)), 0.0)
        egc = jnp.exp(gc)
        g_last = gc[CHUNK - 1:CHUNK, :]
        k_dec = k * jnp.exp(g_last - gc)
        pre[u] = dict(
            a=jnp.where(strict, kkqk[u][:CHUNK] * decay, 0.0), qk=kkqk[u][CHUNK:] * decay,
            vb=v * beta, kbe=k * beta * egc, qe=q * egc, s_decay=jnp.exp(g_last),
            k_dec_t=jnp.concatenate([k_dec, jnp.zeros_like(k_dec)], axis=0).T)

    pw = {u: _pmul(pre[u]["a"], pre[u]["a"], m_a) for u in units}
    t_inv = {u: eye_f - pre[u]["a"] for u in units}
    later_chunks = list(range(1, n_chunks))
    for step in range(4):
        if step % 2 == 0 and later_chunks:
            recurrence(later_chunks.pop(0))
        both = {u: _dot(jnp.concatenate([pw[u], t_inv[u]], axis=0), _block_diag(pw[u], m_a)) for u in units}
        pw = {u: both[u][:CHUNK] for u in units}
        t_inv = {u: t_inv[u] + both[u][CHUNK:] for u in units}
    for c in later_chunks:
        recurrence(c)
    t_inv = {u: t_inv[u] + _pmul(t_inv[u], pw[u], m_a) for u in units}
    for p, c in units:
        d = pre[p, c]
        uw = _dot(t_inv[p, c], jnp.concatenate([_block_diag(d["vb"], m_a), _block_diag(d["kbe"], m_a)], axis=1))
        i = unit_index(p, c)
        u_s[i] = uw[:, :LANES]
        wq_s[i] = jnp.concatenate([uw[:, LANES:], d["qe"]], axis=0).astype(BF16)
        qk_s[i] = d["qk"].astype(BF16)
        kt_s[i] = d["k_dec_t"].astype(BF16)
        sd_s[i] = jnp.broadcast_to(d["s_decay"], (8, LANES))
    for p in range(n_pairs):
        state[p] = s_bd[p]


def _ret_kernel(x_ref, cos_ref, sin_ref, gain_ref, y_ref, state):
    tb = x_ref.shape[0]

    @pl.when(pl.program_id(1) == 0)
    def _():
        state[...] = jnp.zeros_like(state)

    lane, row = _lane_masks(CHUNK)
    m_a = lane < HEAD_DIM
    lane_h = lane % HEAD_DIM
    tril = lane_h <= row
    lane2 = lax.broadcasted_iota(jnp.int32, (LANES, LANES), 1)
    row2 = lax.broadcasted_iota(jnp.int32, (LANES, LANES), 0)
    bd_mask = (lane2 < HEAD_DIM) == (row2 < HEAD_DIM)
    lane_t = lax.broadcasted_iota(jnp.int32, (tb, LANES), 1)
    first_half = (lane_t % HEAD_DIM) < (HEAD_DIM // 2)
    cos = cos_ref[...]
    sin = sin_ref[...]
    rowf = row.astype(F32)
    pos_diff = (row - lane_h).astype(F32)

    def rope(x):
        swapped = jnp.where(first_half, pltpu.roll(x, LANES - HEAD_DIM // 2, axis=1),
                            pltpu.roll(x, HEAD_DIM // 2, axis=1))
        return x * cos + swapped * sin

    n_pairs = RET_HEADS // 2
    n_chunks = tb // CHUNK
    units = [(p, c) for p in range(n_pairs) for c in range(n_chunks)]
    chunk_decay, decay, q_scale, k_scale, qkv = [], [], [], [], {}
    for p in range(n_pairs):
        lg_a = math.log(1.0 - 2.0 ** (-5.0 - 2 * p))
        lg_b = math.log(1.0 - 2.0 ** (-5.0 - (2 * p + 1)))
        lg = jnp.where(m_a, lg_a, lg_b)
        decay.append(jnp.where(tril, jnp.exp(pos_diff * lg), 0.0))
        q_scale.append(jnp.exp(lg * (rowf + 1.0)))
        k_scale.append(jnp.exp(lg * (CHUNK - 1.0 - rowf)))
        chunk_decay.append(jnp.exp(lg[0:1, :] * CHUNK))
        q_t = rope(x_ref[:, p * LANES:(p + 1) * LANES])
        k_t = rope(x_ref[:, RET_W + p * LANES:RET_W + (p + 1) * LANES]) * (HEAD_DIM ** -0.5)
        for c in range(n_chunks):
            rs = slice(c * CHUNK, (c + 1) * CHUNK)
            qkv[p, c] = (q_t[rs], k_t[rs], x_ref[rs, 2 * RET_W + p * LANES:2 * RET_W + (p + 1) * LANES])
    scores = {(p, c): _dot_nt(qkv[p, c][0], _block_diag(qkv[p, c][1], m_a)) * decay[p] for p, c in units}
    o_intra = {u: _pmul(scores[u], qkv[u][2], m_a) for u in units}
    kv_all = {(p, c): _outer_state(qkv[p, c][1] * k_scale[p], qkv[p, c][2], bd_mask) for p, c in units}
    pre = {(p, c): (o_intra[p, c], qkv[p, c][0] * q_scale[p], kv_all[p, c]) for p, c in units}

    s_in = {}
    for p in range(n_pairs):
        s_bd = state[p]
        for c in range(n_chunks):
            s_in[p, c] = s_bd
            s_bd = chunk_decay[p] * s_bd + pre[p, c][2]
        state[p] = s_bd
    o_all = {u: pre[u][0] + _dot(pre[u][1], s_in[u]) for u in units}
    ms_all = {u: _half_sum(o_all[u] * o_all[u], m_a) * (1.0 / HEAD_DIM) for u in units}
    for p, c in units:
        rs = slice(c * CHUNK, (c + 1) * CHUNK)
        cs = slice(p * LANES, (p + 1) * LANES)
        gate = x_ref[rs, 3 * RET_W + p * LANES:3 * RET_W + (p + 1) * LANES]
        y = o_all[p, c] * lax.rsqrt(ms_all[p, c] + EPS) * gain_ref[:, cs] * _silu(gate)
        y_ref[rs, cs] = y.astype(y_ref.dtype)


def _mlstm_kernel(x_ref, gt_ref, conv_ref, prm_ref, gain_ref, y_ref, c_state, nm_state, cbuf):
    tb = x_ref.shape[0]
    first = pl.program_id(1) == 0

    @pl.when(first)
    def _():
        c_state[...] = jnp.zeros_like(c_state)
        nm_state[...] = jnp.zeros_like(nm_state)
        cbuf[0:8, :] = jnp.zeros((8, cbuf.shape[1]), F32)

    qk_all = _silu(_causal_conv(x_ref[:, 0:2 * MLSTM_W], cbuf, conv_ref))
    gt = gt_ref[...]
    log_i_all = gt + prm_ref[0:1, :]
    f_pre = gt + prm_ref[1:2, :]
    log_f_all = jnp.minimum(f_pre, 0.0) - jnp.log(1.0 + jnp.exp(-jnp.abs(f_pre)))
    bc_all = _chunk_cumsum(log_f_all)

    lane, row = _lane_masks(CHUNK)
    m_a = lane < HEAD_DIM
    lane_h = lane % HEAD_DIM
    eye = lane_h == row
    tril = lane_h <= row
    lane2 = lax.broadcasted_iota(jnp.int32, (LANES, LANES), 1)
    row2 = lax.broadcasted_iota(jnp.int32, (LANES, LANES), 0)
    bd_mask = (lane2 < HEAD_DIM) == (row2 < HEAD_DIM)
    lane_t = lax.broadcasted_iota(jnp.int32, (tb, LANES), 1)
    m_a_t = lane_t < HEAD_DIM

    n_pairs = MLSTM_HEADS // 2
    n_chunks = tb // CHUNK
    units = [(p, c) for p in range(n_pairs) for c in range(n_chunks)]
    gates = {}
    for p in range(n_pairs):
        b_t = _col_form(bc_all, M_F + 2 * p, m_a_t)
        i_t = _col_form(log_i_all, M_I + 2 * p, m_a_t)
        for c in range(n_chunks):
            rs = slice(c * CHUNK, (c + 1) * CHUNK)
            gates[p, c] = (b_t[rs], i_t[rs])
    qkv = {(p, c): (qk_all[c * CHUNK:(c + 1) * CHUNK, p * LANES:(p + 1) * LANES] * (HEAD_DIM ** -0.5),
                    qk_all[c * CHUNK:(c + 1) * CHUNK, MLSTM_W + p * LANES:MLSTM_W + (p + 1) * LANES],
                    x_ref[c * CHUNK:(c + 1) * CHUNK, 2 * MLSTM_W + p * LANES:2 * MLSTM_W + (p + 1) * LANES])
           for p, c in units}
    qk = {u: _dot_nt(qkv[u][0], _block_diag(qkv[u][1], m_a)) for u in units}
    log_d = {u: jnp.where(tril, gates[u][0] - _row_form(gates[u][0], eye) + _row_form(gates[u][1], eye), -jnp.inf)
             for u in units}
    m_intra = {u: _half_max(log_d[u], m_a) for u in units}
    a_c = {u: gates[u][0][CHUNK - 1:CHUNK, :] - gates[u][0] + gates[u][1] for u in units}
    a_max = {u: jnp.max(a_c[u], axis=0, keepdims=True) for u in units}
    m_in, m_out = {}, {}
    for p in range(n_pairs):
        m_row = nm_state[p, 1:2, :]
        for c in range(n_chunks):
            m_in[p, c] = m_row
            m_row = jnp.maximum(gates[p, c][0][CHUNK - 1:CHUNK, :] + m_row, a_max[p, c])
            m_out[p, c] = m_row
    m_t = {u: jnp.maximum(gates[u][0] + m_in[u], m_intra[u]) for u in units}
    inter = {u: jnp.exp(gates[u][0] + m_in[u] - m_t[u]) for u in units}
    wmat = {u: jnp.where(tril, jnp.exp(log_d[u] - m_t[u]), 0.0) * qk[u] for u in units}
    kw = {u: qkv[u][1] * jnp.exp(a_c[u] - m_out[u]) for u in units}
    dec = {u: jnp.exp(gates[u][0][CHUNK - 1:CHUNK, :] + m_in[u] - m_out[u]) for u in units}
    num_intra = {u: _pmul(wmat[u], qkv[u][2], m_a) for u in units}
    den_intra = {u: _half_sum(wmat[u], m_a) for u in units}
    kv = {u: _outer_state(kw[u], qkv[u][2], bd_mask) for u in units}
    kn = {u: jnp.sum(kw[u], axis=0, keepdims=True) for u in units}
    c_in, n_in = {}, {}
    for p in range(n_pairs):
        c_bd, n_row = c_state[p], nm_state[p, 0:1, :]
        for c in range(n_chunks):
            c_in[p, c], n_in[p, c] = c_bd, n_row
            c_bd = dec[p, c] * c_bd + kv[p, c]
            n_row = dec[p, c] * n_row + kn[p, c]
        c_state[p] = c_bd
        nm_state[p, 0:1, :] = n_row
        nm_state[p, 1:2, :] = m_out[p, n_chunks - 1]
    num = {u: inter[u] * _dot(qkv[u][0], c_in[u]) + num_intra[u] for u in units}
    den = {u: inter[u] * _half_sum(qkv[u][0] * n_in[u], m_a) + den_intra[u] for u in units}
    h = {u: num[u] / jnp.maximum(jnp.abs(den[u]), jnp.exp(-m_t[u])) for u in units}
    ms = {u: _half_sum(h[u] * h[u], m_a) * (1.0 / HEAD_DIM) for u in units}
    for p, c in units:
        rs = slice(c * CHUNK, (c + 1) * CHUNK)
        cs = slice(p * LANES, (p + 1) * LANES)
        o_logit = x_ref[rs, 3 * MLSTM_W + p * LANES:3 * MLSTM_W + (p + 1) * LANES]
        y = _sigmoid(o_logit) * (h[p, c] * lax.rsqrt(ms[p, c] + EPS) * gain_ref[:, cs])
        y_ref[rs, cs] = y.astype(y_ref.dtype)


def _mixer_call(kernel, name, proj3, sec_block, sec_width, out_width, extra_inputs, extra_specs, scratch):
    b, t, _ = proj3.shape
    tb = min(TIME_BLOCK, t)
    return pl.pallas_call(
        kernel,
        grid=(b, t // tb),
        in_specs=[pl.BlockSpec((None, tb, sec_width), lambda i, j: (i, j, sec_block))] + extra_specs(tb),
        out_specs=pl.BlockSpec((None, tb, out_width), lambda i, j: (i, j, 0)),
        out_shape=jax.ShapeDtypeStruct((b, t, out_width), BF16),
        scratch_shapes=scratch(tb),
        compiler_params=_cparams(("parallel", "arbitrary")),
        name=name,
    )(proj3, *extra_inputs)


def _full2(a, b):
    return pl.BlockSpec((a, b), lambda i, j: (0, 0))


def _gdn_mixer(proj3, gates3, conv_w, prm, gain):
    b, t, _ = proj3.shape
    tb = min(TIME_BLOCK, t)
    n_blocks = t // tb
    n_units = (GDN_HEADS // 2) * (tb // CHUNK)
    cur = lambda i, j: (i, jnp.minimum(j, n_blocks - 1))
    prv = lambda i, j: (i, jnp.maximum(j - 1, 0))
    z_block = (SEC_M + 3 * GDN_W) // GDN_W
    assert z_block * GDN_W == SEC_M + 3 * GDN_W
    return pl.pallas_call(
        _gdn_kernel,
        grid=(b, n_blocks + 1),
        in_specs=[pl.BlockSpec((None, tb, SEC_G), lambda i, j: cur(i, j) + (1,)),
                  pl.BlockSpec((None, tb, LANES), lambda i, j: cur(i, j) + (0,)),
                  pl.BlockSpec((None, tb, GDN_W), lambda i, j: prv(i, j) + (z_block,)),
                  _full2(CONV_K, 3 * GDN_W), _full2(8, LANES), _full2(1, GDN_W)],
        out_specs=pl.BlockSpec((None, tb, GDN_W), lambda i, j: prv(i, j) + (0,)),
        out_shape=jax.ShapeDtypeStruct((b, t, GDN_W), BF16),
        scratch_shapes=[pltpu.VMEM((GDN_HEADS // 2, LANES, LANES), F32),
                        pltpu.VMEM((n_units, CHUNK, LANES), F32),
                        pltpu.VMEM((n_units, 2 * CHUNK, LANES), BF16),
                        pltpu.VMEM((n_units, CHUNK, LANES), BF16),
                        pltpu.VMEM((n_units, LANES, LANES), BF16),
                        pltpu.VMEM((n_units, 8, LANES), F32),
                        pltpu.VMEM((tb + 8, 3 * GDN_W), F32)],
        compiler_params=_cparams(("parallel", "arbitrary")),
        name="gdn_mixer",
    )(proj3, gates3, proj3, conv_w, prm, gain)


def _mlstm_mixer(proj3, gates3, conv_w, prm, gain):
    return _mixer_call(
        _mlstm_kernel, "mlstm_mixer", proj3, 0, SEC_M, MLSTM_W, (gates3, conv_w, prm, gain),
        lambda tb: [pl.BlockSpec((None, tb, LANES), lambda i, j: (i, j, 0)),
                    _full2(CONV_K, 2 * MLSTM_W), _full2(8, LANES), _full2(1, MLSTM_W)],
        lambda tb: [pltpu.VMEM((MLSTM_HEADS // 2, LANES, LANES), F32),
                    pltpu.VMEM((MLSTM_HEADS // 2, 8, LANES), F32),
                    pltpu.VMEM((tb + 8, 2 * MLSTM_W), F32)])


def _ret_mixer(proj3, cos_tab, sin_tab, gain):
    return _mixer_call(
        _ret_kernel, "retention_mixer", proj3, 3, SEC_R, RET_W, (cos_tab, sin_tab, gain),
        lambda tb: [pl.BlockSpec((tb, LANES), lambda i, j: (j, 0)),
                    pl.BlockSpec((tb, LANES), lambda i, j: (j, 0)),
                    _full2(1, RET_W)],
        lambda tb: [pltpu.VMEM((RET_HEADS // 2, LANES, LANES), F32)])


DISPATCH_TILE = 512


def _dispatch_kernel(pos_ref, src_ref, zero_ref, out_ref, sem):
    del zero_ref
    rows = src_ref.shape[0]
    base = pl.program_id(0) * rows

    def issue(r, carry):
        pltpu.make_async_copy(src_ref.at[pl.ds(r, 1)], out_ref.at[pl.ds(pos_ref[base + r], 1)], sem).start()
        return carry

    lax.fori_loop(0, rows, issue, 0, unroll=8)
    pltpu.make_async_copy(src_ref, out_ref.at[pl.ds(0, rows)], sem).wait()


def _dispatch(src, pos, n_rows):
    n, width = src.shape
    tile = min(DISPATCH_TILE, n)
    zeros = jnp.zeros((n_rows, width), src.dtype)
    return pl.pallas_call(
        _dispatch_kernel,
        grid_spec=pltpu.PrefetchScalarGridSpec(
            num_scalar_prefetch=1,
            grid=(n // tile,),
            in_specs=[pl.BlockSpec((tile, width), lambda i, p: (i, 0)), pl.BlockSpec(memory_space=pl.ANY)],
            out_specs=pl.BlockSpec(memory_space=pl.ANY),
            scratch_shapes=[pltpu.SemaphoreType.DMA(())],
        ),
        out_shape=jax.ShapeDtypeStruct((n_rows, width), src.dtype),
        input_output_aliases={2: 0},
        compiler_params=_cparams(("arbitrary",)),
        name="moe_dispatch",
    )(pos, src, zeros)


def _ffn_kernel(layer, plan_ref, used_ref, x_ref, wg_hbm, wu_hbm, wd_hbm, y_ref, wg_buf, wu_buf, wd_buf, sems):
    def group_copies(g, to_slot):
        lo = pl.multiple_of(g * EXPERTS_PER_GROUP, EXPERTS_PER_GROUP)
        experts = pl.ds(lo, EXPERTS_PER_GROUP)
        return [pltpu.make_async_copy(hbm.at[layer, experts], buf.at[to_slot], sems.at[to_slot])
                for hbm, buf in ((wg_hbm, wg_buf), (wu_hbm, wu_buf), (wd_hbm, wd_buf))]

    def one_tile(i, rows):
        used = i < used_ref[0]
        group, slot, starts_group, next_group = plan_ref[0, i], plan_ref[1, i], plan_ref[2, i], plan_ref[3, i]

        @pl.when(used & (i == 0))
        def _():
            for cp in group_copies(group, slot):
                cp.start()

        @pl.when(used & (starts_group == 1) & (next_group >= 0))
        def _():
            for cp in group_copies(next_group, 1 - slot):
                cp.start()

        @pl.when(used & (starts_group == 1))
        def _():
            for cp in group_copies(group, slot):
                cp.wait()

        @pl.when(jnp.logical_not(used))
        def _():
            y_ref[rows, :] = jnp.zeros((MOE_TILE, D_MODEL), F32)

        @pl.when(used)
        def _():
            e1, e2 = plan_ref[4, i], plan_ref[5, i]
            x = x_ref[rows, 0:D_MODEL].astype(BF16)
            gate1 = jnp.dot(x, wg_buf[slot, e1], preferred_element_type=F32)
            gate2 = jnp.dot(x, wg_buf[slot, e2], preferred_element_type=F32)
            up1 = jnp.dot(x, wu_buf[slot, e1], preferred_element_type=F32)
            up2 = jnp.dot(x, wu_buf[slot, e2], preferred_element_type=F32)
            hid1 = (_silu(gate1) * up1).astype(BF16)
            hid2 = (_silu(gate2) * up2).astype(BF16)
            y1 = jnp.dot(hid1, wd_buf[slot, e1], preferred_element_type=F32)
            y2 = jnp.dot(hid2, wd_buf[slot, e2], preferred_element_type=F32)
            rw = x_ref[rows, D_MODEL:]
            y_ref[rows, :] = rw[:, 0:1] * y1 + rw[:, 1:2] * y2

    for sub in range(FFN_STEP_TILES):
        one_tile(pl.program_id(0) * FFN_STEP_TILES + sub, slice(sub * MOE_TILE, (sub + 1) * MOE_TILE))


def _expert_ffn(xs, tile_plan, n_used, wg, wu, wd, layer):
    r = xs.shape[0]
    step_rows = FFN_STEP_TILES * MOE_TILE
    assert r % step_rows == 0
    last_used = lambda nu: (nu[0] - 1) // FFN_STEP_TILES
    return pl.pallas_call(
        functools.partial(_ffn_kernel, layer),
        grid_spec=pltpu.PrefetchScalarGridSpec(
            num_scalar_prefetch=2,
            grid=(r // step_rows,),
            in_specs=[pl.BlockSpec((step_rows, HP_WIDTH), lambda i, plan, nu: (jnp.minimum(i, last_used(nu)), 0)),
                      pl.BlockSpec(memory_space=pl.ANY), pl.BlockSpec(memory_space=pl.ANY),
                      pl.BlockSpec(memory_space=pl.ANY)],
            out_specs=pl.BlockSpec((step_rows, D_MODEL), lambda i, plan, nu: (i, 0)),
            scratch_shapes=[pltpu.VMEM((2, EXPERTS_PER_GROUP, D_MODEL, D_EXPERT), BF16),
                            pltpu.VMEM((2, EXPERTS_PER_GROUP, D_MODEL, D_EXPERT), BF16),
                            pltpu.VMEM((2, EXPERTS_PER_GROUP, D_EXPERT, D_MODEL), BF16),
                            pltpu.SemaphoreType.DMA((2,))],
        ),
        out_shape=jax.ShapeDtypeStruct((r, D_MODEL), F32),
        compiler_params=_cparams(("arbitrary",)),
        name="expert_pair_ffn",
    )(tile_plan, n_used, xs, wg, wu, wd)


def _combine_kernel(final_norm, pos_ref, x_ref, gain_ref, ys_ref, o_ref, buf, sems):
    rows = o_ref.shape[0]
    i = pl.program_id(0)
    slot = i & 1

    def gather_tile(tile, to_slot):
        def issue(r, carry):
            pltpu.make_async_copy(ys_ref.at[pl.ds(pos_ref[tile * rows + r], 1)],
                                  buf.at[to_slot, pl.ds(r, 1)], sems.at[to_slot]).start()
            return carry

        lax.fori_loop(0, rows, issue, 0, unroll=8)

    @pl.when(i == 0)
    def _():
        gather_tile(0, 0)

    @pl.when(i + 1 < pl.num_programs(0))
    def _():
        gather_tile(i + 1, 1 - slot)

    pltpu.make_async_copy(ys_ref.at[pl.ds(0, rows)], buf.at[slot], sems.at[slot]).wait()
    out = x_ref[...] + buf[slot]
    if final_norm:
        ms = jnp.mean(out * out, axis=-1, keepdims=True)
        out = out * lax.rsqrt(ms + EPS) * gain_ref[...]
    o_ref[...] = out


def _combine(x2, ys, pos, gain, final_norm):
    n = x2.shape[0]
    return pl.pallas_call(
        functools.partial(_combine_kernel, final_norm),
        grid_spec=pltpu.PrefetchScalarGridSpec(
            num_scalar_prefetch=1,
            grid=(n // ROW_TILE,),
            in_specs=[pl.BlockSpec((ROW_TILE, D_MODEL), lambda i, p: (i, 0)),
                      pl.BlockSpec((1, D_MODEL), lambda i, p: (0, 0)),
                      pl.BlockSpec(memory_space=pl.ANY)],
            out_specs=pl.BlockSpec((ROW_TILE, D_MODEL), lambda i, p: (i, 0)),
            scratch_shapes=[pltpu.VMEM((2, ROW_TILE, D_MODEL), F32), pltpu.SemaphoreType.DMA((2,))],
        ),
        out_shape=jax.ShapeDtypeStruct((n, D_MODEL), F32),
        compiler_params=_cparams(("arbitrary",)),
        name="moe_combine",
    )(pos, x2, gain, ys)


def _plan(route, counts, n):
    cls, rank = route[:, 0], route[:, 1]
    counts = counts.astype(jnp.int32)
    tiles_per = (counts + MOE_TILE - 1) // MOE_TILE
    tile_end = jnp.cumsum(tiles_per)
    tile_start = tile_end - tiles_per
    class_ids = jnp.arange(N_CLASSES, dtype=jnp.int32)
    start_of = jnp.sum(jnp.where(cls[:, None] == class_ids[None, :], tile_start[None, :], 0), axis=1)
    pos = start_of * MOE_TILE + rank
    n_pairs = N_GROUPS * (EXPERTS_PER_GROUP * (EXPERTS_PER_GROUP - 1)) // 2
    n_tiles = n // MOE_TILE + n_pairs
    n_used = tile_end[-1]
    tile_ids = jnp.minimum(jnp.arange(n_tiles, dtype=jnp.int32), n_used - 1)
    tile_cls = jnp.sum((tile_end[None, :] <= tile_ids[:, None]).astype(jnp.int32), axis=1)
    per_group = EXPERTS_PER_GROUP * EXPERTS_PER_GROUP
    tile_group = tile_cls // per_group
    starts_group = jnp.concatenate([jnp.ones((1,), jnp.int32),
                                    (tile_group[1:] != tile_group[:-1]).astype(jnp.int32)])
    slot = (jnp.cumsum(starts_group) - 1) % 2
    present = jnp.sum(tiles_per.reshape(N_GROUPS, per_group), axis=1) > 0
    nxt = [jnp.int32(-1)] * N_GROUPS
    for g in range(N_GROUPS - 2, -1, -1):
        nxt[g] = jnp.where(present[g + 1], g + 1, nxt[g + 1])
    next_group = jnp.stack(nxt)[tile_group]
    tile_plan = jnp.stack([tile_group, slot, starts_group, next_group,
                           (tile_cls // EXPERTS_PER_GROUP) % EXPERTS_PER_GROUP, tile_cls % EXPERTS_PER_GROUP])
    return pos, tile_plan.astype(jnp.int32), n_used.reshape(1), n_tiles * MOE_TILE


def _rope_tables(t):
    inv_freq = ROPE_BASE ** (-jnp.arange(0, HEAD_DIM, 2, dtype=F32) / HEAD_DIM)
    ang = jnp.arange(t, dtype=F32)[:, None] * inv_freq[None, :]
    cos, sin = jnp.cos(ang), jnp.sin(ang)
    cos_tab = jnp.tile(cos, (1, 2 * LANES // HEAD_DIM))
    sin_tab = jnp.tile(jnp.concatenate([-sin, sin], axis=-1), (1, LANES // HEAD_DIM))
    return cos_tab, sin_tab


def _split_w_in(w):
    o = 0
    parts = {}
    for name, width in (("g_qkv", 3 * GDN_W), ("g_z", GDN_W), ("g_b", GDN_HEADS), ("g_a", GDN_HEADS),
                        ("r_q", RET_W), ("r_k", RET_W), ("r_v", RET_W), ("r_g", RET_W),
                        ("m_qk", 2 * MLSTM_W), ("m_v", MLSTM_W), ("m_o", MLSTM_W),
                        ("m_i", MLSTM_HEADS), ("m_f", MLSTM_HEADS)):
        parts[name] = w[:, o:o + width]
        o += width
    main = jnp.concatenate([parts[k] for k in ("m_qk", "m_v", "m_o", "g_qkv", "g_z", "r_q", "r_k", "r_v", "r_g")],
                           axis=1).astype(BF16)
    gate = jnp.concatenate([parts[k] for k in ("g_b", "g_a", "m_i", "m_f")], axis=1)
    gate = jnp.pad(gate, ((0, 0), (0, LANES - gate.shape[1]))).astype(BF16)
    return main, gate


def _lane_row(values, start):
    return jnp.zeros((LANES,), F32).at[start:start + values.shape[0]].set(values)


def kernel(x, norm_mix, w_in, gdn_conv, gdn_a_log, gdn_dt_bias, gdn_norm, ret_norm, mlstm_conv, mlstm_i_bias, mlstm_f_bias, mlstm_norm, w_out, norm_ffn, router_group, router_expert, router_bias, expert_gate, expert_up, expert_down, norm_final):
    b, t, d = x.shape
    n = b * t
    depth = w_in.shape[0]
    cos_tab, sin_tab = _rope_tables(t)
    wg_all, wu_all, wd_all = expert_gate.astype(BF16), expert_up.astype(BF16), expert_down.astype(BF16)
    x2 = x.reshape(n, d)
    for l in range(depth):
        w_main, w_gate = _split_w_in(w_in[l])
        proj, gates = _inproj(x2, norm_mix[l][None, :], w_main, w_gate)
        proj3 = proj.reshape(b, t, D_MAIN)
        gates3 = gates.reshape(b, t, LANES)
        gdn_prm = jnp.zeros((8, LANES), F32).at[0].set(_lane_row(gdn_a_log[l], G_A)).at[1].set(
            _lane_row(gdn_dt_bias[l], G_A))
        mlstm_prm = jnp.zeros((8, LANES), F32).at[0].set(_lane_row(mlstm_i_bias[l], M_I)).at[1].set(
            _lane_row(mlstm_f_bias[l], M_F))
        y_m = _mlstm_mixer(proj3, gates3, mlstm_conv[l], mlstm_prm, mlstm_norm[l][None, :])
        y_g = _gdn_mixer(proj3, gates3, gdn_conv[l], gdn_prm, gdn_norm[l][None, :])
        y_r = _ret_mixer(proj3, cos_tab, sin_tab, ret_norm[l][None, :])
        wo = w_out[l].astype(BF16)
        w_router = jnp.pad(jnp.concatenate([router_group[l], router_expert[l]], axis=1),
                           ((0, 0), (0, LANES - N_GROUPS - N_EXPERTS))).astype(BF16)
        rb = _lane_row(router_bias[l], N_GROUPS)[None, :]
        x2, h_packed, route, counts = _outproj(
            x2, y_m.reshape(n, MLSTM_W), y_g.reshape(n, GDN_W), y_r.reshape(n, RET_W),
            wo[GDN_W + RET_W:], wo[:GDN_W], wo[GDN_W:GDN_W + RET_W],
            norm_ffn[l][None, :], w_router, rb)
        pos, tile_plan, n_used, n_rows = _plan(route, counts[0], n)
        xs = _dispatch(h_packed, pos, n_rows)
        ys = _expert_ffn(xs, tile_plan, n_used, wg_all, wu_all, wd_all, l)
        x2 = _combine(x2, ys, pos, norm_final[None, :], l == depth - 1)
    return x2.reshape(b, t, d)
```

```python
import functools
import math

import jax
import jax.numpy as jnp
from jax import lax
from jax.experimental import pallas as pl
from jax.experimental.pallas import tpu as pltpu

F32 = jnp.float32
BF16 = jnp.bfloat16

D_MODEL = 1024
HEAD_DIM = 64
CHUNK = 64
GDN_HEADS, RET_HEADS, MLSTM_HEADS = 6, 4, 6
GDN_W, RET_W, MLSTM_W = GDN_HEADS * HEAD_DIM, RET_HEADS * HEAD_DIM, MLSTM_HEADS * HEAD_DIM
CONV_K = 4
ROPE_BASE = 10000.0
N_GROUPS, EXPERTS_PER_GROUP = 4, 8
N_EXPERTS = N_GROUPS * EXPERTS_PER_GROUP
D_EXPERT = D_MODEL // 4
EPS = 1e-6
LANES = 128
SEC_M, SEC_G, SEC_R = 4 * MLSTM_W, 4 * GDN_W, 4 * RET_W
D_MAIN = SEC_M + SEC_G + SEC_R
G_BETA, G_A, M_I, M_F = 0, GDN_HEADS, 2 * GDN_HEADS, 2 * GDN_HEADS + MLSTM_HEADS

ROW_TILE = 256
OUT_TILE = 512
OUT_SUB = 128
TIME_BLOCK = 256
MOE_TILE = 128
FFN_STEP_TILES = 4
VMEM_LIMIT = 56 * 1024 * 1024


def _cparams(sem):
    return pltpu.CompilerParams(dimension_semantics=sem, vmem_limit_bytes=VMEM_LIMIT)


def _silu(x):
    return x * (1.0 / (1.0 + jnp.exp(-x)))


def _sigmoid(x):
    return 1.0 / (1.0 + jnp.exp(-x))


def _softplus(x):
    return jnp.maximum(x, 0.0) + jnp.log(1.0 + jnp.exp(-jnp.abs(x)))


def _dot(a, b):
    return jnp.dot(a.astype(BF16), b.astype(BF16), preferred_element_type=F32)


def _dot_nt(a, b):
    return lax.dot_general(a.astype(BF16), b.astype(BF16), (((1,), (1,)), ((), ())),
                           preferred_element_type=F32)


def _lane_masks(rows):
    lane = lax.broadcasted_iota(jnp.int32, (rows, LANES), 1)
    row = lax.broadcasted_iota(jnp.int32, (rows, LANES), 0)
    return lane, row


def _half_sum(x, m_a):
    s_a = jnp.sum(jnp.where(m_a, x, 0.0), axis=-1, keepdims=True)
    s_b = jnp.sum(jnp.where(m_a, 0.0, x), axis=-1, keepdims=True)
    return jnp.where(m_a, s_a, s_b)


def _half_max(x, m_a):
    s_a = jnp.max(jnp.where(m_a, x, -jnp.inf), axis=-1, keepdims=True)
    s_b = jnp.max(jnp.where(m_a, -jnp.inf, x), axis=-1, keepdims=True)
    return jnp.where(m_a, s_a, s_b)


def _col_form(g, lane_a, m_a):
    rows = g.shape[0]
    ca = jnp.broadcast_to(g[:, lane_a:lane_a + 1], (rows, LANES))
    cb = jnp.broadcast_to(g[:, lane_a + 1:lane_a + 2], (rows, LANES))
    return jnp.where(m_a, ca, cb)


def _row_form(col, eye):
    return jnp.sum(jnp.where(eye, col, 0.0), axis=0, keepdims=True)


def _block_diag(y, m_a):
    return jnp.concatenate([jnp.where(m_a, y, 0.0), jnp.where(m_a, 0.0, y)], axis=0)


def _pmul(x, y, m_a):
    return _dot(x, _block_diag(y, m_a))


def _outer_state(k, v, bd_mask):
    zero = jnp.zeros_like(k)
    kt = jnp.concatenate([k, zero], axis=0).T
    vp = jnp.concatenate([v, zero], axis=0)
    return jnp.where(bd_mask, _dot(kt, vp), 0.0)


def _chunk_cumsum(x):
    rows = x.shape[0]
    r = lax.broadcasted_iota(jnp.int32, (rows, LANES), 0) % CHUNK
    s = 1
    while s < CHUNK:
        x = x + jnp.where(r >= s, pltpu.roll(x, s, axis=0), 0.0)
        s *= 2
    return x


def _causal_conv(raw, cbuf, w_ref):
    rows = raw.shape[0]
    cbuf[8:8 + rows, :] = raw
    acc = raw * w_ref[CONV_K - 1:CONV_K, :]
    for j in range(CONV_K - 1):
        off = 8 - (CONV_K - 1) + j
        acc = acc + cbuf[off:off + rows, :] * w_ref[j:j + 1, :]
    cbuf[0:8, :] = raw[rows - 8:rows, :]
    return acc


def _inproj_kernel(x_ref, gain_ref, w_ref, wg_ref, o_ref, og_ref):
    x = x_ref[...]
    ms = jnp.mean(x * x, axis=-1, keepdims=True)
    h = (x * lax.rsqrt(ms + EPS) * gain_ref[...]).astype(BF16)

    def proj(lo, hi):
        return jnp.dot(h, w_ref[:, lo:hi], preferred_element_type=F32)

    step = 512
    for c in range(D_MAIN // step):
        o_ref[:, c * step:(c + 1) * step] = proj(c * step, (c + 1) * step)
    og_ref[...] = jnp.dot(h, wg_ref[...], preferred_element_type=F32)


def _inproj(x2, gain, w_main, w_gate):
    n = x2.shape[0]
    full = lambda a, b: pl.BlockSpec((a, b), lambda i: (0, 0))
    return pl.pallas_call(
        _inproj_kernel,
        grid=(n // OUT_TILE,),
        in_specs=[
            pl.BlockSpec((OUT_TILE, D_MODEL), lambda i: (i, 0)),
            full(1, D_MODEL), full(D_MODEL, D_MAIN), full(D_MODEL, LANES),
        ],
        out_specs=[
            pl.BlockSpec((OUT_TILE, D_MAIN), lambda i: (i, 0)),
            pl.BlockSpec((OUT_TILE, LANES), lambda i: (i, 0)),
        ],
        out_shape=[jax.ShapeDtypeStruct((n, D_MAIN), F32), jax.ShapeDtypeStruct((n, LANES), F32)],
        compiler_params=_cparams(("parallel",)),
        name="norm_inproj",
    )(x2, gain, w_main, w_gate)


N_CLASSES = N_GROUPS * EXPERTS_PER_GROUP * EXPERTS_PER_GROUP
HP_WIDTH = D_MODEL + LANES


def _route_rows(lg):
    rows = lg.shape[0]
    lane = lax.broadcasted_iota(jnp.int32, (rows, LANES), 1)
    lane_f = lane.astype(F32)
    neg = -jnp.inf

    def first_lane(mask):
        return jnp.min(jnp.where(mask, lane_f, float(LANES)), axis=-1, keepdims=True).astype(jnp.int32)

    gl = jnp.where(lane < N_GROUPS, lg, neg)
    gm = jnp.max(gl, axis=-1, keepdims=True)
    gsum = jnp.sum(jnp.where(lane < N_GROUPS, jnp.exp(gl - gm), 0.0), axis=-1, keepdims=True)
    gidx = first_lane(gl == gm)
    group_p = 1.0 / gsum
    in_group = (lane >= N_GROUPS) & (lane < N_GROUPS + N_EXPERTS) & (
        lax.shift_right_arithmetic(lane - N_GROUPS, 3) == gidx)
    el = jnp.where(in_group, lg, neg)
    m1 = jnp.max(el, axis=-1, keepdims=True)
    i1 = first_lane(el == m1)
    el2 = jnp.where(lane == i1, neg, el)
    m2 = jnp.max(el2, axis=-1, keepdims=True)
    i2 = first_lane(el2 == m2)
    e2 = jnp.exp(m2 - m1)
    w1 = group_p * (1.0 / (1.0 + e2))
    w2 = group_p * (e2 / (1.0 + e2))
    first_lower = i1 < i2
    base = N_GROUPS + gidx * EXPERTS_PER_GROUP
    lo = jnp.minimum(i1, i2) - base
    hi = jnp.maximum(i1, i2) - base
    cls = (gidx * EXPERTS_PER_GROUP + lo) * EXPERTS_PER_GROUP + hi
    return cls, jnp.where(first_lower, w1, w2), jnp.where(first_lower, w2, w1)


def _outproj_kernel(x_ref, ym_ref, yg_ref, yr_ref, wm_ref, wg_ref, wr_ref, gain_ref, wrt_ref, rb_ref,
                    xo_ref, hp_ref, rt_ref, cnt_ref, running):
    @pl.when(pl.program_id(0) == 0)
    def _():
        running[...] = jnp.zeros_like(running)

    rows = OUT_SUB
    subs = [slice(s * rows, (s + 1) * rows) for s in range(x_ref.shape[0] // rows)]
    acc = [x_ref[r, :] + jnp.dot(yg_ref[r, :], wg_ref[...], preferred_element_type=F32) for r in subs]
    acc = [a + jnp.dot(yr_ref[r, :], wr_ref[...], preferred_element_type=F32) for a, r in zip(acc, subs)]
    acc = [a + jnp.dot(ym_ref[r, :], wm_ref[...], preferred_element_type=F32) for a, r in zip(acc, subs)]
    for a, r in zip(acc, subs):
        xo_ref[r, :] = a
    hn = [a * lax.rsqrt(jnp.mean(a * a, axis=-1, keepdims=True) + EPS) * gain_ref[...] for a in acc]
    for h, r in zip(hn, subs):
        hp_ref[r, 0:D_MODEL] = h
    logits = [jnp.dot(h.astype(BF16), wrt_ref[...], preferred_element_type=F32) + rb_ref[...] for h in hn]
    routed = [_route_rows(lg) for lg in logits]

    cls_lane = lax.broadcasted_iota(jnp.int32, (rows, N_CLASSES), 1)
    before = (lax.broadcasted_iota(jnp.int32, (rows, rows), 1)
              < lax.broadcasted_iota(jnp.int32, (rows, rows), 0)).astype(BF16)
    onehot = [cls_lane == cls for cls, _, _ in routed]
    prefix = [jnp.dot(before, oh.astype(BF16), preferred_element_type=F32) for oh in onehot]
    counts = [jnp.sum(oh.astype(F32), axis=0, keepdims=True) for oh in onehot]
    lane = lax.broadcasted_iota(jnp.int32, (rows, LANES), 1)
    seen = running[0:1, :]
    for r, (cls, w_lo, w_hi), oh, pre, cnt in zip(subs, routed, onehot, prefix, counts):
        rank = jnp.sum(jnp.where(oh, pre + seen, 0.0), axis=-1, keepdims=True)
        seen = seen + cnt
        rt_ref[r, :] = jnp.where(lane == 0, cls, jnp.where(lane == 1, rank.astype(jnp.int32), 0))
        hp_ref[r, D_MODEL:] = jnp.where(lane == 0, w_lo, jnp.where(lane == 1, w_hi, 0.0))
    running[0:1, :] = seen
    cnt_ref[...] = jnp.broadcast_to(seen, cnt_ref.shape)


def _outproj(x2, ym, yg, yr, wo_m, wo_g, wo_r, gain, w_router, router_bias):
    n = x2.shape[0]
    row = lambda w: pl.BlockSpec((OUT_TILE, w), lambda i: (i, 0))
    full = lambda a, b: pl.BlockSpec((a, b), lambda i: (0, 0))
    return pl.pallas_call(
        _outproj_kernel,
        grid=(n // OUT_TILE,),
        in_specs=[row(D_MODEL), row(MLSTM_W), row(GDN_W), row(RET_W),
                  full(MLSTM_W, D_MODEL), full(GDN_W, D_MODEL), full(RET_W, D_MODEL),
                  full(1, D_MODEL), full(D_MODEL, LANES), full(1, LANES)],
        out_specs=[row(D_MODEL), row(HP_WIDTH), row(LANES), full(8, N_CLASSES)],
        out_shape=[jax.ShapeDtypeStruct((n, D_MODEL), F32),
                   jax.ShapeDtypeStruct((n, HP_WIDTH), F32),
                   jax.ShapeDtypeStruct((n, LANES), jnp.int32),
                   jax.ShapeDtypeStruct((8, N_CLASSES), F32)],
        scratch_shapes=[pltpu.VMEM((8, N_CLASSES), F32)],
        compiler_params=_cparams(("arbitrary",)),
        name="outproj_norm_router",
    )(x2, ym, yg, yr, wo_m, wo_g, wo_r, gain, w_router, router_bias)


def _gdn_kernel(x_ref, gt_ref, z_ref, conv_ref, prm_ref, gain_ref, y_ref,
                state, u_s, wq_s, qk_s, kt_s, sd_s, cbuf):
    tb = x_ref.shape[0]
    first = pl.program_id(1) == 0

    @pl.when(first)
    def _():
        for ref in (state, u_s, wq_s, qk_s, kt_s, sd_s):
            ref[...] = jnp.zeros_like(ref)
        cbuf[0:8, :] = jnp.zeros((8, cbuf.shape[1]), F32)

    gt = gt_ref[...]
    beta_all = _sigmoid(gt)
    g_all = -jnp.exp(prm_ref[0:1, :]) * _softplus(gt + prm_ref[1:2, :])
    gc_all = _chunk_cumsum(g_all)

    lane, row = _lane_masks(CHUNK)
    m_a = lane < HEAD_DIM
    lane_h = lane % HEAD_DIM
    eye = lane_h == row
    tril = lane_h <= row
    strict = lane_h < row
    eye_f = eye.astype(F32)
    lane2 = lax.broadcasted_iota(jnp.int32, (LANES, LANES), 1)
    row2 = lax.broadcasted_iota(jnp.int32, (LANES, LANES), 0)
    bd_mask = (lane2 < HEAD_DIM) == (row2 < HEAD_DIM)
    lane_t = lax.broadcasted_iota(jnp.int32, (tb, LANES), 1)
    m_a_t = lane_t < HEAD_DIM

    n_pairs = GDN_HEADS // 2
    n_chunks = tb // CHUNK
    units = [(p, c) for p in range(n_pairs) for c in range(n_chunks)]

    def unit_index(p, c):
        return p * n_chunks + c

    prev = {}
    for p, c in units:
        i = unit_index(p, c)
        prev[p, c] = (u_s[i], wq_s[i], qk_s[i], kt_s[i], sd_s[i, 0:1, :])
    s_bd = [state[p] for p in range(n_pairs)]

    def recurrence(c):
        rs = slice(c * CHUNK, (c + 1) * CHUNK)
        for p in range(n_pairs):
            cs = slice(p * LANES, (p + 1) * LANES)
            u_mat, wq, qk, k_dec_t, s_decay = prev[p, c]
            ws_qs = jnp.dot(wq, s_bd[p].astype(BF16), preferred_element_type=F32)
            v_new = u_mat - ws_qs[:CHUNK]
            o = ws_qs[CHUNK:] + jnp.dot(qk, _block_diag(v_new, m_a).astype(BF16), preferred_element_type=F32)
            v_pad = jnp.concatenate([v_new, jnp.zeros_like(v_new)], axis=0).astype(BF16)
            s_bd[p] = s_bd[p] * s_decay + jnp.where(
                bd_mask, jnp.dot(k_dec_t, v_pad, preferred_element_type=F32), 0.0)
            ms = _half_sum(o * o, m_a) * (1.0 / HEAD_DIM)
            y = o * lax.rsqrt(ms + EPS) * gain_ref[:, cs] * _silu(z_ref[rs, cs])
            y_ref[rs, cs] = y.astype(y_ref.dtype)

    recurrence(0)
    qkv = _silu(_causal_conv(x_ref[:, 0:3 * GDN_W], cbuf, conv_ref))
    ins = {}
    for p in range(n_pairs):
        q_t = qkv[:, p * LANES:(p + 1) * LANES]
        k_t = qkv[:, GDN_W + p * LANES:GDN_W + (p + 1) * LANES]
        v_t = qkv[:, 2 * GDN_W + p * LANES:2 * GDN_W + (p + 1) * LANES]
        q_t = q_t * lax.rsqrt(_half_sum(q_t * q_t, m_a_t) + EPS) * (HEAD_DIM ** -0.5)
        k_t = k_t * lax.rsqrt(_half_sum(k_t * k_t, m_a_t) + EPS)
        beta_t = _col_form(beta_all, G_BETA + 2 * p, m_a_t)
        gc_t = _col_form(gc_all, G_A + 2 * p, m_a_t)
        for c in range(n_chunks):
            rs = slice(c * CHUNK, (c + 1) * CHUNK)
            ins[p, c] = (q_t[rs], k_t[rs], v_t[rs], beta_t[rs], gc_t[rs])
    kkqk = {u: _dot_nt(jnp.concatenate([ins[u][1] * ins[u][3], ins[u][0]], axis=0), _block_diag(ins[u][1], m_a))
            for u in units}
    pre = {}
    for u in units:
        q, k, v, beta, gc = ins[u]
        decay = jnp.where(tril, jnp.exp(gc - _row_form(gc, eye)), 0.0)
        egc = jnp.exp(gc)
        g_last = gc[CHUNK - 1:CHUNK, :]
        k_dec = k * jnp.exp(g_last - gc)
        pre[u] = dict(
            a=jnp.where(strict, kkqk[u][:CHUNK] * decay, 0.0), qk=kkqk[u][CHUNK:] * decay,
            vb=v * beta, kbe=k * beta * egc, qe=q * egc, s_decay=jnp.exp(g_last),
            k_dec_t=jnp.concatenate([k_dec, jnp.zeros_like(k_dec)], axis=0).T)

    pw = {u: _pmul(pre[u]["a"], pre[u]["a"], m_a) for u in units}
    t_inv = {u: eye_f - pre[u]["a"] for u in units}
    later_chunks = list(range(1, n_chunks))
    for step in range(4):
        if step % 2 == 0 and later_chunks:
            recurrence(later_chunks.pop(0))
        both = {u: _dot(jnp.concatenate([pw[u], t_inv[u]], axis=0), _block_diag(pw[u], m_a)) for u in units}
        pw = {u: both[u][:CHUNK] for u in units}
        t_inv = {u: t_inv[u] + both[u][CHUNK:] for u in units}
    for c in later_chunks:
        recurrence(c)
    t_inv = {u: t_inv[u] + _pmul(t_inv[u], pw[u], m_a) for u in units}
    for p, c in units:
        d = pre[p, c]
        uw = _dot(t_inv[p, c], jnp.concatenate([_block_diag(d["vb"], m_a), _block_diag(d["kbe"], m_a)], axis=1))
        i = unit_index(p, c)
        u_s[i] = uw[:, :LANES]
        wq_s[i] = jnp.concatenate([uw[:, LANES:], d["qe"]], axis=0).astype(BF16)
        qk_s[i] = d["qk"].astype(BF16)
        kt_s[i] = d["k_dec_t"].astype(BF16)
        sd_s[i] = jnp.broadcast_to(d["s_decay"], (8, LANES))
    for p in range(n_pairs):
        state[p] = s_bd[p]


def _ret_kernel(x_ref, cos_ref, sin_ref, gain_ref, y_ref, state):
    tb = x_ref.shape[0]

    @pl.when(pl.program_id(1) == 0)
    def _():
        state[...] = jnp.zeros_like(state)

    lane, row = _lane_masks(CHUNK)
    m_a = lane < HEAD_DIM
    lane_h = lane % HEAD_DIM
    tril = lane_h <= row
    lane2 = lax.broadcasted_iota(jnp.int32, (LANES, LANES), 1)
    row2 = lax.broadcasted_iota(jnp.int32, (LANES, LANES), 0)
    bd_mask = (lane2 < HEAD_DIM) == (row2 < HEAD_DIM)
    lane_t = lax.broadcasted_iota(jnp.int32, (tb, LANES), 1)
    first_half = (lane_t % HEAD_DIM) < (HEAD_DIM // 2)
    cos = cos_ref[...]
    sin = sin_ref[...]
    rowf = row.astype(F32)
    pos_diff = (row - lane_h).astype(F32)

    def rope(x):
        swapped = jnp.where(first_half, pltpu.roll(x, LANES - HEAD_DIM // 2, axis=1),
                            pltpu.roll(x, HEAD_DIM // 2, axis=1))
        return x * cos + swapped * sin

    n_pairs = RET_HEADS // 2
    n_chunks = tb // CHUNK
    units = [(p, c) for p in range(n_pairs) for c in range(n_chunks)]
    chunk_decay, decay, q_scale, k_scale, qkv = [], [], [], [], {}
    for p in range(n_pairs):
        lg_a = math.log(1.0 - 2.0 ** (-5.0 - 2 * p))
        lg_b = math.log(1.0 - 2.0 ** (-5.0 - (2 * p + 1)))
        lg = jnp.where(m_a, lg_a, lg_b)
        decay.append(jnp.where(tril, jnp.exp(pos_diff * lg), 0.0))
        q_scale.append(jnp.exp(lg * (rowf + 1.0)))
        k_scale.append(jnp.exp(lg * (CHUNK - 1.0 - rowf)))
        chunk_decay.append(jnp.exp(lg[0:1, :] * CHUNK))
        q_t = rope(x_ref[:, p * LANES:(p + 1) * LANES])
        k_t = rope(x_ref[:, RET_W + p * LANES:RET_W + (p + 1) * LANES]) * (HEAD_DIM ** -0.5)
        for c in range(n_chunks):
            rs = slice(c * CHUNK, (c + 1) * CHUNK)
            qkv[p, c] = (q_t[rs], k_t[rs], x_ref[rs, 2 * RET_W + p * LANES:2 * RET_W + (p + 1) * LANES])
    scores = {(p, c): _dot_nt(qkv[p, c][0], _block_diag(qkv[p, c][1], m_a)) * decay[p] for p, c in units}
    o_intra = {u: _pmul(scores[u], qkv[u][2], m_a) for u in units}
    kv_all = {(p, c): _outer_state(qkv[p, c][1] * k_scale[p], qkv[p, c][2], bd_mask) for p, c in units}
    pre = {(p, c): (o_intra[p, c], qkv[p, c][0] * q_scale[p], kv_all[p, c]) for p, c in units}

    s_in = {}
    for p in range(n_pairs):
        s_bd = state[p]
        for c in range(n_chunks):
            s_in[p, c] = s_bd
            s_bd = chunk_decay[p] * s_bd + pre[p, c][2]
        state[p] = s_bd
    o_all = {u: pre[u][0] + _dot(pre[u][1], s_in[u]) for u in units}
    ms_all = {u: _half_sum(o_all[u] * o_all[u], m_a) * (1.0 / HEAD_DIM) for u in units}
    for p, c in units:
        rs = slice(c * CHUNK, (c + 1) * CHUNK)
        cs = slice(p * LANES, (p + 1) * LANES)
        gate = x_ref[rs, 3 * RET_W + p * LANES:3 * RET_W + (p + 1) * LANES]
        y = o_all[p, c] * lax.rsqrt(ms_all[p, c] + EPS) * gain_ref[:, cs] * _silu(gate)
        y_ref[rs, cs] = y.astype(y_ref.dtype)


def _mlstm_kernel(x_ref, gt_ref, conv_ref, prm_ref, gain_ref, y_ref, c_state, nm_state, cbuf):
    tb = x_ref.shape[0]
    first = pl.program_id(1) == 0

    @pl.when(first)
    def _():
        c_state[...] = jnp.zeros_like(c_state)
        nm_state[...] = jnp.zeros_like(nm_state)
        cbuf[0:8, :] = jnp.zeros((8, cbuf.shape[1]), F32)

    qk_all = _silu(_causal_conv(x_ref[:, 0:2 * MLSTM_W], cbuf, conv_ref))
    gt = gt_ref[...]
    log_i_all = gt + prm_ref[0:1, :]
    f_pre = gt + prm_ref[1:2, :]
    log_f_all = jnp.minimum(f_pre, 0.0) - jnp.log(1.0 + jnp.exp(-jnp.abs(f_pre)))
    bc_all = _chunk_cumsum(log_f_all)

    lane, row = _lane_masks(CHUNK)
    m_a = lane < HEAD_DIM
    lane_h = lane % HEAD_DIM
    eye = lane_h == row
    tril = lane_h <= row
    lane2 = lax.broadcasted_iota(jnp.int32, (LANES, LANES), 1)
    row2 = lax.broadcasted_iota(jnp.int32, (LANES, LANES), 0)
    bd_mask = (lane2 < HEAD_DIM) == (row2 < HEAD_DIM)
    lane_t = lax.broadcasted_iota(jnp.int32, (tb, LANES), 1)
    m_a_t = lane_t < HEAD_DIM

    n_pairs = MLSTM_HEADS // 2
    n_chunks = tb // CHUNK
    units = [(p, c) for p in range(n_pairs) for c in range(n_chunks)]
    gates = {}
    for p in range(n_pairs):
        b_t = _col_form(bc_all, M_F + 2 * p, m_a_t)
        i_t = _col_form(log_i_all, M_I + 2 * p, m_a_t)
        for c in range(n_chunks):
            rs = slice(c * CHUNK, (c + 1) * CHUNK)
            gates[p, c] = (b_t[rs], i_t[rs])
    qkv = {(p, c): (qk_all[c * CHUNK:(c + 1) * CHUNK, p * LANES:(p + 1) * LANES] * (HEAD_DIM ** -0.5),
                    qk_all[c * CHUNK:(c + 1) * CHUNK, MLSTM_W + p * LANES:MLSTM_W + (p + 1) * LANES],
                    x_ref[c * CHUNK:(c + 1) * CHUNK, 2 * MLSTM_W + p * LANES:2 * MLSTM_W + (p + 1) * LANES])
           for p, c in units}
    qk = {u: _dot_nt(qkv[u][0], _block_diag(qkv[u][1], m_a)) for u in units}
    log_d = {u: jnp.where(tril, gates[u][0] - _row_form(gates[u][0], eye) + _row_form(gates[u][1], eye), -jnp.inf)
             for u in units}
    m_intra = {u: _half_max(log_d[u], m_a) for u in units}
    a_c = {u: gates[u][0][CHUNK - 1:CHUNK, :] - gates[u][0] + gates[u][1] for u in units}
    a_max = {u: jnp.max(a_c[u], axis=0, keepdims=True) for u in units}
    m_in, m_out = {}, {}
    for p in range(n_pairs):
        m_row = nm_state[p, 1:2, :]
        for c in range(n_chunks):
            m_in[p, c] = m_row
            m_row = jnp.maximum(gates[p, c][0][CHUNK - 1:CHUNK, :] + m_row, a_max[p, c])
            m_out[p, c] = m_row
    m_t = {u: jnp.maximum(gates[u][0] + m_in[u], m_intra[u]) for u in units}
    inter = {u: jnp.exp(gates[u][0] + m_in[u] - m_t[u]) for u in units}
    wmat = {u: jnp.where(tril, jnp.exp(log_d[u] - m_t[u]), 0.0) * qk[u] for u in units}
    kw = {u: qkv[u][1] * jnp.exp(a_c[u] - m_out[u]) for u in units}
    dec = {u: jnp.exp(gates[u][0][CHUNK - 1:CHUNK, :] + m_in[u] - m_out[u]) for u in units}
    num_intra = {u: _pmul(wmat[u], qkv[u][2], m_a) for u in units}
    den_intra = {u: _half_sum(wmat[u], m_a) for u in units}
    kv = {u: _outer_state(kw[u], qkv[u][2], bd_mask) for u in units}
    kn = {u: jnp.sum(kw[u], axis=0, keepdims=True) for u in units}
    c_in, n_in = {}, {}
    for p in range(n_pairs):
        c_bd, n_row = c_state[p], nm_state[p, 0:1, :]
        for c in range(n_chunks):
            c_in[p, c], n_in[p, c] = c_bd, n_row
            c_bd = dec[p, c] * c_bd + kv[p, c]
            n_row = dec[p, c] * n_row + kn[p, c]
        c_state[p] = c_bd
        nm_state[p, 0:1, :] = n_row
        nm_state[p, 1:2, :] = m_out[p, n_chunks - 1]
    num = {u: inter[u] * _dot(qkv[u][0], c_in[u]) + num_intra[u] for u in units}
    den = {u: inter[u] * _half_sum(qkv[u][0] * n_in[u], m_a) + den_intra[u] for u in units}
    h = {u: num[u] / jnp.maximum(jnp.abs(den[u]), jnp.exp(-m_t[u])) for u in units}
    ms = {u: _half_sum(h[u] * h[u], m_a) * (1.0 / HEAD_DIM) for u in units}
    for p, c in units:
        rs = slice(c * CHUNK, (c + 1) * CHUNK)
        cs = slice(p * LANES, (p + 1) * LANES)
        o_logit = x_ref[rs, 3 * MLSTM_W + p * LANES:3 * MLSTM_W + (p + 1) * LANES]
        y = _sigmoid(o_logit) * (h[p, c] * lax.rsqrt(ms[p, c] + EPS) * gain_ref[:, cs])
        y_ref[rs, cs] = y.astype(y_ref.dtype)


def _mixer_call(kernel, name, proj3, sec_block, sec_width, out_width, extra_inputs, extra_specs, scratch):
    b, t, _ = proj3.shape
    tb = min(TIME_BLOCK, t)
    return pl.pallas_call(
        kernel,
        grid=(b, t // tb),
        in_specs=[pl.BlockSpec((None, tb, sec_width), lambda i, j: (i, j, sec_block))] + extra_specs(tb),
        out_specs=pl.BlockSpec((None, tb, out_width), lambda i, j: (i, j, 0)),
        out_shape=jax.ShapeDtypeStruct((b, t, out_width), BF16),
        scratch_shapes=scratch(tb),
        compiler_params=_cparams(("parallel", "arbitrary")),
        name=name,
    )(proj3, *extra_inputs)


def _full2(a, b):
    return pl.BlockSpec((a, b), lambda i, j: (0, 0))


def _gdn_mixer(proj3, gates3, conv_w, prm, gain):
    b, t, _ = proj3.shape
    tb = min(TIME_BLOCK, t)
    n_blocks = t // tb
    n_units = (GDN_HEADS // 2) * (tb // CHUNK)
    cur = lambda i, j: (i, jnp.minimum(j, n_blocks - 1))
    prv = lambda i, j: (i, jnp.maximum(j - 1, 0))
    z_block = (SEC_M + 3 * GDN_W) // GDN_W
    assert z_block * GDN_W == SEC_M + 3 * GDN_W
    return pl.pallas_call(
        _gdn_kernel,
        grid=(b, n_blocks + 1),
        in_specs=[pl.BlockSpec((None, tb, SEC_G), lambda i, j: cur(i, j) + (1,)),
                  pl.BlockSpec((None, tb, LANES), lambda i, j: cur(i, j) + (0,)),
                  pl.BlockSpec((None, tb, GDN_W), lambda i, j: prv(i, j) + (z_block,)),
                  _full2(CONV_K, 3 * GDN_W), _full2(8, LANES), _full2(1, GDN_W)],
        out_specs=pl.BlockSpec((None, tb, GDN_W), lambda i, j: prv(i, j) + (0,)),
        out_shape=jax.ShapeDtypeStruct((b, t, GDN_W), BF16),
        scratch_shapes=[pltpu.VMEM((GDN_HEADS // 2, LANES, LANES), F32),
                        pltpu.VMEM((n_units, CHUNK, LANES), F32),
                        pltpu.VMEM((n_units, 2 * CHUNK, LANES), BF16),
                        pltpu.VMEM((n_units, CHUNK, LANES), BF16),
                        pltpu.VMEM((n_units, LANES, LANES), BF16),
                        pltpu.VMEM((n_units, 8, LANES), F32),
                        pltpu.VMEM((tb + 8, 3 * GDN_W), F32)],
        compiler_params=_cparams(("parallel", "arbitrary")),
        name="gdn_mixer",
    )(proj3, gates3, proj3, conv_w, prm, gain)


def _mlstm_mixer(proj3, gates3, conv_w, prm, gain):
    return _mixer_call(
        _mlstm_kernel, "mlstm_mixer", proj3, 0, SEC_M, MLSTM_W, (gates3, conv_w, prm, gain),
        lambda tb: [pl.BlockSpec((None, tb, LANES), lambda i, j: (i, j, 0)),
                    _full2(CONV_K, 2 * MLSTM_W), _full2(8, LANES), _full2(1, MLSTM_W)],
        lambda tb: [pltpu.VMEM((MLSTM_HEADS // 2, LANES, LANES), F32),
                    pltpu.VMEM((MLSTM_HEADS // 2, 8, LANES), F32),
                    pltpu.VMEM((tb + 8, 2 * MLSTM_W), F32)])


def _ret_mixer(proj3, cos_tab, sin_tab, gain):
    return _mixer_call(
        _ret_kernel, "retention_mixer", proj3, 3, SEC_R, RET_W, (cos_tab, sin_tab, gain),
        lambda tb: [pl.BlockSpec((tb, LANES), lambda i, j: (j, 0)),
                    pl.BlockSpec((tb, LANES), lambda i, j: (j, 0)),
                    _full2(1, RET_W)],
        lambda tb: [pltpu.VMEM((RET_HEADS // 2, LANES, LANES), F32)])


DISPATCH_TILE = 1024


def _dispatch_kernel(pos_ref, src_ref, zero_ref, out_ref, sem):
    del zero_ref
    rows = src_ref.shape[0]
    base = pl.program_id(0) * rows

    def issue(r, carry):
        pltpu.make_async_copy(src_ref.at[pl.ds(r, 1)], out_ref.at[pl.ds(pos_ref[base + r], 1)], sem).start()
        return carry

    lax.fori_loop(0, rows, issue, 0, unroll=8)
    pltpu.make_async_copy(src_ref, out_ref.at[pl.ds(0, rows)], sem).wait()


def _dispatch(src, pos, n_rows):
    n, width = src.shape
    tile = min(DISPATCH_TILE, n)
    zeros = jnp.zeros((n_rows, width), src.dtype)
    return pl.pallas_call(
        _dispatch_kernel,
        grid_spec=pltpu.PrefetchScalarGridSpec(
            num_scalar_prefetch=1,
            grid=(n // tile,),
            in_specs=[pl.BlockSpec((tile, width), lambda i, p: (i, 0)), pl.BlockSpec(memory_space=pl.ANY)],
            out_specs=pl.BlockSpec(memory_space=pl.ANY),
            scratch_shapes=[pltpu.SemaphoreType.DMA(())],
        ),
        out_shape=jax.ShapeDtypeStruct((n_rows, width), src.dtype),
        input_output_aliases={2: 0},
        compiler_params=_cparams(("arbitrary",)),
        name="moe_dispatch",
    )(pos, src, zeros)


def _ffn_kernel(layer, plan_ref, used_ref, x_ref, wg_hbm, wu_hbm, wd_hbm, y_ref, wg_buf, wu_buf, wd_buf, sems):
    def group_copies(g, to_slot):
        lo = pl.multiple_of(g * EXPERTS_PER_GROUP, EXPERTS_PER_GROUP)
        experts = pl.ds(lo, EXPERTS_PER_GROUP)
        return [pltpu.make_async_copy(hbm.at[layer, experts], buf.at[to_slot], sems.at[to_slot])
                for hbm, buf in ((wg_hbm, wg_buf), (wu_hbm, wu_buf), (wd_hbm, wd_buf))]

    def one_tile(i, rows):
        used = i < used_ref[0]
        group, slot, starts_group, next_group = plan_ref[0, i], plan_ref[1, i], plan_ref[2, i], plan_ref[3, i]

        @pl.when(used & (i == 0))
        def _():
            for cp in group_copies(group, slot):
                cp.start()

        @pl.when(used & (starts_group == 1) & (next_group >= 0))
        def _():
            for cp in group_copies(next_group, 1 - slot):
                cp.start()

        @pl.when(used & (starts_group == 1))
        def _():
            for cp in group_copies(group, slot):
                cp.wait()

        @pl.when(jnp.logical_not(used))
        def _():
            y_ref[rows, :] = jnp.zeros((MOE_TILE, D_MODEL), F32)

        @pl.when(used)
        def _():
            e1, e2 = plan_ref[4, i], plan_ref[5, i]
            x = x_ref[rows, 0:D_MODEL].astype(BF16)
            gate1 = jnp.dot(x, wg_buf[slot, e1], preferred_element_type=F32)
            gate2 = jnp.dot(x, wg_buf[slot, e2], preferred_element_type=F32)
            up1 = jnp.dot(x, wu_buf[slot, e1], preferred_element_type=F32)
            up2 = jnp.dot(x, wu_buf[slot, e2], preferred_element_type=F32)
            hid1 = (_silu(gate1) * up1).astype(BF16)
            hid2 = (_silu(gate2) * up2).astype(BF16)
            y1 = jnp.dot(hid1, wd_buf[slot, e1], preferred_element_type=F32)
            y2 = jnp.dot(hid2, wd_buf[slot, e2], preferred_element_type=F32)
            rw = x_ref[rows, D_MODEL:]
            y_ref[rows, :] = rw[:, 0:1] * y1 + rw[:, 1:2] * y2

    for sub in range(FFN_STEP_TILES):
        one_tile(pl.program_id(0) * FFN_STEP_TILES + sub, slice(sub * MOE_TILE, (sub + 1) * MOE_TILE))


def _expert_ffn(xs, tile_plan, n_used, wg, wu, wd, layer):
    r = xs.shape[0]
    step_rows = FFN_STEP_TILES * MOE_TILE
    assert r % step_rows == 0
    last_used = lambda nu: (nu[0] - 1) // FFN_STEP_TILES
    return pl.pallas_call(
        functools.partial(_ffn_kernel, layer),
        grid_spec=pltpu.PrefetchScalarGridSpec(
            num_scalar_prefetch=2,
            grid=(r // step_rows,),
            in_specs=[pl.BlockSpec((step_rows, HP_WIDTH), lambda i, plan, nu: (jnp.minimum(i, last_used(nu)), 0)),
                      pl.BlockSpec(memory_space=pl.ANY), pl.BlockSpec(memory_space=pl.ANY),
                      pl.BlockSpec(memory_space=pl.ANY)],
            out_specs=pl.BlockSpec((step_rows, D_MODEL), lambda i, plan, nu: (i, 0)),
            scratch_shapes=[pltpu.VMEM((2, EXPERTS_PER_GROUP, D_MODEL, D_EXPERT), BF16),
                            pltpu.VMEM((2, EXPERTS_PER_GROUP, D_MODEL, D_EXPERT), BF16),
                            pltpu.VMEM((2, EXPERTS_PER_GROUP, D_EXPERT, D_MODEL), BF16),
                            pltpu.SemaphoreType.DMA((2,))],
        ),
        out_shape=jax.ShapeDtypeStruct((r, D_MODEL), F32),
        compiler_params=_cparams(("arbitrary",)),
        name="expert_pair_ffn",
    )(tile_plan, n_used, xs, wg, wu, wd)


def _combine_kernel(final_norm, pos_ref, x_ref, gain_ref, ys_ref, o_ref, buf, sems):
    rows = o_ref.shape[0]
    i = pl.program_id(0)
    slot = i & 1

    def gather_tile(tile, to_slot):
        def issue(r, carry):
            pltpu.make_async_copy(ys_ref.at[pl.ds(pos_ref[tile * rows + r], 1)],
                                  buf.at[to_slot, pl.ds(r, 1)], sems.at[to_slot]).start()
            return carry

        lax.fori_loop(0, rows, issue, 0, unroll=8)

    @pl.when(i == 0)
    def _():
        gather_tile(0, 0)

    @pl.when(i + 1 < pl.num_programs(0))
    def _():
        gather_tile(i + 1, 1 - slot)

    pltpu.make_async_copy(ys_ref.at[pl.ds(0, rows)], buf.at[slot], sems.at[slot]).wait()
    out = x_ref[...] + buf[slot]
    if final_norm:
        ms = jnp.mean(out * out, axis=-1, keepdims=True)
        out = out * lax.rsqrt(ms + EPS) * gain_ref[...]
    o_ref[...] = out


def _combine(x2, ys, pos, gain, final_norm):
    n = x2.shape[0]
    return pl.pallas_call(
        functools.partial(_combine_kernel, final_norm),
        grid_spec=pltpu.PrefetchScalarGridSpec(
            num_scalar_prefetch=1,
            grid=(n // ROW_TILE,),
            in_specs=[pl.BlockSpec((ROW_TILE, D_MODEL), lambda i, p: (i, 0)),
                      pl.BlockSpec((1, D_MODEL), lambda i, p: (0, 0)),
                      pl.BlockSpec(memory_space=pl.ANY)],
            out_specs=pl.BlockSpec((ROW_TILE, D_MODEL), lambda i, p: (i, 0)),
            scratch_shapes=[pltpu.VMEM((2, ROW_TILE, D_MODEL), F32), pltpu.SemaphoreType.DMA((2,))],
        ),
        out_shape=jax.ShapeDtypeStruct((n, D_MODEL), F32),
        compiler_params=_cparams(("arbitrary",)),
        name="moe_combine",
    )(pos, x2, gain, ys)


def _plan(route, counts, n):
    cls, rank = route[:, 0], route[:, 1]
    counts = counts.astype(jnp.int32)
    tiles_per = (counts + MOE_TILE - 1) // MOE_TILE
    tile_end = jnp.cumsum(tiles_per)
    tile_start = tile_end - tiles_per
    class_ids = jnp.arange(N_CLASSES, dtype=jnp.int32)
    start_of = jnp.sum(jnp.where(cls[:, None] == class_ids[None, :], tile_start[None, :], 0), axis=1)
    pos = start_of * MOE_TILE + rank
    n_pairs = N_GROUPS * (EXPERTS_PER_GROUP * (EXPERTS_PER_GROUP - 1)) // 2
    n_tiles = n // MOE_TILE + n_pairs
    n_used = tile_end[-1]
    tile_ids = jnp.minimum(jnp.arange(n_tiles, dtype=jnp.int32), n_used - 1)
    tile_cls = jnp.sum((tile_end[None, :] <= tile_ids[:, None]).astype(jnp.int32), axis=1)
    per_group = EXPERTS_PER_GROUP * EXPERTS_PER_GROUP
    tile_group = tile_cls // per_group
    starts_group = jnp.concatenate([jnp.ones((1,), jnp.int32),
                                    (tile_group[1:] != tile_group[:-1]).astype(jnp.int32)])
    slot = (jnp.cumsum(starts_group) - 1) % 2
    present = jnp.sum(tiles_per.reshape(N_GROUPS, per_group), axis=1) > 0
    nxt = [jnp.int32(-1)] * N_GROUPS
    for g in range(N_GROUPS - 2, -1, -1):
        nxt[g] = jnp.where(present[g + 1], g + 1, nxt[g + 1])
    next_group = jnp.stack(nxt)[tile_group]
    tile_plan = jnp.stack([tile_group, slot, starts_group, next_group,
                           (tile_cls // EXPERTS_PER_GROUP) % EXPERTS_PER_GROUP, tile_cls % EXPERTS_PER_GROUP])
    return pos, tile_plan.astype(jnp.int32), n_used.reshape(1), n_tiles * MOE_TILE


def _rope_tables(t):
    inv_freq = ROPE_BASE ** (-jnp.arange(0, HEAD_DIM, 2, dtype=F32) / HEAD_DIM)
    ang = jnp.arange(t, dtype=F32)[:, None] * inv_freq[None, :]
    cos, sin = jnp.cos(ang), jnp.sin(ang)
    cos_tab = jnp.tile(cos, (1, 2 * LANES // HEAD_DIM))
    sin_tab = jnp.tile(jnp.concatenate([-sin, sin], axis=-1), (1, LANES // HEAD_DIM))
    return cos_tab, sin_tab


def _split_w_in(w):
    o = 0
    parts = {}
    for name, width in (("g_qkv", 3 * GDN_W), ("g_z", GDN_W), ("g_b", GDN_HEADS), ("g_a", GDN_HEADS),
                        ("r_q", RET_W), ("r_k", RET_W), ("r_v", RET_W), ("r_g", RET_W),
                        ("m_qk", 2 * MLSTM_W), ("m_v", MLSTM_W), ("m_o", MLSTM_W),
                        ("m_i", MLSTM_HEADS), ("m_f", MLSTM_HEADS)):
        parts[name] = w[:, o:o + width]
        o += width
    main = jnp.concatenate([parts[k] for k in ("m_qk", "m_v", "m_o", "g_qkv", "g_z", "r_q", "r_k", "r_v", "r_g")],
                           axis=1).astype(BF16)
    gate = jnp.concatenate([parts[k] for k in ("g_b", "g_a", "m_i", "m_f")], axis=1)
    gate = jnp.pad(gate, ((0, 0), (0, LANES - gate.shape[1]))).astype(BF16)
    return main, gate


def _lane_row(values, start):
    return jnp.zeros((LANES,), F32).at[start:start + values.shape[0]].set(values)


def kernel(x, norm_mix, w_in, gdn_conv, gdn_a_log, gdn_dt_bias, gdn_norm, ret_norm, mlstm_conv, mlstm_i_bias, mlstm_f_bias, mlstm_norm, w_out, norm_ffn, router_group, router_expert, router_bias, expert_gate, expert_up, expert_down, norm_final):
    b, t, d = x.shape
    n = b * t
    depth = w_in.shape[0]
    cos_tab, sin_tab = _rope_tables(t)
    wg_all, wu_all, wd_all = expert_gate.astype(BF16), expert_up.astype(BF16), expert_down.astype(BF16)
    x2 = x.reshape(n, d)
    for l in range(depth):
        w_main, w_gate = _split_w_in(w_in[l])
        proj, gates = _inproj(x2, norm_mix[l][None, :], w_main, w_gate)
        proj3 = proj.reshape(b, t, D_MAIN)
        gates3 = gates.reshape(b, t, LANES)
        gdn_prm = jnp.zeros((8, LANES), F32).at[0].set(_lane_row(gdn_a_log[l], G_A)).at[1].set(
            _lane_row(gdn_dt_bias[l], G_A))
        mlstm_prm = jnp.zeros((8, LANES), F32).at[0].set(_lane_row(mlstm_i_bias[l], M_I)).at[1].set(
            _lane_row(mlstm_f_bias[l], M_F))
        y_m = _mlstm_mixer(proj3, gates3, mlstm_conv[l], mlstm_prm, mlstm_norm[l][None, :])
        y_g = _gdn_mixer(proj3, gates3, gdn_conv[l], gdn_prm, gdn_norm[l][None, :])
        y_r = _ret_mixer(proj3, cos_tab, sin_tab, ret_norm[l][None, :])
        wo = w_out[l].astype(BF16)
        w_router = jnp.pad(jnp.concatenate([router_group[l], router_expert[l]], axis=1),
                           ((0, 0), (0, LANES - N_GROUPS - N_EXPERTS))).astype(BF16)
        rb = _lane_row(router_bias[l], N_GROUPS)[None, :]
        x2, h_packed, route, counts = _outproj(
            x2, y_m.reshape(n, MLSTM_W), y_g.reshape(n, GDN_W), y_r.reshape(n, RET_W),
            wo[GDN_W + RET_W:], wo[:GDN_W], wo[GDN_W:GDN_W + RET_W],
            norm_ffn[l][None, :], w_router, rb)
        pos, tile_plan, n_used, n_rows = _plan(route, counts[0], n)
        xs = _dispatch(h_packed, pos, n_rows)
        ys = _expert_ffn(xs, tile_plan, n_used, wg_all, wu_all, wd_all, l)
        x2 = _combine(x2, ys, pos, norm_final[None, :], l == depth - 1)
    return x2.reshape(b, t, d)
```

```python
import functools
import math

import jax
import jax.numpy as jnp
from jax import lax
from jax.experimental import pallas as pl
from jax.experimental.pallas import tpu as pltpu

F32 = jnp.float32
BF16 = jnp.bfloat16

D_MODEL = 1024
HEAD_DIM = 64
CHUNK = 64
GDN_HEADS, RET_HEADS, MLSTM_HEADS = 6, 4, 6
GDN_W, RET_W, MLSTM_W = GDN_HEADS * HEAD_DIM, RET_HEADS * HEAD_DIM, MLSTM_HEADS * HEAD_DIM
CONV_K = 4
ROPE_BASE = 10000.0
N_GROUPS, EXPERTS_PER_GROUP = 4, 8
N_EXPERTS = N_GROUPS * EXPERTS_PER_GROUP
D_EXPERT = D_MODEL // 4
EPS = 1e-6
LANES = 128
SEC_M, SEC_G, SEC_R = 4 * MLSTM_W, 4 * GDN_W, 4 * RET_W
D_MAIN = SEC_M + SEC_G + SEC_R
G_BETA, G_A, M_I, M_F = 0, GDN_HEADS, 2 * GDN_HEADS, 2 * GDN_HEADS + MLSTM_HEADS

ROW_TILE = 256
OUT_TILE = 512
OUT_SUB = 128
TIME_BLOCK = 256
LONG_BLOCK = 512
MOE_TILE = 128
FFN_STEP_TILES = 4
VMEM_LIMIT = 56 * 1024 * 1024


def _cparams(sem):
    return pltpu.CompilerParams(dimension_semantics=sem, vmem_limit_bytes=VMEM_LIMIT)


def _silu(x):
    return x * (1.0 / (1.0 + jnp.exp(-x)))


def _sigmoid(x):
    return 1.0 / (1.0 + jnp.exp(-x))


def _softplus(x):
    return jnp.maximum(x, 0.0) + jnp.log(1.0 + jnp.exp(-jnp.abs(x)))


def _dot(a, b):
    return jnp.dot(a.astype(BF16), b.astype(BF16), preferred_element_type=F32)


def _dot_nt(a, b):
    return lax.dot_general(a.astype(BF16), b.astype(BF16), (((1,), (1,)), ((), ())),
                           preferred_element_type=F32)


def _lane_masks(rows):
    lane = lax.broadcasted_iota(jnp.int32, (rows, LANES), 1)
    row = lax.broadcasted_iota(jnp.int32, (rows, LANES), 0)
    return lane, row


def _half_sum(x, m_a):
    s_a = jnp.sum(jnp.where(m_a, x, 0.0), axis=-1, keepdims=True)
    s_b = jnp.sum(jnp.where(m_a, 0.0, x), axis=-1, keepdims=True)
    return jnp.where(m_a, s_a, s_b)


def _half_max(x, m_a):
    s_a = jnp.max(jnp.where(m_a, x, -jnp.inf), axis=-1, keepdims=True)
    s_b = jnp.max(jnp.where(m_a, -jnp.inf, x), axis=-1, keepdims=True)
    return jnp.where(m_a, s_a, s_b)


def _col_form(g, lane_a, m_a):
    rows = g.shape[0]
    ca = jnp.broadcast_to(g[:, lane_a:lane_a + 1], (rows, LANES))
    cb = jnp.broadcast_to(g[:, lane_a + 1:lane_a + 2], (rows, LANES))
    return jnp.where(m_a, ca, cb)


def _row_form(col, eye):
    return jnp.sum(jnp.where(eye, col, 0.0), axis=0, keepdims=True)


def _block_diag(y, m_a):
    return jnp.concatenate([jnp.where(m_a, y, 0.0), jnp.where(m_a, 0.0, y)], axis=0)


def _pmul(x, y, m_a):
    return _dot(x, _block_diag(y, m_a))


def _outer_state(k, v, bd_mask):
    zero = jnp.zeros_like(k)
    kt = jnp.concatenate([k, zero], axis=0).T
    vp = jnp.concatenate([v, zero], axis=0)
    return jnp.where(bd_mask, _dot(kt, vp), 0.0)


def _chunk_cumsum(x):
    rows = x.shape[0]
    r = lax.broadcasted_iota(jnp.int32, (rows, LANES), 0) % CHUNK
    s = 1
    while s < CHUNK:
        x = x + jnp.where(r >= s, pltpu.roll(x, s, axis=0), 0.0)
        s *= 2
    return x


def _causal_conv(raw, cbuf, w_ref):
    rows = raw.shape[0]
    cbuf[8:8 + rows, :] = raw
    acc = raw * w_ref[CONV_K - 1:CONV_K, :]
    for j in range(CONV_K - 1):
        off = 8 - (CONV_K - 1) + j
        acc = acc + cbuf[off:off + rows, :] * w_ref[j:j + 1, :]
    cbuf[0:8, :] = raw[rows - 8:rows, :]
    return acc


def _inproj_kernel(x_ref, gain_ref, w_ref, wg_ref, o_ref, og_ref):
    x = x_ref[...]
    ms = jnp.mean(x * x, axis=-1, keepdims=True)
    h = (x * lax.rsqrt(ms + EPS) * gain_ref[...]).astype(BF16)

    def proj(lo, hi):
        return jnp.dot(h, w_ref[:, lo:hi], preferred_element_type=F32)

    step = 512
    for c in range(D_MAIN // step):
        o_ref[:, c * step:(c + 1) * step] = proj(c * step, (c + 1) * step)
    og_ref[...] = jnp.dot(h, wg_ref[...], preferred_element_type=F32)


def _inproj(x2, gain, w_main, w_gate):
    n = x2.shape[0]
    full = lambda a, b: pl.BlockSpec((a, b), lambda i: (0, 0))
    return pl.pallas_call(
        _inproj_kernel,
        grid=(n // OUT_TILE,),
        in_specs=[
            pl.BlockSpec((OUT_TILE, D_MODEL), lambda i: (i, 0)),
            full(1, D_MODEL), full(D_MODEL, D_MAIN), full(D_MODEL, LANES),
        ],
        out_specs=[
            pl.BlockSpec((OUT_TILE, D_MAIN), lambda i: (i, 0)),
            pl.BlockSpec((OUT_TILE, LANES), lambda i: (i, 0)),
        ],
        out_shape=[jax.ShapeDtypeStruct((n, D_MAIN), F32), jax.ShapeDtypeStruct((n, LANES), F32)],
        compiler_params=_cparams(("parallel",)),
        name="norm_inproj",
    )(x2, gain, w_main, w_gate)


N_CLASSES = N_GROUPS * EXPERTS_PER_GROUP * EXPERTS_PER_GROUP
HP_WIDTH = D_MODEL + LANES


def _route_rows(lg):
    rows = lg.shape[0]
    lane = lax.broadcasted_iota(jnp.int32, (rows, LANES), 1)
    lane_f = lane.astype(F32)
    neg = -jnp.inf

    def first_lane(mask):
        return jnp.min(jnp.where(mask, lane_f, float(LANES)), axis=-1, keepdims=True).astype(jnp.int32)

    gl = jnp.where(lane < N_GROUPS, lg, neg)
    gm = jnp.max(gl, axis=-1, keepdims=True)
    gsum = jnp.sum(jnp.where(lane < N_GROUPS, jnp.exp(gl - gm), 0.0), axis=-1, keepdims=True)
    gidx = first_lane(gl == gm)
    group_p = 1.0 / gsum
    in_group = (lane >= N_GROUPS) & (lane < N_GROUPS + N_EXPERTS) & (
        lax.shift_right_arithmetic(lane - N_GROUPS, 3) == gidx)
    el = jnp.where(in_group, lg, neg)
    m1 = jnp.max(el, axis=-1, keepdims=True)
    i1 = first_lane(el == m1)
    el2 = jnp.where(lane == i1, neg, el)
    m2 = jnp.max(el2, axis=-1, keepdims=True)
    i2 = first_lane(el2 == m2)
    e2 = jnp.exp(m2 - m1)
    w1 = group_p * (1.0 / (1.0 + e2))
    w2 = group_p * (e2 / (1.0 + e2))
    first_lower = i1 < i2
    base = N_GROUPS + gidx * EXPERTS_PER_GROUP
    lo = jnp.minimum(i1, i2) - base
    hi = jnp.maximum(i1, i2) - base
    cls = (gidx * EXPERTS_PER_GROUP + lo) * EXPERTS_PER_GROUP + hi
    return cls, jnp.where(first_lower, w1, w2), jnp.where(first_lower, w2, w1)


def _outproj_kernel(x_ref, ym_ref, yg_ref, yr_ref, wm_ref, wg_ref, wr_ref, gain_ref, wrt_ref, rb_ref,
                    xo_ref, hp_ref, rt_ref, cnt_ref, running):
    @pl.when(pl.program_id(0) == 0)
    def _():
        running[...] = jnp.zeros_like(running)

    rows = OUT_SUB
    subs = [slice(s * rows, (s + 1) * rows) for s in range(x_ref.shape[0] // rows)]
    acc = [x_ref[r, :] + jnp.dot(yg_ref[r, :], wg_ref[...], preferred_element_type=F32) for r in subs]
    acc = [a + jnp.dot(yr_ref[r, :], wr_ref[...], preferred_element_type=F32) for a, r in zip(acc, subs)]
    acc = [a + jnp.dot(ym_ref[r, :], wm_ref[...], preferred_element_type=F32) for a, r in zip(acc, subs)]
    for a, r in zip(acc, subs):
        xo_ref[r, :] = a
    hn = [a * lax.rsqrt(jnp.mean(a * a, axis=-1, keepdims=True) + EPS) * gain_ref[...] for a in acc]
    for h, r in zip(hn, subs):
        hp_ref[r, 0:D_MODEL] = h
    logits = [jnp.dot(h.astype(BF16), wrt_ref[...], preferred_element_type=F32) + rb_ref[...] for h in hn]
    routed = [_route_rows(lg) for lg in logits]

    cls_lane = lax.broadcasted_iota(jnp.int32, (rows, N_CLASSES), 1)
    before = (lax.broadcasted_iota(jnp.int32, (rows, rows), 1)
              < lax.broadcasted_iota(jnp.int32, (rows, rows), 0)).astype(BF16)
    onehot = [cls_lane == cls for cls, _, _ in routed]
    prefix = [jnp.dot(before, oh.astype(BF16), preferred_element_type=F32) for oh in onehot]
    counts = [jnp.sum(oh.astype(F32), axis=0, keepdims=True) for oh in onehot]
    lane = lax.broadcasted_iota(jnp.int32, (rows, LANES), 1)
    seen = running[0:1, :]
    for r, (cls, w_lo, w_hi), oh, pre, cnt in zip(subs, routed, onehot, prefix, counts):
        rank = jnp.sum(jnp.where(oh, pre + seen, 0.0), axis=-1, keepdims=True)
        seen = seen + cnt
        rt_ref[r, :] = jnp.where(lane == 0, cls, jnp.where(lane == 1, rank.astype(jnp.int32), 0))
        hp_ref[r, D_MODEL:] = jnp.where(lane == 0, w_lo, jnp.where(lane == 1, w_hi, 0.0))
    running[0:1, :] = seen
    cnt_ref[...] = jnp.broadcast_to(seen, cnt_ref.shape)


def _outproj(x2, ym, yg, yr, wo_m, wo_g, wo_r, gain, w_router, router_bias):
    n = x2.shape[0]
    row = lambda w: pl.BlockSpec((OUT_TILE, w), lambda i: (i, 0))
    full = lambda a, b: pl.BlockSpec((a, b), lambda i: (0, 0))
    return pl.pallas_call(
        _outproj_kernel,
        grid=(n // OUT_TILE,),
        in_specs=[row(D_MODEL), row(MLSTM_W), row(GDN_W), row(RET_W),
                  full(MLSTM_W, D_MODEL), full(GDN_W, D_MODEL), full(RET_W, D_MODEL),
                  full(1, D_MODEL), full(D_MODEL, LANES), full(1, LANES)],
        out_specs=[row(D_MODEL), row(HP_WIDTH), row(LANES), full(8, N_CLASSES)],
        out_shape=[jax.ShapeDtypeStruct((n, D_MODEL), F32),
                   jax.ShapeDtypeStruct((n, HP_WIDTH), F32),
                   jax.ShapeDtypeStruct((n, LANES), jnp.int32),
                   jax.ShapeDtypeStruct((8, N_CLASSES), F32)],
        scratch_shapes=[pltpu.VMEM((8, N_CLASSES), F32)],
        compiler_params=_cparams(("arbitrary",)),
        name="outproj_norm_router",
    )(x2, ym, yg, yr, wo_m, wo_g, wo_r, gain, w_router, router_bias)


def _gdn_kernel(x_ref, gt_ref, z_ref, conv_ref, prm_ref, gain_ref, y_ref,
                state, u_s, wq_s, qk_s, kt_s, sd_s, cbuf):
    tb = x_ref.shape[0]
    first = pl.program_id(1) == 0

    @pl.when(first)
    def _():
        for ref in (state, u_s, wq_s, qk_s, kt_s, sd_s):
            ref[...] = jnp.zeros_like(ref)
        cbuf[0:8, :] = jnp.zeros((8, cbuf.shape[1]), F32)

    gt = gt_ref[...]
    beta_all = _sigmoid(gt)
    g_all = -jnp.exp(prm_ref[0:1, :]) * _softplus(gt + prm_ref[1:2, :])
    gc_all = _chunk_cumsum(g_all)

    lane, row = _lane_masks(CHUNK)
    m_a = lane < HEAD_DIM
    lane_h = lane % HEAD_DIM
    eye = lane_h == row
    tril = lane_h <= row
    strict = lane_h < row
    eye_f = eye.astype(F32)
    lane2 = lax.broadcasted_iota(jnp.int32, (LANES, LANES), 1)
    row2 = lax.broadcasted_iota(jnp.int32, (LANES, LANES), 0)
    bd_mask = (lane2 < HEAD_DIM) == (row2 < HEAD_DIM)
    lane_t = lax.broadcasted_iota(jnp.int32, (tb, LANES), 1)
    m_a_t = lane_t < HEAD_DIM

    n_pairs = GDN_HEADS // 2
    n_chunks = tb // CHUNK
    units = [(p, c) for p in range(n_pairs) for c in range(n_chunks)]

    def unit_index(p, c):
        return p * n_chunks + c

    prev = {}
    for p, c in units:
        i = unit_index(p, c)
        prev[p, c] = (u_s[i], wq_s[i], qk_s[i], kt_s[i], sd_s[i, 0:1, :])
    s_bd = [state[p] for p in range(n_pairs)]

    def recurrence(c):
        rs = slice(c * CHUNK, (c + 1) * CHUNK)
        for p in range(n_pairs):
            cs = slice(p * LANES, (p + 1) * LANES)
            u_mat, wq, qk, k_dec_t, s_decay = prev[p, c]
            ws_qs = jnp.dot(wq, s_bd[p].astype(BF16), preferred_element_type=F32)
            v_new = u_mat - ws_qs[:CHUNK]
            o = ws_qs[CHUNK:] + jnp.dot(qk, _block_diag(v_new, m_a).astype(BF16), preferred_element_type=F32)
            v_pad = jnp.concatenate([v_new, jnp.zeros_like(v_new)], axis=0).astype(BF16)
            s_bd[p] = s_bd[p] * s_decay + jnp.where(
                bd_mask, jnp.dot(k_dec_t, v_pad, preferred_element_type=F32), 0.0)
            ms = _half_sum(o * o, m_a) * (1.0 / HEAD_DIM)
            y = o * lax.rsqrt(ms + EPS) * gain_ref[:, cs] * _silu(z_ref[rs, cs])
            y_ref[rs, cs] = y.astype(y_ref.dtype)

    recurrence(0)
    qkv = _silu(_causal_conv(x_ref[:, 0:3 * GDN_W], cbuf, conv_ref))
    ins = {}
    for p in range(n_pairs):
        q_t = qkv[:, p * LANES:(p + 1) * LANES]
        k_t = qkv[:, GDN_W + p * LANES:GDN_W + (p + 1) * LANES]
        v_t = qkv[:, 2 * GDN_W + p * LANES:2 * GDN_W + (p + 1) * LANES]
        q_t = q_t * lax.rsqrt(_half_sum(q_t * q_t, m_a_t) + EPS) * (HEAD_DIM ** -0.5)
        k_t = k_t * lax.rsqrt(_half_sum(k_t * k_t, m_a_t) + EPS)
        beta_t = _col_form(beta_all, G_BETA + 2 * p, m_a_t)
        gc_t = _col_form(gc_all, G_A + 2 * p, m_a_t)
        for c in range(n_chunks):
            rs = slice(c * CHUNK, (c + 1) * CHUNK)
            ins[p, c] = (q_t[rs], k_t[rs], v_t[rs], beta_t[rs], gc_t[rs])
    kkqk = {u: _dot_nt(jnp.concatenate([ins[u][1] * ins[u][3], ins[u][0]], axis=0), _block_diag(ins[u][1], m_a))
            for u in units}
    pre = {}
    for u in units:
        q, k, v, beta, gc = ins[u]
        decay = jnp.where(tril, jnp.exp(gc - _row_form(gc, eye)), 0.0)
        egc = jnp.exp(gc)
        g_last = gc[CHUNK - 1:CHUNK, :]
        k_dec = k * jnp.exp(g_last - gc)
        pre[u] = dict(
            a=jnp.where(strict, kkqk[u][:CHUNK] * decay, 0.0), qk=kkqk[u][CHUNK:] * decay,
            vb=v * beta, kbe=k * beta * egc, qe=q * egc, s_decay=jnp.exp(g_last),
            k_dec_t=jnp.concatenate([k_dec, jnp.zeros_like(k_dec)], axis=0).T)

    pw = {u: _pmul(pre[u]["a"], pre[u]["a"], m_a) for u in units}
    t_inv = {u: eye_f - pre[u]["a"] for u in units}
    later_chunks = list(range(1, n_chunks))
    for step in range(4):
        if step % 2 == 0 and later_chunks:
            recurrence(later_chunks.pop(0))
        both = {u: _dot(jnp.concatenate([pw[u], t_inv[u]], axis=0), _block_diag(pw[u], m_a)) for u in units}
        pw = {u: both[u][:CHUNK] for u in units}
        t_inv = {u: t_inv[u] + both[u][CHUNK:] for u in units}
    for c in later_chunks:
        recurrence(c)
    t_inv = {u: t_inv[u] + _pmul(t_inv[u], pw[u], m_a) for u in units}
    for p, c in units:
        d = pre[p, c]
        uw = _dot(t_inv[p, c], jnp.concatenate([_block_diag(d["vb"], m_a), _block_diag(d["kbe"], m_a)], axis=1))
        i = unit_index(p, c)
        u_s[i] = uw[:, :LANES]
        wq_s[i] = jnp.concatenate([uw[:, LANES:], d["qe"]], axis=0).astype(BF16)
        qk_s[i] = d["qk"].astype(BF16)
        kt_s[i] = d["k_dec_t"].astype(BF16)
        sd_s[i] = jnp.broadcast_to(d["s_decay"], (8, LANES))
    for p in range(n_pairs):
        state[p] = s_bd[p]


def _ret_kernel(x_ref, cos_ref, sin_ref, gain_ref, y_ref, state):
    tb = x_ref.shape[0]

    @pl.when(pl.program_id(1) == 0)
    def _():
        state[...] = jnp.zeros_like(state)

    lane, row = _lane_masks(CHUNK)
    m_a = lane < HEAD_DIM
    lane_h = lane % HEAD_DIM
    tril = lane_h <= row
    lane2 = lax.broadcasted_iota(jnp.int32, (LANES, LANES), 1)
    row2 = lax.broadcasted_iota(jnp.int32, (LANES, LANES), 0)
    bd_mask = (lane2 < HEAD_DIM) == (row2 < HEAD_DIM)
    lane_t = lax.broadcasted_iota(jnp.int32, (tb, LANES), 1)
    first_half = (lane_t % HEAD_DIM) < (HEAD_DIM // 2)
    cos = cos_ref[...]
    sin = sin_ref[...]
    rowf = row.astype(F32)
    pos_diff = (row - lane_h).astype(F32)

    def rope(x):
        swapped = jnp.where(first_half, pltpu.roll(x, LANES - HEAD_DIM // 2, axis=1),
                            pltpu.roll(x, HEAD_DIM // 2, axis=1))
        return x * cos + swapped * sin

    n_pairs = RET_HEADS // 2
    n_chunks = tb // CHUNK
    units = [(p, c) for p in range(n_pairs) for c in range(n_chunks)]
    chunk_decay, decay, q_scale, k_scale, qkv = [], [], [], [], {}
    for p in range(n_pairs):
        lg_a = math.log(1.0 - 2.0 ** (-5.0 - 2 * p))
        lg_b = math.log(1.0 - 2.0 ** (-5.0 - (2 * p + 1)))
        lg = jnp.where(m_a, lg_a, lg_b)
        decay.append(jnp.where(tril, jnp.exp(pos_diff * lg), 0.0))
        q_scale.append(jnp.exp(lg * (rowf + 1.0)))
        k_scale.append(jnp.exp(lg * (CHUNK - 1.0 - rowf)))
        chunk_decay.append(jnp.exp(lg[0:1, :] * CHUNK))
        q_t = rope(x_ref[:, p * LANES:(p + 1) * LANES])
        k_t = rope(x_ref[:, RET_W + p * LANES:RET_W + (p + 1) * LANES]) * (HEAD_DIM ** -0.5)
        for c in range(n_chunks):
            rs = slice(c * CHUNK, (c + 1) * CHUNK)
            qkv[p, c] = (q_t[rs], k_t[rs], x_ref[rs, 2 * RET_W + p * LANES:2 * RET_W + (p + 1) * LANES])
    scores = {(p, c): _dot_nt(qkv[p, c][0], _block_diag(qkv[p, c][1], m_a)) * decay[p] for p, c in units}
    o_intra = {u: _pmul(scores[u], qkv[u][2], m_a) for u in units}
    kv_all = {(p, c): _outer_state(qkv[p, c][1] * k_scale[p], qkv[p, c][2], bd_mask) for p, c in units}
    pre = {(p, c): (o_intra[p, c], qkv[p, c][0] * q_scale[p], kv_all[p, c]) for p, c in units}

    s_in = {}
    for p in range(n_pairs):
        s_bd = state[p]
        for c in range(n_chunks):
            s_in[p, c] = s_bd
            s_bd = chunk_decay[p] * s_bd + pre[p, c][2]
        state[p] = s_bd
    o_all = {u: pre[u][0] + _dot(pre[u][1], s_in[u]) for u in units}
    ms_all = {u: _half_sum(o_all[u] * o_all[u], m_a) * (1.0 / HEAD_DIM) for u in units}
    for p, c in units:
        rs = slice(c * CHUNK, (c + 1) * CHUNK)
        cs = slice(p * LANES, (p + 1) * LANES)
        gate = x_ref[rs, 3 * RET_W + p * LANES:3 * RET_W + (p + 1) * LANES]
        y = o_all[p, c] * lax.rsqrt(ms_all[p, c] + EPS) * gain_ref[:, cs] * _silu(gate)
        y_ref[rs, cs] = y.astype(y_ref.dtype)


def _mlstm_kernel(x_ref, gt_ref, conv_ref, prm_ref, gain_ref, y_ref, c_state, nm_state, cbuf):
    tb = x_ref.shape[0]
    first = pl.program_id(1) == 0

    @pl.when(first)
    def _():
        c_state[...] = jnp.zeros_like(c_state)
        nm_state[...] = jnp.zeros_like(nm_state)
        cbuf[0:8, :] = jnp.zeros((8, cbuf.shape[1]), F32)

    qk_all = _silu(_causal_conv(x_ref[:, 0:2 * MLSTM_W], cbuf, conv_ref))
    gt = gt_ref[...]
    log_i_all = gt + prm_ref[0:1, :]
    f_pre = gt + prm_ref[1:2, :]
    log_f_all = jnp.minimum(f_pre, 0.0) - jnp.log(1.0 + jnp.exp(-jnp.abs(f_pre)))
    bc_all = _chunk_cumsum(log_f_all)

    lane, row = _lane_masks(CHUNK)
    m_a = lane < HEAD_DIM
    lane_h = lane % HEAD_DIM
    eye = lane_h == row
    tril = lane_h <= row
    lane2 = lax.broadcasted_iota(jnp.int32, (LANES, LANES), 1)
    row2 = lax.broadcasted_iota(jnp.int32, (LANES, LANES), 0)
    bd_mask = (lane2 < HEAD_DIM) == (row2 < HEAD_DIM)
    lane_t = lax.broadcasted_iota(jnp.int32, (tb, LANES), 1)
    m_a_t = lane_t < HEAD_DIM

    n_pairs = MLSTM_HEADS // 2
    n_chunks = tb // CHUNK
    units = [(p, c) for p in range(n_pairs) for c in range(n_chunks)]
    gates = {}
    for p in range(n_pairs):
        b_t = _col_form(bc_all, M_F + 2 * p, m_a_t)
        i_t = _col_form(log_i_all, M_I + 2 * p, m_a_t)
        for c in range(n_chunks):
            rs = slice(c * CHUNK, (c + 1) * CHUNK)
            gates[p, c] = (b_t[rs], i_t[rs])
    qkv = {(p, c): (qk_all[c * CHUNK:(c + 1) * CHUNK, p * LANES:(p + 1) * LANES] * (HEAD_DIM ** -0.5),
                    qk_all[c * CHUNK:(c + 1) * CHUNK, MLSTM_W + p * LANES:MLSTM_W + (p + 1) * LANES],
                    x_ref[c * CHUNK:(c + 1) * CHUNK, 2 * MLSTM_W + p * LANES:2 * MLSTM_W + (p + 1) * LANES])
           for p, c in units}
    qk = {u: _dot_nt(qkv[u][0], _block_diag(qkv[u][1], m_a)) for u in units}
    log_d = {u: jnp.where(tril, gates[u][0] - _row_form(gates[u][0], eye) + _row_form(gates[u][1], eye), -jnp.inf)
             for u in units}
    m_intra = {u: _half_max(log_d[u], m_a) for u in units}
    a_c = {u: gates[u][0][CHUNK - 1:CHUNK, :] - gates[u][0] + gates[u][1] for u in units}
    a_max = {u: jnp.max(a_c[u], axis=0, keepdims=True) for u in units}
    m_in, m_out = {}, {}
    for p in range(n_pairs):
        m_row = nm_state[p, 1:2, :]
        for c in range(n_chunks):
            m_in[p, c] = m_row
            m_row = jnp.maximum(gates[p, c][0][CHUNK - 1:CHUNK, :] + m_row, a_max[p, c])
            m_out[p, c] = m_row
    m_t = {u: jnp.maximum(gates[u][0] + m_in[u], m_intra[u]) for u in units}
    inter = {u: jnp.exp(gates[u][0] + m_in[u] - m_t[u]) for u in units}
    wmat = {u: jnp.where(tril, jnp.exp(log_d[u] - m_t[u]), 0.0) * qk[u] for u in units}
    kw = {u: qkv[u][1] * jnp.exp(a_c[u] - m_out[u]) for u in units}
    dec = {u: jnp.exp(gates[u][0][CHUNK - 1:CHUNK, :] + m_in[u] - m_out[u]) for u in units}
    num_intra = {u: _pmul(wmat[u], qkv[u][2], m_a) for u in units}
    den_intra = {u: _half_sum(wmat[u], m_a) for u in units}
    kv = {u: _outer_state(kw[u], qkv[u][2], bd_mask) for u in units}
    kn = {u: jnp.sum(kw[u], axis=0, keepdims=True) for u in units}
    c_in, n_in = {}, {}
    for p in range(n_pairs):
        c_bd, n_row = c_state[p], nm_state[p, 0:1, :]
        for c in range(n_chunks):
            c_in[p, c], n_in[p, c] = c_bd, n_row
            c_bd = dec[p, c] * c_bd + kv[p, c]
            n_row = dec[p, c] * n_row + kn[p, c]
        c_state[p] = c_bd
        nm_state[p, 0:1, :] = n_row
        nm_state[p, 1:2, :] = m_out[p, n_chunks - 1]
    num = {u: inter[u] * _dot(qkv[u][0], c_in[u]) + num_intra[u] for u in units}
    den = {u: inter[u] * _half_sum(qkv[u][0] * n_in[u], m_a) + den_intra[u] for u in units}
    h = {u: num[u] / jnp.maximum(jnp.abs(den[u]), jnp.exp(-m_t[u])) for u in units}
    ms = {u: _half_sum(h[u] * h[u], m_a) * (1.0 / HEAD_DIM) for u in units}
    for p, c in units:
        rs = slice(c * CHUNK, (c + 1) * CHUNK)
        cs = slice(p * LANES, (p + 1) * LANES)
        o_logit = x_ref[rs, 3 * MLSTM_W + p * LANES:3 * MLSTM_W + (p + 1) * LANES]
        y = _sigmoid(o_logit) * (h[p, c] * lax.rsqrt(ms[p, c] + EPS) * gain_ref[:, cs])
        y_ref[rs, cs] = y.astype(y_ref.dtype)


def _mixer_call(kernel, name, proj3, sec_block, sec_width, out_width, extra_inputs, extra_specs, scratch):
    b, t, _ = proj3.shape
    tb = min(LONG_BLOCK, t)
    return pl.pallas_call(
        kernel,
        grid=(b, t // tb),
        in_specs=[pl.BlockSpec((None, tb, sec_width), lambda i, j: (i, j, sec_block))] + extra_specs(tb),
        out_specs=pl.BlockSpec((None, tb, out_width), lambda i, j: (i, j, 0)),
        out_shape=jax.ShapeDtypeStruct((b, t, out_width), BF16),
        scratch_shapes=scratch(tb),
        compiler_params=_cparams(("parallel", "arbitrary")),
        name=name,
    )(proj3, *extra_inputs)


def _full2(a, b):
    return pl.BlockSpec((a, b), lambda i, j: (0, 0))


def _gdn_mixer(proj3, gates3, conv_w, prm, gain):
    b, t, _ = proj3.shape
    tb = min(TIME_BLOCK, t)
    n_blocks = t // tb
    n_units = (GDN_HEADS // 2) * (tb // CHUNK)
    cur = lambda i, j: (i, jnp.minimum(j, n_blocks - 1))
    prv = lambda i, j: (i, jnp.maximum(j - 1, 0))
    z_block = (SEC_M + 3 * GDN_W) // GDN_W
    assert z_block * GDN_W == SEC_M + 3 * GDN_W
    return pl.pallas_call(
        _gdn_kernel,
        grid=(b, n_blocks + 1),
        in_specs=[pl.BlockSpec((None, tb, SEC_G), lambda i, j: cur(i, j) + (1,)),
                  pl.BlockSpec((None, tb, LANES), lambda i, j: cur(i, j) + (0,)),
                  pl.BlockSpec((None, tb, GDN_W), lambda i, j: prv(i, j) + (z_block,)),
                  _full2(CONV_K, 3 * GDN_W), _full2(8, LANES), _full2(1, GDN_W)],
        out_specs=pl.BlockSpec((None, tb, GDN_W), lambda i, j: prv(i, j) + (0,)),
        out_shape=jax.ShapeDtypeStruct((b, t, GDN_W), BF16),
        scratch_shapes=[pltpu.VMEM((GDN_HEADS // 2, LANES, LANES), F32),
                        pltpu.VMEM((n_units, CHUNK, LANES), F32),
                        pltpu.VMEM((n_units, 2 * CHUNK, LANES), BF16),
                        pltpu.VMEM((n_units, CHUNK, LANES), BF16),
                        pltpu.VMEM((n_units, LANES, LANES), BF16),
                        pltpu.VMEM((n_units, 8, LANES), F32),
                        pltpu.VMEM((tb + 8, 3 * GDN_W), F32)],
        compiler_params=_cparams(("parallel", "arbitrary")),
        name="gdn_mixer",
    )(proj3, gates3, proj3, conv_w, prm, gain)


def _mlstm_mixer(proj3, gates3, conv_w, prm, gain):
    return _mixer_call(
        _mlstm_kernel, "mlstm_mixer", proj3, 0, SEC_M, MLSTM_W, (gates3, conv_w, prm, gain),
        lambda tb: [pl.BlockSpec((None, tb, LANES), lambda i, j: (i, j, 0)),
                    _full2(CONV_K, 2 * MLSTM_W), _full2(8, LANES), _full2(1, MLSTM_W)],
        lambda tb: [pltpu.VMEM((MLSTM_HEADS // 2, LANES, LANES), F32),
                    pltpu.VMEM((MLSTM_HEADS // 2, 8, LANES), F32),
                    pltpu.VMEM((tb + 8, 2 * MLSTM_W), F32)])


def _ret_mixer(proj3, cos_tab, sin_tab, gain):
    return _mixer_call(
        _ret_kernel, "retention_mixer", proj3, 3, SEC_R, RET_W, (cos_tab, sin_tab, gain),
        lambda tb: [pl.BlockSpec((tb, LANES), lambda i, j: (j, 0)),
                    pl.BlockSpec((tb, LANES), lambda i, j: (j, 0)),
                    _full2(1, RET_W)],
        lambda tb: [pltpu.VMEM((RET_HEADS // 2, LANES, LANES), F32)])


DISPATCH_TILE = 1024


def _dispatch_kernel(pos_ref, src_ref, zero_ref, out_ref, sem):
    del zero_ref
    rows = src_ref.shape[0]
    base = pl.program_id(0) * rows

    def issue(r, carry):
        pltpu.make_async_copy(src_ref.at[pl.ds(r, 1)], out_ref.at[pl.ds(pos_ref[base + r], 1)], sem).start()
        return carry

    lax.fori_loop(0, rows, issue, 0, unroll=8)
    pltpu.make_async_copy(src_ref, out_ref.at[pl.ds(0, rows)], sem).wait()


def _dispatch(src, pos, n_rows):
    n, width = src.shape
    tile = min(DISPATCH_TILE, n)
    zeros = jnp.zeros((n_rows, width), src.dtype)
    return pl.pallas_call(
        _dispatch_kernel,
        grid_spec=pltpu.PrefetchScalarGridSpec(
            num_scalar_prefetch=1,
            grid=(n // tile,),
            in_specs=[pl.BlockSpec((tile, width), lambda i, p: (i, 0)), pl.BlockSpec(memory_space=pl.ANY)],
            out_specs=pl.BlockSpec(memory_space=pl.ANY),
            scratch_shapes=[pltpu.SemaphoreType.DMA(())],
        ),
        out_shape=jax.ShapeDtypeStruct((n_rows, width), src.dtype),
        input_output_aliases={2: 0},
        compiler_params=_cparams(("arbitrary",)),
        name="moe_dispatch",
    )(pos, src, zeros)


def _ffn_kernel(layer, plan_ref, used_ref, x_ref, wg_hbm, wu_hbm, wd_hbm, y_ref, wg_buf, wu_buf, wd_buf, sems):
    def group_copies(g, to_slot):
        lo = pl.multiple_of(g * EXPERTS_PER_GROUP, EXPERTS_PER_GROUP)
        experts = pl.ds(lo, EXPERTS_PER_GROUP)
        return [pltpu.make_async_copy(hbm.at[layer, experts], buf.at[to_slot], sems.at[to_slot])
                for hbm, buf in ((wg_hbm, wg_buf), (wu_hbm, wu_buf), (wd_hbm, wd_buf))]

    def one_tile(i, rows):
        used = i < used_ref[0]
        group, slot, starts_group, next_group = plan_ref[0, i], plan_ref[1, i], plan_ref[2, i], plan_ref[3, i]

        @pl.when(used & (i == 0))
        def _():
            for cp in group_copies(group, slot):
                cp.start()

        @pl.when(used & (starts_group == 1) & (next_group >= 0))
        def _():
            for cp in group_copies(next_group, 1 - slot):
                cp.start()

        @pl.when(used & (starts_group == 1))
        def _():
            for cp in group_copies(group, slot):
                cp.wait()

        @pl.when(jnp.logical_not(used))
        def _():
            y_ref[rows, :] = jnp.zeros((MOE_TILE, D_MODEL), F32)

        @pl.when(used)
        def _():
            e1, e2 = plan_ref[4, i], plan_ref[5, i]
            x = x_ref[rows, 0:D_MODEL].astype(BF16)
            gate1 = jnp.dot(x, wg_buf[slot, e1], preferred_element_type=F32)
            gate2 = jnp.dot(x, wg_buf[slot, e2], preferred_element_type=F32)
            up1 = jnp.dot(x, wu_buf[slot, e1], preferred_element_type=F32)
            up2 = jnp.dot(x, wu_buf[slot, e2], preferred_element_type=F32)
            hid1 = (_silu(gate1) * up1).astype(BF16)
            hid2 = (_silu(gate2) * up2).astype(BF16)
            y1 = jnp.dot(hid1, wd_buf[slot, e1], preferred_element_type=F32)
            y2 = jnp.dot(hid2, wd_buf[slot, e2], preferred_element_type=F32)
            rw = x_ref[rows, D_MODEL:]
            y_ref[rows, :] = rw[:, 0:1] * y1 + rw[:, 1:2] * y2

    for sub in range(FFN_STEP_TILES):
        one_tile(pl.program_id(0) * FFN_STEP_TILES + sub, slice(sub * MOE_TILE, (sub + 1) * MOE_TILE))


def _expert_ffn(xs, tile_plan, n_used, wg, wu, wd, layer):
    r = xs.shape[0]
    step_rows = FFN_STEP_TILES * MOE_TILE
    assert r % step_rows == 0
    last_used = lambda nu: (nu[0] - 1) // FFN_STEP_TILES
    return pl.pallas_call(
        functools.partial(_ffn_kernel, layer),
        grid_spec=pltpu.PrefetchScalarGridSpec(
            num_scalar_prefetch=2,
            grid=(r // step_rows,),
            in_specs=[pl.BlockSpec((step_rows, HP_WIDTH), lambda i, plan, nu: (jnp.minimum(i, last_used(nu)), 0)),
                      pl.BlockSpec(memory_space=pl.ANY), pl.BlockSpec(memory_space=pl.ANY),
                      pl.BlockSpec(memory_space=pl.ANY)],
            out_specs=pl.BlockSpec((step_rows, D_MODEL), lambda i, plan, nu: (i, 0)),
            scratch_shapes=[pltpu.VMEM((2, EXPERTS_PER_GROUP, D_MODEL, D_EXPERT), BF16),
                            pltpu.VMEM((2, EXPERTS_PER_GROUP, D_MODEL, D_EXPERT), BF16),
                            pltpu.VMEM((2, EXPERTS_PER_GROUP, D_EXPERT, D_MODEL), BF16),
                            pltpu.SemaphoreType.DMA((2,))],
        ),
        out_shape=jax.ShapeDtypeStruct((r, D_MODEL), F32),
        compiler_params=_cparams(("arbitrary",)),
        name="expert_pair_ffn",
    )(tile_plan, n_used, xs, wg, wu, wd)


def _combine_kernel(final_norm, pos_ref, x_ref, gain_ref, ys_ref, o_ref, buf, sems):
    rows = o_ref.shape[0]
    i = pl.program_id(0)
    slot = i & 1

    def gather_tile(tile, to_slot):
        def issue(r, carry):
            pltpu.make_async_copy(ys_ref.at[pl.ds(pos_ref[tile * rows + r], 1)],
                                  buf.at[to_slot, pl.ds(r, 1)], sems.at[to_slot]).start()
            return carry

        lax.fori_loop(0, rows, issue, 0, unroll=8)

    @pl.when(i == 0)
    def _():
        gather_tile(0, 0)

    @pl.when(i + 1 < pl.num_programs(0))
    def _():
        gather_tile(i + 1, 1 - slot)

    pltpu.make_async_copy(ys_ref.at[pl.ds(0, rows)], buf.at[slot], sems.at[slot]).wait()
    out = x_ref[...] + buf[slot]
    if final_norm:
        ms = jnp.mean(out * out, axis=-1, keepdims=True)
        out = out * lax.rsqrt(ms + EPS) * gain_ref[...]
    o_ref[...] = out


def _combine(x2, ys, pos, gain, final_norm):
    n = x2.shape[0]
    return pl.pallas_call(
        functools.partial(_combine_kernel, final_norm),
        grid_spec=pltpu.PrefetchScalarGridSpec(
            num_scalar_prefetch=1,
            grid=(n // ROW_TILE,),
            in_specs=[pl.BlockSpec((ROW_TILE, D_MODEL), lambda i, p: (i, 0)),
                      pl.BlockSpec((1, D_MODEL), lambda i, p: (0, 0)),
                      pl.BlockSpec(memory_space=pl.ANY)],
            out_specs=pl.BlockSpec((ROW_TILE, D_MODEL), lambda i, p: (i, 0)),
            scratch_shapes=[pltpu.VMEM((2, ROW_TILE, D_MODEL), F32), pltpu.SemaphoreType.DMA((2,))],
        ),
        out_shape=jax.ShapeDtypeStruct((n, D_MODEL), F32),
        compiler_params=_cparams(("arbitrary",)),
        name="moe_combine",
    )(pos, x2, gain, ys)


def _plan(route, counts, n):
    cls, rank = route[:, 0], route[:, 1]
    counts = counts.astype(jnp.int32)
    tiles_per = (counts + MOE_TILE - 1) // MOE_TILE
    tile_end = jnp.cumsum(tiles_per)
    tile_start = tile_end - tiles_per
    class_ids = jnp.arange(N_CLASSES, dtype=jnp.int32)
    start_of = jnp.sum(jnp.where(cls[:, None] == class_ids[None, :], tile_start[None, :], 0), axis=1)
    pos = start_of * MOE_TILE + rank
    n_pairs = N_GROUPS * (EXPERTS_PER_GROUP * (EXPERTS_PER_GROUP - 1)) // 2
    n_tiles = n // MOE_TILE + n_pairs
    n_used = tile_end[-1]
    tile_ids = jnp.minimum(jnp.arange(n_tiles, dtype=jnp.int32), n_used - 1)
    tile_cls = jnp.sum((tile_end[None, :] <= tile_ids[:, None]).astype(jnp.int32), axis=1)
    per_group = EXPERTS_PER_GROUP * EXPERTS_PER_GROUP
    tile_group = tile_cls // per_group
    starts_group = jnp.concatenate([jnp.ones((1,), jnp.int32),
                                    (tile_group[1:] != tile_group[:-1]).astype(jnp.int32)])
    slot = (jnp.cumsum(starts_group) - 1) % 2
    present = jnp.sum(tiles_per.reshape(N_GROUPS, per_group), axis=1) > 0
    nxt = [jnp.int32(-1)] * N_GROUPS
    for g in range(N_GROUPS - 2, -1, -1):
        nxt[g] = jnp.where(present[g + 1], g + 1, nxt[g + 1])
    next_group = jnp.stack(nxt)[tile_group]
    tile_plan = jnp.stack([tile_group, slot, starts_group, next_group,
                           (tile_cls // EXPERTS_PER_GROUP) % EXPERTS_PER_GROUP, tile_cls % EXPERTS_PER_GROUP])
    return pos, tile_plan.astype(jnp.int32), n_used.reshape(1), n_tiles * MOE_TILE


def _rope_tables(t):
    inv_freq = ROPE_BASE ** (-jnp.arange(0, HEAD_DIM, 2, dtype=F32) / HEAD_DIM)
    ang = jnp.arange(t, dtype=F32)[:, None] * inv_freq[None, :]
    cos, sin = jnp.cos(ang), jnp.sin(ang)
    cos_tab = jnp.tile(cos, (1, 2 * LANES // HEAD_DIM))
    sin_tab = jnp.tile(jnp.concatenate([-sin, sin], axis=-1), (1, LANES // HEAD_DIM))
    return cos_tab, sin_tab


def _split_w_in(w):
    g0, g1 = 0, SEC_G
    r0, r1 = g1 + 2 * GDN_HEADS, g1 + 2 * GDN_HEADS + SEC_R
    m0, m1 = r1, r1 + SEC_M
    assert m1 + 2 * MLSTM_HEADS == w.shape[1]
    main = jnp.concatenate([w[:, m0:m1], w[:, g0:g1], w[:, r0:r1]], axis=1).astype(BF16)
    gate = jnp.concatenate([w[:, g1:r0], w[:, m1:]], axis=1)
    gate = jnp.pad(gate, ((0, 0), (0, LANES - gate.shape[1]))).astype(BF16)
    return main, gate


def _lane_row(values, start):
    return jnp.zeros((LANES,), F32).at[start:start + values.shape[0]].set(values)


def kernel(x, norm_mix, w_in, gdn_conv, gdn_a_log, gdn_dt_bias, gdn_norm, ret_norm, mlstm_conv, mlstm_i_bias, mlstm_f_bias, mlstm_norm, w_out, norm_ffn, router_group, router_expert, router_bias, expert_gate, expert_up, expert_down, norm_final):
    b, t, d = x.shape
    n = b * t
    depth = w_in.shape[0]
    cos_tab, sin_tab = _rope_tables(t)
    wg_all, wu_all, wd_all = expert_gate.astype(BF16), expert_up.astype(BF16), expert_down.astype(BF16)
    x2 = x.reshape(n, d)
    for l in range(depth):
        w_main, w_gate = _split_w_in(w_in[l])
        proj, gates = _inproj(x2, norm_mix[l][None, :], w_main, w_gate)
        proj3 = proj.reshape(b, t, D_MAIN)
        gates3 = gates.reshape(b, t, LANES)
        gdn_prm = jnp.zeros((8, LANES), F32).at[0].set(_lane_row(gdn_a_log[l], G_A)).at[1].set(
            _lane_row(gdn_dt_bias[l], G_A))
        mlstm_prm = jnp.zeros((8, LANES), F32).at[0].set(_lane_row(mlstm_i_bias[l], M_I)).at[1].set(
            _lane_row(mlstm_f_bias[l], M_F))
        y_m = _mlstm_mixer(proj3, gates3, mlstm_conv[l], mlstm_prm, mlstm_norm[l][None, :])
        y_g = _gdn_mixer(proj3, gates3, gdn_conv[l], gdn_prm, gdn_norm[l][None, :])
        y_r = _ret_mixer(proj3, cos_tab, sin_tab, ret_norm[l][None, :])
        wo = w_out[l].astype(BF16)
        w_router = jnp.pad(jnp.concatenate([router_group[l], router_expert[l]], axis=1),
                           ((0, 0), (0, LANES - N_GROUPS - N_EXPERTS))).astype(BF16)
        rb = _lane_row(router_bias[l], N_GROUPS)[None, :]
        x2, h_packed, route, counts = _outproj(
            x2, y_m.reshape(n, MLSTM_W), y_g.reshape(n, GDN_W), y_r.reshape(n, RET_W),
            wo[GDN_W + RET_W:], wo[:GDN_W], wo[GDN_W:GDN_W + RET_W],
            norm_ffn[l][None, :], w_router, rb)
        pos, tile_plan, n_used, n_rows = _plan(route, counts[0], n)
        xs = _dispatch(h_packed, pos, n_rows)
        ys = _expert_ffn(xs, tile_plan, n_used, wg_all, wu_all, wd_all, l)
        x2 = _combine(x2, ys, pos, norm_final[None, :], l == depth - 1)
    return x2.reshape(b, t, d)
```

```python
import functools
import math

import jax
import jax.numpy as jnp
from jax import lax
from jax.experimental import pallas as pl
from jax.experimental.pallas import tpu as pltpu

F32 = jnp.float32
BF16 = jnp.bfloat16

D_MODEL = 1024
HEAD_DIM = 64
CHUNK = 64
GDN_HEADS, RET_HEADS, MLSTM_HEADS = 6, 4, 6
GDN_W, RET_W, MLSTM_W = GDN_HEADS * HEAD_DIM, RET_HEADS * HEAD_DIM, MLSTM_HEADS * HEAD_DIM
CONV_K = 4
ROPE_BASE = 10000.0
N_GROUPS, EXPERTS_PER_GROUP = 4, 8
N_EXPERTS = N_GROUPS * EXPERTS_PER_GROUP
D_EXPERT = D_MODEL // 4
EPS = 1e-6
LANES = 128
SEC_M, SEC_G, SEC_R = 4 * MLSTM_W, 4 * GDN_W, 4 * RET_W
D_MAIN = SEC_M + SEC_G + SEC_R
G_BETA, G_A, M_I, M_F = 0, GDN_HEADS, 2 * GDN_HEADS, 2 * GDN_HEADS + MLSTM_HEADS

ROW_TILE = 512
OUT_TILE = 512
OUT_SUB = 128
TIME_BLOCK = 256
LONG_BLOCK = 512
MOE_TILE = 128
FFN_STEP_TILES = 8
VMEM_LIMIT = 56 * 1024 * 1024


def _cparams(sem):
    return pltpu.CompilerParams(dimension_semantics=sem, vmem_limit_bytes=VMEM_LIMIT)


def _silu(x):
    return x * (1.0 / (1.0 + jnp.exp(-x)))


def _sigmoid(x):
    return 1.0 / (1.0 + jnp.exp(-x))


def _softplus(x):
    return jnp.maximum(x, 0.0) + jnp.log(1.0 + jnp.exp(-jnp.abs(x)))


def _dot(a, b):
    return jnp.dot(a.astype(BF16), b.astype(BF16), preferred_element_type=F32)


def _dot_nt(a, b):
    return lax.dot_general(a.astype(BF16), b.astype(BF16), (((1,), (1,)), ((), ())),
                           preferred_element_type=F32)


def _lane_masks(rows):
    lane = lax.broadcasted_iota(jnp.int32, (rows, LANES), 1)
    row = lax.broadcasted_iota(jnp.int32, (rows, LANES), 0)
    return lane, row


def _half_sum(x, m_a):
    s_a = jnp.sum(jnp.where(m_a, x, 0.0), axis=-1, keepdims=True)
    s_b = jnp.sum(jnp.where(m_a, 0.0, x), axis=-1, keepdims=True)
    return jnp.where(m_a, s_a, s_b)


def _half_max(x, m_a):
    s_a = jnp.max(jnp.where(m_a, x, -jnp.inf), axis=-1, keepdims=True)
    s_b = jnp.max(jnp.where(m_a, -jnp.inf, x), axis=-1, keepdims=True)
    return jnp.where(m_a, s_a, s_b)


def _col_form(g, lane_a, m_a):
    rows = g.shape[0]
    ca = jnp.broadcast_to(g[:, lane_a:lane_a + 1], (rows, LANES))
    cb = jnp.broadcast_to(g[:, lane_a + 1:lane_a + 2], (rows, LANES))
    return jnp.where(m_a, ca, cb)


def _row_form(col, eye):
    return jnp.sum(jnp.where(eye, col, 0.0), axis=0, keepdims=True)


def _block_diag(y, m_a):
    return jnp.concatenate([jnp.where(m_a, y, 0.0), jnp.where(m_a, 0.0, y)], axis=0)


def _pmul(x, y, m_a):
    return _dot(x, _block_diag(y, m_a))


def _outer_state(k, v, bd_mask):
    zero = jnp.zeros_like(k)
    kt = jnp.concatenate([k, zero], axis=0).T
    vp = jnp.concatenate([v, zero], axis=0)
    return jnp.where(bd_mask, _dot(kt, vp), 0.0)


def _chunk_cumsum(x):
    rows = x.shape[0]
    r = lax.broadcasted_iota(jnp.int32, (rows, LANES), 0) % CHUNK
    s = 1
    while s < CHUNK:
        x = x + jnp.where(r >= s, pltpu.roll(x, s, axis=0), 0.0)
        s *= 2
    return x


def _causal_conv(raw, cbuf, w_ref):
    rows = raw.shape[0]
    cbuf[8:8 + rows, :] = raw
    acc = raw * w_ref[CONV_K - 1:CONV_K, :]
    for j in range(CONV_K - 1):
        off = 8 - (CONV_K - 1) + j
        acc = acc + cbuf[off:off + rows, :] * w_ref[j:j + 1, :]
    cbuf[0:8, :] = raw[rows - 8:rows, :]
    return acc


def _inproj_kernel(x_ref, gain_ref, w_ref, wg_ref, o_ref, og_ref):
    x = x_ref[...]
    ms = jnp.mean(x * x, axis=-1, keepdims=True)
    h = (x * lax.rsqrt(ms + EPS) * gain_ref[...]).astype(BF16)

    def proj(lo, hi):
        return jnp.dot(h, w_ref[:, lo:hi], preferred_element_type=F32)

    step = 512
    for c in range(D_MAIN // step):
        o_ref[:, c * step:(c + 1) * step] = proj(c * step, (c + 1) * step)
    og_ref[...] = jnp.dot(h, wg_ref[...], preferred_element_type=F32)


def _inproj(x2, gain, w_main, w_gate):
    n = x2.shape[0]
    full = lambda a, b: pl.BlockSpec((a, b), lambda i: (0, 0))
    return pl.pallas_call(
        _inproj_kernel,
        grid=(n // OUT_TILE,),
        in_specs=[
            pl.BlockSpec((OUT_TILE, D_MODEL), lambda i: (i, 0)),
            full(1, D_MODEL), full(D_MODEL, D_MAIN), full(D_MODEL, LANES),
        ],
        out_specs=[
            pl.BlockSpec((OUT_TILE, D_MAIN), lambda i: (i, 0)),
            pl.BlockSpec((OUT_TILE, LANES), lambda i: (i, 0)),
        ],
        out_shape=[jax.ShapeDtypeStruct((n, D_MAIN), F32), jax.ShapeDtypeStruct((n, LANES), F32)],
        compiler_params=_cparams(("parallel",)),
        name="norm_inproj",
    )(x2, gain, w_main, w_gate)


N_CLASSES = N_GROUPS * EXPERTS_PER_GROUP * EXPERTS_PER_GROUP
HP_WIDTH = D_MODEL + LANES


def _route_rows(lg):
    rows = lg.shape[0]
    lane = lax.broadcasted_iota(jnp.int32, (rows, LANES), 1)
    lane_f = lane.astype(F32)
    neg = -jnp.inf

    def first_lane(mask):
        return jnp.min(jnp.where(mask, lane_f, float(LANES)), axis=-1, keepdims=True).astype(jnp.int32)

    gl = jnp.where(lane < N_GROUPS, lg, neg)
    gm = jnp.max(gl, axis=-1, keepdims=True)
    gsum = jnp.sum(jnp.where(lane < N_GROUPS, jnp.exp(gl - gm), 0.0), axis=-1, keepdims=True)
    gidx = first_lane(gl == gm)
    group_p = 1.0 / gsum
    in_group = (lane >= N_GROUPS) & (lane < N_GROUPS + N_EXPERTS) & (
        lax.shift_right_arithmetic(lane - N_GROUPS, 3) == gidx)
    el = jnp.where(in_group, lg, neg)
    m1 = jnp.max(el, axis=-1, keepdims=True)
    i1 = first_lane(el == m1)
    el2 = jnp.where(lane == i1, neg, el)
    m2 = jnp.max(el2, axis=-1, keepdims=True)
    i2 = first_lane(el2 == m2)
    e2 = jnp.exp(m2 - m1)
    w1 = group_p * (1.0 / (1.0 + e2))
    w2 = group_p * (e2 / (1.0 + e2))
    first_lower = i1 < i2
    base = N_GROUPS + gidx * EXPERTS_PER_GROUP
    lo = jnp.minimum(i1, i2) - base
    hi = jnp.maximum(i1, i2) - base
    cls = (gidx * EXPERTS_PER_GROUP + lo) * EXPERTS_PER_GROUP + hi
    return cls, jnp.where(first_lower, w1, w2), jnp.where(first_lower, w2, w1)


def _outproj_kernel(x_ref, ym_ref, yg_ref, yr_ref, wm_ref, wg_ref, wr_ref, gain_ref, wrt_ref, rb_ref,
                    xo_ref, hp_ref, rt_ref, cnt_ref, running):
    @pl.when(pl.program_id(0) == 0)
    def _():
        running[...] = jnp.zeros_like(running)

    rows = OUT_SUB
    subs = [slice(s * rows, (s + 1) * rows) for s in range(x_ref.shape[0] // rows)]
    acc = [x_ref[r, :] + jnp.dot(yg_ref[r, :], wg_ref[...], preferred_element_type=F32) for r in subs]
    acc = [a + jnp.dot(yr_ref[r, :], wr_ref[...], preferred_element_type=F32) for a, r in zip(acc, subs)]
    acc = [a + jnp.dot(ym_ref[r, :], wm_ref[...], preferred_element_type=F32) for a, r in zip(acc, subs)]
    for a, r in zip(acc, subs):
        xo_ref[r, :] = a
    hn = [a * lax.rsqrt(jnp.mean(a * a, axis=-1, keepdims=True) + EPS) * gain_ref[...] for a in acc]
    for h, r in zip(hn, subs):
        hp_ref[r, 0:D_MODEL] = h
    logits = [jnp.dot(h.astype(BF16), wrt_ref[...], preferred_element_type=F32) + rb_ref[...] for h in hn]
    routed = [_route_rows(lg) for lg in logits]

    cls_lane = lax.broadcasted_iota(jnp.int32, (rows, N_CLASSES), 1)
    before = (lax.broadcasted_iota(jnp.int32, (rows, rows), 1)
              < lax.broadcasted_iota(jnp.int32, (rows, rows), 0)).astype(BF16)
    onehot = [cls_lane == cls for cls, _, _ in routed]
    prefix = [jnp.dot(before, oh.astype(BF16), preferred_element_type=F32) for oh in onehot]
    counts = [jnp.sum(oh.astype(F32), axis=0, keepdims=True) for oh in onehot]
    lane = lax.broadcasted_iota(jnp.int32, (rows, LANES), 1)
    seen = running[0:1, :]
    for r, (cls, w_lo, w_hi), oh, pre, cnt in zip(subs, routed, onehot, prefix, counts):
        rank = jnp.sum(jnp.where(oh, pre + seen, 0.0), axis=-1, keepdims=True)
        seen = seen + cnt
        rt_ref[r, :] = jnp.where(lane == 0, cls, jnp.where(lane == 1, rank.astype(jnp.int32), 0))
        hp_ref[r, D_MODEL:] = jnp.where(lane == 0, w_lo, jnp.where(lane == 1, w_hi, 0.0))
    running[0:1, :] = seen
    cnt_ref[...] = jnp.broadcast_to(seen, cnt_ref.shape)


def _outproj(x2, ym, yg, yr, wo_m, wo_g, wo_r, gain, w_router, router_bias):
    n = x2.shape[0]
    row = lambda w: pl.BlockSpec((OUT_TILE, w), lambda i: (i, 0))
    full = lambda a, b: pl.BlockSpec((a, b), lambda i: (0, 0))
    return pl.pallas_call(
        _outproj_kernel,
        grid=(n // OUT_TILE,),
        in_specs=[row(D_MODEL), row(MLSTM_W), row(GDN_W), row(RET_W),
                  full(MLSTM_W, D_MODEL), full(GDN_W, D_MODEL), full(RET_W, D_MODEL),
                  full(1, D_MODEL), full(D_MODEL, LANES), full(1, LANES)],
        out_specs=[row(D_MODEL), row(HP_WIDTH), row(LANES), full(8, N_CLASSES)],
        out_shape=[jax.ShapeDtypeStruct((n, D_MODEL), F32),
                   jax.ShapeDtypeStruct((n, HP_WIDTH), F32),
                   jax.ShapeDtypeStruct((n, LANES), jnp.int32),
                   jax.ShapeDtypeStruct((8, N_CLASSES), F32)],
        scratch_shapes=[pltpu.VMEM((8, N_CLASSES), F32)],
        compiler_params=_cparams(("arbitrary",)),
        name="outproj_norm_router",
    )(x2, ym, yg, yr, wo_m, wo_g, wo_r, gain, w_router, router_bias)


def _gdn_kernel(n_blocks, *refs):
    state, u_s, wq_s, qk_s, kt_s, sd_s, cbuf = refs[-7:]
    j = pl.program_id(1)

    @pl.when(j == 0)
    def _():
        for ref in (state, u_s, wq_s, qk_s, kt_s, sd_s):
            ref[...] = jnp.zeros_like(ref)
        cbuf[0:8, :] = jnp.zeros((8, cbuf.shape[1]), F32)

    @pl.when(j < n_blocks)
    def _():
        _gdn_step(True, *refs)

    @pl.when(j == n_blocks)
    def _():
        _gdn_step(False, *refs)


def _gdn_step(prepare, x_ref, gt_ref, z_ref, conv_ref, prm_ref, gain_ref, y_ref,
              state, u_s, wq_s, qk_s, kt_s, sd_s, cbuf):
    tb = x_ref.shape[0]
    gt = gt_ref[...]
    beta_all = _sigmoid(gt)
    g_all = -jnp.exp(prm_ref[0:1, :]) * _softplus(gt + prm_ref[1:2, :])
    gc_all = _chunk_cumsum(g_all)

    lane, row = _lane_masks(CHUNK)
    m_a = lane < HEAD_DIM
    lane_h = lane % HEAD_DIM
    eye = lane_h == row
    tril = lane_h <= row
    strict = lane_h < row
    eye_f = eye.astype(F32)
    lane2 = lax.broadcasted_iota(jnp.int32, (LANES, LANES), 1)
    row2 = lax.broadcasted_iota(jnp.int32, (LANES, LANES), 0)
    bd_mask = (lane2 < HEAD_DIM) == (row2 < HEAD_DIM)
    lane_t = lax.broadcasted_iota(jnp.int32, (tb, LANES), 1)
    m_a_t = lane_t < HEAD_DIM

    n_pairs = GDN_HEADS // 2
    n_chunks = tb // CHUNK
    units = [(p, c) for p in range(n_pairs) for c in range(n_chunks)]

    def unit_index(p, c):
        return p * n_chunks + c

    prev = {}
    for p, c in units:
        i = unit_index(p, c)
        prev[p, c] = (u_s[i], wq_s[i], qk_s[i], kt_s[i], sd_s[i, 0:1, :])
    s_bd = [state[p] for p in range(n_pairs)]

    def recurrence(c):
        rs = slice(c * CHUNK, (c + 1) * CHUNK)
        for p in range(n_pairs):
            cs = slice(p * LANES, (p + 1) * LANES)
            u_mat, wq, qk, k_dec_t, s_decay = prev[p, c]
            ws_qs = jnp.dot(wq, s_bd[p].astype(BF16), preferred_element_type=F32)
            v_new = u_mat - ws_qs[:CHUNK]
            o = ws_qs[CHUNK:] + jnp.dot(qk, _block_diag(v_new, m_a).astype(BF16), preferred_element_type=F32)
            v_pad = jnp.concatenate([v_new, jnp.zeros_like(v_new)], axis=0).astype(BF16)
            s_bd[p] = s_bd[p] * s_decay + jnp.where(
                bd_mask, jnp.dot(k_dec_t, v_pad, preferred_element_type=F32), 0.0)
            ms = _half_sum(o * o, m_a) * (1.0 / HEAD_DIM)
            y = o * lax.rsqrt(ms + EPS) * gain_ref[:, cs] * _silu(z_ref[rs, cs])
            y_ref[rs, cs] = y.astype(y_ref.dtype)

    if not prepare:
        for c in range(n_chunks):
            recurrence(c)
        for p in range(n_pairs):
            state[p] = s_bd[p]
        return

    recurrence(0)
    qkv = _silu(_causal_conv(x_ref[:, 0:3 * GDN_W], cbuf, conv_ref))
    ins = {}
    for p in range(n_pairs):
        q_t = qkv[:, p * LANES:(p + 1) * LANES]
        k_t = qkv[:, GDN_W + p * LANES:GDN_W + (p + 1) * LANES]
        v_t = qkv[:, 2 * GDN_W + p * LANES:2 * GDN_W + (p + 1) * LANES]
        q_t = q_t * lax.rsqrt(_half_sum(q_t * q_t, m_a_t) + EPS) * (HEAD_DIM ** -0.5)
        k_t = k_t * lax.rsqrt(_half_sum(k_t * k_t, m_a_t) + EPS)
        beta_t = _col_form(beta_all, G_BETA + 2 * p, m_a_t)
        gc_t = _col_form(gc_all, G_A + 2 * p, m_a_t)
        for c in range(n_chunks):
            rs = slice(c * CHUNK, (c + 1) * CHUNK)
            ins[p, c] = (q_t[rs], k_t[rs], v_t[rs], beta_t[rs], gc_t[rs])
    kkqk = {u: _dot_nt(jnp.concatenate([ins[u][1] * ins[u][3], ins[u][0]], axis=0), _block_diag(ins[u][1], m_a))
            for u in units}
    pre = {}
    for u in units:
        q, k, v, beta, gc = ins[u]
        decay = jnp.where(tril, jnp.exp(gc - _row_form(gc, eye)), 0.0)
        egc = jnp.exp(gc)
        g_last = gc[CHUNK - 1:CHUNK, :]
        k_dec = k * jnp.exp(g_last - gc)
        pre[u] = dict(
            a=jnp.where(strict, kkqk[u][:CHUNK] * decay, 0.0), qk=kkqk[u][CHUNK:] * decay,
            vb=v * beta, kbe=k * beta * egc, qe=q * egc, s_decay=jnp.exp(g_last),
            k_dec_t=jnp.concatenate([k_dec, jnp.zeros_like(k_dec)], axis=0).T)

    pw = {u: _pmul(pre[u]["a"], pre[u]["a"], m_a) for u in units}
    t_inv = {u: eye_f - pre[u]["a"] for u in units}
    later_chunks = list(range(1, n_chunks))
    for step in range(4):
        if step % 2 == 0 and later_chunks:
            recurrence(later_chunks.pop(0))
        both = {u: _dot(jnp.concatenate([pw[u], t_inv[u]], axis=0), _block_diag(pw[u], m_a)) for u in units}
        pw = {u: both[u][:CHUNK] for u in units}
        t_inv = {u: t_inv[u] + both[u][CHUNK:] for u in units}
    for c in later_chunks:
        recurrence(c)
    t_inv = {u: t_inv[u] + _pmul(t_inv[u], pw[u], m_a) for u in units}
    for p, c in units:
        d = pre[p, c]
        uw = _dot(t_inv[p, c], jnp.concatenate([_block_diag(d["vb"], m_a), _block_diag(d["kbe"], m_a)], axis=1))
        i = unit_index(p, c)
        u_s[i] = uw[:, :LANES]
        wq_s[i] = jnp.concatenate([uw[:, LANES:], d["qe"]], axis=0).astype(BF16)
        qk_s[i] = d["qk"].astype(BF16)
        kt_s[i] = d["k_dec_t"].astype(BF16)
        sd_s[i] = jnp.broadcast_to(d["s_decay"], (8, LANES))
    for p in range(n_pairs):
        state[p] = s_bd[p]


def _ret_kernel(x_ref, cos_ref, sin_ref, gain_ref, y_ref, state):
    tb = x_ref.shape[0]

    @pl.when(pl.program_id(1) == 0)
    def _():
        state[...] = jnp.zeros_like(state)

    lane, row = _lane_masks(CHUNK)
    m_a = lane < HEAD_DIM
    lane_h = lane % HEAD_DIM
    tril = lane_h <= row
    lane2 = lax.broadcasted_iota(jnp.int32, (LANES, LANES), 1)
    row2 = lax.broadcasted_iota(jnp.int32, (LANES, LANES), 0)
    bd_mask = (lane2 < HEAD_DIM) == (row2 < HEAD_DIM)
    lane_t = lax.broadcasted_iota(jnp.int32, (tb, LANES), 1)
    first_half = (lane_t % HEAD_DIM) < (HEAD_DIM // 2)
    cos = cos_ref[...]
    sin = sin_ref[...]
    rowf = row.astype(F32)
    pos_diff = (row - lane_h).astype(F32)

    def rope(x):
        swapped = jnp.where(first_half, pltpu.roll(x, LANES - HEAD_DIM // 2, axis=1),
                            pltpu.roll(x, HEAD_DIM // 2, axis=1))
        return x * cos + swapped * sin

    n_pairs = RET_HEADS // 2
    n_chunks = tb // CHUNK
    units = [(p, c) for p in range(n_pairs) for c in range(n_chunks)]
    chunk_decay, decay, q_scale, k_scale, qkv = [], [], [], [], {}
    for p in range(n_pairs):
        lg_a = math.log(1.0 - 2.0 ** (-5.0 - 2 * p))
        lg_b = math.log(1.0 - 2.0 ** (-5.0 - (2 * p + 1)))
        lg = jnp.where(m_a, lg_a, lg_b)
        decay.append(jnp.where(tril, jnp.exp(pos_diff * lg), 0.0))
        q_scale.append(jnp.exp(lg * (rowf + 1.0)))
        k_scale.append(jnp.exp(lg * (CHUNK - 1.0 - rowf)))
        chunk_decay.append(jnp.exp(lg[0:1, :] * CHUNK))
        q_t = rope(x_ref[:, p * LANES:(p + 1) * LANES])
        k_t = rope(x_ref[:, RET_W + p * LANES:RET_W + (p + 1) * LANES]) * (HEAD_DIM ** -0.5)
        for c in range(n_chunks):
            rs = slice(c * CHUNK, (c + 1) * CHUNK)
            qkv[p, c] = (q_t[rs], k_t[rs], x_ref[rs, 2 * RET_W + p * LANES:2 * RET_W + (p + 1) * LANES])
    scores = {(p, c): _dot_nt(qkv[p, c][0], _block_diag(qkv[p, c][1], m_a)) * decay[p] for p, c in units}
    o_intra = {u: _pmul(scores[u], qkv[u][2], m_a) for u in units}
    kv_all = {(p, c): _outer_state(qkv[p, c][1] * k_scale[p], qkv[p, c][2], bd_mask) for p, c in units}
    pre = {(p, c): (o_intra[p, c], qkv[p, c][0] * q_scale[p], kv_all[p, c]) for p, c in units}

    s_in = {}
    for p in range(n_pairs):
        s_bd = state[p]
        for c in range(n_chunks):
            s_in[p, c] = s_bd
            s_bd = chunk_decay[p] * s_bd + pre[p, c][2]
        state[p] = s_bd
    o_all = {u: pre[u][0] + _dot(pre[u][1], s_in[u]) for u in units}
    ms_all = {u: _half_sum(o_all[u] * o_all[u], m_a) * (1.0 / HEAD_DIM) for u in units}
    for p, c in units:
        rs = slice(c * CHUNK, (c + 1) * CHUNK)
        cs = slice(p * LANES, (p + 1) * LANES)
        gate = x_ref[rs, 3 * RET_W + p * LANES:3 * RET_W + (p + 1) * LANES]
        y = o_all[p, c] * lax.rsqrt(ms_all[p, c] + EPS) * gain_ref[:, cs] * _silu(gate)
        y_ref[rs, cs] = y.astype(y_ref.dtype)


def _mlstm_kernel(x_ref, gt_ref, conv_ref, prm_ref, gain_ref, y_ref, c_state, nm_state, cbuf):
    tb = x_ref.shape[0]
    first = pl.program_id(1) == 0

    @pl.when(first)
    def _():
        c_state[...] = jnp.zeros_like(c_state)
        nm_state[...] = jnp.zeros_like(nm_state)
        cbuf[0:8, :] = jnp.zeros((8, cbuf.shape[1]), F32)

    qk_all = _silu(_causal_conv(x_ref[:, 0:2 * MLSTM_W], cbuf, conv_ref))
    gt = gt_ref[...]
    log_i_all = gt + prm_ref[0:1, :]
    f_pre = gt + prm_ref[1:2, :]
    log_f_all = jnp.minimum(f_pre, 0.0) - jnp.log(1.0 + jnp.exp(-jnp.abs(f_pre)))
    bc_all = _chunk_cumsum(log_f_all)

    lane, row = _lane_masks(CHUNK)
    m_a = lane < HEAD_DIM
    lane_h = lane % HEAD_DIM
    eye = lane_h == row
    tril = lane_h <= row
    lane2 = lax.broadcasted_iota(jnp.int32, (LANES, LANES), 1)
    row2 = lax.broadcasted_iota(jnp.int32, (LANES, LANES), 0)
    bd_mask = (lane2 < HEAD_DIM) == (row2 < HEAD_DIM)
    lane_t = lax.broadcasted_iota(jnp.int32, (tb, LANES), 1)
    m_a_t = lane_t < HEAD_DIM

    n_pairs = MLSTM_HEADS // 2
    n_chunks = tb // CHUNK
    units = [(p, c) for p in range(n_pairs) for c in range(n_chunks)]
    gates = {}
    for p in range(n_pairs):
        b_t = _col_form(bc_all, M_F + 2 * p, m_a_t)
        i_t = _col_form(log_i_all, M_I + 2 * p, m_a_t)
        for c in range(n_chunks):
            rs = slice(c * CHUNK, (c + 1) * CHUNK)
            gates[p, c] = (b_t[rs], i_t[rs])
    qkv = {(p, c): (qk_all[c * CHUNK:(c + 1) * CHUNK, p * LANES:(p + 1) * LANES] * (HEAD_DIM ** -0.5),
                    qk_all[c * CHUNK:(c + 1) * CHUNK, MLSTM_W + p * LANES:MLSTM_W + (p + 1) * LANES],
                    x_ref[c * CHUNK:(c + 1) * CHUNK, 2 * MLSTM_W + p * LANES:2 * MLSTM_W + (p + 1) * LANES])
           for p, c in units}
    qk = {u: _dot_nt(qkv[u][0], _block_diag(qkv[u][1], m_a)) for u in units}
    log_d = {u: jnp.where(tril, gates[u][0] - _row_form(gates[u][0], eye) + _row_form(gates[u][1], eye), -jnp.inf)
             for u in units}
    m_intra = {u: _half_max(log_d[u], m_a) for u in units}
    a_c = {u: gates[u][0][CHUNK - 1:CHUNK, :] - gates[u][0] + gates[u][1] for u in units}
    a_max = {u: jnp.max(a_c[u], axis=0, keepdims=True) for u in units}
    m_in, m_out = {}, {}
    for p in range(n_pairs):
        m_row = nm_state[p, 1:2, :]
        for c in range(n_chunks):
            m_in[p, c] = m_row
            m_row = jnp.maximum(gates[p, c][0][CHUNK - 1:CHUNK, :] + m_row, a_max[p, c])
            m_out[p, c] = m_row
    m_t = {u: jnp.maximum(gates[u][0] + m_in[u], m_intra[u]) for u in units}
    inter = {u: jnp.exp(gates[u][0] + m_in[u] - m_t[u]) for u in units}
    wmat = {u: jnp.where(tril, jnp.exp(log_d[u] - m_t[u]), 0.0) * qk[u] for u in units}
    kw = {u: qkv[u][1] * jnp.exp(a_c[u] - m_out[u]) for u in units}
    dec = {u: jnp.exp(gates[u][0][CHUNK - 1:CHUNK, :] + m_in[u] - m_out[u]) for u in units}
    num_intra = {u: _pmul(wmat[u], qkv[u][2], m_a) for u in units}
    den_intra = {u: _half_sum(wmat[u], m_a) for u in units}
    kv = {u: _outer_state(kw[u], qkv[u][2], bd_mask) for u in units}
    kn = {u: jnp.sum(kw[u], axis=0, keepdims=True) for u in units}
    c_in, n_in = {}, {}
    for p in range(n_pairs):
        c_bd, n_row = c_state[p], nm_state[p, 0:1, :]
        for c in range(n_chunks):
            c_in[p, c], n_in[p, c] = c_bd, n_row
            c_bd = dec[p, c] * c_bd + kv[p, c]
            n_row = dec[p, c] * n_row + kn[p, c]
        c_state[p] = c_bd
        nm_state[p, 0:1, :] = n_row
        nm_state[p, 1:2, :] = m_out[p, n_chunks - 1]
    num = {u: inter[u] * _dot(qkv[u][0], c_in[u]) + num_intra[u] for u in units}
    den = {u: inter[u] * _half_sum(qkv[u][0] * n_in[u], m_a) + den_intra[u] for u in units}
    h = {u: num[u] / jnp.maximum(jnp.abs(den[u]), jnp.exp(-m_t[u])) for u in units}
    ms = {u: _half_sum(h[u] * h[u], m_a) * (1.0 / HEAD_DIM) for u in units}
    for p, c in units:
        rs = slice(c * CHUNK, (c + 1) * CHUNK)
        cs = slice(p * LANES, (p + 1) * LANES)
        o_logit = x_ref[rs, 3 * MLSTM_W + p * LANES:3 * MLSTM_W + (p + 1) * LANES]
        y = _sigmoid(o_logit) * (h[p, c] * lax.rsqrt(ms[p, c] + EPS) * gain_ref[:, cs])
        y_ref[rs, cs] = y.astype(y_ref.dtype)


def _mixer_call(kernel, name, proj3, sec_block, sec_width, out_width, extra_inputs, extra_specs, scratch):
    b, t, _ = proj3.shape
    tb = min(LONG_BLOCK, t)
    return pl.pallas_call(
        kernel,
        grid=(b, t // tb),
        in_specs=[pl.BlockSpec((None, tb, sec_width), lambda i, j: (i, j, sec_block))] + extra_specs(tb),
        out_specs=pl.BlockSpec((None, tb, out_width), lambda i, j: (i, j, 0)),
        out_shape=jax.ShapeDtypeStruct((b, t, out_width), BF16),
        scratch_shapes=scratch(tb),
        compiler_params=_cparams(("parallel", "arbitrary")),
        name=name,
    )(proj3, *extra_inputs)


def _full2(a, b):
    return pl.BlockSpec((a, b), lambda i, j: (0, 0))


def _gdn_mixer(proj3, gates3, conv_w, prm, gain):
    b, t, _ = proj3.shape
    tb = min(TIME_BLOCK, t)
    n_blocks = t // tb
    n_units = (GDN_HEADS // 2) * (tb // CHUNK)
    cur = lambda i, j: (i, jnp.minimum(j, n_blocks - 1))
    prv = lambda i, j: (i, jnp.maximum(j - 1, 0))
    z_block = (SEC_M + 3 * GDN_W) // GDN_W
    assert z_block * GDN_W == SEC_M + 3 * GDN_W
    return pl.pallas_call(
        functools.partial(_gdn_kernel, n_blocks),
        grid=(b, n_blocks + 1),
        in_specs=[pl.BlockSpec((None, tb, SEC_G), lambda i, j: cur(i, j) + (1,)),
                  pl.BlockSpec((None, tb, LANES), lambda i, j: cur(i, j) + (0,)),
                  pl.BlockSpec((None, tb, GDN_W), lambda i, j: prv(i, j) + (z_block,)),
                  _full2(CONV_K, 3 * GDN_W), _full2(8, LANES), _full2(1, GDN_W)],
        out_specs=pl.BlockSpec((None, tb, GDN_W), lambda i, j: prv(i, j) + (0,)),
        out_shape=jax.ShapeDtypeStruct((b, t, GDN_W), BF16),
        scratch_shapes=[pltpu.VMEM((GDN_HEADS // 2, LANES, LANES), F32),
                        pltpu.VMEM((n_units, CHUNK, LANES), F32),
                        pltpu.VMEM((n_units, 2 * CHUNK, LANES), BF16),
                        pltpu.VMEM((n_units, CHUNK, LANES), BF16),
                        pltpu.VMEM((n_units, LANES, LANES), BF16),
                        pltpu.VMEM((n_units, 8, LANES), F32),
                        pltpu.VMEM((tb + 8, 3 * GDN_W), F32)],
        compiler_params=_cparams(("parallel", "arbitrary")),
        name="gdn_mixer",
    )(proj3, gates3, proj3, conv_w, prm, gain)


def _mlstm_mixer(proj3, gates3, conv_w, prm, gain):
    return _mixer_call(
        _mlstm_kernel, "mlstm_mixer", proj3, 0, SEC_M, MLSTM_W, (gates3, conv_w, prm, gain),
        lambda tb: [pl.BlockSpec((None, tb, LANES), lambda i, j: (i, j, 0)),
                    _full2(CONV_K, 2 * MLSTM_W), _full2(8, LANES), _full2(1, MLSTM_W)],
        lambda tb: [pltpu.VMEM((MLSTM_HEADS // 2, LANES, LANES), F32),
                    pltpu.VMEM((MLSTM_HEADS // 2, 8, LANES), F32),
                    pltpu.VMEM((tb + 8, 2 * MLSTM_W), F32)])


def _ret_mixer(proj3, cos_tab, sin_tab, gain):
    return _mixer_call(
        _ret_kernel, "retention_mixer", proj3, 3, SEC_R, RET_W, (cos_tab, sin_tab, gain),
        lambda tb: [pl.BlockSpec((tb, LANES), lambda i, j: (j, 0)),
                    pl.BlockSpec((tb, LANES), lambda i, j: (j, 0)),
                    _full2(1, RET_W)],
        lambda tb: [pltpu.VMEM((RET_HEADS // 2, LANES, LANES), F32)])


DISPATCH_TILE = 2048


def _dispatch_kernel(pos_ref, src_ref, zero_ref, out_ref, sem):
    del zero_ref
    rows = src_ref.shape[0]
    base = pl.program_id(0) * rows

    def issue(r, carry):
        pltpu.make_async_copy(src_ref.at[pl.ds(r, 1)], out_ref.at[pl.ds(pos_ref[base + r], 1)], sem).start()
        return carry

    lax.fori_loop(0, rows, issue, 0, unroll=8)
    pltpu.make_async_copy(src_ref, out_ref.at[pl.ds(0, rows)], sem).wait()


def _dispatch(src, pos, n_rows):
    n, width = src.shape
    tile = min(DISPATCH_TILE, n)
    zeros = jnp.zeros((n_rows, width), src.dtype)
    return pl.pallas_call(
        _dispatch_kernel,
        grid_spec=pltpu.PrefetchScalarGridSpec(
            num_scalar_prefetch=1,
            grid=(n // tile,),
            in_specs=[pl.BlockSpec((tile, width), lambda i, p: (i, 0)), pl.BlockSpec(memory_space=pl.ANY)],
            out_specs=pl.BlockSpec(memory_space=pl.ANY),
            scratch_shapes=[pltpu.SemaphoreType.DMA(())],
        ),
        out_shape=jax.ShapeDtypeStruct((n_rows, width), src.dtype),
        input_output_aliases={2: 0},
        compiler_params=_cparams(("arbitrary",)),
        name="moe_dispatch",
    )(pos, src, zeros)


def _ffn_kernel(layer, plan_ref, used_ref, x_ref, wg_hbm, wu_hbm, wd_hbm, y_ref, wg_buf, wu_buf, wd_buf, sems):
    def group_copies(g, to_slot):
        lo = pl.multiple_of(g * EXPERTS_PER_GROUP, EXPERTS_PER_GROUP)
        experts = pl.ds(lo, EXPERTS_PER_GROUP)
        return [pltpu.make_async_copy(hbm.at[layer, experts], buf.at[to_slot], sems.at[to_slot])
                for hbm, buf in ((wg_hbm, wg_buf), (wu_hbm, wu_buf), (wd_hbm, wd_buf))]

    def one_tile(i, rows):
        used = i < used_ref[0]
        group, slot, starts_group, next_group = plan_ref[0, i], plan_ref[1, i], plan_ref[2, i], plan_ref[3, i]

        @pl.when(used & (i == 0))
        def _():
            for cp in group_copies(group, slot):
                cp.start()

        @pl.when(used & (starts_group == 1) & (next_group >= 0))
        def _():
            for cp in group_copies(next_group, 1 - slot):
                cp.start()

        @pl.when(used & (starts_group == 1))
        def _():
            for cp in group_copies(group, slot):
                cp.wait()

        @pl.when(jnp.logical_not(used))
        def _():
            y_ref[rows, :] = jnp.zeros((MOE_TILE, D_MODEL), F32)

        @pl.when(used)
        def _():
            e1, e2 = plan_ref[4, i], plan_ref[5, i]
            x = x_ref[rows, 0:D_MODEL].astype(BF16)
            gate1 = jnp.dot(x, wg_buf[slot, e1], preferred_element_type=F32)
            gate2 = jnp.dot(x, wg_buf[slot, e2], preferred_element_type=F32)
            up1 = jnp.dot(x, wu_buf[slot, e1], preferred_element_type=F32)
            up2 = jnp.dot(x, wu_buf[slot, e2], preferred_element_type=F32)
            hid1 = (_silu(gate1) * up1).astype(BF16)
            hid2 = (_silu(gate2) * up2).astype(BF16)
            y1 = jnp.dot(hid1, wd_buf[slot, e1], preferred_element_type=F32)
            y2 = jnp.dot(hid2, wd_buf[slot, e2], preferred_element_type=F32)
            rw = x_ref[rows, D_MODEL:]
            y_ref[rows, :] = rw[:, 0:1] * y1 + rw[:, 1:2] * y2

    for sub in range(FFN_STEP_TILES):
        one_tile(pl.program_id(0) * FFN_STEP_TILES + sub, slice(sub * MOE_TILE, (sub + 1) * MOE_TILE))


def _expert_ffn(xs, tile_plan, n_used, wg, wu, wd, layer):
    r = xs.shape[0]
    step_rows = FFN_STEP_TILES * MOE_TILE
    assert r % step_rows == 0
    last_used = lambda nu: (nu[0] - 1) // FFN_STEP_TILES
    return pl.pallas_call(
        functools.partial(_ffn_kernel, layer),
        grid_spec=pltpu.PrefetchScalarGridSpec(
            num_scalar_prefetch=2,
            grid=(r // step_rows,),
            in_specs=[pl.BlockSpec((step_rows, HP_WIDTH), lambda i, plan, nu: (jnp.minimum(i, last_used(nu)), 0)),
                      pl.BlockSpec(memory_space=pl.ANY), pl.BlockSpec(memory_space=pl.ANY),
                      pl.BlockSpec(memory_space=pl.ANY)],
            out_specs=pl.BlockSpec((step_rows, D_MODEL), lambda i, plan, nu: (i, 0)),
            scratch_shapes=[pltpu.VMEM((2, EXPERTS_PER_GROUP, D_MODEL, D_EXPERT), BF16),
                            pltpu.VMEM((2, EXPERTS_PER_GROUP, D_MODEL, D_EXPERT), BF16),
                            pltpu.VMEM((2, EXPERTS_PER_GROUP, D_EXPERT, D_MODEL), BF16),
                            pltpu.SemaphoreType.DMA((2,))],
        ),
        out_shape=jax.ShapeDtypeStruct((r, D_MODEL), F32),
        compiler_params=_cparams(("arbitrary",)),
        name="expert_pair_ffn",
    )(tile_plan, n_used, xs, wg, wu, wd)


def _combine_kernel(final_norm, pos_ref, x_ref, gain_ref, ys_ref, o_ref, buf, sems):
    rows = o_ref.shape[0]
    i = pl.program_id(0)
    slot = i & 1

    def gather_tile(tile, to_slot):
        def issue(r, carry):
            pltpu.make_async_copy(ys_ref.at[pl.ds(pos_ref[tile * rows + r], 1)],
                                  buf.at[to_slot, pl.ds(r, 1)], sems.at[to_slot]).start()
            return carry

        lax.fori_loop(0, rows, issue, 0, unroll=8)

    @pl.when(i == 0)
    def _():
        gather_tile(0, 0)

    @pl.when(i + 1 < pl.num_programs(0))
    def _():
        gather_tile(i + 1, 1 - slot)

    pltpu.make_async_copy(ys_ref.at[pl.ds(0, rows)], buf.at[slot], sems.at[slot]).wait()
    out = x_ref[...] + buf[slot]
    if final_norm:
        ms = jnp.mean(out * out, axis=-1, keepdims=True)
        out = out * lax.rsqrt(ms + EPS) * gain_ref[...]
    o_ref[...] = out


def _combine(x2, ys, pos, gain, final_norm):
    n = x2.shape[0]
    return pl.pallas_call(
        functools.partial(_combine_kernel, final_norm),
        grid_spec=pltpu.PrefetchScalarGridSpec(
            num_scalar_prefetch=1,
            grid=(n // ROW_TILE,),
            in_specs=[pl.BlockSpec((ROW_TILE, D_MODEL), lambda i, p: (i, 0)),
                      pl.BlockSpec((1, D_MODEL), lambda i, p: (0, 0)),
                      pl.BlockSpec(memory_space=pl.ANY)],
            out_specs=pl.BlockSpec((ROW_TILE, D_MODEL), lambda i, p: (i, 0)),
            scratch_shapes=[pltpu.VMEM((2, ROW_TILE, D_MODEL), F32), pltpu.SemaphoreType.DMA((2,))],
        ),
        out_shape=jax.ShapeDtypeStruct((n, D_MODEL), F32),
        compiler_params=_cparams(("arbitrary",)),
        name="moe_combine",
    )(pos, x2, gain, ys)


def _plan(route, counts, n):
    cls, rank = route[:, 0], route[:, 1]
    counts = counts.astype(jnp.int32)
    tiles_per = (counts + MOE_TILE - 1) // MOE_TILE
    tile_end = jnp.cumsum(tiles_per)
    tile_start = tile_end - tiles_per
    class_ids = jnp.arange(N_CLASSES, dtype=jnp.int32)
    start_of = jnp.sum(jnp.where(cls[:, None] == class_ids[None, :], tile_start[None, :], 0), axis=1)
    pos = start_of * MOE_TILE + rank
    n_pairs = N_GROUPS * (EXPERTS_PER_GROUP * (EXPERTS_PER_GROUP - 1)) // 2
    n_tiles = n // MOE_TILE + n_pairs
    n_used = tile_end[-1]
    tile_ids = jnp.minimum(jnp.arange(n_tiles, dtype=jnp.int32), n_used - 1)
    tile_cls = jnp.sum((tile_end[None, :] <= tile_ids[:, None]).astype(jnp.int32), axis=1)
    per_group = EXPERTS_PER_GROUP * EXPERTS_PER_GROUP
    tile_group = tile_cls // per_group
    starts_group = jnp.concatenate([jnp.ones((1,), jnp.int32),
                                    (tile_group[1:] != tile_group[:-1]).astype(jnp.int32)])
    slot = (jnp.cumsum(starts_group) - 1) % 2
    present = jnp.sum(tiles_per.reshape(N_GROUPS, per_group), axis=1) > 0
    nxt = [jnp.int32(-1)] * N_GROUPS
    for g in range(N_GROUPS - 2, -1, -1):
        nxt[g] = jnp.where(present[g + 1], g + 1, nxt[g + 1])
    next_group = jnp.stack(nxt)[tile_group]
    tile_plan = jnp.stack([tile_group, slot, starts_group, next_group,
                           (tile_cls // EXPERTS_PER_GROUP) % EXPERTS_PER_GROUP, tile_cls % EXPERTS_PER_GROUP])
    return pos, tile_plan.astype(jnp.int32), n_used.reshape(1), n_tiles * MOE_TILE


def _rope_tables(t):
    inv_freq = ROPE_BASE ** (-jnp.arange(0, HEAD_DIM, 2, dtype=F32) / HEAD_DIM)
    ang = jnp.arange(t, dtype=F32)[:, None] * inv_freq[None, :]
    cos, sin = jnp.cos(ang), jnp.sin(ang)
    cos_tab = jnp.tile(cos, (1, 2 * LANES // HEAD_DIM))
    sin_tab = jnp.tile(jnp.concatenate([-sin, sin], axis=-1), (1, LANES // HEAD_DIM))
    return cos_tab, sin_tab


def _split_w_in(w):
    g0, g1 = 0, SEC_G
    r0, r1 = g1 + 2 * GDN_HEADS, g1 + 2 * GDN_HEADS + SEC_R
    m0, m1 = r1, r1 + SEC_M
    assert m1 + 2 * MLSTM_HEADS == w.shape[1]
    main = jnp.concatenate([w[:, m0:m1], w[:, g0:g1], w[:, r0:r1]], axis=1).astype(BF16)
    gate = jnp.concatenate([w[:, g1:r0], w[:, m1:]], axis=1)
    gate = jnp.pad(gate, ((0, 0), (0, LANES - gate.shape[1]))).astype(BF16)
    return main, gate


def _lane_row(values, start):
    return jnp.zeros((LANES,), F32).at[start:start + values.shape[0]].set(values)


def kernel(x, norm_mix, w_in, gdn_conv, gdn_a_log, gdn_dt_bias, gdn_norm, ret_norm, mlstm_conv, mlstm_i_bias, mlstm_f_bias, mlstm_norm, w_out, norm_ffn, router_group, router_expert, router_bias, expert_gate, expert_up, expert_down, norm_final):
    b, t, d = x.shape
    n = b * t
    depth = w_in.shape[0]
    cos_tab, sin_tab = _rope_tables(t)
    wg_all, wu_all, wd_all = expert_gate.astype(BF16), expert_up.astype(BF16), expert_down.astype(BF16)
    x2 = x.reshape(n, d)
    for l in range(depth):
        w_main, w_gate = _split_w_in(w_in[l])
        proj, gates = _inproj(x2, norm_mix[l][None, :], w_main, w_gate)
        proj3 = proj.reshape(b, t, D_MAIN)
        gates3 = gates.reshape(b, t, LANES)
        gdn_prm = jnp.zeros((8, LANES), F32).at[0].set(_lane_row(gdn_a_log[l], G_A)).at[1].set(
            _lane_row(gdn_dt_bias[l], G_A))
        mlstm_prm = jnp.zeros((8, LANES), F32).at[0].set(_lane_row(mlstm_i_bias[l], M_I)).at[1].set(
            _lane_row(mlstm_f_bias[l], M_F))
        y_m = _mlstm_mixer(proj3, gates3, mlstm_conv[l], mlstm_prm, mlstm_norm[l][None, :])
        y_g = _gdn_mixer(proj3, gates3, gdn_conv[l], gdn_prm, gdn_norm[l][None, :])
        y_r = _ret_mixer(proj3, cos_tab, sin_tab, ret_norm[l][None, :])
        wo = w_out[l].astype(BF16)
        w_router = jnp.pad(jnp.concatenate([router_group[l], router_expert[l]], axis=1),
                           ((0, 0), (0, LANES - N_GROUPS - N_EXPERTS))).astype(BF16)
        rb = _lane_row(router_bias[l], N_GROUPS)[None, :]
        x2, h_packed, route, counts = _outproj(
            x2, y_m.reshape(n, MLSTM_W), y_g.reshape(n, GDN_W), y_r.reshape(n, RET_W),
            wo[GDN_W + RET_W:], wo[:GDN_W], wo[GDN_W:GDN_W + RET_W],
            norm_ffn[l][None, :], w_router, rb)
        pos, tile_plan, n_used, n_rows = _plan(route, counts[0], n)
        xs = _dispatch(h_packed, pos, n_rows)
        ys = _expert_ffn(xs, tile_plan, n_used, wg_all, wu_all, wd_all, l)
        x2 = _combine(x2, ys, pos, norm_final[None, :], l == depth - 1)
    return x2.reshape(b, t, d)
```

```python
import functools
import math

import jax
import jax.numpy as jnp
from jax import lax
from jax.experimental import pallas as pl
from jax.experimental.pallas import tpu as pltpu

F32 = jnp.float32
BF16 = jnp.bfloat16

D_MODEL = 1024
HEAD_DIM = 64
CHUNK = 64
GDN_HEADS, RET_HEADS, MLSTM_HEADS = 6, 4, 6
GDN_W, RET_W, MLSTM_W = GDN_HEADS * HEAD_DIM, RET_HEADS * HEAD_DIM, MLSTM_HEADS * HEAD_DIM
CONV_K = 4
ROPE_BASE = 10000.0
N_GROUPS, EXPERTS_PER_GROUP = 4, 8
N_EXPERTS = N_GROUPS * EXPERTS_PER_GROUP
D_EXPERT = D_MODEL // 4
EPS = 1e-6
LANES = 128
SEC_M, SEC_G, SEC_R = 4 * MLSTM_W, 4 * GDN_W, 4 * RET_W
D_MAIN = SEC_M + SEC_G + SEC_R
G_BETA, G_A, M_I, M_F = 0, GDN_HEADS, 2 * GDN_HEADS, 2 * GDN_HEADS + MLSTM_HEADS

ROW_TILE = 512
OUT_TILE = 512
OUT_SUB = 128
TIME_BLOCK = 256
LONG_BLOCK = 512
MOE_TILE = 128
FFN_STEP_TILES = 8
VMEM_LIMIT = 56 * 1024 * 1024


def _cparams(sem):
    return pltpu.CompilerParams(dimension_semantics=sem, vmem_limit_bytes=VMEM_LIMIT)


def _silu(x):
    return x * (1.0 / (1.0 + jnp.exp(-x)))


def _sigmoid(x):
    return 1.0 / (1.0 + jnp.exp(-x))


def _softplus(x):
    return jnp.maximum(x, 0.0) + jnp.log(1.0 + jnp.exp(-jnp.abs(x)))


def _dot(a, b):
    return jnp.dot(a.astype(BF16), b.astype(BF16), preferred_element_type=F32)


def _dot_nt(a, b):
    return lax.dot_general(a.astype(BF16), b.astype(BF16), (((1,), (1,)), ((), ())),
                           preferred_element_type=F32)


def _lane_masks(rows):
    lane = lax.broadcasted_iota(jnp.int32, (rows, LANES), 1)
    row = lax.broadcasted_iota(jnp.int32, (rows, LANES), 0)
    return lane, row


def _half_sum(x, m_a):
    s_a = jnp.sum(jnp.where(m_a, x, 0.0), axis=-1, keepdims=True)
    s_b = jnp.sum(jnp.where(m_a, 0.0, x), axis=-1, keepdims=True)
    return jnp.where(m_a, s_a, s_b)


def _half_max(x, m_a):
    s_a = jnp.max(jnp.where(m_a, x, -jnp.inf), axis=-1, keepdims=True)
    s_b = jnp.max(jnp.where(m_a, -jnp.inf, x), axis=-1, keepdims=True)
    return jnp.where(m_a, s_a, s_b)


def _col_form(g, lane_a, m_a):
    rows = g.shape[0]
    ca = jnp.broadcast_to(g[:, lane_a:lane_a + 1], (rows, LANES))
    cb = jnp.broadcast_to(g[:, lane_a + 1:lane_a + 2], (rows, LANES))
    return jnp.where(m_a, ca, cb)


def _row_form(col, eye):
    return jnp.sum(jnp.where(eye, col, 0.0), axis=0, keepdims=True)


def _block_diag(y, m_a):
    return jnp.concatenate([jnp.where(m_a, y, 0.0), jnp.where(m_a, 0.0, y)], axis=0)


def _pmul(x, y, m_a):
    return _dot(x, _block_diag(y, m_a))


def _outer_state(k, v, bd_mask):
    zero = jnp.zeros_like(k)
    kt = jnp.concatenate([k, zero], axis=0).T
    vp = jnp.concatenate([v, zero], axis=0)
    return jnp.where(bd_mask, _dot(kt, vp), 0.0)


def _chunk_cumsum(x):
    rows = x.shape[0]
    r = lax.broadcasted_iota(jnp.int32, (rows, LANES), 0) % CHUNK
    s = 1
    while s < CHUNK:
        x = x + jnp.where(r >= s, pltpu.roll(x, s, axis=0), 0.0)
        s *= 2
    return x


def _causal_conv(raw, cbuf, w_ref):
    rows = raw.shape[0]
    cbuf[8:8 + rows, :] = raw
    acc = raw * w_ref[CONV_K - 1:CONV_K, :]
    for j in range(CONV_K - 1):
        off = 8 - (CONV_K - 1) + j
        acc = acc + cbuf[off:off + rows, :] * w_ref[j:j + 1, :]
    cbuf[0:8, :] = raw[rows - 8:rows, :]
    return acc


def _inproj_kernel(x_ref, gain_ref, w_ref, wg_ref, *rest):
    cast_refs, (o_ref, og_ref) = rest[:3] + rest[5:], rest[3:5]
    x = x_ref[...]
    ms = jnp.mean(x * x, axis=-1, keepdims=True)
    h = (x * lax.rsqrt(ms + EPS) * gain_ref[...]).astype(BF16)

    def proj(lo, hi):
        return jnp.dot(h, w_ref[:, lo:hi], preferred_element_type=F32)

    step = 512
    for c in range(D_MAIN // step):
        o_ref[:, c * step:(c + 1) * step] = proj(c * step, (c + 1) * step)
    og_ref[...] = jnp.dot(h, wg_ref[...], preferred_element_type=F32)
    for src, dst in zip(cast_refs[:3], cast_refs[3:]):
        dst[...] = src[...].astype(BF16)


def _inproj(x2, gain, w_main, w_gate, expert_weights):
    n = x2.shape[0]
    steps = n // OUT_TILE
    full = lambda a, b: pl.BlockSpec((a, b), lambda i: (0, 0))
    flat = [w.reshape(-1, w.shape[-1]) for w in expert_weights]
    slabs = [pl.BlockSpec((w.shape[0] // steps, w.shape[1]), lambda i: (i, 0)) for w in flat]
    assert all(w.shape[0] % (16 * steps) == 0 for w in flat)
    outs = pl.pallas_call(
        _inproj_kernel,
        grid=(steps,),
        in_specs=[
            pl.BlockSpec((OUT_TILE, D_MODEL), lambda i: (i, 0)),
            full(1, D_MODEL), full(D_MODEL, D_MAIN), full(D_MODEL, LANES),
        ] + slabs,
        out_specs=[
            pl.BlockSpec((OUT_TILE, D_MAIN), lambda i: (i, 0)),
            pl.BlockSpec((OUT_TILE, LANES), lambda i: (i, 0)),
        ] + slabs,
        out_shape=[jax.ShapeDtypeStruct((n, D_MAIN), F32), jax.ShapeDtypeStruct((n, LANES), F32)] + [
            jax.ShapeDtypeStruct(w.shape, BF16) for w in flat],
        compiler_params=_cparams(("parallel",)),
        name="norm_inproj",
    )(x2, gain, w_main, w_gate, *flat)
    return outs[0], outs[1], [o.reshape(w.shape) for o, w in zip(outs[2:], expert_weights)]


N_CLASSES = N_GROUPS * EXPERTS_PER_GROUP * EXPERTS_PER_GROUP
N_PAIRS = N_GROUPS * (EXPERTS_PER_GROUP * (EXPERTS_PER_GROUP - 1)) // 2


def _sorted_rows(n):
    return (n // MOE_TILE + N_PAIRS) * MOE_TILE


HP_WIDTH = D_MODEL + LANES


def _route_rows(lg):
    rows = lg.shape[0]
    lane = lax.broadcasted_iota(jnp.int32, (rows, LANES), 1)
    lane_f = lane.astype(F32)
    neg = -jnp.inf

    def first_lane(mask):
        return jnp.min(jnp.where(mask, lane_f, float(LANES)), axis=-1, keepdims=True).astype(jnp.int32)

    gl = jnp.where(lane < N_GROUPS, lg, neg)
    gm = jnp.max(gl, axis=-1, keepdims=True)
    gsum = jnp.sum(jnp.where(lane < N_GROUPS, jnp.exp(gl - gm), 0.0), axis=-1, keepdims=True)
    gidx = first_lane(gl == gm)
    group_p = 1.0 / gsum
    in_group = (lane >= N_GROUPS) & (lane < N_GROUPS + N_EXPERTS) & (
        lax.shift_right_arithmetic(lane - N_GROUPS, 3) == gidx)
    el = jnp.where(in_group, lg, neg)
    m1 = jnp.max(el, axis=-1, keepdims=True)
    i1 = first_lane(el == m1)
    el2 = jnp.where(lane == i1, neg, el)
    m2 = jnp.max(el2, axis=-1, keepdims=True)
    i2 = first_lane(el2 == m2)
    e2 = jnp.exp(m2 - m1)
    w1 = group_p * (1.0 / (1.0 + e2))
    w2 = group_p * (e2 / (1.0 + e2))
    first_lower = i1 < i2
    base = N_GROUPS + gidx * EXPERTS_PER_GROUP
    lo = jnp.minimum(i1, i2) - base
    hi = jnp.maximum(i1, i2) - base
    cls = (gidx * EXPERTS_PER_GROUP + lo) * EXPERTS_PER_GROUP + hi
    return cls, jnp.where(first_lower, w1, w2), jnp.where(first_lower, w2, w1)


def _outproj_kernel(x_ref, ym_ref, yg_ref, yr_ref, wm_ref, wg_ref, wr_ref, gain_ref, wrt_ref, rb_ref,
                    xo_ref, hp_ref, rt_ref, cnt_ref, zero_ref, running):
    @pl.when(pl.program_id(0) == 0)
    def _():
        running[...] = jnp.zeros_like(running)

    zero_ref[...] = jnp.zeros_like(zero_ref)

    rows = OUT_SUB
    subs = [slice(s * rows, (s + 1) * rows) for s in range(x_ref.shape[0] // rows)]
    acc = [x_ref[r, :] + jnp.dot(yg_ref[r, :], wg_ref[...], preferred_element_type=F32) for r in subs]
    acc = [a + jnp.dot(yr_ref[r, :], wr_ref[...], preferred_element_type=F32) for a, r in zip(acc, subs)]
    acc = [a + jnp.dot(ym_ref[r, :], wm_ref[...], preferred_element_type=F32) for a, r in zip(acc, subs)]
    for a, r in zip(acc, subs):
        xo_ref[r, :] = a
    hn = [a * lax.rsqrt(jnp.mean(a * a, axis=-1, keepdims=True) + EPS) * gain_ref[...] for a in acc]
    for h, r in zip(hn, subs):
        hp_ref[r, 0:D_MODEL] = h
    logits = [jnp.dot(h.astype(BF16), wrt_ref[...], preferred_element_type=F32) + rb_ref[...] for h in hn]
    routed = [_route_rows(lg) for lg in logits]

    cls_lane = lax.broadcasted_iota(jnp.int32, (rows, N_CLASSES), 1)
    before = (lax.broadcasted_iota(jnp.int32, (rows, rows), 1)
              < lax.broadcasted_iota(jnp.int32, (rows, rows), 0)).astype(BF16)
    onehot = [cls_lane == cls for cls, _, _ in routed]
    prefix = [jnp.dot(before, oh.astype(BF16), preferred_element_type=F32) for oh in onehot]
    counts = [jnp.sum(oh.astype(F32), axis=0, keepdims=True) for oh in onehot]
    lane = lax.broadcasted_iota(jnp.int32, (rows, LANES), 1)
    seen = running[0:1, :]
    for r, (cls, w_lo, w_hi), oh, pre, cnt in zip(subs, routed, onehot, prefix, counts):
        rank = jnp.sum(jnp.where(oh, pre + seen, 0.0), axis=-1, keepdims=True)
        seen = seen + cnt
        rt_ref[r, :] = jnp.where(lane == 0, cls, jnp.where(lane == 1, rank.astype(jnp.int32), 0))
        hp_ref[r, D_MODEL:] = jnp.where(lane == 0, w_lo, jnp.where(lane == 1, w_hi, 0.0))
    running[0:1, :] = seen
    cnt_ref[...] = jnp.broadcast_to(seen, cnt_ref.shape)


def _outproj(x2, ym, yg, yr, wo_m, wo_g, wo_r, gain, w_router, router_bias):
    n = x2.shape[0]
    row = lambda w: pl.BlockSpec((OUT_TILE, w), lambda i: (i, 0))
    full = lambda a, b: pl.BlockSpec((a, b), lambda i: (0, 0))
    steps = n // OUT_TILE
    n_rows = _sorted_rows(n)
    zero_rows = n_rows // steps
    assert zero_rows * steps == n_rows and zero_rows % 8 == 0
    return pl.pallas_call(
        _outproj_kernel,
        grid=(steps,),
        in_specs=[row(D_MODEL), row(MLSTM_W), row(GDN_W), row(RET_W),
                  full(MLSTM_W, D_MODEL), full(GDN_W, D_MODEL), full(RET_W, D_MODEL),
                  full(1, D_MODEL), full(D_MODEL, LANES), full(1, LANES)],
        out_specs=[row(D_MODEL), row(HP_WIDTH), row(LANES), full(8, N_CLASSES),
                   pl.BlockSpec((zero_rows, HP_WIDTH), lambda i: (i, 0))],
        out_shape=[jax.ShapeDtypeStruct((n, D_MODEL), F32),
                   jax.ShapeDtypeStruct((n, HP_WIDTH), F32),
                   jax.ShapeDtypeStruct((n, LANES), jnp.int32),
                   jax.ShapeDtypeStruct((8, N_CLASSES), F32),
                   jax.ShapeDtypeStruct((n_rows, HP_WIDTH), F32)],
        scratch_shapes=[pltpu.VMEM((8, N_CLASSES), F32)],
        compiler_params=_cparams(("arbitrary",)),
        name="outproj_norm_router",
    )(x2, ym, yg, yr, wo_m, wo_g, wo_r, gain, w_router, router_bias)


def _gdn_kernel(n_blocks, *refs):
    state, u_s, wq_s, qk_s, kt_s, sd_s, cbuf = refs[-7:]
    j = pl.program_id(1)

    @pl.when(j == 0)
    def _():
        for ref in (state, u_s, wq_s, qk_s, kt_s, sd_s):
            ref[...] = jnp.zeros_like(ref)
        cbuf[0:8, :] = jnp.zeros((8, cbuf.shape[1]), F32)

    @pl.when(j < n_blocks)
    def _():
        _gdn_step(True, *refs)

    @pl.when(j == n_blocks)
    def _():
        _gdn_step(False, *refs)


def _gdn_step(prepare, x_ref, gt_ref, z_ref, conv_ref, prm_ref, gain_ref, y_ref,
              state, u_s, wq_s, qk_s, kt_s, sd_s, cbuf):
    tb = x_ref.shape[0]
    gt = gt_ref[...]
    beta_all = _sigmoid(gt)
    g_all = -jnp.exp(prm_ref[0:1, :]) * _softplus(gt + prm_ref[1:2, :])
    gc_all = _chunk_cumsum(g_all)

    lane, row = _lane_masks(CHUNK)
    m_a = lane < HEAD_DIM
    lane_h = lane % HEAD_DIM
    eye = lane_h == row
    tril = lane_h <= row
    strict = lane_h < row
    eye_f = eye.astype(F32)
    lane2 = lax.broadcasted_iota(jnp.int32, (LANES, LANES), 1)
    row2 = lax.broadcasted_iota(jnp.int32, (LANES, LANES), 0)
    bd_mask = (lane2 < HEAD_DIM) == (row2 < HEAD_DIM)
    lane_t = lax.broadcasted_iota(jnp.int32, (tb, LANES), 1)
    m_a_t = lane_t < HEAD_DIM

    n_pairs = GDN_HEADS // 2
    n_chunks = tb // CHUNK
    units = [(p, c) for p in range(n_pairs) for c in range(n_chunks)]

    def unit_index(p, c):
        return p * n_chunks + c

    prev = {}
    for p, c in units:
        i = unit_index(p, c)
        prev[p, c] = (u_s[i], wq_s[i], qk_s[i], kt_s[i], sd_s[i, 0:1, :])
    s_bd = [state[p] for p in range(n_pairs)]

    def recurrence(c):
        rs = slice(c * CHUNK, (c + 1) * CHUNK)
        for p in range(n_pairs):
            cs = slice(p * LANES, (p + 1) * LANES)
            u_mat, wq, qk, k_dec_t, s_decay = prev[p, c]
            ws_qs = jnp.dot(wq, s_bd[p].astype(BF16), preferred_element_type=F32)
            v_new = u_mat - ws_qs[:CHUNK]
            o = ws_qs[CHUNK:] + jnp.dot(qk, _block_diag(v_new, m_a).astype(BF16), preferred_element_type=F32)
            v_pad = jnp.concatenate([v_new, jnp.zeros_like(v_new)], axis=0).astype(BF16)
            s_bd[p] = s_bd[p] * s_decay + jnp.where(
                bd_mask, jnp.dot(k_dec_t, v_pad, preferred_element_type=F32), 0.0)
            ms = _half_sum(o * o, m_a) * (1.0 / HEAD_DIM)
            y = o * lax.rsqrt(ms + EPS) * gain_ref[:, cs] * _silu(z_ref[rs, cs])
            y_ref[rs, cs] = y.astype(y_ref.dtype)

    if not prepare:
        for c in range(n_chunks):
            recurrence(c)
        for p in range(n_pairs):
            state[p] = s_bd[p]
        return

    recurrence(0)
    qkv = _silu(_causal_conv(x_ref[:, 0:3 * GDN_W], cbuf, conv_ref))
    ins = {}
    for p in range(n_pairs):
        q_t = qkv[:, p * LANES:(p + 1) * LANES]
        k_t = qkv[:, GDN_W + p * LANES:GDN_W + (p + 1) * LANES]
        v_t = qkv[:, 2 * GDN_W + p * LANES:2 * GDN_W + (p + 1) * LANES]
        q_t = q_t * lax.rsqrt(_half_sum(q_t * q_t, m_a_t) + EPS) * (HEAD_DIM ** -0.5)
        k_t = k_t * lax.rsqrt(_half_sum(k_t * k_t, m_a_t) + EPS)
        beta_t = _col_form(beta_all, G_BETA + 2 * p, m_a_t)
        gc_t = _col_form(gc_all, G_A + 2 * p, m_a_t)
        for c in range(n_chunks):
            rs = slice(c * CHUNK, (c + 1) * CHUNK)
            ins[p, c] = (q_t[rs], k_t[rs], v_t[rs], beta_t[rs], gc_t[rs])
    kkqk = {u: _dot_nt(jnp.concatenate([ins[u][1] * ins[u][3], ins[u][0]], axis=0), _block_diag(ins[u][1], m_a))
            for u in units}
    pre = {}
    for u in units:
        q, k, v, beta, gc = ins[u]
        decay = jnp.where(tril, jnp.exp(gc - _row_form(gc, eye)), 0.0)
        egc = jnp.exp(gc)
        g_last = gc[CHUNK - 1:CHUNK, :]
        k_dec = k * jnp.exp(g_last - gc)
        pre[u] = dict(
            a=jnp.where(strict, kkqk[u][:CHUNK] * decay, 0.0), qk=kkqk[u][CHUNK:] * decay,
            vb=v * beta, kbe=k * beta * egc, qe=q * egc, s_decay=jnp.exp(g_last),
            k_dec_t=jnp.concatenate([k_dec, jnp.zeros_like(k_dec)], axis=0).T)

    pw = {u: _pmul(pre[u]["a"], pre[u]["a"], m_a) for u in units}
    t_inv = {u: eye_f - pre[u]["a"] for u in units}
    later_chunks = list(range(1, n_chunks))
    for step in range(4):
        if step % 2 == 0 and later_chunks:
            recurrence(later_chunks.pop(0))
        both = {u: _dot(jnp.concatenate([pw[u], t_inv[u]], axis=0), _block_diag(pw[u], m_a)) for u in units}
        pw = {u: both[u][:CHUNK] for u in units}
        t_inv = {u: t_inv[u] + both[u][CHUNK:] for u in units}
    for c in later_chunks:
        recurrence(c)
    t_inv = {u: t_inv[u] + _pmul(t_inv[u], pw[u], m_a) for u in units}
    for p, c in units:
        d = pre[p, c]
        uw = _dot(t_inv[p, c], jnp.concatenate([_block_diag(d["vb"], m_a), _block_diag(d["kbe"], m_a)], axis=1))
        i = unit_index(p, c)
        u_s[i] = uw[:, :LANES]
        wq_s[i] = jnp.concatenate([uw[:, LANES:], d["qe"]], axis=0).astype(BF16)
        qk_s[i] = d["qk"].astype(BF16)
        kt_s[i] = d["k_dec_t"].astype(BF16)
        sd_s[i] = jnp.broadcast_to(d["s_decay"], (8, LANES))
    for p in range(n_pairs):
        state[p] = s_bd[p]


def _ret_kernel(x_ref, cos_ref, sin_ref, gain_ref, y_ref, state):
    tb = x_ref.shape[0]

    @pl.when(pl.program_id(1) == 0)
    def _():
        state[...] = jnp.zeros_like(state)

    lane, row = _lane_masks(CHUNK)
    m_a = lane < HEAD_DIM
    lane_h = lane % HEAD_DIM
    tril = lane_h <= row
    lane2 = lax.broadcasted_iota(jnp.int32, (LANES, LANES), 1)
    row2 = lax.broadcasted_iota(jnp.int32, (LANES, LANES), 0)
    bd_mask = (lane2 < HEAD_DIM) == (row2 < HEAD_DIM)
    lane_t = lax.broadcasted_iota(jnp.int32, (tb, LANES), 1)
    first_half = (lane_t % HEAD_DIM) < (HEAD_DIM // 2)
    cos = cos_ref[...]
    sin = sin_ref[...]
    rowf = row.astype(F32)
    pos_diff = (row - lane_h).astype(F32)

    def rope(x):
        swapped = jnp.where(first_half, pltpu.roll(x, LANES - HEAD_DIM // 2, axis=1),
                            pltpu.roll(x, HEAD_DIM // 2, axis=1))
        return x * cos + swapped * sin

    n_pairs = RET_HEADS // 2
    n_chunks = tb // CHUNK
    units = [(p, c) for p in range(n_pairs) for c in range(n_chunks)]
    chunk_decay, decay, q_scale, k_scale, qkv = [], [], [], [], {}
    for p in range(n_pairs):
        lg_a = math.log(1.0 - 2.0 ** (-5.0 - 2 * p))
        lg_b = math.log(1.0 - 2.0 ** (-5.0 - (2 * p + 1)))
        lg = jnp.where(m_a, lg_a, lg_b)
        decay.append(jnp.where(tril, jnp.exp(pos_diff * lg), 0.0))
        q_scale.append(jnp.exp(lg * (rowf + 1.0)))
        k_scale.append(jnp.exp(lg * (CHUNK - 1.0 - rowf)))
        chunk_decay.append(jnp.exp(lg[0:1, :] * CHUNK))
        q_t = rope(x_ref[:, p * LANES:(p + 1) * LANES])
        k_t = rope(x_ref[:, RET_W + p * LANES:RET_W + (p + 1) * LANES]) * (HEAD_DIM ** -0.5)
        for c in range(n_chunks):
            rs = slice(c * CHUNK, (c + 1) * CHUNK)
            qkv[p, c] = (q_t[rs], k_t[rs], x_ref[rs, 2 * RET_W + p * LANES:2 * RET_W + (p + 1) * LANES])
    scores = {(p, c): _dot_nt(qkv[p, c][0], _block_diag(qkv[p, c][1], m_a)) * decay[p] for p, c in units}
    o_intra = {u: _pmul(scores[u], qkv[u][2], m_a) for u in units}
    kv_all = {(p, c): _outer_state(qkv[p, c][1] * k_scale[p], qkv[p, c][2], bd_mask) for p, c in units}
    pre = {(p, c): (o_intra[p, c], qkv[p, c][0] * q_scale[p], kv_all[p, c]) for p, c in units}

    s_in = {}
    for p in range(n_pairs):
        s_bd = state[p]
        for c in range(n_chunks):
            s_in[p, c] = s_bd
            s_bd = chunk_decay[p] * s_bd + pre[p, c][2]
        state[p] = s_bd
    o_all = {u: pre[u][0] + _dot(pre[u][1], s_in[u]) for u in units}
    ms_all = {u: _half_sum(o_all[u] * o_all[u], m_a) * (1.0 / HEAD_DIM) for u in units}
    for p, c in units:
        rs = slice(c * CHUNK, (c + 1) * CHUNK)
        cs = slice(p * LANES, (p + 1) * LANES)
        gate = x_ref[rs, 3 * RET_W + p * LANES:3 * RET_W + (p + 1) * LANES]
        y = o_all[p, c] * lax.rsqrt(ms_all[p, c] + EPS) * gain_ref[:, cs] * _silu(gate)
        y_ref[rs, cs] = y.astype(y_ref.dtype)


def _mlstm_kernel(x_ref, gt_ref, conv_ref, prm_ref, gain_ref, y_ref, c_state, nm_state, cbuf):
    tb = x_ref.shape[0]
    first = pl.program_id(1) == 0

    @pl.when(first)
    def _():
        c_state[...] = jnp.zeros_like(c_state)
        nm_state[...] = jnp.zeros_like(nm_state)
        cbuf[0:8, :] = jnp.zeros((8, cbuf.shape[1]), F32)

    qk_all = _silu(_causal_conv(x_ref[:, 0:2 * MLSTM_W], cbuf, conv_ref))
    gt = gt_ref[...]
    log_i_all = gt + prm_ref[0:1, :]
    f_pre = gt + prm_ref[1:2, :]
    log_f_all = jnp.minimum(f_pre, 0.0) - jnp.log(1.0 + jnp.exp(-jnp.abs(f_pre)))
    bc_all = _chunk_cumsum(log_f_all)

    lane, row = _lane_masks(CHUNK)
    m_a = lane < HEAD_DIM
    lane_h = lane % HEAD_DIM
    eye = lane_h == row
    tril = lane_h <= row
    lane2 = lax.broadcasted_iota(jnp.int32, (LANES, LANES), 1)
    row2 = lax.broadcasted_iota(jnp.int32, (LANES, LANES), 0)
    bd_mask = (lane2 < HEAD_DIM) == (row2 < HEAD_DIM)
    lane_t = lax.broadcasted_iota(jnp.int32, (tb, LANES), 1)
    m_a_t = lane_t < HEAD_DIM

    n_pairs = MLSTM_HEADS // 2
    n_chunks = tb // CHUNK
    units = [(p, c) for p in range(n_pairs) for c in range(n_chunks)]
    gates = {}
    for p in range(n_pairs):
        b_t = _col_form(bc_all, M_F + 2 * p, m_a_t)
        i_t = _col_form(log_i_all, M_I + 2 * p, m_a_t)
        for c in range(n_chunks):
            rs = slice(c * CHUNK, (c + 1) * CHUNK)
            gates[p, c] = (b_t[rs], i_t[rs])
    qkv = {(p, c): (qk_all[c * CHUNK:(c + 1) * CHUNK, p * LANES:(p + 1) * LANES] * (HEAD_DIM ** -0.5),
                    qk_all[c * CHUNK:(c + 1) * CHUNK, MLSTM_W + p * LANES:MLSTM_W + (p + 1) * LANES],
                    x_ref[c * CHUNK:(c + 1) * CHUNK, 2 * MLSTM_W + p * LANES:2 * MLSTM_W + (p + 1) * LANES])
           for p, c in units}
    qk = {u: _dot_nt(qkv[u][0], _block_diag(qkv[u][1], m_a)) for u in units}
    log_d = {u: jnp.where(tril, gates[u][0] - _row_form(gates[u][0], eye) + _row_form(gates[u][1], eye), -jnp.inf)
             for u in units}
    m_intra = {u: _half_max(log_d[u], m_a) for u in units}
    a_c = {u: gates[u][0][CHUNK - 1:CHUNK, :] - gates[u][0] + gates[u][1] for u in units}
    a_max = {u: jnp.max(a_c[u], axis=0, keepdims=True) for u in units}
    m_in, m_out = {}, {}
    for p in range(n_pairs):
        m_row = nm_state[p, 1:2, :]
        for c in range(n_chunks):
            m_in[p, c] = m_row
            m_row = jnp.maximum(gates[p, c][0][CHUNK - 1:CHUNK, :] + m_row, a_max[p, c])
            m_out[p, c] = m_row
    m_t = {u: jnp.maximum(gates[u][0] + m_in[u], m_intra[u]) for u in units}
    inter = {u: jnp.exp(gates[u][0] + m_in[u] - m_t[u]) for u in units}
    wmat = {u: jnp.where(tril, jnp.exp(log_d[u] - m_t[u]), 0.0) * qk[u] for u in units}
    kw = {u: qkv[u][1] * jnp.exp(a_c[u] - m_out[u]) for u in units}
    dec = {u: jnp.exp(gates[u][0][CHUNK - 1:CHUNK, :] + m_in[u] - m_out[u]) for u in units}
    num_intra = {u: _pmul(wmat[u], qkv[u][2], m_a) for u in units}
    den_intra = {u: _half_sum(wmat[u], m_a) for u in units}
    kv = {u: _outer_state(kw[u], qkv[u][2], bd_mask) for u in units}
    kn = {u: jnp.sum(kw[u], axis=0, keepdims=True) for u in units}
    c_in, n_in = {}, {}
    for p in range(n_pairs):
        c_bd, n_row = c_state[p], nm_state[p, 0:1, :]
        for c in range(n_chunks):
            c_in[p, c], n_in[p, c] = c_bd, n_row
            c_bd = dec[p, c] * c_bd + kv[p, c]
            n_row = dec[p, c] * n_row + kn[p, c]
        c_state[p] = c_bd
        nm_state[p, 0:1, :] = n_row
        nm_state[p, 1:2, :] = m_out[p, n_chunks - 1]
    num = {u: inter[u] * _dot(qkv[u][0], c_in[u]) + num_intra[u] for u in units}
    den = {u: inter[u] * _half_sum(qkv[u][0] * n_in[u], m_a) + den_intra[u] for u in units}
    h = {u: num[u] / jnp.maximum(jnp.abs(den[u]), jnp.exp(-m_t[u])) for u in units}
    ms = {u: _half_sum(h[u] * h[u], m_a) * (1.0 / HEAD_DIM) for u in units}
    for p, c in units:
        rs = slice(c * CHUNK, (c + 1) * CHUNK)
        cs = slice(p * LANES, (p + 1) * LANES)
        o_logit = x_ref[rs, 3 * MLSTM_W + p * LANES:3 * MLSTM_W + (p + 1) * LANES]
        y = _sigmoid(o_logit) * (h[p, c] * lax.rsqrt(ms[p, c] + EPS) * gain_ref[:, cs])
        y_ref[rs, cs] = y.astype(y_ref.dtype)


def _mixer_call(kernel, name, proj3, sec_block, sec_width, out_width, extra_inputs, extra_specs, scratch):
    b, t, _ = proj3.shape
    tb = min(LONG_BLOCK, t)
    return pl.pallas_call(
        kernel,
        grid=(b, t // tb),
        in_specs=[pl.BlockSpec((None, tb, sec_width), lambda i, j: (i, j, sec_block))] + extra_specs(tb),
        out_specs=pl.BlockSpec((None, tb, out_width), lambda i, j: (i, j, 0)),
        out_shape=jax.ShapeDtypeStruct((b, t, out_width), BF16),
        scratch_shapes=scratch(tb),
        compiler_params=_cparams(("parallel", "arbitrary")),
        name=name,
    )(proj3, *extra_inputs)


def _full2(a, b):
    return pl.BlockSpec((a, b), lambda i, j: (0, 0))


def _gdn_mixer(proj3, gates3, conv_w, prm, gain):
    b, t, _ = proj3.shape
    tb = min(TIME_BLOCK, t)
    n_blocks = t // tb
    n_units = (GDN_HEADS // 2) * (tb // CHUNK)
    cur = lambda i, j: (i, jnp.minimum(j, n_blocks - 1))
    prv = lambda i, j: (i, jnp.maximum(j - 1, 0))
    z_block = (SEC_M + 3 * GDN_W) // GDN_W
    assert z_block * GDN_W == SEC_M + 3 * GDN_W
    return pl.pallas_call(
        functools.partial(_gdn_kernel, n_blocks),
        grid=(b, n_blocks + 1),
        in_specs=[pl.BlockSpec((None, tb, SEC_G), lambda i, j: cur(i, j) + (1,)),
                  pl.BlockSpec((None, tb, LANES), lambda i, j: cur(i, j) + (0,)),
                  pl.BlockSpec((None, tb, GDN_W), lambda i, j: prv(i, j) + (z_block,)),
                  _full2(CONV_K, 3 * GDN_W), _full2(8, LANES), _full2(1, GDN_W)],
        out_specs=pl.BlockSpec((None, tb, GDN_W), lambda i, j: prv(i, j) + (0,)),
        out_shape=jax.ShapeDtypeStruct((b, t, GDN_W), BF16),
        scratch_shapes=[pltpu.VMEM((GDN_HEADS // 2, LANES, LANES), F32),
                        pltpu.VMEM((n_units, CHUNK, LANES), F32),
                        pltpu.VMEM((n_units, 2 * CHUNK, LANES), BF16),
                        pltpu.VMEM((n_units, CHUNK, LANES), BF16),
                        pltpu.VMEM((n_units, LANES, LANES), BF16),
                        pltpu.VMEM((n_units, 8, LANES), F32),
                        pltpu.VMEM((tb + 8, 3 * GDN_W), F32)],
        compiler_params=_cparams(("parallel", "arbitrary")),
        name="gdn_mixer",
    )(proj3, gates3, proj3, conv_w, prm, gain)


def _mlstm_mixer(proj3, gates3, conv_w, prm, gain):
    return _mixer_call(
        _mlstm_kernel, "mlstm_mixer", proj3, 0, SEC_M, MLSTM_W, (gates3, conv_w, prm, gain),
        lambda tb: [pl.BlockSpec((None, tb, LANES), lambda i, j: (i, j, 0)),
                    _full2(CONV_K, 2 * MLSTM_W), _full2(8, LANES), _full2(1, MLSTM_W)],
        lambda tb: [pltpu.VMEM((MLSTM_HEADS // 2, LANES, LANES), F32),
                    pltpu.VMEM((MLSTM_HEADS // 2, 8, LANES), F32),
                    pltpu.VMEM((tb + 8, 2 * MLSTM_W), F32)])


def _ret_mixer(proj3, cos_tab, sin_tab, gain):
    return _mixer_call(
        _ret_kernel, "retention_mixer", proj3, 3, SEC_R, RET_W, (cos_tab, sin_tab, gain),
        lambda tb: [pl.BlockSpec((tb, LANES), lambda i, j: (j, 0)),
                    pl.BlockSpec((tb, LANES), lambda i, j: (j, 0)),
                    _full2(1, RET_W)],
        lambda tb: [pltpu.VMEM((RET_HEADS // 2, LANES, LANES), F32)])


DISPATCH_TILE = 2048


def _dispatch_kernel(pos_ref, src_ref, zero_ref, out_ref, sem):
    del zero_ref
    rows = src_ref.shape[0]
    base = pl.program_id(0) * rows

    def issue(r, carry):
        pltpu.make_async_copy(src_ref.at[pl.ds(r, 1)], out_ref.at[pl.ds(pos_ref[base + r], 1)], sem).start()
        return carry

    lax.fori_loop(0, rows, issue, 0, unroll=8)
    pltpu.make_async_copy(src_ref, out_ref.at[pl.ds(0, rows)], sem).wait()


def _dispatch(src, pos, zeros):
    n, width = src.shape
    n_rows = zeros.shape[0]
    tile = min(DISPATCH_TILE, n)
    return pl.pallas_call(
        _dispatch_kernel,
        grid_spec=pltpu.PrefetchScalarGridSpec(
            num_scalar_prefetch=1,
            grid=(n // tile,),
            in_specs=[pl.BlockSpec((tile, width), lambda i, p: (i, 0)), pl.BlockSpec(memory_space=pl.ANY)],
            out_specs=pl.BlockSpec(memory_space=pl.ANY),
            scratch_shapes=[pltpu.SemaphoreType.DMA(())],
        ),
        out_shape=jax.ShapeDtypeStruct((n_rows, width), src.dtype),
        input_output_aliases={2: 0},
        compiler_params=_cparams(("arbitrary",)),
        name="moe_dispatch",
    )(pos, src, zeros)


def _ffn_kernel(plan_ref, used_ref, x_ref, wg_hbm, wu_hbm, wd_hbm, y_ref, wg_buf, wu_buf, wd_buf, sems):
    def group_copies(g, to_slot):
        lo = pl.multiple_of(g * EXPERTS_PER_GROUP, EXPERTS_PER_GROUP)
        experts = pl.ds(lo, EXPERTS_PER_GROUP)
        return [pltpu.make_async_copy(hbm.at[experts], buf.at[to_slot], sems.at[to_slot])
                for hbm, buf in ((wg_hbm, wg_buf), (wu_hbm, wu_buf), (wd_hbm, wd_buf))]

    def one_tile(i, rows):
        used = i < used_ref[0]
        group, slot, starts_group, next_group = plan_ref[0, i], plan_ref[1, i], plan_ref[2, i], plan_ref[3, i]

        @pl.when(used & (i == 0))
        def _():
            for cp in group_copies(group, slot):
                cp.start()

        @pl.when(used & (starts_group == 1) & (next_group >= 0))
        def _():
            for cp in group_copies(next_group, 1 - slot):
                cp.start()

        @pl.when(used & (starts_group == 1))
        def _():
            for cp in group_copies(group, slot):
                cp.wait()

        @pl.when(jnp.logical_not(used))
        def _():
            y_ref[rows, :] = jnp.zeros((MOE_TILE, D_MODEL), F32)

        @pl.when(used)
        def _():
            e1, e2 = plan_ref[4, i], plan_ref[5, i]
            x = x_ref[rows, 0:D_MODEL].astype(BF16)
            gate1 = jnp.dot(x, wg_buf[slot, e1], preferred_element_type=F32)
            gate2 = jnp.dot(x, wg_buf[slot, e2], preferred_element_type=F32)
            up1 = jnp.dot(x, wu_buf[slot, e1], preferred_element_type=F32)
            up2 = jnp.dot(x, wu_buf[slot, e2], preferred_element_type=F32)
            hid1 = (_silu(gate1) * up1).astype(BF16)
            hid2 = (_silu(gate2) * up2).astype(BF16)
            y1 = jnp.dot(hid1, wd_buf[slot, e1], preferred_element_type=F32)
            y2 = jnp.dot(hid2, wd_buf[slot, e2], preferred_element_type=F32)
            rw = x_ref[rows, D_MODEL:]
            y_ref[rows, :] = rw[:, 0:1] * y1 + rw[:, 1:2] * y2

    for sub in range(FFN_STEP_TILES):
        one_tile(pl.program_id(0) * FFN_STEP_TILES + sub, slice(sub * MOE_TILE, (sub + 1) * MOE_TILE))


def _expert_ffn(xs, tile_plan, n_used, wg, wu, wd):
    r = xs.shape[0]
    step_rows = FFN_STEP_TILES * MOE_TILE
    assert r % step_rows == 0
    last_used = lambda nu: (nu[0] - 1) // FFN_STEP_TILES
    return pl.pallas_call(
        _ffn_kernel,
        grid_spec=pltpu.PrefetchScalarGridSpec(
            num_scalar_prefetch=2,
            grid=(r // step_rows,),
            in_specs=[pl.BlockSpec((step_rows, HP_WIDTH), lambda i, plan, nu: (jnp.minimum(i, last_used(nu)), 0)),
                      pl.BlockSpec(memory_space=pl.ANY), pl.BlockSpec(memory_space=pl.ANY),
                      pl.BlockSpec(memory_space=pl.ANY)],
            out_specs=pl.BlockSpec((step_rows, D_MODEL), lambda i, plan, nu: (i, 0)),
            scratch_shapes=[pltpu.VMEM((2, EXPERTS_PER_GROUP, D_MODEL, D_EXPERT), BF16),
                            pltpu.VMEM((2, EXPERTS_PER_GROUP, D_MODEL, D_EXPERT), BF16),
                            pltpu.VMEM((2, EXPERTS_PER_GROUP, D_EXPERT, D_MODEL), BF16),
                            pltpu.SemaphoreType.DMA((2,))],
        ),
        out_shape=jax.ShapeDtypeStruct((r, D_MODEL), F32),
        compiler_params=_cparams(("arbitrary",)),
        name="expert_pair_ffn",
    )(tile_plan, n_used, xs, wg, wu, wd)


def _combine_kernel(final_norm, pos_ref, x_ref, gain_ref, ys_ref, o_ref, buf, sems):
    rows = o_ref.shape[0]
    i = pl.program_id(0)
    slot = i & 1

    def gather_tile(tile, to_slot):
        def issue(r, carry):
            pltpu.make_async_copy(ys_ref.at[pl.ds(pos_ref[tile * rows + r], 1)],
                                  buf.at[to_slot, pl.ds(r, 1)], sems.at[to_slot]).start()
            return carry

        lax.fori_loop(0, rows, issue, 0, unroll=8)

    @pl.when(i == 0)
    def _():
        gather_tile(0, 0)

    @pl.when(i + 1 < pl.num_programs(0))
    def _():
        gather_tile(i + 1, 1 - slot)

    pltpu.make_async_copy(ys_ref.at[pl.ds(0, rows)], buf.at[slot], sems.at[slot]).wait()
    out = x_ref[...] + buf[slot]
    if final_norm:
        ms = jnp.mean(out * out, axis=-1, keepdims=True)
        out = out * lax.rsqrt(ms + EPS) * gain_ref[...]
    o_ref[...] = out


def _combine(x2, ys, pos, gain, final_norm):
    n = x2.shape[0]
    return pl.pallas_call(
        functools.partial(_combine_kernel, final_norm),
        grid_spec=pltpu.PrefetchScalarGridSpec(
            num_scalar_prefetch=1,
            grid=(n // ROW_TILE,),
            in_specs=[pl.BlockSpec((ROW_TILE, D_MODEL), lambda i, p: (i, 0)),
                      pl.BlockSpec((1, D_MODEL), lambda i, p: (0, 0)),
                      pl.BlockSpec(memory_space=pl.ANY)],
            out_specs=pl.BlockSpec((ROW_TILE, D_MODEL), lambda i, p: (i, 0)),
            scratch_shapes=[pltpu.VMEM((2, ROW_TILE, D_MODEL), F32), pltpu.SemaphoreType.DMA((2,))],
        ),
        out_shape=jax.ShapeDtypeStruct((n, D_MODEL), F32),
        compiler_params=_cparams(("arbitrary",)),
        name="moe_combine",
    )(pos, x2, gain, ys)


def _plan(route, counts, n):
    cls, rank = route[:, 0], route[:, 1]
    counts = counts.astype(jnp.int32)
    tiles_per = (counts + MOE_TILE - 1) // MOE_TILE
    tile_end = jnp.cumsum(tiles_per)
    tile_start = tile_end - tiles_per
    class_ids = jnp.arange(N_CLASSES, dtype=jnp.int32)
    start_of = jnp.sum(jnp.where(cls[:, None] == class_ids[None, :], tile_start[None, :], 0), axis=1)
    pos = start_of * MOE_TILE + rank
    n_tiles = _sorted_rows(n) // MOE_TILE
    n_used = tile_end[-1]
    tile_ids = jnp.minimum(jnp.arange(n_tiles, dtype=jnp.int32), n_used - 1)
    tile_cls = jnp.sum((tile_end[None, :] <= tile_ids[:, None]).astype(jnp.int32), axis=1)
    per_group = EXPERTS_PER_GROUP * EXPERTS_PER_GROUP
    tile_group = tile_cls // per_group
    starts_group = jnp.concatenate([jnp.ones((1,), jnp.int32),
                                    (tile_group[1:] != tile_group[:-1]).astype(jnp.int32)])
    slot = (jnp.cumsum(starts_group) - 1) % 2
    present = jnp.sum(tiles_per.reshape(N_GROUPS, per_group), axis=1) > 0
    nxt = [jnp.int32(-1)] * N_GROUPS
    for g in range(N_GROUPS - 2, -1, -1):
        nxt[g] = jnp.where(present[g + 1], g + 1, nxt[g + 1])
    next_group = jnp.stack(nxt)[tile_group]
    tile_plan = jnp.stack([tile_group, slot, starts_group, next_group,
                           (tile_cls // EXPERTS_PER_GROUP) % EXPERTS_PER_GROUP, tile_cls % EXPERTS_PER_GROUP])
    return pos, tile_plan.astype(jnp.int32), n_used.reshape(1)


def _rope_tables(t):
    inv_freq = ROPE_BASE ** (-jnp.arange(0, HEAD_DIM, 2, dtype=F32) / HEAD_DIM)
    ang = jnp.arange(t, dtype=F32)[:, None] * inv_freq[None, :]
    cos, sin = jnp.cos(ang), jnp.sin(ang)
    cos_tab = jnp.tile(cos, (1, 2 * LANES // HEAD_DIM))
    sin_tab = jnp.tile(jnp.concatenate([-sin, sin], axis=-1), (1, LANES // HEAD_DIM))
    return cos_tab, sin_tab


def _split_w_in(w):
    g0, g1 = 0, SEC_G
    r0, r1 = g1 + 2 * GDN_HEADS, g1 + 2 * GDN_HEADS + SEC_R
    m0, m1 = r1, r1 + SEC_M
    assert m1 + 2 * MLSTM_HEADS == w.shape[1]
    main = jnp.concatenate([w[:, m0:m1], w[:, g0:g1], w[:, r0:r1]], axis=1).astype(BF16)
    gate = jnp.concatenate([w[:, g1:r0], w[:, m1:]], axis=1)
    gate = jnp.pad(gate, ((0, 0), (0, LANES - gate.shape[1]))).astype(BF16)
    return main, gate


def _lane_row(values, start):
    return jnp.zeros((LANES,), F32).at[start:start + values.shape[0]].set(values)


def kernel(x, norm_mix, w_in, gdn_conv, gdn_a_log, gdn_dt_bias, gdn_norm, ret_norm, mlstm_conv, mlstm_i_bias, mlstm_f_bias, mlstm_norm, w_out, norm_ffn, router_group, router_expert, router_bias, expert_gate, expert_up, expert_down, norm_final):
    b, t, d = x.shape
    n = b * t
    depth = w_in.shape[0]
    cos_tab, sin_tab = _rope_tables(t)
    x2 = x.reshape(n, d)
    for l in range(depth):
        w_main, w_gate = _split_w_in(w_in[l])
        proj, gates, (wg, wu, wd) = _inproj(x2, norm_mix[l][None, :], w_main, w_gate,
                                            (expert_gate[l], expert_up[l], expert_down[l]))
        proj3 = proj.reshape(b, t, D_MAIN)
        gates3 = gates.reshape(b, t, LANES)
        gdn_prm = jnp.zeros((8, LANES), F32).at[0].set(_lane_row(gdn_a_log[l], G_A)).at[1].set(
            _lane_row(gdn_dt_bias[l], G_A))
        mlstm_prm = jnp.zeros((8, LANES), F32).at[0].set(_lane_row(mlstm_i_bias[l], M_I)).at[1].set(
            _lane_row(mlstm_f_bias[l], M_F))
        y_m = _mlstm_mixer(proj3, gates3, mlstm_conv[l], mlstm_prm, mlstm_norm[l][None, :])
        y_g = _gdn_mixer(proj3, gates3, gdn_conv[l], gdn_prm, gdn_norm[l][None, :])
        y_r = _ret_mixer(proj3, cos_tab, sin_tab, ret_norm[l][None, :])
        wo = w_out[l].astype(BF16)
        w_router = jnp.pad(jnp.concatenate([router_group[l], router_expert[l]], axis=1),
                           ((0, 0), (0, LANES - N_GROUPS - N_EXPERTS))).astype(BF16)
        rb = _lane_row(router_bias[l], N_GROUPS)[None, :]
        x2, h_packed, route, counts, zeros = _outproj(
            x2, y_m.reshape(n, MLSTM_W), y_g.reshape(n, GDN_W), y_r.reshape(n, RET_W),
            wo[GDN_W + RET_W:], wo[:GDN_W], wo[GDN_W:GDN_W + RET_W],
            norm_ffn[l][None, :], w_router, rb)
        pos, tile_plan, n_used = _plan(route, counts[0], n)
        xs = _dispatch(h_packed, pos, zeros)
        ys = _expert_ffn(xs, tile_plan, n_used, wg, wu, wd)
        x2 = _combine(x2, ys, pos, norm_final[None, :], l == depth - 1)
    return x2.reshape(b, t, d)
```

```python
import functools
import math

import jax
import jax.numpy as jnp
from jax import lax
from jax.experimental import pallas as pl
from jax.experimental.pallas import tpu as pltpu

F32 = jnp.float32
BF16 = jnp.bfloat16

D_MODEL = 1024
HEAD_DIM = 64
CHUNK = 64
GDN_HEADS, RET_HEADS, MLSTM_HEADS = 6, 4, 6
GDN_W, RET_W, MLSTM_W = GDN_HEADS * HEAD_DIM, RET_HEADS * HEAD_DIM, MLSTM_HEADS * HEAD_DIM
CONV_K = 4
ROPE_BASE = 10000.0
N_GROUPS, EXPERTS_PER_GROUP = 4, 8
N_EXPERTS = N_GROUPS * EXPERTS_PER_GROUP
D_EXPERT = D_MODEL // 4
EPS = 1e-6
LANES = 128
SEC_M, SEC_G, SEC_R = 4 * MLSTM_W, 4 * GDN_W, 4 * RET_W
D_MAIN = SEC_M + SEC_G + SEC_R
G_BETA, G_A, M_I, M_F = 0, GDN_HEADS, 2 * GDN_HEADS, 2 * GDN_HEADS + MLSTM_HEADS

ROW_TILE = 512
OUT_TILE = 512
OUT_SUB = 128
TIME_BLOCK = 256
LONG_BLOCK = 512
MOE_TILE = 128
FFN_STEP_TILES = 8
VMEM_LIMIT = 56 * 1024 * 1024


def _cparams(sem):
    return pltpu.CompilerParams(dimension_semantics=sem, vmem_limit_bytes=VMEM_LIMIT)


def _silu(x):
    return x * (1.0 / (1.0 + jnp.exp(-x)))


def _sigmoid(x):
    return 1.0 / (1.0 + jnp.exp(-x))


def _softplus(x):
    return jnp.maximum(x, 0.0) + jnp.log(1.0 + jnp.exp(-jnp.abs(x)))


def _dot(a, b):
    return jnp.dot(a.astype(BF16), b.astype(BF16), preferred_element_type=F32)


def _dot_nt(a, b):
    return lax.dot_general(a.astype(BF16), b.astype(BF16), (((1,), (1,)), ((), ())),
                           preferred_element_type=F32)


def _lane_masks(rows):
    lane = lax.broadcasted_iota(jnp.int32, (rows, LANES), 1)
    row = lax.broadcasted_iota(jnp.int32, (rows, LANES), 0)
    return lane, row


def _half_sum(x, m_a):
    s_a = jnp.sum(jnp.where(m_a, x, 0.0), axis=-1, keepdims=True)
    s_b = jnp.sum(jnp.where(m_a, 0.0, x), axis=-1, keepdims=True)
    return jnp.where(m_a, s_a, s_b)


def _half_max(x, m_a):
    s_a = jnp.max(jnp.where(m_a, x, -jnp.inf), axis=-1, keepdims=True)
    s_b = jnp.max(jnp.where(m_a, -jnp.inf, x), axis=-1, keepdims=True)
    return jnp.where(m_a, s_a, s_b)


def _col_form(g, lane_a, m_a):
    rows = g.shape[0]
    ca = jnp.broadcast_to(g[:, lane_a:lane_a + 1], (rows, LANES))
    cb = jnp.broadcast_to(g[:, lane_a + 1:lane_a + 2], (rows, LANES))
    return jnp.where(m_a, ca, cb)


def _row_form(col, eye):
    return jnp.sum(jnp.where(eye, col, 0.0), axis=0, keepdims=True)


def _block_diag(y, m_a):
    return jnp.concatenate([jnp.where(m_a, y, 0.0), jnp.where(m_a, 0.0, y)], axis=0)


def _pmul(x, y, m_a):
    return _dot(x, _block_diag(y, m_a))


def _outer_state(k, v, bd_mask):
    zero = jnp.zeros_like(k)
    kt = jnp.concatenate([k, zero], axis=0).T
    vp = jnp.concatenate([v, zero], axis=0)
    return jnp.where(bd_mask, _dot(kt, vp), 0.0)


def _chunk_cumsum(x):
    rows = x.shape[0]
    r = lax.broadcasted_iota(jnp.int32, (rows, LANES), 0) % CHUNK
    s = 1
    while s < CHUNK:
        x = x + jnp.where(r >= s, pltpu.roll(x, s, axis=0), 0.0)
        s *= 2
    return x


def _causal_conv(raw, cbuf, w_ref):
    rows = raw.shape[0]
    cbuf[8:8 + rows, :] = raw
    acc = raw * w_ref[CONV_K - 1:CONV_K, :]
    for j in range(CONV_K - 1):
        off = 8 - (CONV_K - 1) + j
        acc = acc + cbuf[off:off + rows, :] * w_ref[j:j + 1, :]
    cbuf[0:8, :] = raw[rows - 8:rows, :]
    return acc


def _inproj_kernel(x_ref, gain_ref, w_ref, wg_ref, *rest):
    cast_refs, (o_ref, og_ref) = rest[:3] + rest[5:], rest[3:5]
    x = x_ref[...]
    ms = jnp.mean(x * x, axis=-1, keepdims=True)
    h = (x * lax.rsqrt(ms + EPS) * gain_ref[...]).astype(BF16)

    def proj(lo, hi):
        return jnp.dot(h, w_ref[:, lo:hi], preferred_element_type=F32)

    step = 512
    for c in range(D_MAIN // step):
        o_ref[:, c * step:(c + 1) * step] = proj(c * step, (c + 1) * step)
    og_ref[...] = jnp.dot(h, wg_ref[...], preferred_element_type=F32)
    for src, dst in zip(cast_refs[:3], cast_refs[3:]):
        dst[...] = src[...].astype(BF16)


def _inproj(x2, gain, w_main, w_gate, expert_weights, layer):
    n = x2.shape[0]
    steps = n // OUT_TILE
    full = lambda a, b: pl.BlockSpec((a, b), lambda i: (0, 0))
    flat = [w.reshape(-1, w.shape[-1]) for w in expert_weights]
    rows = [w.shape[0] // expert_weights[0].shape[0] for w in flat]
    assert all(r % (16 * steps) == 0 for r in rows)
    slab_in = [pl.BlockSpec((r // steps, w.shape[1]), lambda i: (layer * steps + i, 0)) for r, w in zip(rows, flat)]
    slab_out = [pl.BlockSpec((r // steps, w.shape[1]), lambda i: (i, 0)) for r, w in zip(rows, flat)]
    outs = pl.pallas_call(
        _inproj_kernel,
        grid=(steps,),
        in_specs=[
            pl.BlockSpec((OUT_TILE, D_MODEL), lambda i: (i, 0)),
            full(1, D_MODEL), full(D_MODEL, D_MAIN), full(D_MODEL, LANES),
        ] + slab_in,
        out_specs=[
            pl.BlockSpec((OUT_TILE, D_MAIN), lambda i: (i, 0)),
            pl.BlockSpec((OUT_TILE, LANES), lambda i: (i, 0)),
        ] + slab_out,
        out_shape=[jax.ShapeDtypeStruct((n, D_MAIN), F32), jax.ShapeDtypeStruct((n, LANES), F32)] + [
            jax.ShapeDtypeStruct((r, w.shape[1]), BF16) for r, w in zip(rows, flat)],
        compiler_params=_cparams(("parallel",)),
        name="norm_inproj",
    )(x2, gain, w_main, w_gate, *flat)
    return outs[0], outs[1], [o.reshape(w.shape[1:]) for o, w in zip(outs[2:], expert_weights)]


N_CLASSES = N_GROUPS * EXPERTS_PER_GROUP * EXPERTS_PER_GROUP
N_PAIRS = N_GROUPS * (EXPERTS_PER_GROUP * (EXPERTS_PER_GROUP - 1)) // 2


def _sorted_rows(n):
    return (n // MOE_TILE + N_PAIRS) * MOE_TILE


HP_WIDTH = D_MODEL + LANES


def _route_rows(lg):
    rows = lg.shape[0]
    lane = lax.broadcasted_iota(jnp.int32, (rows, LANES), 1)
    lane_f = lane.astype(F32)
    neg = -jnp.inf

    def first_lane(mask):
        return jnp.min(jnp.where(mask, lane_f, float(LANES)), axis=-1, keepdims=True).astype(jnp.int32)

    gl = jnp.where(lane < N_GROUPS, lg, neg)
    gm = jnp.max(gl, axis=-1, keepdims=True)
    gsum = jnp.sum(jnp.where(lane < N_GROUPS, jnp.exp(gl - gm), 0.0), axis=-1, keepdims=True)
    gidx = first_lane(gl == gm)
    group_p = 1.0 / gsum
    in_group = (lane >= N_GROUPS) & (lane < N_GROUPS + N_EXPERTS) & (
        lax.shift_right_arithmetic(lane - N_GROUPS, 3) == gidx)
    el = jnp.where(in_group, lg, neg)
    m1 = jnp.max(el, axis=-1, keepdims=True)
    i1 = first_lane(el == m1)
    el2 = jnp.where(lane == i1, neg, el)
    m2 = jnp.max(el2, axis=-1, keepdims=True)
    i2 = first_lane(el2 == m2)
    e2 = jnp.exp(m2 - m1)
    w1 = group_p * (1.0 / (1.0 + e2))
    w2 = group_p * (e2 / (1.0 + e2))
    first_lower = i1 < i2
    base = N_GROUPS + gidx * EXPERTS_PER_GROUP
    lo = jnp.minimum(i1, i2) - base
    hi = jnp.maximum(i1, i2) - base
    cls = (gidx * EXPERTS_PER_GROUP + lo) * EXPERTS_PER_GROUP + hi
    return cls, jnp.where(first_lower, w1, w2), jnp.where(first_lower, w2, w1)


def _outproj_kernel(x_ref, ym_ref, yg_ref, yr_ref, wm_ref, wg_ref, wr_ref, gain_ref, wrt_ref, rb_ref,
                    xo_ref, hp_ref, rt_ref, cnt_ref, zero_ref, running):
    @pl.when(pl.program_id(0) == 0)
    def _():
        running[...] = jnp.zeros_like(running)

    zero_ref[...] = jnp.zeros_like(zero_ref)

    rows = OUT_SUB
    subs = [slice(s * rows, (s + 1) * rows) for s in range(x_ref.shape[0] // rows)]
    acc = [x_ref[r, :] + jnp.dot(yg_ref[r, :], wg_ref[...], preferred_element_type=F32) for r in subs]
    acc = [a + jnp.dot(yr_ref[r, :], wr_ref[...], preferred_element_type=F32) for a, r in zip(acc, subs)]
    acc = [a + jnp.dot(ym_ref[r, :], wm_ref[...], preferred_element_type=F32) for a, r in zip(acc, subs)]
    for a, r in zip(acc, subs):
        xo_ref[r, :] = a
    hn = [a * lax.rsqrt(jnp.mean(a * a, axis=-1, keepdims=True) + EPS) * gain_ref[...] for a in acc]
    for h, r in zip(hn, subs):
        hp_ref[r, 0:D_MODEL] = h
    logits = [jnp.dot(h.astype(BF16), wrt_ref[...], preferred_element_type=F32) + rb_ref[...] for h in hn]
    routed = [_route_rows(lg) for lg in logits]

    cls_lane = lax.broadcasted_iota(jnp.int32, (rows, N_CLASSES), 1)
    before = (lax.broadcasted_iota(jnp.int32, (rows, rows), 1)
              < lax.broadcasted_iota(jnp.int32, (rows, rows), 0)).astype(BF16)
    onehot = [cls_lane == cls for cls, _, _ in routed]
    prefix = [jnp.dot(before, oh.astype(BF16), preferred_element_type=F32) for oh in onehot]
    counts = [jnp.sum(oh.astype(F32), axis=0, keepdims=True) for oh in onehot]
    lane = lax.broadcasted_iota(jnp.int32, (rows, LANES), 1)
    seen = running[0:1, :]
    for r, (cls, w_lo, w_hi), oh, pre, cnt in zip(subs, routed, onehot, prefix, counts):
        rank = jnp.sum(jnp.where(oh, pre + seen, 0.0), axis=-1, keepdims=True)
        seen = seen + cnt
        rt_ref[r, :] = jnp.where(lane == 0, cls, jnp.where(lane == 1, rank.astype(jnp.int32), 0))
        hp_ref[r, D_MODEL:] = jnp.where(lane == 0, w_lo, jnp.where(lane == 1, w_hi, 0.0))
    running[0:1, :] = seen
    cnt_ref[...] = jnp.broadcast_to(seen, cnt_ref.shape)


def _outproj(x2, ym, yg, yr, wo_m, wo_g, wo_r, gain, w_router, router_bias):
    n = x2.shape[0]
    row = lambda w: pl.BlockSpec((OUT_TILE, w), lambda i: (i, 0))
    full = lambda a, b: pl.BlockSpec((a, b), lambda i: (0, 0))
    steps = n // OUT_TILE
    n_rows = _sorted_rows(n)
    zero_rows = n_rows // steps
    assert zero_rows * steps == n_rows and zero_rows % 8 == 0
    return pl.pallas_call(
        _outproj_kernel,
        grid=(steps,),
        in_specs=[row(D_MODEL), row(MLSTM_W), row(GDN_W), row(RET_W),
                  full(MLSTM_W, D_MODEL), full(GDN_W, D_MODEL), full(RET_W, D_MODEL),
                  full(1, D_MODEL), full(D_MODEL, LANES), full(1, LANES)],
        out_specs=[row(D_MODEL), row(HP_WIDTH), row(LANES), full(8, N_CLASSES),
                   pl.BlockSpec((zero_rows, HP_WIDTH), lambda i: (i, 0))],
        out_shape=[jax.ShapeDtypeStruct((n, D_MODEL), F32),
                   jax.ShapeDtypeStruct((n, HP_WIDTH), F32),
                   jax.ShapeDtypeStruct((n, LANES), jnp.int32),
                   jax.ShapeDtypeStruct((8, N_CLASSES), F32),
                   jax.ShapeDtypeStruct((n_rows, HP_WIDTH), F32)],
        scratch_shapes=[pltpu.VMEM((8, N_CLASSES), F32)],
        compiler_params=_cparams(("arbitrary",)),
        name="outproj_norm_router",
    )(x2, ym, yg, yr, wo_m, wo_g, wo_r, gain, w_router, router_bias)


def _gdn_kernel(n_blocks, *refs):
    state, u_s, wq_s, qk_s, kt_s, sd_s, cbuf = refs[-7:]
    j = pl.program_id(1)

    @pl.when(j == 0)
    def _():
        for ref in (state, u_s, wq_s, qk_s, kt_s, sd_s):
            ref[...] = jnp.zeros_like(ref)
        cbuf[0:8, :] = jnp.zeros((8, cbuf.shape[1]), F32)

    @pl.when(j < n_blocks)
    def _():
        _gdn_step(True, *refs)

    @pl.when(j == n_blocks)
    def _():
        _gdn_step(False, *refs)


def _gdn_step(prepare, x_ref, gt_ref, z_ref, conv_ref, prm_ref, gain_ref, y_ref,
              state, u_s, wq_s, qk_s, kt_s, sd_s, cbuf):
    tb = x_ref.shape[0]
    gt = gt_ref[...]
    beta_all = _sigmoid(gt)
    g_all = -jnp.exp(prm_ref[0:1, :]) * _softplus(gt + prm_ref[1:2, :])
    gc_all = _chunk_cumsum(g_all)

    lane, row = _lane_masks(CHUNK)
    m_a = lane < HEAD_DIM
    lane_h = lane % HEAD_DIM
    eye = lane_h == row
    tril = lane_h <= row
    strict = lane_h < row
    eye_f = eye.astype(F32)
    lane2 = lax.broadcasted_iota(jnp.int32, (LANES, LANES), 1)
    row2 = lax.broadcasted_iota(jnp.int32, (LANES, LANES), 0)
    bd_mask = (lane2 < HEAD_DIM) == (row2 < HEAD_DIM)
    lane_t = lax.broadcasted_iota(jnp.int32, (tb, LANES), 1)
    m_a_t = lane_t < HEAD_DIM

    n_pairs = GDN_HEADS // 2
    n_chunks = tb // CHUNK
    units = [(p, c) for p in range(n_pairs) for c in range(n_chunks)]

    def unit_index(p, c):
        return p * n_chunks + c

    prev = {}
    for p, c in units:
        i = unit_index(p, c)
        prev[p, c] = (u_s[i], wq_s[i], qk_s[i], kt_s[i], sd_s[i, 0:1, :])
    s_bd = [state[p] for p in range(n_pairs)]

    def recurrence(c):
        rs = slice(c * CHUNK, (c + 1) * CHUNK)
        for p in range(n_pairs):
            cs = slice(p * LANES, (p + 1) * LANES)
            u_mat, wq, qk, k_dec_t, s_decay = prev[p, c]
            ws_qs = jnp.dot(wq, s_bd[p].astype(BF16), preferred_element_type=F32)
            v_new = u_mat - ws_qs[:CHUNK]
            o = ws_qs[CHUNK:] + jnp.dot(qk, _block_diag(v_new, m_a).astype(BF16), preferred_element_type=F32)
            v_pad = jnp.concatenate([v_new, jnp.zeros_like(v_new)], axis=0).astype(BF16)
            s_bd[p] = s_bd[p] * s_decay + jnp.where(
                bd_mask, jnp.dot(k_dec_t, v_pad, preferred_element_type=F32), 0.0)
            ms = _half_sum(o * o, m_a) * (1.0 / HEAD_DIM)
            y = o * lax.rsqrt(ms + EPS) * gain_ref[:, cs] * _silu(z_ref[rs, cs])
            y_ref[rs, cs] = y.astype(y_ref.dtype)

    if not prepare:
        for c in range(n_chunks):
            recurrence(c)
        for p in range(n_pairs):
            state[p] = s_bd[p]
        return

    recurrence(0)
    qkv = _silu(_causal_conv(x_ref[:, 0:3 * GDN_W], cbuf, conv_ref))
    ins = {}
    for p in range(n_pairs):
        q_t = qkv[:, p * LANES:(p + 1) * LANES]
        k_t = qkv[:, GDN_W + p * LANES:GDN_W + (p + 1) * LANES]
        v_t = qkv[:, 2 * GDN_W + p * LANES:2 * GDN_W + (p + 1) * LANES]
        q_t = q_t * lax.rsqrt(_half_sum(q_t * q_t, m_a_t) + EPS) * (HEAD_DIM ** -0.5)
        k_t = k_t * lax.rsqrt(_half_sum(k_t * k_t, m_a_t) + EPS)
        beta_t = _col_form(beta_all, G_BETA + 2 * p, m_a_t)
        gc_t = _col_form(gc_all, G_A + 2 * p, m_a_t)
        for c in range(n_chunks):
            rs = slice(c * CHUNK, (c + 1) * CHUNK)
            ins[p, c] = (q_t[rs], k_t[rs], v_t[rs], beta_t[rs], gc_t[rs])
    kkqk = {u: _dot_nt(jnp.concatenate([ins[u][1] * ins[u][3], ins[u][0]], axis=0), _block_diag(ins[u][1], m_a))
            for u in units}
    pre = {}
    for u in units:
        q, k, v, beta, gc = ins[u]
        decay = jnp.where(tril, jnp.exp(gc - _row_form(gc, eye)), 0.0)
        egc = jnp.exp(gc)
        g_last = gc[CHUNK - 1:CHUNK, :]
        k_dec = k * jnp.exp(g_last - gc)
        pre[u] = dict(
            a=jnp.where(strict, kkqk[u][:CHUNK] * decay, 0.0), qk=kkqk[u][CHUNK:] * decay,
            vb=v * beta, kbe=k * beta * egc, qe=q * egc, s_decay=jnp.exp(g_last),
            k_dec_t=jnp.concatenate([k_dec, jnp.zeros_like(k_dec)], axis=0).T)

    pw = {u: _pmul(pre[u]["a"], pre[u]["a"], m_a) for u in units}
    t_inv = {u: eye_f - pre[u]["a"] for u in units}
    later_chunks = list(range(1, n_chunks))
    for step in range(4):
        if step % 2 == 0 and later_chunks:
            recurrence(later_chunks.pop(0))
        both = {u: _dot(jnp.concatenate([pw[u], t_inv[u]], axis=0), _block_diag(pw[u], m_a)) for u in units}
        pw = {u: both[u][:CHUNK] for u in units}
        t_inv = {u: t_inv[u] + both[u][CHUNK:] for u in units}
    for c in later_chunks:
        recurrence(c)
    t_inv = {u: t_inv[u] + _pmul(t_inv[u], pw[u], m_a) for u in units}
    for p, c in units:
        d = pre[p, c]
        uw = _dot(t_inv[p, c], jnp.concatenate([_block_diag(d["vb"], m_a), _block_diag(d["kbe"], m_a)], axis=1))
        i = unit_index(p, c)
        u_s[i] = uw[:, :LANES]
        wq_s[i] = jnp.concatenate([uw[:, LANES:], d["qe"]], axis=0).astype(BF16)
        qk_s[i] = d["qk"].astype(BF16)
        kt_s[i] = d["k_dec_t"].astype(BF16)
        sd_s[i] = jnp.broadcast_to(d["s_decay"], (8, LANES))
    for p in range(n_pairs):
        state[p] = s_bd[p]


def _ret_kernel(x_ref, cos_ref, sin_ref, gain_ref, y_ref, state):
    tb = x_ref.shape[0]

    @pl.when(pl.program_id(1) == 0)
    def _():
        state[...] = jnp.zeros_like(state)

    lane, row = _lane_masks(CHUNK)
    m_a = lane < HEAD_DIM
    lane_h = lane % HEAD_DIM
    tril = lane_h <= row
    lane2 = lax.broadcasted_iota(jnp.int32, (LANES, LANES), 1)
    row2 = lax.broadcasted_iota(jnp.int32, (LANES, LANES), 0)
    bd_mask = (lane2 < HEAD_DIM) == (row2 < HEAD_DIM)
    lane_t = lax.broadcasted_iota(jnp.int32, (tb, LANES), 1)
    first_half = (lane_t % HEAD_DIM) < (HEAD_DIM // 2)
    cos = cos_ref[...]
    sin = sin_ref[...]
    rowf = row.astype(F32)
    pos_diff = (row - lane_h).astype(F32)

    def rope(x):
        swapped = jnp.where(first_half, pltpu.roll(x, LANES - HEAD_DIM // 2, axis=1),
                            pltpu.roll(x, HEAD_DIM // 2, axis=1))
        return x * cos + swapped * sin

    n_pairs = RET_HEADS // 2
    n_chunks = tb // CHUNK
    units = [(p, c) for p in range(n_pairs) for c in range(n_chunks)]
    chunk_decay, decay, q_scale, k_scale, qkv = [], [], [], [], {}
    for p in range(n_pairs):
        lg_a = math.log(1.0 - 2.0 ** (-5.0 - 2 * p))
        lg_b = math.log(1.0 - 2.0 ** (-5.0 - (2 * p + 1)))
        lg = jnp.where(m_a, lg_a, lg_b)
        decay.append(jnp.where(tril, jnp.exp(pos_diff * lg), 0.0))
        q_scale.append(jnp.exp(lg * (rowf + 1.0)))
        k_scale.append(jnp.exp(lg * (CHUNK - 1.0 - rowf)))
        chunk_decay.append(jnp.exp(lg[0:1, :] * CHUNK))
        q_t = rope(x_ref[:, p * LANES:(p + 1) * LANES])
        k_t = rope(x_ref[:, RET_W + p * LANES:RET_W + (p + 1) * LANES]) * (HEAD_DIM ** -0.5)
        for c in range(n_chunks):
            rs = slice(c * CHUNK, (c + 1) * CHUNK)
            qkv[p, c] = (q_t[rs], k_t[rs], x_ref[rs, 2 * RET_W + p * LANES:2 * RET_W + (p + 1) * LANES])
    scores = {(p, c): _dot_nt(qkv[p, c][0], _block_diag(qkv[p, c][1], m_a)) * decay[p] for p, c in units}
    o_intra = {u: _pmul(scores[u], qkv[u][2], m_a) for u in units}
    kv_all = {(p, c): _outer_state(qkv[p, c][1] * k_scale[p], qkv[p, c][2], bd_mask) for p, c in units}
    pre = {(p, c): (o_intra[p, c], qkv[p, c][0] * q_scale[p], kv_all[p, c]) for p, c in units}

    s_in = {}
    for p in range(n_pairs):
        s_bd = state[p]
        for c in range(n_chunks):
            s_in[p, c] = s_bd
            s_bd = chunk_decay[p] * s_bd + pre[p, c][2]
        state[p] = s_bd
    o_all = {u: pre[u][0] + _dot(pre[u][1], s_in[u]) for u in units}
    ms_all = {u: _half_sum(o_all[u] * o_all[u], m_a) * (1.0 / HEAD_DIM) for u in units}
    for p, c in units:
        rs = slice(c * CHUNK, (c + 1) * CHUNK)
        cs = slice(p * LANES, (p + 1) * LANES)
        gate = x_ref[rs, 3 * RET_W + p * LANES:3 * RET_W + (p + 1) * LANES]
        y = o_all[p, c] * lax.rsqrt(ms_all[p, c] + EPS) * gain_ref[:, cs] * _silu(gate)
        y_ref[rs, cs] = y.astype(y_ref.dtype)


def _mlstm_kernel(x_ref, gt_ref, conv_ref, prm_ref, gain_ref, y_ref, c_state, nm_state, cbuf):
    tb = x_ref.shape[0]
    first = pl.program_id(1) == 0

    @pl.when(first)
    def _():
        c_state[...] = jnp.zeros_like(c_state)
        nm_state[...] = jnp.zeros_like(nm_state)
        cbuf[0:8, :] = jnp.zeros((8, cbuf.shape[1]), F32)

    qk_all = _silu(_causal_conv(x_ref[:, 0:2 * MLSTM_W], cbuf, conv_ref))
    gt = gt_ref[...]
    log_i_all = gt + prm_ref[0:1, :]
    f_pre = gt + prm_ref[1:2, :]
    log_f_all = jnp.minimum(f_pre, 0.0) - jnp.log(1.0 + jnp.exp(-jnp.abs(f_pre)))
    bc_all = _chunk_cumsum(log_f_all)

    lane, row = _lane_masks(CHUNK)
    m_a = lane < HEAD_DIM
    lane_h = lane % HEAD_DIM
    eye = lane_h == row
    tril = lane_h <= row
    lane2 = lax.broadcasted_iota(jnp.int32, (LANES, LANES), 1)
    row2 = lax.broadcasted_iota(jnp.int32, (LANES, LANES), 0)
    bd_mask = (lane2 < HEAD_DIM) == (row2 < HEAD_DIM)
    lane_t = lax.broadcasted_iota(jnp.int32, (tb, LANES), 1)
    m_a_t = lane_t < HEAD_DIM

    n_pairs = MLSTM_HEADS // 2
    n_chunks = tb // CHUNK
    units = [(p, c) for p in range(n_pairs) for c in range(n_chunks)]
    gates = {}
    for p in range(n_pairs):
        b_t = _col_form(bc_all, M_F + 2 * p, m_a_t)
        i_t = _col_form(log_i_all, M_I + 2 * p, m_a_t)
        for c in range(n_chunks):
            rs = slice(c * CHUNK, (c + 1) * CHUNK)
            gates[p, c] = (b_t[rs], i_t[rs])
    qkv = {(p, c): (qk_all[c * CHUNK:(c + 1) * CHUNK, p * LANES:(p + 1) * LANES] * (HEAD_DIM ** -0.5),
                    qk_all[c * CHUNK:(c + 1) * CHUNK, MLSTM_W + p * LANES:MLSTM_W + (p + 1) * LANES],
                    x_ref[c * CHUNK:(c + 1) * CHUNK, 2 * MLSTM_W + p * LANES:2 * MLSTM_W + (p + 1) * LANES])
           for p, c in units}
    qk = {u: _dot_nt(qkv[u][0], _block_diag(qkv[u][1], m_a)) for u in units}
    log_d = {u: jnp.where(tril, gates[u][0] - _row_form(gates[u][0], eye) + _row_form(gates[u][1], eye), -jnp.inf)
             for u in units}
    m_intra = {u: _half_max(log_d[u], m_a) for u in units}
    a_c = {u: gates[u][0][CHUNK - 1:CHUNK, :] - gates[u][0] + gates[u][1] for u in units}
    a_max = {u: jnp.max(a_c[u], axis=0, keepdims=True) for u in units}
    m_in, m_out = {}, {}
    for p in range(n_pairs):
        m_row = nm_state[p, 1:2, :]
        for c in range(n_chunks):
            m_in[p, c] = m_row
            m_row = jnp.maximum(gates[p, c][0][CHUNK - 1:CHUNK, :] + m_row, a_max[p, c])
            m_out[p, c] = m_row
    m_t = {u: jnp.maximum(gates[u][0] + m_in[u], m_intra[u]) for u in units}
    inter = {u: jnp.exp(gates[u][0] + m_in[u] - m_t[u]) for u in units}
    wmat = {u: jnp.where(tril, jnp.exp(log_d[u] - m_t[u]), 0.0) * qk[u] for u in units}
    kw = {u: qkv[u][1] * jnp.exp(a_c[u] - m_out[u]) for u in units}
    dec = {u: jnp.exp(gates[u][0][CHUNK - 1:CHUNK, :] + m_in[u] - m_out[u]) for u in units}
    num_intra = {u: _pmul(wmat[u], qkv[u][2], m_a) for u in units}
    den_intra = {u: _half_sum(wmat[u], m_a) for u in units}
    kv = {u: _outer_state(kw[u], qkv[u][2], bd_mask) for u in units}
    kn = {u: jnp.sum(kw[u], axis=0, keepdims=True) for u in units}
    c_in, n_in = {}, {}
    for p in range(n_pairs):
        c_bd, n_row = c_state[p], nm_state[p, 0:1, :]
        for c in range(n_chunks):
            c_in[p, c], n_in[p, c] = c_bd, n_row
            c_bd = dec[p, c] * c_bd + kv[p, c]
            n_row = dec[p, c] * n_row + kn[p, c]
        c_state[p] = c_bd
        nm_state[p, 0:1, :] = n_row
        nm_state[p, 1:2, :] = m_out[p, n_chunks - 1]
    num = {u: inter[u] * _dot(qkv[u][0], c_in[u]) + num_intra[u] for u in units}
    den = {u: inter[u] * _half_sum(qkv[u][0] * n_in[u], m_a) + den_intra[u] for u in units}
    h = {u: num[u] / jnp.maximum(jnp.abs(den[u]), jnp.exp(-m_t[u])) for u in units}
    ms = {u: _half_sum(h[u] * h[u], m_a) * (1.0 / HEAD_DIM) for u in units}
    for p, c in units:
        rs = slice(c * CHUNK, (c + 1) * CHUNK)
        cs = slice(p * LANES, (p + 1) * LANES)
        o_logit = x_ref[rs, 3 * MLSTM_W + p * LANES:3 * MLSTM_W + (p + 1) * LANES]
        y = _sigmoid(o_logit) * (h[p, c] * lax.rsqrt(ms[p, c] + EPS) * gain_ref[:, cs])
        y_ref[rs, cs] = y.astype(y_ref.dtype)


def _mixer_call(kernel, name, proj3, sec_block, sec_width, out_width, extra_inputs, extra_specs, scratch):
    b, t, _ = proj3.shape
    tb = min(LONG_BLOCK, t)
    return pl.pallas_call(
        kernel,
        grid=(b, t // tb),
        in_specs=[pl.BlockSpec((None, tb, sec_width), lambda i, j: (i, j, sec_block))] + extra_specs(tb),
        out_specs=pl.BlockSpec((None, tb, out_width), lambda i, j: (i, j, 0)),
        out_shape=jax.ShapeDtypeStruct((b, t, out_width), BF16),
        scratch_shapes=scratch(tb),
        compiler_params=_cparams(("parallel", "arbitrary")),
        name=name,
    )(proj3, *extra_inputs)


def _full2(a, b):
    return pl.BlockSpec((a, b), lambda i, j: (0, 0))


def _gdn_mixer(proj3, gates3, conv_w, prm, gain):
    b, t, _ = proj3.shape
    tb = min(TIME_BLOCK, t)
    n_blocks = t // tb
    n_units = (GDN_HEADS // 2) * (tb // CHUNK)
    cur = lambda i, j: (i, jnp.minimum(j, n_blocks - 1))
    prv = lambda i, j: (i, jnp.maximum(j - 1, 0))
    z_block = (SEC_M + 3 * GDN_W) // GDN_W
    assert z_block * GDN_W == SEC_M + 3 * GDN_W
    return pl.pallas_call(
        functools.partial(_gdn_kernel, n_blocks),
        grid=(b, n_blocks + 1),
        in_specs=[pl.BlockSpec((None, tb, SEC_G), lambda i, j: cur(i, j) + (1,)),
                  pl.BlockSpec((None, tb, LANES), lambda i, j: cur(i, j) + (0,)),
                  pl.BlockSpec((None, tb, GDN_W), lambda i, j: prv(i, j) + (z_block,)),
                  _full2(CONV_K, 3 * GDN_W), _full2(8, LANES), _full2(1, GDN_W)],
        out_specs=pl.BlockSpec((None, tb, GDN_W), lambda i, j: prv(i, j) + (0,)),
        out_shape=jax.ShapeDtypeStruct((b, t, GDN_W), BF16),
        scratch_shapes=[pltpu.VMEM((GDN_HEADS // 2, LANES, LANES), F32),
                        pltpu.VMEM((n_units, CHUNK, LANES), F32),
                        pltpu.VMEM((n_units, 2 * CHUNK, LANES), BF16),
                        pltpu.VMEM((n_units, CHUNK, LANES), BF16),
                        pltpu.VMEM((n_units, LANES, LANES), BF16),
                        pltpu.VMEM((n_units, 8, LANES), F32),
                        pltpu.VMEM((tb + 8, 3 * GDN_W), F32)],
        compiler_params=_cparams(("parallel", "arbitrary")),
        name="gdn_mixer",
    )(proj3, gates3, proj3, conv_w, prm, gain)


def _mlstm_mixer(proj3, gates3, conv_w, prm, gain):
    return _mixer_call(
        _mlstm_kernel, "mlstm_mixer", proj3, 0, SEC_M, MLSTM_W, (gates3, conv_w, prm, gain),
        lambda tb: [pl.BlockSpec((None, tb, LANES), lambda i, j: (i, j, 0)),
                    _full2(CONV_K, 2 * MLSTM_W), _full2(8, LANES), _full2(1, MLSTM_W)],
        lambda tb: [pltpu.VMEM((MLSTM_HEADS // 2, LANES, LANES), F32),
                    pltpu.VMEM((MLSTM_HEADS // 2, 8, LANES), F32),
                    pltpu.VMEM((tb + 8, 2 * MLSTM_W), F32)])


def _ret_mixer(proj3, cos_tab, sin_tab, gain):
    return _mixer_call(
        _ret_kernel, "retention_mixer", proj3, 3, SEC_R, RET_W, (cos_tab, sin_tab, gain),
        lambda tb: [pl.BlockSpec((tb, LANES), lambda i, j: (j, 0)),
                    pl.BlockSpec((tb, LANES), lambda i, j: (j, 0)),
                    _full2(1, RET_W)],
        lambda tb: [pltpu.VMEM((RET_HEADS // 2, LANES, LANES), F32)])


DISPATCH_TILE = 2048


def _dispatch_kernel(pos_ref, src_ref, zero_ref, out_ref, sem):
    del zero_ref
    rows = src_ref.shape[0]
    base = pl.program_id(0) * rows

    def issue(r, carry):
        pltpu.make_async_copy(src_ref.at[pl.ds(r, 1)], out_ref.at[pl.ds(pos_ref[base + r], 1)], sem).start()
        return carry

    lax.fori_loop(0, rows, issue, 0, unroll=8)
    pltpu.make_async_copy(src_ref, out_ref.at[pl.ds(0, rows)], sem).wait()


def _dispatch(src, pos, zeros):
    n, width = src.shape
    n_rows = zeros.shape[0]
    tile = min(DISPATCH_TILE, n)
    return pl.pallas_call(
        _dispatch_kernel,
        grid_spec=pltpu.PrefetchScalarGridSpec(
            num_scalar_prefetch=1,
            grid=(n // tile,),
            in_specs=[pl.BlockSpec((tile, width), lambda i, p: (i, 0)), pl.BlockSpec(memory_space=pl.ANY)],
            out_specs=pl.BlockSpec(memory_space=pl.ANY),
            scratch_shapes=[pltpu.SemaphoreType.DMA(())],
        ),
        out_shape=jax.ShapeDtypeStruct((n_rows, width), src.dtype),
        input_output_aliases={2: 0},
        compiler_params=_cparams(("arbitrary",)),
        name="moe_dispatch",
    )(pos, src, zeros)


def _ffn_kernel(plan_ref, used_ref, x_ref, wg_hbm, wu_hbm, wd_hbm, y_ref, wg_buf, wu_buf, wd_buf, sems):
    def group_copies(g, to_slot):
        lo = pl.multiple_of(g * EXPERTS_PER_GROUP, EXPERTS_PER_GROUP)
        experts = pl.ds(lo, EXPERTS_PER_GROUP)
        return [pltpu.make_async_copy(hbm.at[experts], buf.at[to_slot], sems.at[to_slot])
                for hbm, buf in ((wg_hbm, wg_buf), (wu_hbm, wu_buf), (wd_hbm, wd_buf))]

    def one_tile(i, rows):
        used = i < used_ref[0]
        group, slot, starts_group, next_group = plan_ref[0, i], plan_ref[1, i], plan_ref[2, i], plan_ref[3, i]

        @pl.when(used & (i == 0))
        def _():
            for cp in group_copies(group, slot):
                cp.start()

        @pl.when(used & (starts_group == 1) & (next_group >= 0))
        def _():
            for cp in group_copies(next_group, 1 - slot):
                cp.start()

        @pl.when(used & (starts_group == 1))
        def _():
            for cp in group_copies(group, slot):
                cp.wait()

        @pl.when(jnp.logical_not(used))
        def _():
            y_ref[rows, :] = jnp.zeros((MOE_TILE, D_MODEL), F32)

        @pl.when(used)
        def _():
            e1, e2 = plan_ref[4, i], plan_ref[5, i]
            x = x_ref[rows, 0:D_MODEL].astype(BF16)
            gate1 = jnp.dot(x, wg_buf[slot, e1], preferred_element_type=F32)
            gate2 = jnp.dot(x, wg_buf[slot, e2], preferred_element_type=F32)
            up1 = jnp.dot(x, wu_buf[slot, e1], preferred_element_type=F32)
            up2 = jnp.dot(x, wu_buf[slot, e2], preferred_element_type=F32)
            hid1 = (_silu(gate1) * up1).astype(BF16)
            hid2 = (_silu(gate2) * up2).astype(BF16)
            y1 = jnp.dot(hid1, wd_buf[slot, e1], preferred_element_type=F32)
            y2 = jnp.dot(hid2, wd_buf[slot, e2], preferred_element_type=F32)
            rw = x_ref[rows, D_MODEL:]
            y_ref[rows, :] = rw[:, 0:1] * y1 + rw[:, 1:2] * y2

    for sub in range(FFN_STEP_TILES):
        one_tile(pl.program_id(0) * FFN_STEP_TILES + sub, slice(sub * MOE_TILE, (sub + 1) * MOE_TILE))


def _expert_ffn(xs, tile_plan, n_used, wg, wu, wd):
    r = xs.shape[0]
    step_rows = FFN_STEP_TILES * MOE_TILE
    assert r % step_rows == 0
    last_used = lambda nu: (nu[0] - 1) // FFN_STEP_TILES
    return pl.pallas_call(
        _ffn_kernel,
        grid_spec=pltpu.PrefetchScalarGridSpec(
            num_scalar_prefetch=2,
            grid=(r // step_rows,),
            in_specs=[pl.BlockSpec((step_rows, HP_WIDTH), lambda i, plan, nu: (jnp.minimum(i, last_used(nu)), 0)),
                      pl.BlockSpec(memory_space=pl.ANY), pl.BlockSpec(memory_space=pl.ANY),
                      pl.BlockSpec(memory_space=pl.ANY)],
            out_specs=pl.BlockSpec((step_rows, D_MODEL), lambda i, plan, nu: (i, 0)),
            scratch_shapes=[pltpu.VMEM((2, EXPERTS_PER_GROUP, D_MODEL, D_EXPERT), BF16),
                            pltpu.VMEM((2, EXPERTS_PER_GROUP, D_MODEL, D_EXPERT), BF16),
                            pltpu.VMEM((2, EXPERTS_PER_GROUP, D_EXPERT, D_MODEL), BF16),
                            pltpu.SemaphoreType.DMA((2,))],
        ),
        out_shape=jax.ShapeDtypeStruct((r, D_MODEL), F32),
        compiler_params=_cparams(("arbitrary",)),
        name="expert_pair_ffn",
    )(tile_plan, n_used, xs, wg, wu, wd)


def _combine_kernel(final_norm, pos_ref, x_ref, gain_ref, ys_ref, o_ref, buf, sems):
    rows = o_ref.shape[0]
    i = pl.program_id(0)
    slot = i & 1

    def gather_tile(tile, to_slot):
        def issue(r, carry):
            pltpu.make_async_copy(ys_ref.at[pl.ds(pos_ref[tile * rows + r], 1)],
                                  buf.at[to_slot, pl.ds(r, 1)], sems.at[to_slot]).start()
            return carry

        lax.fori_loop(0, rows, issue, 0, unroll=8)

    @pl.when(i == 0)
    def _():
        gather_tile(0, 0)

    @pl.when(i + 1 < pl.num_programs(0))
    def _():
        gather_tile(i + 1, 1 - slot)

    pltpu.make_async_copy(ys_ref.at[pl.ds(0, rows)], buf.at[slot], sems.at[slot]).wait()
    out = x_ref[...] + buf[slot]
    if final_norm:
        ms = jnp.mean(out * out, axis=-1, keepdims=True)
        out = out * lax.rsqrt(ms + EPS) * gain_ref[...]
    o_ref[...] = out


def _combine(x2, ys, pos, gain, final_norm):
    n = x2.shape[0]
    return pl.pallas_call(
        functools.partial(_combine_kernel, final_norm),
        grid_spec=pltpu.PrefetchScalarGridSpec(
            num_scalar_prefetch=1,
            grid=(n // ROW_TILE,),
            in_specs=[pl.BlockSpec((ROW_TILE, D_MODEL), lambda i, p: (i, 0)),
                      pl.BlockSpec((1, D_MODEL), lambda i, p: (0, 0)),
                      pl.BlockSpec(memory_space=pl.ANY)],
            out_specs=pl.BlockSpec((ROW_TILE, D_MODEL), lambda i, p: (i, 0)),
            scratch_shapes=[pltpu.VMEM((2, ROW_TILE, D_MODEL), F32), pltpu.SemaphoreType.DMA((2,))],
        ),
        out_shape=jax.ShapeDtypeStruct((n, D_MODEL), F32),
        compiler_params=_cparams(("arbitrary",)),
        name="moe_combine",
    )(pos, x2, gain, ys)


def _plan(route, counts, n):
    cls, rank = route[:, 0], route[:, 1]
    counts = counts.astype(jnp.int32)
    tiles_per = (counts + MOE_TILE - 1) // MOE_TILE
    tile_end = jnp.cumsum(tiles_per)
    tile_start = tile_end - tiles_per
    class_ids = jnp.arange(N_CLASSES, dtype=jnp.int32)
    start_of = jnp.sum(jnp.where(cls[:, None] == class_ids[None, :], tile_start[None, :], 0), axis=1)
    pos = start_of * MOE_TILE + rank
    n_tiles = _sorted_rows(n) // MOE_TILE
    n_used = tile_end[-1]
    tile_ids = jnp.minimum(jnp.arange(n_tiles, dtype=jnp.int32), n_used - 1)
    tile_cls = jnp.sum((tile_end[None, :] <= tile_ids[:, None]).astype(jnp.int32), axis=1)
    per_group = EXPERTS_PER_GROUP * EXPERTS_PER_GROUP
    tile_group = tile_cls // per_group
    starts_group = jnp.concatenate([jnp.ones((1,), jnp.int32),
                                    (tile_group[1:] != tile_group[:-1]).astype(jnp.int32)])
    slot = (jnp.cumsum(starts_group) - 1) % 2
    present = jnp.sum(tiles_per.reshape(N_GROUPS, per_group), axis=1) > 0
    nxt = [jnp.int32(-1)] * N_GROUPS
    for g in range(N_GROUPS - 2, -1, -1):
        nxt[g] = jnp.where(present[g + 1], g + 1, nxt[g + 1])
    next_group = jnp.stack(nxt)[tile_group]
    tile_plan = jnp.stack([tile_group, slot, starts_group, next_group,
                           (tile_cls // EXPERTS_PER_GROUP) % EXPERTS_PER_GROUP, tile_cls % EXPERTS_PER_GROUP])
    return pos, tile_plan.astype(jnp.int32), n_used.reshape(1)


def _rope_tables(t):
    inv_freq = ROPE_BASE ** (-jnp.arange(0, HEAD_DIM, 2, dtype=F32) / HEAD_DIM)
    ang = jnp.arange(t, dtype=F32)[:, None] * inv_freq[None, :]
    cos, sin = jnp.cos(ang), jnp.sin(ang)
    cos_tab = jnp.tile(cos, (1, 2 * LANES // HEAD_DIM))
    sin_tab = jnp.tile(jnp.concatenate([-sin, sin], axis=-1), (1, LANES // HEAD_DIM))
    return cos_tab, sin_tab


def _split_w_in(w):
    g0, g1 = 0, SEC_G
    r0, r1 = g1 + 2 * GDN_HEADS, g1 + 2 * GDN_HEADS + SEC_R
    m0, m1 = r1, r1 + SEC_M
    assert m1 + 2 * MLSTM_HEADS == w.shape[1]
    main = jnp.concatenate([w[:, m0:m1], w[:, g0:g1], w[:, r0:r1]], axis=1).astype(BF16)
    gate = jnp.concatenate([w[:, g1:r0], w[:, m1:]], axis=1)
    gate = jnp.pad(gate, ((0, 0), (0, LANES - gate.shape[1]))).astype(BF16)
    return main, gate


def _lane_row(values, start):
    return jnp.zeros((LANES,), F32).at[start:start + values.shape[0]].set(values)


def kernel(x, norm_mix, w_in, gdn_conv, gdn_a_log, gdn_dt_bias, gdn_norm, ret_norm, mlstm_conv, mlstm_i_bias, mlstm_f_bias, mlstm_norm, w_out, norm_ffn, router_group, router_expert, router_bias, expert_gate, expert_up, expert_down, norm_final):
    b, t, d = x.shape
    n = b * t
    depth = w_in.shape[0]
    cos_tab, sin_tab = _rope_tables(t)
    x2 = x.reshape(n, d)
    for l in range(depth):
        w_main, w_gate = _split_w_in(w_in[l])
        proj, gates, (wg, wu, wd) = _inproj(x2, norm_mix[l][None, :], w_main, w_gate,
                                            (expert_gate, expert_up, expert_down), l)
        proj3 = proj.reshape(b, t, D_MAIN)
        gates3 = gates.reshape(b, t, LANES)
        gdn_prm = jnp.zeros((8, LANES), F32).at[0].set(_lane_row(gdn_a_log[l], G_A)).at[1].set(
            _lane_row(gdn_dt_bias[l], G_A))
        mlstm_prm = jnp.zeros((8, LANES), F32).at[0].set(_lane_row(mlstm_i_bias[l], M_I)).at[1].set(
            _lane_row(mlstm_f_bias[l], M_F))
        y_m = _mlstm_mixer(proj3, gates3, mlstm_conv[l], mlstm_prm, mlstm_norm[l][None, :])
        y_g = _gdn_mixer(proj3, gates3, gdn_conv[l], gdn_prm, gdn_norm[l][None, :])
        y_r = _ret_mixer(proj3, cos_tab, sin_tab, ret_norm[l][None, :])
        wo = w_out[l].astype(BF16)
        w_router = jnp.pad(jnp.concatenate([router_group[l], router_expert[l]], axis=1),
                           ((0, 0), (0, LANES - N_GROUPS - N_EXPERTS))).astype(BF16)
        rb = _lane_row(router_bias[l], N_GROUPS)[None, :]
        x2, h_packed, route, counts, zeros = _outproj(
            x2, y_m.reshape(n, MLSTM_W), y_g.reshape(n, GDN_W), y_r.reshape(n, RET_W),
            wo[GDN_W + RET_W:], wo[:GDN_W], wo[GDN_W:GDN_W + RET_W],
            norm_ffn[l][None, :], w_router, rb)
        pos, tile_plan, n_used = _plan(route, counts[0], n)
        xs = _dispatch(h_packed, pos, zeros)
        ys = _expert_ffn(xs, tile_plan, n_used, wg, wu, wd)
        x2 = _combine(x2, ys, pos, norm_final[None, :], l == depth - 1)
    return x2.reshape(b, t, d)
```

```python
import functools
import math

import jax
import jax.numpy as jnp
from jax import lax
from jax.experimental import pallas as pl
from jax.experimental.pallas import tpu as pltpu

F32 = jnp.float32
BF16 = jnp.bfloat16

D_MODEL = 1024
HEAD_DIM = 64
CHUNK = 64
GDN_HEADS, RET_HEADS, MLSTM_HEADS = 6, 4, 6
GDN_W, RET_W, MLSTM_W = GDN_HEADS * HEAD_DIM, RET_HEADS * HEAD_DIM, MLSTM_HEADS * HEAD_DIM
CONV_K = 4
ROPE_BASE = 10000.0
N_GROUPS, EXPERTS_PER_GROUP = 4, 8
N_EXPERTS = N_GROUPS * EXPERTS_PER_GROUP
D_EXPERT = D_MODEL // 4
EPS = 1e-6
LANES = 128
SEC_M, SEC_G, SEC_R = 4 * MLSTM_W, 4 * GDN_W, 4 * RET_W
D_MAIN = SEC_M + SEC_G + SEC_R
G_BETA, G_A, M_I, M_F = 0, GDN_HEADS, 2 * GDN_HEADS, 2 * GDN_HEADS + MLSTM_HEADS

ROW_TILE = 512
OUT_TILE = 512
OUT_SUB = 128
TIME_BLOCK = 256
LONG_BLOCK = 512
MOE_TILE = 128
FFN_STEP_TILES = 8
VMEM_LIMIT = 56 * 1024 * 1024


def _cparams(sem):
    return pltpu.CompilerParams(dimension_semantics=sem, vmem_limit_bytes=VMEM_LIMIT)


def _silu(x):
    return x * (1.0 / (1.0 + jnp.exp(-x)))


def _sigmoid(x):
    return 1.0 / (1.0 + jnp.exp(-x))


def _softplus(x):
    return jnp.maximum(x, 0.0) + jnp.log(1.0 + jnp.exp(-jnp.abs(x)))


def _dot(a, b):
    return jnp.dot(a.astype(BF16), b.astype(BF16), preferred_element_type=F32)


def _dot_nt(a, b):
    return lax.dot_general(a.astype(BF16), b.astype(BF16), (((1,), (1,)), ((), ())),
                           preferred_element_type=F32)


def _lane_masks(rows):
    lane = lax.broadcasted_iota(jnp.int32, (rows, LANES), 1)
    row = lax.broadcasted_iota(jnp.int32, (rows, LANES), 0)
    return lane, row


def _half_sum(x, m_a):
    s_a = jnp.sum(jnp.where(m_a, x, 0.0), axis=-1, keepdims=True)
    s_b = jnp.sum(jnp.where(m_a, 0.0, x), axis=-1, keepdims=True)
    return jnp.where(m_a, s_a, s_b)


def _half_max(x, m_a):
    s_a = jnp.max(jnp.where(m_a, x, -jnp.inf), axis=-1, keepdims=True)
    s_b = jnp.max(jnp.where(m_a, -jnp.inf, x), axis=-1, keepdims=True)
    return jnp.where(m_a, s_a, s_b)


def _col_form(g, lane_a, m_a):
    rows = g.shape[0]
    ca = jnp.broadcast_to(g[:, lane_a:lane_a + 1], (rows, LANES))
    cb = jnp.broadcast_to(g[:, lane_a + 1:lane_a + 2], (rows, LANES))
    return jnp.where(m_a, ca, cb)


def _row_form(col, eye):
    return jnp.sum(jnp.where(eye, col, 0.0), axis=0, keepdims=True)


def _block_diag(y, m_a):
    return jnp.concatenate([jnp.where(m_a, y, 0.0), jnp.where(m_a, 0.0, y)], axis=0)


def _pmul(x, y, m_a):
    return _dot(x, _block_diag(y, m_a))


def _outer_state(k, v, bd_mask):
    zero = jnp.zeros_like(k)
    kt = jnp.concatenate([k, zero], axis=0).T
    vp = jnp.concatenate([v, zero], axis=0)
    return jnp.where(bd_mask, _dot(kt, vp), 0.0)


def _chunk_cumsum(x):
    rows = x.shape[0]
    r = lax.broadcasted_iota(jnp.int32, (rows, LANES), 0) % CHUNK
    s = 1
    while s < CHUNK:
        x = x + jnp.where(r >= s, pltpu.roll(x, s, axis=0), 0.0)
        s *= 2
    return x


def _causal_conv(raw, cbuf, w_ref):
    rows = raw.shape[0]
    cbuf[8:8 + rows, :] = raw
    acc = raw * w_ref[CONV_K - 1:CONV_K, :]
    for j in range(CONV_K - 1):
        off = 8 - (CONV_K - 1) + j
        acc = acc + cbuf[off:off + rows, :] * w_ref[j:j + 1, :]
    cbuf[0:8, :] = raw[rows - 8:rows, :]
    return acc


def _inproj_kernel(x_ref, gain_ref, w_ref, wg_ref, *rest):
    cast_refs, (o_ref, og_ref) = rest[:3] + rest[5:], rest[3:5]
    x = x_ref[...]
    ms = jnp.mean(x * x, axis=-1, keepdims=True)
    h = (x * lax.rsqrt(ms + EPS) * gain_ref[...]).astype(BF16)

    def proj(lo, hi):
        return jnp.dot(h, w_ref[:, lo:hi], preferred_element_type=F32)

    step = 512
    for c in range(D_MAIN // step):
        o_ref[:, c * step:(c + 1) * step] = proj(c * step, (c + 1) * step)
    og_ref[...] = jnp.dot(h, wg_ref[...], preferred_element_type=F32)
    for src, dst in zip(cast_refs[:3], cast_refs[3:]):
        dst[...] = src[...].astype(BF16)


def _inproj(x2, gain, w_main, w_gate, expert_weights, layer):
    n = x2.shape[0]
    steps = n // OUT_TILE
    full = lambda a, b: pl.BlockSpec((a, b), lambda i: (0, 0))
    flat = [w.reshape(-1, w.shape[-1]) for w in expert_weights]
    rows = [w.shape[0] // expert_weights[0].shape[0] for w in flat]
    assert all(r % (16 * steps) == 0 for r in rows)
    slab_in = [pl.BlockSpec((r // steps, w.shape[1]), lambda i: (layer * steps + i, 0)) for r, w in zip(rows, flat)]
    slab_out = [pl.BlockSpec((r // steps, w.shape[1]), lambda i: (i, 0)) for r, w in zip(rows, flat)]
    outs = pl.pallas_call(
        _inproj_kernel,
        grid=(steps,),
        in_specs=[
            pl.BlockSpec((OUT_TILE, D_MODEL), lambda i: (i, 0)),
            full(1, D_MODEL), full(D_MODEL, D_MAIN), full(D_MODEL, LANES),
        ] + slab_in,
        out_specs=[
            pl.BlockSpec((OUT_TILE, D_MAIN), lambda i: (i, 0)),
            pl.BlockSpec((OUT_TILE, LANES), lambda i: (i, 0)),
        ] + slab_out,
        out_shape=[jax.ShapeDtypeStruct((n, D_MAIN), F32), jax.ShapeDtypeStruct((n, LANES), F32)] + [
            jax.ShapeDtypeStruct((r, w.shape[1]), BF16) for r, w in zip(rows, flat)],
        compiler_params=_cparams(("parallel",)),
        name="norm_inproj",
    )(x2, gain, w_main, w_gate, *flat)
    return outs[0], outs[1], [o.reshape(w.shape[1:]) for o, w in zip(outs[2:], expert_weights)]


N_CLASSES = N_GROUPS * EXPERTS_PER_GROUP * EXPERTS_PER_GROUP
N_PAIRS = N_GROUPS * (EXPERTS_PER_GROUP * (EXPERTS_PER_GROUP - 1)) // 2


def _sorted_rows(n):
    return (n // MOE_TILE + N_PAIRS) * MOE_TILE


HP_WIDTH = D_MODEL + LANES


def _route_rows(lg):
    rows = lg.shape[0]
    lane = lax.broadcasted_iota(jnp.int32, (rows, LANES), 1)
    lane_f = lane.astype(F32)
    neg = -jnp.inf

    def first_lane(mask):
        return jnp.min(jnp.where(mask, lane_f, float(LANES)), axis=-1, keepdims=True).astype(jnp.int32)

    gl = jnp.where(lane < N_GROUPS, lg, neg)
    gm = jnp.max(gl, axis=-1, keepdims=True)
    gsum = jnp.sum(jnp.where(lane < N_GROUPS, jnp.exp(gl - gm), 0.0), axis=-1, keepdims=True)
    gidx = first_lane(gl == gm)
    group_p = 1.0 / gsum
    in_group = (lane >= N_GROUPS) & (lane < N_GROUPS + N_EXPERTS) & (
        lax.shift_right_arithmetic(lane - N_GROUPS, 3) == gidx)
    el = jnp.where(in_group, lg, neg)
    m1 = jnp.max(el, axis=-1, keepdims=True)
    i1 = first_lane(el == m1)
    el2 = jnp.where(lane == i1, neg, el)
    m2 = jnp.max(el2, axis=-1, keepdims=True)
    i2 = first_lane(el2 == m2)
    e2 = jnp.exp(m2 - m1)
    w1 = group_p * (1.0 / (1.0 + e2))
    w2 = group_p * (e2 / (1.0 + e2))
    first_lower = i1 < i2
    base = N_GROUPS + gidx * EXPERTS_PER_GROUP
    lo = jnp.minimum(i1, i2) - base
    hi = jnp.maximum(i1, i2) - base
    cls = (gidx * EXPERTS_PER_GROUP + lo) * EXPERTS_PER_GROUP + hi
    return cls, jnp.where(first_lower, w1, w2), jnp.where(first_lower, w2, w1)


def _outproj_kernel(x_ref, ym_ref, yg_ref, yr_ref, wm_ref, wg_ref, wr_ref, gain_ref, wrt_ref, rb_ref,
                    xo_ref, hp_ref, rt_ref, cnt_ref, running):
    @pl.when(pl.program_id(0) == 0)
    def _():
        running[...] = jnp.zeros_like(running)

    rows = OUT_SUB
    subs = [slice(s * rows, (s + 1) * rows) for s in range(x_ref.shape[0] // rows)]
    acc = [x_ref[r, :] + jnp.dot(yg_ref[r, :], wg_ref[...], preferred_element_type=F32) for r in subs]
    acc = [a + jnp.dot(yr_ref[r, :], wr_ref[...], preferred_element_type=F32) for a, r in zip(acc, subs)]
    acc = [a + jnp.dot(ym_ref[r, :], wm_ref[...], preferred_element_type=F32) for a, r in zip(acc, subs)]
    for a, r in zip(acc, subs):
        xo_ref[r, :] = a
    hn = [a * lax.rsqrt(jnp.mean(a * a, axis=-1, keepdims=True) + EPS) * gain_ref[...] for a in acc]
    for h, r in zip(hn, subs):
        hp_ref[r, 0:D_MODEL] = h
    logits = [jnp.dot(h.astype(BF16), wrt_ref[...], preferred_element_type=F32) + rb_ref[...] for h in hn]
    routed = [_route_rows(lg) for lg in logits]

    cls_lane = lax.broadcasted_iota(jnp.int32, (rows, N_CLASSES), 1)
    before = (lax.broadcasted_iota(jnp.int32, (rows, rows), 1)
              < lax.broadcasted_iota(jnp.int32, (rows, rows), 0)).astype(BF16)
    onehot = [cls_lane == cls for cls, _, _ in routed]
    prefix = [jnp.dot(before, oh.astype(BF16), preferred_element_type=F32) for oh in onehot]
    counts = [jnp.sum(oh.astype(F32), axis=0, keepdims=True) for oh in onehot]
    lane = lax.broadcasted_iota(jnp.int32, (rows, LANES), 1)
    seen = running[0:1, :]
    for r, (cls, w_lo, w_hi), oh, pre, cnt in zip(subs, routed, onehot, prefix, counts):
        rank = jnp.sum(jnp.where(oh, pre + seen, 0.0), axis=-1, keepdims=True)
        seen = seen + cnt
        rt_ref[r, :] = jnp.where(lane == 0, cls, jnp.where(lane == 1, rank.astype(jnp.int32), 0))
        hp_ref[r, D_MODEL:] = jnp.where(lane == 0, w_lo, jnp.where(lane == 1, w_hi, 0.0))
    running[0:1, :] = seen
    cnt_ref[...] = jnp.broadcast_to(seen, cnt_ref.shape)


def _outproj(x2, ym, yg, yr, wo_m, wo_g, wo_r, gain, w_router, router_bias):
    n = x2.shape[0]
    row = lambda w: pl.BlockSpec((OUT_TILE, w), lambda i: (i, 0))
    full = lambda a, b: pl.BlockSpec((a, b), lambda i: (0, 0))
    return pl.pallas_call(
        _outproj_kernel,
        grid=(n // OUT_TILE,),
        in_specs=[row(D_MODEL), row(MLSTM_W), row(GDN_W), row(RET_W),
                  full(MLSTM_W, D_MODEL), full(GDN_W, D_MODEL), full(RET_W, D_MODEL),
                  full(1, D_MODEL), full(D_MODEL, LANES), full(1, LANES)],
        out_specs=[row(D_MODEL), row(HP_WIDTH), row(LANES), full(8, N_CLASSES)],
        out_shape=[jax.ShapeDtypeStruct((n, D_MODEL), F32),
                   jax.ShapeDtypeStruct((n, HP_WIDTH), F32),
                   jax.ShapeDtypeStruct((n, LANES), jnp.int32),
                   jax.ShapeDtypeStruct((8, N_CLASSES), F32)],
        scratch_shapes=[pltpu.VMEM((8, N_CLASSES), F32)],
        compiler_params=_cparams(("arbitrary",)),
        name="outproj_norm_router",
    )(x2, ym, yg, yr, wo_m, wo_g, wo_r, gain, w_router, router_bias)


def _gdn_kernel(n_blocks, *refs):
    state, u_s, wq_s, qk_s, kt_s, sd_s, cbuf = refs[-7:]
    j = pl.program_id(1)

    @pl.when(j == 0)
    def _():
        for ref in (state, u_s, wq_s, qk_s, kt_s, sd_s):
            ref[...] = jnp.zeros_like(ref)
        cbuf[0:8, :] = jnp.zeros((8, cbuf.shape[1]), F32)

    @pl.when(j < n_blocks)
    def _():
        _gdn_step(True, *refs)

    @pl.when(j == n_blocks)
    def _():
        _gdn_step(False, *refs)


def _gdn_step(prepare, x_ref, gt_ref, z_ref, conv_ref, prm_ref, gain_ref, y_ref,
              state, u_s, wq_s, qk_s, kt_s, sd_s, cbuf):
    tb = x_ref.shape[0]
    gt = gt_ref[...]
    beta_all = _sigmoid(gt)
    g_all = -jnp.exp(prm_ref[0:1, :]) * _softplus(gt + prm_ref[1:2, :])
    gc_all = _chunk_cumsum(g_all)

    lane, row = _lane_masks(CHUNK)
    m_a = lane < HEAD_DIM
    lane_h = lane % HEAD_DIM
    eye = lane_h == row
    tril = lane_h <= row
    strict = lane_h < row
    eye_f = eye.astype(F32)
    lane2 = lax.broadcasted_iota(jnp.int32, (LANES, LANES), 1)
    row2 = lax.broadcasted_iota(jnp.int32, (LANES, LANES), 0)
    bd_mask = (lane2 < HEAD_DIM) == (row2 < HEAD_DIM)
    lane_t = lax.broadcasted_iota(jnp.int32, (tb, LANES), 1)
    m_a_t = lane_t < HEAD_DIM

    n_pairs = GDN_HEADS // 2
    n_chunks = tb // CHUNK
    units = [(p, c) for p in range(n_pairs) for c in range(n_chunks)]

    def unit_index(p, c):
        return p * n_chunks + c

    prev = {}
    for p, c in units:
        i = unit_index(p, c)
        prev[p, c] = (u_s[i], wq_s[i], qk_s[i], kt_s[i], sd_s[i, 0:1, :])
    s_bd = [state[p] for p in range(n_pairs)]

    def recurrence(c):
        rs = slice(c * CHUNK, (c + 1) * CHUNK)
        for p in range(n_pairs):
            cs = slice(p * LANES, (p + 1) * LANES)
            u_mat, wq, qk, k_dec_t, s_decay = prev[p, c]
            ws_qs = jnp.dot(wq, s_bd[p].astype(BF16), preferred_element_type=F32)
            v_new = u_mat - ws_qs[:CHUNK]
            o = ws_qs[CHUNK:] + jnp.dot(qk, _block_diag(v_new, m_a).astype(BF16), preferred_element_type=F32)
            v_pad = jnp.concatenate([v_new, jnp.zeros_like(v_new)], axis=0).astype(BF16)
            s_bd[p] = s_bd[p] * s_decay + jnp.where(
                bd_mask, jnp.dot(k_dec_t, v_pad, preferred_element_type=F32), 0.0)
            ms = _half_sum(o * o, m_a) * (1.0 / HEAD_DIM)
            y = o * lax.rsqrt(ms + EPS) * gain_ref[:, cs] * _silu(z_ref[rs, cs])
            y_ref[rs, cs] = y.astype(y_ref.dtype)

    if not prepare:
        for c in range(n_chunks):
            recurrence(c)
        for p in range(n_pairs):
            state[p] = s_bd[p]
        return

    recurrence(0)
    qkv = _silu(_causal_conv(x_ref[:, 0:3 * GDN_W], cbuf, conv_ref))
    ins = {}
    for p in range(n_pairs):
        q_t = qkv[:, p * LANES:(p + 1) * LANES]
        k_t = qkv[:, GDN_W + p * LANES:GDN_W + (p + 1) * LANES]
        v_t = qkv[:, 2 * GDN_W + p * LANES:2 * GDN_W + (p + 1) * LANES]
        q_t = q_t * lax.rsqrt(_half_sum(q_t * q_t, m_a_t) + EPS) * (HEAD_DIM ** -0.5)
        k_t = k_t * lax.rsqrt(_half_sum(k_t * k_t, m_a_t) + EPS)
        beta_t = _col_form(beta_all, G_BETA + 2 * p, m_a_t)
        gc_t = _col_form(gc_all, G_A + 2 * p, m_a_t)
        for c in range(n_chunks):
            rs = slice(c * CHUNK, (c + 1) * CHUNK)
            ins[p, c] = (q_t[rs], k_t[rs], v_t[rs], beta_t[rs], gc_t[rs])
    kkqk = {u: _dot_nt(jnp.concatenate([ins[u][1] * ins[u][3], ins[u][0]], axis=0), _block_diag(ins[u][1], m_a))
            for u in units}
    pre = {}
    for u in units:
        q, k, v, beta, gc = ins[u]
        decay = jnp.where(tril, jnp.exp(gc - _row_form(gc, eye)), 0.0)
        egc = jnp.exp(gc)
        g_last = gc[CHUNK - 1:CHUNK, :]
        k_dec = k * jnp.exp(g_last - gc)
        pre[u] = dict(
            a=jnp.where(strict, kkqk[u][:CHUNK] * decay, 0.0), qk=kkqk[u][CHUNK:] * decay,
            vb=v * beta, kbe=k * beta * egc, qe=q * egc, s_decay=jnp.exp(g_last),
            k_dec_t=jnp.concatenate([k_dec, jnp.zeros_like(k_dec)], axis=0).T)

    pw = {u: _pmul(pre[u]["a"], pre[u]["a"], m_a) for u in units}
    t_inv = {u: eye_f - pre[u]["a"] for u in units}
    later_chunks = list(range(1, n_chunks))
    for step in range(4):
        if step % 2 == 0 and later_chunks:
            recurrence(later_chunks.pop(0))
        both = {u: _dot(jnp.concatenate([pw[u], t_inv[u]], axis=0), _block_diag(pw[u], m_a)) for u in units}
        pw = {u: both[u][:CHUNK] for u in units}
        t_inv = {u: t_inv[u] + both[u][CHUNK:] for u in units}
    for c in later_chunks:
        recurrence(c)
    t_inv = {u: t_inv[u] + _pmul(t_inv[u], pw[u], m_a) for u in units}
    for p, c in units:
        d = pre[p, c]
        uw = _dot(t_inv[p, c], jnp.concatenate([_block_diag(d["vb"], m_a), _block_diag(d["kbe"], m_a)], axis=1))
        i = unit_index(p, c)
        u_s[i] = uw[:, :LANES]
        wq_s[i] = jnp.concatenate([uw[:, LANES:], d["qe"]], axis=0).astype(BF16)
        qk_s[i] = d["qk"].astype(BF16)
        kt_s[i] = d["k_dec_t"].astype(BF16)
        sd_s[i] = jnp.broadcast_to(d["s_decay"], (8, LANES))
    for p in range(n_pairs):
        state[p] = s_bd[p]


def _ret_kernel(x_ref, cos_ref, sin_ref, gain_ref, y_ref, state):
    tb = x_ref.shape[0]

    @pl.when(pl.program_id(1) == 0)
    def _():
        state[...] = jnp.zeros_like(state)

    lane, row = _lane_masks(CHUNK)
    m_a = lane < HEAD_DIM
    lane_h = lane % HEAD_DIM
    tril = lane_h <= row
    lane2 = lax.broadcasted_iota(jnp.int32, (LANES, LANES), 1)
    row2 = lax.broadcasted_iota(jnp.int32, (LANES, LANES), 0)
    bd_mask = (lane2 < HEAD_DIM) == (row2 < HEAD_DIM)
    lane_t = lax.broadcasted_iota(jnp.int32, (tb, LANES), 1)
    first_half = (lane_t % HEAD_DIM) < (HEAD_DIM // 2)
    cos = cos_ref[...]
    sin = sin_ref[...]
    rowf = row.astype(F32)
    pos_diff = (row - lane_h).astype(F32)

    def rope(x):
        swapped = jnp.where(first_half, pltpu.roll(x, LANES - HEAD_DIM // 2, axis=1),
                            pltpu.roll(x, HEAD_DIM // 2, axis=1))
        return x * cos + swapped * sin

    n_pairs = RET_HEADS // 2
    n_chunks = tb // CHUNK
    units = [(p, c) for p in range(n_pairs) for c in range(n_chunks)]
    chunk_decay, decay, q_scale, k_scale, qkv = [], [], [], [], {}
    for p in range(n_pairs):
        lg_a = math.log(1.0 - 2.0 ** (-5.0 - 2 * p))
        lg_b = math.log(1.0 - 2.0 ** (-5.0 - (2 * p + 1)))
        lg = jnp.where(m_a, lg_a, lg_b)
        decay.append(jnp.where(tril, jnp.exp(pos_diff * lg), 0.0))
        q_scale.append(jnp.exp(lg * (rowf + 1.0)))
        k_scale.append(jnp.exp(lg * (CHUNK - 1.0 - rowf)))
        chunk_decay.append(jnp.exp(lg[0:1, :] * CHUNK))
        q_t = rope(x_ref[:, p * LANES:(p + 1) * LANES])
        k_t = rope(x_ref[:, RET_W + p * LANES:RET_W + (p + 1) * LANES]) * (HEAD_DIM ** -0.5)
        for c in range(n_chunks):
            rs = slice(c * CHUNK, (c + 1) * CHUNK)
            qkv[p, c] = (q_t[rs], k_t[rs], x_ref[rs, 2 * RET_W + p * LANES:2 * RET_W + (p + 1) * LANES])
    scores = {(p, c): _dot_nt(qkv[p, c][0], _block_diag(qkv[p, c][1], m_a)) * decay[p] for p, c in units}
    o_intra = {u: _pmul(scores[u], qkv[u][2], m_a) for u in units}
    kv_all = {(p, c): _outer_state(qkv[p, c][1] * k_scale[p], qkv[p, c][2], bd_mask) for p, c in units}
    pre = {(p, c): (o_intra[p, c], qkv[p, c][0] * q_scale[p], kv_all[p, c]) for p, c in units}

    s_in = {}
    for p in range(n_pairs):
        s_bd = state[p]
        for c in range(n_chunks):
            s_in[p, c] = s_bd
            s_bd = chunk_decay[p] * s_bd + pre[p, c][2]
        state[p] = s_bd
    o_all = {u: pre[u][0] + _dot(pre[u][1], s_in[u]) for u in units}
    ms_all = {u: _half_sum(o_all[u] * o_all[u], m_a) * (1.0 / HEAD_DIM) for u in units}
    for p, c in units:
        rs = slice(c * CHUNK, (c + 1) * CHUNK)
        cs = slice(p * LANES, (p + 1) * LANES)
        gate = x_ref[rs, 3 * RET_W + p * LANES:3 * RET_W + (p + 1) * LANES]
        y = o_all[p, c] * lax.rsqrt(ms_all[p, c] + EPS) * gain_ref[:, cs] * _silu(gate)
        y_ref[rs, cs] = y.astype(y_ref.dtype)


def _mlstm_kernel(x_ref, gt_ref, conv_ref, prm_ref, gain_ref, y_ref, zero_ref, c_state, nm_state, cbuf):
    tb = x_ref.shape[0]
    first = pl.program_id(1) == 0

    @pl.when(first)
    def _():
        c_state[...] = jnp.zeros_like(c_state)
        nm_state[...] = jnp.zeros_like(nm_state)
        cbuf[0:8, :] = jnp.zeros((8, cbuf.shape[1]), F32)

    zero_ref[...] = jnp.zeros_like(zero_ref)

    qk_all = _silu(_causal_conv(x_ref[:, 0:2 * MLSTM_W], cbuf, conv_ref))
    gt = gt_ref[...]
    log_i_all = gt + prm_ref[0:1, :]
    f_pre = gt + prm_ref[1:2, :]
    log_f_all = jnp.minimum(f_pre, 0.0) - jnp.log(1.0 + jnp.exp(-jnp.abs(f_pre)))
    bc_all = _chunk_cumsum(log_f_all)

    lane, row = _lane_masks(CHUNK)
    m_a = lane < HEAD_DIM
    lane_h = lane % HEAD_DIM
    eye = lane_h == row
    tril = lane_h <= row
    lane2 = lax.broadcasted_iota(jnp.int32, (LANES, LANES), 1)
    row2 = lax.broadcasted_iota(jnp.int32, (LANES, LANES), 0)
    bd_mask = (lane2 < HEAD_DIM) == (row2 < HEAD_DIM)
    lane_t = lax.broadcasted_iota(jnp.int32, (tb, LANES), 1)
    m_a_t = lane_t < HEAD_DIM

    n_pairs = MLSTM_HEADS // 2
    n_chunks = tb // CHUNK
    units = [(p, c) for p in range(n_pairs) for c in range(n_chunks)]
    gates = {}
    for p in range(n_pairs):
        b_t = _col_form(bc_all, M_F + 2 * p, m_a_t)
        i_t = _col_form(log_i_all, M_I + 2 * p, m_a_t)
        for c in range(n_chunks):
            rs = slice(c * CHUNK, (c + 1) * CHUNK)
            gates[p, c] = (b_t[rs], i_t[rs])
    qkv = {(p, c): (qk_all[c * CHUNK:(c + 1) * CHUNK, p * LANES:(p + 1) * LANES] * (HEAD_DIM ** -0.5),
                    qk_all[c * CHUNK:(c + 1) * CHUNK, MLSTM_W + p * LANES:MLSTM_W + (p + 1) * LANES],
                    x_ref[c * CHUNK:(c + 1) * CHUNK, 2 * MLSTM_W + p * LANES:2 * MLSTM_W + (p + 1) * LANES])
           for p, c in units}
    qk = {u: _dot_nt(qkv[u][0], _block_diag(qkv[u][1], m_a)) for u in units}
    log_d = {u: jnp.where(tril, gates[u][0] - _row_form(gates[u][0], eye) + _row_form(gates[u][1], eye), -jnp.inf)
             for u in units}
    m_intra = {u: _half_max(log_d[u], m_a) for u in units}
    a_c = {u: gates[u][0][CHUNK - 1:CHUNK, :] - gates[u][0] + gates[u][1] for u in units}
    a_max = {u: jnp.max(a_c[u], axis=0, keepdims=True) for u in units}
    m_in, m_out = {}, {}
    for p in range(n_pairs):
        m_row = nm_state[p, 1:2, :]
        for c in range(n_chunks):
            m_in[p, c] = m_row
            m_row = jnp.maximum(gates[p, c][0][CHUNK - 1:CHUNK, :] + m_row, a_max[p, c])
            m_out[p, c] = m_row
    m_t = {u: jnp.maximum(gates[u][0] + m_in[u], m_intra[u]) for u in units}
    inter = {u: jnp.exp(gates[u][0] + m_in[u] - m_t[u]) for u in units}
    wmat = {u: jnp.where(tril, jnp.exp(log_d[u] - m_t[u]), 0.0) * qk[u] for u in units}
    kw = {u: qkv[u][1] * jnp.exp(a_c[u] - m_out[u]) for u in units}
    dec = {u: jnp.exp(gates[u][0][CHUNK - 1:CHUNK, :] + m_in[u] - m_out[u]) for u in units}
    num_intra = {u: _pmul(wmat[u], qkv[u][2], m_a) for u in units}
    den_intra = {u: _half_sum(wmat[u], m_a) for u in units}
    kv = {u: _outer_state(kw[u], qkv[u][2], bd_mask) for u in units}
    kn = {u: jnp.sum(kw[u], axis=0, keepdims=True) for u in units}
    c_in, n_in = {}, {}
    for p in range(n_pairs):
        c_bd, n_row = c_state[p], nm_state[p, 0:1, :]
        for c in range(n_chunks):
            c_in[p, c], n_in[p, c] = c_bd, n_row
            c_bd = dec[p, c] * c_bd + kv[p, c]
            n_row = dec[p, c] * n_row + kn[p, c]
        c_state[p] = c_bd
        nm_state[p, 0:1, :] = n_row
        nm_state[p, 1:2, :] = m_out[p, n_chunks - 1]
    num = {u: inter[u] * _dot(qkv[u][0], c_in[u]) + num_intra[u] for u in units}
    den = {u: inter[u] * _half_sum(qkv[u][0] * n_in[u], m_a) + den_intra[u] for u in units}
    h = {u: num[u] / jnp.maximum(jnp.abs(den[u]), jnp.exp(-m_t[u])) for u in units}
    ms = {u: _half_sum(h[u] * h[u], m_a) * (1.0 / HEAD_DIM) for u in units}
    for p, c in units:
        rs = slice(c * CHUNK, (c + 1) * CHUNK)
        cs = slice(p * LANES, (p + 1) * LANES)
        o_logit = x_ref[rs, 3 * MLSTM_W + p * LANES:3 * MLSTM_W + (p + 1) * LANES]
        y = _sigmoid(o_logit) * (h[p, c] * lax.rsqrt(ms[p, c] + EPS) * gain_ref[:, cs])
        y_ref[rs, cs] = y.astype(y_ref.dtype)


def _mixer_call(kernel, name, proj3, sec_block, sec_width, out_width, extra_inputs, extra_specs, scratch):
    b, t, _ = proj3.shape
    tb = min(LONG_BLOCK, t)
    return pl.pallas_call(
        kernel,
        grid=(b, t // tb),
        in_specs=[pl.BlockSpec((None, tb, sec_width), lambda i, j: (i, j, sec_block))] + extra_specs(tb),
        out_specs=pl.BlockSpec((None, tb, out_width), lambda i, j: (i, j, 0)),
        out_shape=jax.ShapeDtypeStruct((b, t, out_width), BF16),
        scratch_shapes=scratch(tb),
        compiler_params=_cparams(("parallel", "arbitrary")),
        name=name,
    )(proj3, *extra_inputs)


def _full2(a, b):
    return pl.BlockSpec((a, b), lambda i, j: (0, 0))


def _gdn_mixer(proj3, gates3, conv_w, prm, gain):
    b, t, _ = proj3.shape
    tb = min(TIME_BLOCK, t)
    n_blocks = t // tb
    n_units = (GDN_HEADS // 2) * (tb // CHUNK)
    cur = lambda i, j: (i, jnp.minimum(j, n_blocks - 1))
    prv = lambda i, j: (i, jnp.maximum(j - 1, 0))
    z_block = (SEC_M + 3 * GDN_W) // GDN_W
    assert z_block * GDN_W == SEC_M + 3 * GDN_W
    return pl.pallas_call(
        functools.partial(_gdn_kernel, n_blocks),
        grid=(b, n_blocks + 1),
        in_specs=[pl.BlockSpec((None, tb, SEC_G), lambda i, j: cur(i, j) + (1,)),
                  pl.BlockSpec((None, tb, LANES), lambda i, j: cur(i, j) + (0,)),
                  pl.BlockSpec((None, tb, GDN_W), lambda i, j: prv(i, j) + (z_block,)),
                  _full2(CONV_K, 3 * GDN_W), _full2(8, LANES), _full2(1, GDN_W)],
        out_specs=pl.BlockSpec((None, tb, GDN_W), lambda i, j: prv(i, j) + (0,)),
        out_shape=jax.ShapeDtypeStruct((b, t, GDN_W), BF16),
        scratch_shapes=[pltpu.VMEM((GDN_HEADS // 2, LANES, LANES), F32),
                        pltpu.VMEM((n_units, CHUNK, LANES), F32),
                        pltpu.VMEM((n_units, 2 * CHUNK, LANES), BF16),
                        pltpu.VMEM((n_units, CHUNK, LANES), BF16),
                        pltpu.VMEM((n_units, LANES, LANES), BF16),
                        pltpu.VMEM((n_units, 8, LANES), F32),
                        pltpu.VMEM((tb + 8, 3 * GDN_W), F32)],
        compiler_params=_cparams(("parallel", "arbitrary")),
        name="gdn_mixer",
    )(proj3, gates3, proj3, conv_w, prm, gain)


def _mlstm_mixer(proj3, gates3, conv_w, prm, gain):
    b, t, _ = proj3.shape
    tb = min(LONG_BLOCK, t)
    steps_t = t // tb
    n_rows = _sorted_rows(b * t)
    zero_rows = n_rows // (b * steps_t)
    assert zero_rows * b * steps_t == n_rows and zero_rows % 8 == 0
    return pl.pallas_call(
        _mlstm_kernel,
        grid=(b, steps_t),
        in_specs=[pl.BlockSpec((None, tb, SEC_M), lambda i, j: (i, j, 0)),
                  pl.BlockSpec((None, tb, LANES), lambda i, j: (i, j, 0)),
                  _full2(CONV_K, 2 * MLSTM_W), _full2(8, LANES), _full2(1, MLSTM_W)],
        out_specs=[pl.BlockSpec((None, tb, MLSTM_W), lambda i, j: (i, j, 0)),
                   pl.BlockSpec((zero_rows, HP_WIDTH), lambda i, j: (i * steps_t + j, 0))],
        out_shape=[jax.ShapeDtypeStruct((b, t, MLSTM_W), BF16),
                   jax.ShapeDtypeStruct((n_rows, HP_WIDTH), F32)],
        scratch_shapes=[pltpu.VMEM((MLSTM_HEADS // 2, LANES, LANES), F32),
                        pltpu.VMEM((MLSTM_HEADS // 2, 8, LANES), F32),
                        pltpu.VMEM((tb + 8, 2 * MLSTM_W), F32)],
        compiler_params=_cparams(("parallel", "arbitrary")),
        name="mlstm_mixer",
    )(proj3, gates3, conv_w, prm, gain)


def _ret_mixer(proj3, cos_tab, sin_tab, gain):
    return _mixer_call(
        _ret_kernel, "retention_mixer", proj3, 3, SEC_R, RET_W, (cos_tab, sin_tab, gain),
        lambda tb: [pl.BlockSpec((tb, LANES), lambda i, j: (j, 0)),
                    pl.BlockSpec((tb, LANES), lambda i, j: (j, 0)),
                    _full2(1, RET_W)],
        lambda tb: [pltpu.VMEM((RET_HEADS // 2, LANES, LANES), F32)])


DISPATCH_TILE = 2048


def _dispatch_kernel(pos_ref, src_ref, zero_ref, out_ref, sem):
    del zero_ref
    rows = src_ref.shape[0]
    base = pl.program_id(0) * rows

    def issue(r, carry):
        pltpu.make_async_copy(src_ref.at[pl.ds(r, 1)], out_ref.at[pl.ds(pos_ref[base + r], 1)], sem).start()
        return carry

    lax.fori_loop(0, rows, issue, 0, unroll=8)
    pltpu.make_async_copy(src_ref, out_ref.at[pl.ds(0, rows)], sem).wait()


def _dispatch(src, pos, zeros):
    n, width = src.shape
    n_rows = zeros.shape[0]
    tile = min(DISPATCH_TILE, n)
    return pl.pallas_call(
        _dispatch_kernel,
        grid_spec=pltpu.PrefetchScalarGridSpec(
            num_scalar_prefetch=1,
            grid=(n // tile,),
            in_specs=[pl.BlockSpec((tile, width), lambda i, p: (i, 0)), pl.BlockSpec(memory_space=pl.ANY)],
            out_specs=pl.BlockSpec(memory_space=pl.ANY),
            scratch_shapes=[pltpu.SemaphoreType.DMA(())],
        ),
        out_shape=jax.ShapeDtypeStruct((n_rows, width), src.dtype),
        input_output_aliases={2: 0},
        compiler_params=_cparams(("arbitrary",)),
        name="moe_dispatch",
    )(pos, src, zeros)


def _ffn_kernel(plan_ref, used_ref, x_ref, wg_hbm, wu_hbm, wd_hbm, y_ref, wg_buf, wu_buf, wd_buf, sems):
    def group_copies(g, to_slot):
        lo = pl.multiple_of(g * EXPERTS_PER_GROUP, EXPERTS_PER_GROUP)
        experts = pl.ds(lo, EXPERTS_PER_GROUP)
        return [pltpu.make_async_copy(hbm.at[experts], buf.at[to_slot], sems.at[to_slot])
                for hbm, buf in ((wg_hbm, wg_buf), (wu_hbm, wu_buf), (wd_hbm, wd_buf))]

    def one_tile(i, rows):
        used = i < used_ref[0]
        group, slot, starts_group, next_group = plan_ref[0, i], plan_ref[1, i], plan_ref[2, i], plan_ref[3, i]

        @pl.when(used & (i == 0))
        def _():
            for cp in group_copies(group, slot):
                cp.start()

        @pl.when(used & (starts_group == 1) & (next_group >= 0))
        def _():
            for cp in group_copies(next_group, 1 - slot):
                cp.start()

        @pl.when(used & (starts_group == 1))
        def _():
            for cp in group_copies(group, slot):
                cp.wait()

        @pl.when(jnp.logical_not(used))
        def _():
            y_ref[rows, :] = jnp.zeros((MOE_TILE, D_MODEL), F32)

        @pl.when(used)
        def _():
            e1, e2 = plan_ref[4, i], plan_ref[5, i]
            x = x_ref[rows, 0:D_MODEL].astype(BF16)
            gate1 = jnp.dot(x, wg_buf[slot, e1], preferred_element_type=F32)
            gate2 = jnp.dot(x, wg_buf[slot, e2], preferred_element_type=F32)
            up1 = jnp.dot(x, wu_buf[slot, e1], preferred_element_type=F32)
            up2 = jnp.dot(x, wu_buf[slot, e2], preferred_element_type=F32)
            hid1 = (_silu(gate1) * up1).astype(BF16)
            hid2 = (_silu(gate2) * up2).astype(BF16)
            y1 = jnp.dot(hid1, wd_buf[slot, e1], preferred_element_type=F32)
            y2 = jnp.dot(hid2, wd_buf[slot, e2], preferred_element_type=F32)
            rw = x_ref[rows, D_MODEL:]
            y_ref[rows, :] = rw[:, 0:1] * y1 + rw[:, 1:2] * y2

    for sub in range(FFN_STEP_TILES):
        one_tile(pl.program_id(0) * FFN_STEP_TILES + sub, slice(sub * MOE_TILE, (sub + 1) * MOE_TILE))


def _expert_ffn(xs, tile_plan, n_used, wg, wu, wd):
    r = xs.shape[0]
    step_rows = FFN_STEP_TILES * MOE_TILE
    assert r % step_rows == 0
    last_used = lambda nu: (nu[0] - 1) // FFN_STEP_TILES
    return pl.pallas_call(
        _ffn_kernel,
        grid_spec=pltpu.PrefetchScalarGridSpec(
            num_scalar_prefetch=2,
            grid=(r // step_rows,),
            in_specs=[pl.BlockSpec((step_rows, HP_WIDTH), lambda i, plan, nu: (jnp.minimum(i, last_used(nu)), 0)),
                      pl.BlockSpec(memory_space=pl.ANY), pl.BlockSpec(memory_space=pl.ANY),
                      pl.BlockSpec(memory_space=pl.ANY)],
            out_specs=pl.BlockSpec((step_rows, D_MODEL), lambda i, plan, nu: (i, 0)),
            scratch_shapes=[pltpu.VMEM((2, EXPERTS_PER_GROUP, D_MODEL, D_EXPERT), BF16),
                            pltpu.VMEM((2, EXPERTS_PER_GROUP, D_MODEL, D_EXPERT), BF16),
                            pltpu.VMEM((2, EXPERTS_PER_GROUP, D_EXPERT, D_MODEL), BF16),
                            pltpu.SemaphoreType.DMA((2,))],
        ),
        out_shape=jax.ShapeDtypeStruct((r, D_MODEL), F32),
        compiler_params=_cparams(("arbitrary",)),
        name="expert_pair_ffn",
    )(tile_plan, n_used, xs, wg, wu, wd)


def _combine_kernel(final_norm, pos_ref, x_ref, gain_ref, ys_ref, o_ref, buf, sems):
    rows = o_ref.shape[0]
    i = pl.program_id(0)
    slot = i & 1

    def gather_tile(tile, to_slot):
        def issue(r, carry):
            pltpu.make_async_copy(ys_ref.at[pl.ds(pos_ref[tile * rows + r], 1)],
                                  buf.at[to_slot, pl.ds(r, 1)], sems.at[to_slot]).start()
            return carry

        lax.fori_loop(0, rows, issue, 0, unroll=8)

    @pl.when(i == 0)
    def _():
        gather_tile(0, 0)

    @pl.when(i + 1 < pl.num_programs(0))
    def _():
        gather_tile(i + 1, 1 - slot)

    pltpu.make_async_copy(ys_ref.at[pl.ds(0, rows)], buf.at[slot], sems.at[slot]).wait()
    out = x_ref[...] + buf[slot]
    if final_norm:
        ms = jnp.mean(out * out, axis=-1, keepdims=True)
        out = out * lax.rsqrt(ms + EPS) * gain_ref[...]
    o_ref[...] = out


def _combine(x2, ys, pos, gain, final_norm):
    n = x2.shape[0]
    return pl.pallas_call(
        functools.partial(_combine_kernel, final_norm),
        grid_spec=pltpu.PrefetchScalarGridSpec(
            num_scalar_prefetch=1,
            grid=(n // ROW_TILE,),
            in_specs=[pl.BlockSpec((ROW_TILE, D_MODEL), lambda i, p: (i, 0)),
                      pl.BlockSpec((1, D_MODEL), lambda i, p: (0, 0)),
                      pl.BlockSpec(memory_space=pl.ANY)],
            out_specs=pl.BlockSpec((ROW_TILE, D_MODEL), lambda i, p: (i, 0)),
            scratch_shapes=[pltpu.VMEM((2, ROW_TILE, D_MODEL), F32), pltpu.SemaphoreType.DMA((2,))],
        ),
        out_shape=jax.ShapeDtypeStruct((n, D_MODEL), F32),
        compiler_params=_cparams(("arbitrary",)),
        name="moe_combine",
    )(pos, x2, gain, ys)


def _plan(route, counts, n):
    cls, rank = route[:, 0], route[:, 1]
    counts = counts.astype(jnp.int32)
    tiles_per = (counts + MOE_TILE - 1) // MOE_TILE
    tile_end = jnp.cumsum(tiles_per)
    tile_start = tile_end - tiles_per
    class_ids = jnp.arange(N_CLASSES, dtype=jnp.int32)
    start_of = jnp.sum(jnp.where(cls[:, None] == class_ids[None, :], tile_start[None, :], 0), axis=1)
    pos = start_of * MOE_TILE + rank
    n_tiles = _sorted_rows(n) // MOE_TILE
    n_used = tile_end[-1]
    tile_ids = jnp.minimum(jnp.arange(n_tiles, dtype=jnp.int32), n_used - 1)
    tile_cls = jnp.sum((tile_end[None, :] <= tile_ids[:, None]).astype(jnp.int32), axis=1)
    per_group = EXPERTS_PER_GROUP * EXPERTS_PER_GROUP
    tile_group = tile_cls // per_group
    starts_group = jnp.concatenate([jnp.ones((1,), jnp.int32),
                                    (tile_group[1:] != tile_group[:-1]).astype(jnp.int32)])
    slot = (jnp.cumsum(starts_group) - 1) % 2
    present = jnp.sum(tiles_per.reshape(N_GROUPS, per_group), axis=1) > 0
    nxt = [jnp.int32(-1)] * N_GROUPS
    for g in range(N_GROUPS - 2, -1, -1):
        nxt[g] = jnp.where(present[g + 1], g + 1, nxt[g + 1])
    next_group = jnp.stack(nxt)[tile_group]
    tile_plan = jnp.stack([tile_group, slot, starts_group, next_group,
                           (tile_cls // EXPERTS_PER_GROUP) % EXPERTS_PER_GROUP, tile_cls % EXPERTS_PER_GROUP])
    return pos, tile_plan.astype(jnp.int32), n_used.reshape(1)


def _rope_tables(t):
    inv_freq = ROPE_BASE ** (-jnp.arange(0, HEAD_DIM, 2, dtype=F32) / HEAD_DIM)
    ang = jnp.arange(t, dtype=F32)[:, None] * inv_freq[None, :]
    cos, sin = jnp.cos(ang), jnp.sin(ang)
    cos_tab = jnp.tile(cos, (1, 2 * LANES // HEAD_DIM))
    sin_tab = jnp.tile(jnp.concatenate([-sin, sin], axis=-1), (1, LANES // HEAD_DIM))
    return cos_tab, sin_tab


def _split_w_in(w):
    g0, g1 = 0, SEC_G
    r0, r1 = g1 + 2 * GDN_HEADS, g1 + 2 * GDN_HEADS + SEC_R
    m0, m1 = r1, r1 + SEC_M
    assert m1 + 2 * MLSTM_HEADS == w.shape[1]
    main = jnp.concatenate([w[:, m0:m1], w[:, g0:g1], w[:, r0:r1]], axis=1).astype(BF16)
    gate = jnp.concatenate([w[:, g1:r0], w[:, m1:]], axis=1)
    gate = jnp.pad(gate, ((0, 0), (0, LANES - gate.shape[1]))).astype(BF16)
    return main, gate


def _lane_row(values, start):
    return jnp.zeros((LANES,), F32).at[start:start + values.shape[0]].set(values)


def kernel(x, norm_mix, w_in, gdn_conv, gdn_a_log, gdn_dt_bias, gdn_norm, ret_norm, mlstm_conv, mlstm_i_bias, mlstm_f_bias, mlstm_norm, w_out, norm_ffn, router_group, router_expert, router_bias, expert_gate, expert_up, expert_down, norm_final):
    b, t, d = x.shape
    n = b * t
    depth = w_in.shape[0]
    cos_tab, sin_tab = _rope_tables(t)
    x2 = x.reshape(n, d)
    for l in range(depth):
        w_main, w_gate = _split_w_in(w_in[l])
        proj, gates, (wg, wu, wd) = _inproj(x2, norm_mix[l][None, :], w_main, w_gate,
                                            (expert_gate, expert_up, expert_down), l)
        proj3 = proj.reshape(b, t, D_MAIN)
        gates3 = gates.reshape(b, t, LANES)
        gdn_prm = jnp.zeros((8, LANES), F32).at[0].set(_lane_row(gdn_a_log[l], G_A)).at[1].set(
            _lane_row(gdn_dt_bias[l], G_A))
        mlstm_prm = jnp.zeros((8, LANES), F32).at[0].set(_lane_row(mlstm_i_bias[l], M_I)).at[1].set(
            _lane_row(mlstm_f_bias[l], M_F))
        y_m, zeros = _mlstm_mixer(proj3, gates3, mlstm_conv[l], mlstm_prm, mlstm_norm[l][None, :])
        y_g = _gdn_mixer(proj3, gates3, gdn_conv[l], gdn_prm, gdn_norm[l][None, :])
        y_r = _ret_mixer(proj3, cos_tab, sin_tab, ret_norm[l][None, :])
        wo = w_out[l].astype(BF16)
        w_router = jnp.pad(jnp.concatenate([router_group[l], router_expert[l]], axis=1),
                           ((0, 0), (0, LANES - N_GROUPS - N_EXPERTS))).astype(BF16)
        rb = _lane_row(router_bias[l], N_GROUPS)[None, :]
        x2, h_packed, route, counts = _outproj(
            x2, y_m.reshape(n, MLSTM_W), y_g.reshape(n, GDN_W), y_r.reshape(n, RET_W),
            wo[GDN_W + RET_W:], wo[:GDN_W], wo[GDN_W:GDN_W + RET_W],
            norm_ffn[l][None, :], w_router, rb)
        pos, tile_plan, n_used = _plan(route, counts[0], n)
        xs = _dispatch(h_packed, pos, zeros)
        ys = _expert_ffn(xs, tile_plan, n_used, wg, wu, wd)
        x2 = _combine(x2, ys, pos, norm_final[None, :], l == depth - 1)
    return x2.reshape(b, t, d)
```

```python
import functools
import math

import jax
import jax.numpy as jnp
from jax import lax
from jax.experimental import pallas as pl
from jax.experimental.pallas import tpu as pltpu

F32 = jnp.float32
BF16 = jnp.bfloat16

D_MODEL = 1024
HEAD_DIM = 64
CHUNK = 64
GDN_HEADS, RET_HEADS, MLSTM_HEADS = 6, 4, 6
GDN_W, RET_W, MLSTM_W = GDN_HEADS * HEAD_DIM, RET_HEADS * HEAD_DIM, MLSTM_HEADS * HEAD_DIM
CONV_K = 4
ROPE_BASE = 10000.0
N_GROUPS, EXPERTS_PER_GROUP = 4, 8
N_EXPERTS = N_GROUPS * EXPERTS_PER_GROUP
D_EXPERT = D_MODEL // 4
EPS = 1e-6
LANES = 128
SEC_M, SEC_G, SEC_R = 4 * MLSTM_W, 4 * GDN_W, 4 * RET_W
D_MAIN = SEC_M + SEC_G + SEC_R
G_BETA, G_A, M_I, M_F = 0, GDN_HEADS, 2 * GDN_HEADS, 2 * GDN_HEADS + MLSTM_HEADS

ROW_TILE = 512
OUT_TILE = 512
OUT_SUB = 128
TIME_BLOCK = 256
LONG_BLOCK = 512
MOE_TILE = 128
FFN_STEP_TILES = 8
VMEM_LIMIT = 56 * 1024 * 1024


def _cparams(sem):
    return pltpu.CompilerParams(dimension_semantics=sem, vmem_limit_bytes=VMEM_LIMIT)


def _silu(x):
    return x * (1.0 / (1.0 + jnp.exp(-x)))


def _sigmoid(x):
    return 1.0 / (1.0 + jnp.exp(-x))


def _softplus(x):
    return jnp.maximum(x, 0.0) + jnp.log(1.0 + jnp.exp(-jnp.abs(x)))


def _dot(a, b):
    return jnp.dot(a.astype(BF16), b.astype(BF16), preferred_element_type=F32)


def _dot_nt(a, b):
    return lax.dot_general(a.astype(BF16), b.astype(BF16), (((1,), (1,)), ((), ())),
                           preferred_element_type=F32)


def _lane_masks(rows):
    lane = lax.broadcasted_iota(jnp.int32, (rows, LANES), 1)
    row = lax.broadcasted_iota(jnp.int32, (rows, LANES), 0)
    return lane, row


def _half_sum(x, m_a):
    s_a = jnp.sum(jnp.where(m_a, x, 0.0), axis=-1, keepdims=True)
    s_b = jnp.sum(jnp.where(m_a, 0.0, x), axis=-1, keepdims=True)
    return jnp.where(m_a, s_a, s_b)


def _half_max(x, m_a):
    s_a = jnp.max(jnp.where(m_a, x, -jnp.inf), axis=-1, keepdims=True)
    s_b = jnp.max(jnp.where(m_a, -jnp.inf, x), axis=-1, keepdims=True)
    return jnp.where(m_a, s_a, s_b)


def _col_form(g, lane_a, m_a):
    rows = g.shape[0]
    ca = jnp.broadcast_to(g[:, lane_a:lane_a + 1], (rows, LANES))
    cb = jnp.broadcast_to(g[:, lane_a + 1:lane_a + 2], (rows, LANES))
    return jnp.where(m_a, ca, cb)


def _row_form(col, eye):
    return jnp.sum(jnp.where(eye, col, 0.0), axis=0, keepdims=True)


def _block_diag(y, m_a):
    return jnp.concatenate([jnp.where(m_a, y, 0.0), jnp.where(m_a, 0.0, y)], axis=0)


def _pmul(x, y, m_a):
    return _dot(x, _block_diag(y, m_a))


def _outer_state(k, v, bd_mask):
    zero = jnp.zeros_like(k)
    kt = jnp.concatenate([k, zero], axis=0).T
    vp = jnp.concatenate([v, zero], axis=0)
    return jnp.where(bd_mask, _dot(kt, vp), 0.0)


def _chunk_cumsum(x):
    rows = x.shape[0]
    r = lax.broadcasted_iota(jnp.int32, (rows, LANES), 0) % CHUNK
    s = 1
    while s < CHUNK:
        x = x + jnp.where(r >= s, pltpu.roll(x, s, axis=0), 0.0)
        s *= 2
    return x


def _causal_conv(raw, cbuf, w_ref):
    rows = raw.shape[0]
    cbuf[8:8 + rows, :] = raw
    acc = raw * w_ref[CONV_K - 1:CONV_K, :]
    for j in range(CONV_K - 1):
        off = 8 - (CONV_K - 1) + j
        acc = acc + cbuf[off:off + rows, :] * w_ref[j:j + 1, :]
    cbuf[0:8, :] = raw[rows - 8:rows, :]
    return acc


def _inproj_kernel(x_ref, gain_ref, w_ref, wg_ref, *rest):
    cast_refs, (o_ref, og_ref) = rest[:3] + rest[5:], rest[3:5]
    x = x_ref[...]
    ms = jnp.mean(x * x, axis=-1, keepdims=True)
    h = (x * lax.rsqrt(ms + EPS) * gain_ref[...]).astype(BF16)

    def proj(lo, hi):
        return jnp.dot(h, w_ref[:, lo:hi], preferred_element_type=F32)

    step = 512
    for c in range(D_MAIN // step):
        o_ref[:, c * step:(c + 1) * step] = proj(c * step, (c + 1) * step)
    og_ref[...] = jnp.dot(h, wg_ref[...], preferred_element_type=F32)
    for src, dst in zip(cast_refs[:3], cast_refs[3:]):
        dst[...] = src[...].astype(BF16)


def _inproj(x2, gain, w_main, w_gate, expert_weights, layer):
    n = x2.shape[0]
    steps = n // OUT_TILE
    full = lambda a, b: pl.BlockSpec((a, b), lambda i: (0, 0))
    flat = [w.reshape(-1, w.shape[-1]) for w in expert_weights]
    rows = [w.shape[0] // expert_weights[0].shape[0] for w in flat]
    assert all(r % (16 * steps) == 0 for r in rows)
    slab_in = [pl.BlockSpec((r // steps, w.shape[1]), lambda i: (layer * steps + i, 0)) for r, w in zip(rows, flat)]
    slab_out = [pl.BlockSpec((r // steps, w.shape[1]), lambda i: (i, 0)) for r, w in zip(rows, flat)]
    outs = pl.pallas_call(
        _inproj_kernel,
        grid=(steps,),
        in_specs=[
            pl.BlockSpec((OUT_TILE, D_MODEL), lambda i: (i, 0)),
            full(1, D_MODEL), full(D_MODEL, D_MAIN), full(D_MODEL, LANES),
        ] + slab_in,
        out_specs=[
            pl.BlockSpec((OUT_TILE, D_MAIN), lambda i: (i, 0)),
            pl.BlockSpec((OUT_TILE, LANES), lambda i: (i, 0)),
        ] + slab_out,
        out_shape=[jax.ShapeDtypeStruct((n, D_MAIN), F32), jax.ShapeDtypeStruct((n, LANES), F32)] + [
            jax.ShapeDtypeStruct((r, w.shape[1]), BF16) for r, w in zip(rows, flat)],
        compiler_params=_cparams(("parallel",)),
        name="norm_inproj",
    )(x2, gain, w_main, w_gate, *flat)
    return outs[0], outs[1], [o.reshape(w.shape[1:]) for o, w in zip(outs[2:], expert_weights)]


N_CLASSES = N_GROUPS * EXPERTS_PER_GROUP * EXPERTS_PER_GROUP
N_PAIRS = N_GROUPS * (EXPERTS_PER_GROUP * (EXPERTS_PER_GROUP - 1)) // 2


def _sorted_rows(n):
    return (n // MOE_TILE + N_PAIRS) * MOE_TILE


HP_WIDTH = D_MODEL + LANES


def _route_rows(lg):
    rows = lg.shape[0]
    lane = lax.broadcasted_iota(jnp.int32, (rows, LANES), 1)
    lane_f = lane.astype(F32)
    neg = -jnp.inf

    def first_lane(mask):
        return jnp.min(jnp.where(mask, lane_f, float(LANES)), axis=-1, keepdims=True).astype(jnp.int32)

    gl = jnp.where(lane < N_GROUPS, lg, neg)
    gm = jnp.max(gl, axis=-1, keepdims=True)
    gsum = jnp.sum(jnp.where(lane < N_GROUPS, jnp.exp(gl - gm), 0.0), axis=-1, keepdims=True)
    gidx = first_lane(gl == gm)
    group_p = 1.0 / gsum
    in_group = (lane >= N_GROUPS) & (lane < N_GROUPS + N_EXPERTS) & (
        lax.shift_right_arithmetic(lane - N_GROUPS, 3) == gidx)
    el = jnp.where(in_group, lg, neg)
    m1 = jnp.max(el, axis=-1, keepdims=True)
    i1 = first_lane(el == m1)
    el2 = jnp.where(lane == i1, neg, el)
    m2 = jnp.max(el2, axis=-1, keepdims=True)
    i2 = first_lane(el2 == m2)
    e2 = jnp.exp(m2 - m1)
    w1 = group_p * (1.0 / (1.0 + e2))
    w2 = group_p * (e2 / (1.0 + e2))
    first_lower = i1 < i2
    base = N_GROUPS + gidx * EXPERTS_PER_GROUP
    lo = jnp.minimum(i1, i2) - base
    hi = jnp.maximum(i1, i2) - base
    cls = (gidx * EXPERTS_PER_GROUP + lo) * EXPERTS_PER_GROUP + hi
    return cls, jnp.where(first_lower, w1, w2), jnp.where(first_lower, w2, w1)


def _outproj_kernel(x_ref, ym_ref, yg_ref, yr_ref, wm_ref, wg_ref, wr_ref, gain_ref, wrt_ref, rb_ref,
                    xo_ref, hp_ref, rt_ref, cnt_ref, running):
    @pl.when(pl.program_id(0) == 0)
    def _():
        running[...] = jnp.zeros_like(running)

    rows = OUT_SUB
    subs = [slice(s * rows, (s + 1) * rows) for s in range(x_ref.shape[0] // rows)]
    acc = [x_ref[r, :] + jnp.dot(yg_ref[r, :], wg_ref[...], preferred_element_type=F32) for r in subs]
    acc = [a + jnp.dot(yr_ref[r, :], wr_ref[...], preferred_element_type=F32) for a, r in zip(acc, subs)]
    acc = [a + jnp.dot(ym_ref[r, :], wm_ref[...], preferred_element_type=F32) for a, r in zip(acc, subs)]
    for a, r in zip(acc, subs):
        xo_ref[r, :] = a
    hn = [a * lax.rsqrt(jnp.mean(a * a, axis=-1, keepdims=True) + EPS) * gain_ref[...] for a in acc]
    for h, r in zip(hn, subs):
        hp_ref[r, 0:D_MODEL] = h
    logits = [jnp.dot(h.astype(BF16), wrt_ref[...], preferred_element_type=F32) + rb_ref[...] for h in hn]
    routed = [_route_rows(lg) for lg in logits]

    cls_lane = lax.broadcasted_iota(jnp.int32, (rows, N_CLASSES), 1)
    before = (lax.broadcasted_iota(jnp.int32, (rows, rows), 1)
              < lax.broadcasted_iota(jnp.int32, (rows, rows), 0)).astype(BF16)
    onehot = [cls_lane == cls for cls, _, _ in routed]
    prefix = [jnp.dot(before, oh.astype(BF16), preferred_element_type=F32) for oh in onehot]
    counts = [jnp.sum(oh.astype(F32), axis=0, keepdims=True) for oh in onehot]
    lane = lax.broadcasted_iota(jnp.int32, (rows, LANES), 1)
    seen = running[0:1, :]
    for r, (cls, w_lo, w_hi), oh, pre, cnt in zip(subs, routed, onehot, prefix, counts):
        rank = jnp.sum(jnp.where(oh, pre + seen, 0.0), axis=-1, keepdims=True)
        seen = seen + cnt
        rt_ref[r, :] = jnp.where(lane == 0, cls, jnp.where(lane == 1, rank.astype(jnp.int32), 0))
        hp_ref[r, D_MODEL:] = jnp.where(lane == 0, w_lo, jnp.where(lane == 1, w_hi, 0.0))
    running[0:1, :] = seen
    cnt_ref[...] = jnp.broadcast_to(seen, cnt_ref.shape)


def _outproj(x2, ym, yg, yr, wo_m, wo_g, wo_r, gain, w_router, router_bias):
    n = x2.shape[0]
    tile = min(2 * OUT_TILE, n)
    row = lambda w: pl.BlockSpec((tile, w), lambda i: (i, 0))
    full = lambda a, b: pl.BlockSpec((a, b), lambda i: (0, 0))
    return pl.pallas_call(
        _outproj_kernel,
        grid=(n // tile,),
        in_specs=[row(D_MODEL), row(MLSTM_W), row(GDN_W), row(RET_W),
                  full(MLSTM_W, D_MODEL), full(GDN_W, D_MODEL), full(RET_W, D_MODEL),
                  full(1, D_MODEL), full(D_MODEL, LANES), full(1, LANES)],
        out_specs=[row(D_MODEL), row(HP_WIDTH), row(LANES), full(8, N_CLASSES)],
        out_shape=[jax.ShapeDtypeStruct((n, D_MODEL), F32),
                   jax.ShapeDtypeStruct((n, HP_WIDTH), F32),
                   jax.ShapeDtypeStruct((n, LANES), jnp.int32),
                   jax.ShapeDtypeStruct((8, N_CLASSES), F32)],
        scratch_shapes=[pltpu.VMEM((8, N_CLASSES), F32)],
        compiler_params=_cparams(("arbitrary",)),
        name="outproj_norm_router",
    )(x2, ym, yg, yr, wo_m, wo_g, wo_r, gain, w_router, router_bias)


def _gdn_kernel(n_blocks, *refs):
    state, u_s, wq_s, qk_s, kt_s, sd_s, cbuf = refs[-7:]
    j = pl.program_id(1)

    @pl.when(j == 0)
    def _():
        for ref in (state, u_s, wq_s, qk_s, kt_s, sd_s):
            ref[...] = jnp.zeros_like(ref)
        cbuf[0:8, :] = jnp.zeros((8, cbuf.shape[1]), F32)

    @pl.when(j < n_blocks)
    def _():
        _gdn_step(True, *refs)

    @pl.when(j == n_blocks)
    def _():
        _gdn_step(False, *refs)


def _gdn_step(prepare, x_ref, gt_ref, z_ref, conv_ref, prm_ref, gain_ref, y_ref,
              state, u_s, wq_s, qk_s, kt_s, sd_s, cbuf):
    tb = x_ref.shape[0]
    gt = gt_ref[...]
    beta_all = _sigmoid(gt)
    g_all = -jnp.exp(prm_ref[0:1, :]) * _softplus(gt + prm_ref[1:2, :])
    gc_all = _chunk_cumsum(g_all)

    lane, row = _lane_masks(CHUNK)
    m_a = lane < HEAD_DIM
    lane_h = lane % HEAD_DIM
    eye = lane_h == row
    tril = lane_h <= row
    strict = lane_h < row
    eye_f = eye.astype(F32)
    lane2 = lax.broadcasted_iota(jnp.int32, (LANES, LANES), 1)
    row2 = lax.broadcasted_iota(jnp.int32, (LANES, LANES), 0)
    bd_mask = (lane2 < HEAD_DIM) == (row2 < HEAD_DIM)
    lane_t = lax.broadcasted_iota(jnp.int32, (tb, LANES), 1)
    m_a_t = lane_t < HEAD_DIM

    n_pairs = GDN_HEADS // 2
    n_chunks = tb // CHUNK
    units = [(p, c) for p in range(n_pairs) for c in range(n_chunks)]

    def unit_index(p, c):
        return p * n_chunks + c

    prev = {}
    for p, c in units:
        i = unit_index(p, c)
        prev[p, c] = (u_s[i], wq_s[i], qk_s[i], kt_s[i], sd_s[i, 0:1, :])
    s_bd = [state[p] for p in range(n_pairs)]

    def recurrence(c):
        rs = slice(c * CHUNK, (c + 1) * CHUNK)
        for p in range(n_pairs):
            cs = slice(p * LANES, (p + 1) * LANES)
            u_mat, wq, qk, k_dec_t, s_decay = prev[p, c]
            ws_qs = jnp.dot(wq, s_bd[p].astype(BF16), preferred_element_type=F32)
            v_new = u_mat - ws_qs[:CHUNK]
            o = ws_qs[CHUNK:] + jnp.dot(qk, _block_diag(v_new, m_a).astype(BF16), preferred_element_type=F32)
            v_pad = jnp.concatenate([v_new, jnp.zeros_like(v_new)], axis=0).astype(BF16)
            s_bd[p] = s_bd[p] * s_decay + jnp.where(
                bd_mask, jnp.dot(k_dec_t, v_pad, preferred_element_type=F32), 0.0)
            ms = _half_sum(o * o, m_a) * (1.0 / HEAD_DIM)
            y = o * lax.rsqrt(ms + EPS) * gain_ref[:, cs] * _silu(z_ref[rs, cs])
            y_ref[rs, cs] = y.astype(y_ref.dtype)

    if not prepare:
        for c in range(n_chunks):
            recurrence(c)
        for p in range(n_pairs):
            state[p] = s_bd[p]
        return

    recurrence(0)
    qkv = _silu(_causal_conv(x_ref[:, 0:3 * GDN_W], cbuf, conv_ref))
    ins = {}
    for p in range(n_pairs):
        q_t = qkv[:, p * LANES:(p + 1) * LANES]
        k_t = qkv[:, GDN_W + p * LANES:GDN_W + (p + 1) * LANES]
        v_t = qkv[:, 2 * GDN_W + p * LANES:2 * GDN_W + (p + 1) * LANES]
        q_t = q_t * lax.rsqrt(_half_sum(q_t * q_t, m_a_t) + EPS) * (HEAD_DIM ** -0.5)
        k_t = k_t * lax.rsqrt(_half_sum(k_t * k_t, m_a_t) + EPS)
        beta_t = _col_form(beta_all, G_BETA + 2 * p, m_a_t)
        gc_t = _col_form(gc_all, G_A + 2 * p, m_a_t)
        for c in range(n_chunks):
            rs = slice(c * CHUNK, (c + 1) * CHUNK)
            ins[p, c] = (q_t[rs], k_t[rs], v_t[rs], beta_t[rs], gc_t[rs])
    kkqk = {u: _dot_nt(jnp.concatenate([ins[u][1] * ins[u][3], ins[u][0]], axis=0), _block_diag(ins[u][1], m_a))
            for u in units}
    pre = {}
    for u in units:
        q, k, v, beta, gc = ins[u]
        decay = jnp.where(tril, jnp.exp(gc - _row_form(gc, eye)), 0.0)
        egc = jnp.exp(gc)
        g_last = gc[CHUNK - 1:CHUNK, :]
        k_dec = k * jnp.exp(g_last - gc)
        pre[u] = dict(
            a=jnp.where(strict, kkqk[u][:CHUNK] * decay, 0.0), qk=kkqk[u][CHUNK:] * decay,
            vb=v * beta, kbe=k * beta * egc, qe=q * egc, s_decay=jnp.exp(g_last),
            k_dec_t=jnp.concatenate([k_dec, jnp.zeros_like(k_dec)], axis=0).T)

    pw = {u: _pmul(pre[u]["a"], pre[u]["a"], m_a) for u in units}
    t_inv = {u: eye_f - pre[u]["a"] for u in units}
    later_chunks = list(range(1, n_chunks))
    for step in range(4):
        if step % 2 == 0 and later_chunks:
            recurrence(later_chunks.pop(0))
        both = {u: _dot(jnp.concatenate([pw[u], t_inv[u]], axis=0), _block_diag(pw[u], m_a)) for u in units}
        pw = {u: both[u][:CHUNK] for u in units}
        t_inv = {u: t_inv[u] + both[u][CHUNK:] for u in units}
    for c in later_chunks:
        recurrence(c)
    t_inv = {u: t_inv[u] + _pmul(t_inv[u], pw[u], m_a) for u in units}
    for p, c in units:
        d = pre[p, c]
        uw = _dot(t_inv[p, c], jnp.concatenate([_block_diag(d["vb"], m_a), _block_diag(d["kbe"], m_a)], axis=1))
        i = unit_index(p, c)
        u_s[i] = uw[:, :LANES]
        wq_s[i] = jnp.concatenate([uw[:, LANES:], d["qe"]], axis=0).astype(BF16)
        qk_s[i] = d["qk"].astype(BF16)
        kt_s[i] = d["k_dec_t"].astype(BF16)
        sd_s[i] = jnp.broadcast_to(d["s_decay"], (8, LANES))
    for p in range(n_pairs):
        state[p] = s_bd[p]


def _ret_kernel(x_ref, cos_ref, sin_ref, gain_ref, y_ref, state):
    tb = x_ref.shape[0]

    @pl.when(pl.program_id(1) == 0)
    def _():
        state[...] = jnp.zeros_like(state)

    lane, row = _lane_masks(CHUNK)
    m_a = lane < HEAD_DIM
    lane_h = lane % HEAD_DIM
    tril = lane_h <= row
    lane2 = lax.broadcasted_iota(jnp.int32, (LANES, LANES), 1)
    row2 = lax.broadcasted_iota(jnp.int32, (LANES, LANES), 0)
    bd_mask = (lane2 < HEAD_DIM) == (row2 < HEAD_DIM)
    lane_t = lax.broadcasted_iota(jnp.int32, (tb, LANES), 1)
    first_half = (lane_t % HEAD_DIM) < (HEAD_DIM // 2)
    cos = cos_ref[...]
    sin = sin_ref[...]
    rowf = row.astype(F32)
    pos_diff = (row - lane_h).astype(F32)

    def rope(x):
        swapped = jnp.where(first_half, pltpu.roll(x, LANES - HEAD_DIM // 2, axis=1),
                            pltpu.roll(x, HEAD_DIM // 2, axis=1))
        return x * cos + swapped * sin

    n_pairs = RET_HEADS // 2
    n_chunks = tb // CHUNK
    units = [(p, c) for p in range(n_pairs) for c in range(n_chunks)]
    chunk_decay, decay, q_scale, k_scale, qkv = [], [], [], [], {}
    for p in range(n_pairs):
        lg_a = math.log(1.0 - 2.0 ** (-5.0 - 2 * p))
        lg_b = math.log(1.0 - 2.0 ** (-5.0 - (2 * p + 1)))
        lg = jnp.where(m_a, lg_a, lg_b)
        decay.append(jnp.where(tril, jnp.exp(pos_diff * lg), 0.0))
        q_scale.append(jnp.exp(lg * (rowf + 1.0)))
        k_scale.append(jnp.exp(lg * (CHUNK - 1.0 - rowf)))
        chunk_decay.append(jnp.exp(lg[0:1, :] * CHUNK))
        q_t = rope(x_ref[:, p * LANES:(p + 1) * LANES])
        k_t = rope(x_ref[:, RET_W + p * LANES:RET_W + (p + 1) * LANES]) * (HEAD_DIM ** -0.5)
        for c in range(n_chunks):
            rs = slice(c * CHUNK, (c + 1) * CHUNK)
            qkv[p, c] = (q_t[rs], k_t[rs], x_ref[rs, 2 * RET_W + p * LANES:2 * RET_W + (p + 1) * LANES])
    scores = {(p, c): _dot_nt(qkv[p, c][0], _block_diag(qkv[p, c][1], m_a)) * decay[p] for p, c in units}
    o_intra = {u: _pmul(scores[u], qkv[u][2], m_a) for u in units}
    kv_all = {(p, c): _outer_state(qkv[p, c][1] * k_scale[p], qkv[p, c][2], bd_mask) for p, c in units}
    pre = {(p, c): (o_intra[p, c], qkv[p, c][0] * q_scale[p], kv_all[p, c]) for p, c in units}

    s_in = {}
    for p in range(n_pairs):
        s_bd = state[p]
        for c in range(n_chunks):
            s_in[p, c] = s_bd
            s_bd = chunk_decay[p] * s_bd + pre[p, c][2]
        state[p] = s_bd
    o_all = {u: pre[u][0] + _dot(pre[u][1], s_in[u]) for u in units}
    ms_all = {u: _half_sum(o_all[u] * o_all[u], m_a) * (1.0 / HEAD_DIM) for u in units}
    for p, c in units:
        rs = slice(c * CHUNK, (c + 1) * CHUNK)
        cs = slice(p * LANES, (p + 1) * LANES)
        gate = x_ref[rs, 3 * RET_W + p * LANES:3 * RET_W + (p + 1) * LANES]
        y = o_all[p, c] * lax.rsqrt(ms_all[p, c] + EPS) * gain_ref[:, cs] * _silu(gate)
        y_ref[rs, cs] = y.astype(y_ref.dtype)


def _mlstm_kernel(x_ref, gt_ref, conv_ref, prm_ref, gain_ref, y_ref, zero_ref, c_state, nm_state, cbuf):
    tb = x_ref.shape[0]
    first = pl.program_id(1) == 0

    @pl.when(first)
    def _():
        c_state[...] = jnp.zeros_like(c_state)
        nm_state[...] = jnp.zeros_like(nm_state)
        cbuf[0:8, :] = jnp.zeros((8, cbuf.shape[1]), F32)

    zero_ref[...] = jnp.zeros_like(zero_ref)

    qk_all = _silu(_causal_conv(x_ref[:, 0:2 * MLSTM_W], cbuf, conv_ref))
    gt = gt_ref[...]
    log_i_all = gt + prm_ref[0:1, :]
    f_pre = gt + prm_ref[1:2, :]
    log_f_all = jnp.minimum(f_pre, 0.0) - jnp.log(1.0 + jnp.exp(-jnp.abs(f_pre)))
    bc_all = _chunk_cumsum(log_f_all)

    lane, row = _lane_masks(CHUNK)
    m_a = lane < HEAD_DIM
    lane_h = lane % HEAD_DIM
    eye = lane_h == row
    tril = lane_h <= row
    lane2 = lax.broadcasted_iota(jnp.int32, (LANES, LANES), 1)
    row2 = lax.broadcasted_iota(jnp.int32, (LANES, LANES), 0)
    bd_mask = (lane2 < HEAD_DIM) == (row2 < HEAD_DIM)
    lane_t = lax.broadcasted_iota(jnp.int32, (tb, LANES), 1)
    m_a_t = lane_t < HEAD_DIM

    n_pairs = MLSTM_HEADS // 2
    n_chunks = tb // CHUNK
    units = [(p, c) for p in range(n_pairs) for c in range(n_chunks)]
    gates = {}
    for p in range(n_pairs):
        b_t = _col_form(bc_all, M_F + 2 * p, m_a_t)
        i_t = _col_form(log_i_all, M_I + 2 * p, m_a_t)
        for c in range(n_chunks):
            rs = slice(c * CHUNK, (c + 1) * CHUNK)
            gates[p, c] = (b_t[rs], i_t[rs])
    qkv = {(p, c): (qk_all[c * CHUNK:(c + 1) * CHUNK, p * LANES:(p + 1) * LANES] * (HEAD_DIM ** -0.5),
                    qk_all[c * CHUNK:(c + 1) * CHUNK, MLSTM_W + p * LANES:MLSTM_W + (p + 1) * LANES],
                    x_ref[c * CHUNK:(c + 1) * CHUNK, 2 * MLSTM_W + p * LANES:2 * MLSTM_W + (p + 1) * LANES])
           for p, c in units}
    qk = {u: _dot_nt(qkv[u][0], _block_diag(qkv[u][1], m_a)) for u in units}
    log_d = {u: jnp.where(tril, gates[u][0] - _row_form(gates[u][0], eye) + _row_form(gates[u][1], eye), -jnp.inf)
             for u in units}
    m_intra = {u: _half_max(log_d[u], m_a) for u in units}
    a_c = {u: gates[u][0][CHUNK - 1:CHUNK, :] - gates[u][0] + gates[u][1] for u in units}
    a_max = {u: jnp.max(a_c[u], axis=0, keepdims=True) for u in units}
    m_in, m_out = {}, {}
    for p in range(n_pairs):
        m_row = nm_state[p, 1:2, :]
        for c in range(n_chunks):
            m_in[p, c] = m_row
            m_row = jnp.maximum(gates[p, c][0][CHUNK - 1:CHUNK, :] + m_row, a_max[p, c])
            m_out[p, c] = m_row
    m_t = {u: jnp.maximum(gates[u][0] + m_in[u], m_intra[u]) for u in units}
    inter = {u: jnp.exp(gates[u][0] + m_in[u] - m_t[u]) for u in units}
    wmat = {u: jnp.where(tril, jnp.exp(log_d[u] - m_t[u]), 0.0) * qk[u] for u in units}
    kw = {u: qkv[u][1] * jnp.exp(a_c[u] - m_out[u]) for u in units}
    dec = {u: jnp.exp(gates[u][0][CHUNK - 1:CHUNK, :] + m_in[u] - m_out[u]) for u in units}
    num_intra = {u: _pmul(wmat[u], qkv[u][2], m_a) for u in units}
    den_intra = {u: _half_sum(wmat[u], m_a) for u in units}
    kv = {u: _outer_state(kw[u], qkv[u][2], bd_mask) for u in units}
    kn = {u: jnp.sum(kw[u], axis=0, keepdims=True) for u in units}
    c_in, n_in = {}, {}
    for p in range(n_pairs):
        c_bd, n_row = c_state[p], nm_state[p, 0:1, :]
        for c in range(n_chunks):
            c_in[p, c], n_in[p, c] = c_bd, n_row
            c_bd = dec[p, c] * c_bd + kv[p, c]
            n_row = dec[p, c] * n_row + kn[p, c]
        c_state[p] = c_bd
        nm_state[p, 0:1, :] = n_row
        nm_state[p, 1:2, :] = m_out[p, n_chunks - 1]
    num = {u: inter[u] * _dot(qkv[u][0], c_in[u]) + num_intra[u] for u in units}
    den = {u: inter[u] * _half_sum(qkv[u][0] * n_in[u], m_a) + den_intra[u] for u in units}
    h = {u: num[u] / jnp.maximum(jnp.abs(den[u]), jnp.exp(-m_t[u])) for u in units}
    ms = {u: _half_sum(h[u] * h[u], m_a) * (1.0 / HEAD_DIM) for u in units}
    for p, c in units:
        rs = slice(c * CHUNK, (c + 1) * CHUNK)
        cs = slice(p * LANES, (p + 1) * LANES)
        o_logit = x_ref[rs, 3 * MLSTM_W + p * LANES:3 * MLSTM_W + (p + 1) * LANES]
        y = _sigmoid(o_logit) * (h[p, c] * lax.rsqrt(ms[p, c] + EPS) * gain_ref[:, cs])
        y_ref[rs, cs] = y.astype(y_ref.dtype)


def _mixer_call(kernel, name, proj3, sec_block, sec_width, out_width, extra_inputs, extra_specs, scratch):
    b, t, _ = proj3.shape
    tb = min(LONG_BLOCK, t)
    return pl.pallas_call(
        kernel,
        grid=(b, t // tb),
        in_specs=[pl.BlockSpec((None, tb, sec_width), lambda i, j: (i, j, sec_block))] + extra_specs(tb),
        out_specs=pl.BlockSpec((None, tb, out_width), lambda i, j: (i, j, 0)),
        out_shape=jax.ShapeDtypeStruct((b, t, out_width), BF16),
        scratch_shapes=scratch(tb),
        compiler_params=_cparams(("parallel", "arbitrary")),
        name=name,
    )(proj3, *extra_inputs)


def _full2(a, b):
    return pl.BlockSpec((a, b), lambda i, j: (0, 0))


def _gdn_mixer(proj3, gates3, conv_w, prm, gain):
    b, t, _ = proj3.shape
    tb = min(TIME_BLOCK, t)
    n_blocks = t // tb
    n_units = (GDN_HEADS // 2) * (tb // CHUNK)
    cur = lambda i, j: (i, jnp.minimum(j, n_blocks - 1))
    prv = lambda i, j: (i, jnp.maximum(j - 1, 0))
    z_block = (SEC_M + 3 * GDN_W) // GDN_W
    assert z_block * GDN_W == SEC_M + 3 * GDN_W
    return pl.pallas_call(
        functools.partial(_gdn_kernel, n_blocks),
        grid=(b, n_blocks + 1),
        in_specs=[pl.BlockSpec((None, tb, SEC_G), lambda i, j: cur(i, j) + (1,)),
                  pl.BlockSpec((None, tb, LANES), lambda i, j: cur(i, j) + (0,)),
                  pl.BlockSpec((None, tb, GDN_W), lambda i, j: prv(i, j) + (z_block,)),
                  _full2(CONV_K, 3 * GDN_W), _full2(8, LANES), _full2(1, GDN_W)],
        out_specs=pl.BlockSpec((None, tb, GDN_W), lambda i, j: prv(i, j) + (0,)),
        out_shape=jax.ShapeDtypeStruct((b, t, GDN_W), BF16),
        scratch_shapes=[pltpu.VMEM((GDN_HEADS // 2, LANES, LANES), F32),
                        pltpu.VMEM((n_units, CHUNK, LANES), F32),
                        pltpu.VMEM((n_units, 2 * CHUNK, LANES), BF16),
                        pltpu.VMEM((n_units, CHUNK, LANES), BF16),
                        pltpu.VMEM((n_units, LANES, LANES), BF16),
                        pltpu.VMEM((n_units, 8, LANES), F32),
                        pltpu.VMEM((tb + 8, 3 * GDN_W), F32)],
        compiler_params=_cparams(("parallel", "arbitrary")),
        name="gdn_mixer",
    )(proj3, gates3, proj3, conv_w, prm, gain)


def _mlstm_mixer(proj3, gates3, conv_w, prm, gain):
    b, t, _ = proj3.shape
    tb = min(LONG_BLOCK, t)
    steps_t = t // tb
    n_rows = _sorted_rows(b * t)
    zero_rows = n_rows // (b * steps_t)
    assert zero_rows * b * steps_t == n_rows and zero_rows % 8 == 0
    return pl.pallas_call(
        _mlstm_kernel,
        grid=(b, steps_t),
        in_specs=[pl.BlockSpec((None, tb, SEC_M), lambda i, j: (i, j, 0)),
                  pl.BlockSpec((None, tb, LANES), lambda i, j: (i, j, 0)),
                  _full2(CONV_K, 2 * MLSTM_W), _full2(8, LANES), _full2(1, MLSTM_W)],
        out_specs=[pl.BlockSpec((None, tb, MLSTM_W), lambda i, j: (i, j, 0)),
                   pl.BlockSpec((zero_rows, HP_WIDTH), lambda i, j: (i * steps_t + j, 0))],
        out_shape=[jax.ShapeDtypeStruct((b, t, MLSTM_W), BF16),
                   jax.ShapeDtypeStruct((n_rows, HP_WIDTH), F32)],
        scratch_shapes=[pltpu.VMEM((MLSTM_HEADS // 2, LANES, LANES), F32),
                        pltpu.VMEM((MLSTM_HEADS // 2, 8, LANES), F32),
                        pltpu.VMEM((tb + 8, 2 * MLSTM_W), F32)],
        compiler_params=_cparams(("parallel", "arbitrary")),
        name="mlstm_mixer",
    )(proj3, gates3, conv_w, prm, gain)


def _ret_mixer(proj3, cos_tab, sin_tab, gain):
    return _mixer_call(
        _ret_kernel, "retention_mixer", proj3, 3, SEC_R, RET_W, (cos_tab, sin_tab, gain),
        lambda tb: [pl.BlockSpec((tb, LANES), lambda i, j: (j, 0)),
                    pl.BlockSpec((tb, LANES), lambda i, j: (j, 0)),
                    _full2(1, RET_W)],
        lambda tb: [pltpu.VMEM((RET_HEADS // 2, LANES, LANES), F32)])


DISPATCH_TILE = 2048


def _dispatch_kernel(pos_ref, src_ref, zero_ref, out_ref, sem):
    del zero_ref
    rows = src_ref.shape[0]
    base = pl.program_id(0) * rows

    def issue(r, carry):
        pltpu.make_async_copy(src_ref.at[pl.ds(r, 1)], out_ref.at[pl.ds(pos_ref[base + r], 1)], sem).start()
        return carry

    lax.fori_loop(0, rows, issue, 0, unroll=8)
    pltpu.make_async_copy(src_ref, out_ref.at[pl.ds(0, rows)], sem).wait()


def _dispatch(src, pos, zeros):
    n, width = src.shape
    n_rows = zeros.shape[0]
    tile = min(DISPATCH_TILE, n)
    return pl.pallas_call(
        _dispatch_kernel,
        grid_spec=pltpu.PrefetchScalarGridSpec(
            num_scalar_prefetch=1,
            grid=(n // tile,),
            in_specs=[pl.BlockSpec((tile, width), lambda i, p: (i, 0)), pl.BlockSpec(memory_space=pl.ANY)],
            out_specs=pl.BlockSpec(memory_space=pl.ANY),
            scratch_shapes=[pltpu.SemaphoreType.DMA(())],
        ),
        out_shape=jax.ShapeDtypeStruct((n_rows, width), src.dtype),
        input_output_aliases={2: 0},
        compiler_params=_cparams(("arbitrary",)),
        name="moe_dispatch",
    )(pos, src, zeros)


def _ffn_kernel(plan_ref, used_ref, x_ref, wg_hbm, wu_hbm, wd_hbm, y_ref, wg_buf, wu_buf, wd_buf, sems):
    def group_copies(g, to_slot):
        lo = pl.multiple_of(g * EXPERTS_PER_GROUP, EXPERTS_PER_GROUP)
        experts = pl.ds(lo, EXPERTS_PER_GROUP)
        return [pltpu.make_async_copy(hbm.at[experts], buf.at[to_slot], sems.at[to_slot])
                for hbm, buf in ((wg_hbm, wg_buf), (wu_hbm, wu_buf), (wd_hbm, wd_buf))]

    def one_tile(i, rows):
        used = i < used_ref[0]
        group, slot, starts_group, next_group = plan_ref[0, i], plan_ref[1, i], plan_ref[2, i], plan_ref[3, i]

        @pl.when(used & (i == 0))
        def _():
            for cp in group_copies(group, slot):
                cp.start()

        @pl.when(used & (starts_group == 1) & (next_group >= 0))
        def _():
            for cp in group_copies(next_group, 1 - slot):
                cp.start()

        @pl.when(used & (starts_group == 1))
        def _():
            for cp in group_copies(group, slot):
                cp.wait()

        @pl.when(jnp.logical_not(used))
        def _():
            y_ref[rows, :] = jnp.zeros((MOE_TILE, D_MODEL), F32)

        @pl.when(used)
        def _():
            e1, e2 = plan_ref[4, i], plan_ref[5, i]
            x = x_ref[rows, 0:D_MODEL].astype(BF16)
            gate1 = jnp.dot(x, wg_buf[slot, e1], preferred_element_type=F32)
            gate2 = jnp.dot(x, wg_buf[slot, e2], preferred_element_type=F32)
            up1 = jnp.dot(x, wu_buf[slot, e1], preferred_element_type=F32)
            up2 = jnp.dot(x, wu_buf[slot, e2], preferred_element_type=F32)
            hid1 = (_silu(gate1) * up1).astype(BF16)
            hid2 = (_silu(gate2) * up2).astype(BF16)
            y1 = jnp.dot(hid1, wd_buf[slot, e1], preferred_element_type=F32)
            y2 = jnp.dot(hid2, wd_buf[slot, e2], preferred_element_type=F32)
            rw = x_ref[rows, D_MODEL:]
            y_ref[rows, :] = rw[:, 0:1] * y1 + rw[:, 1:2] * y2

    for sub in range(FFN_STEP_TILES):
        one_tile(pl.program_id(0) * FFN_STEP_TILES + sub, slice(sub * MOE_TILE, (sub + 1) * MOE_TILE))


def _expert_ffn(xs, tile_plan, n_used, wg, wu, wd):
    r = xs.shape[0]
    step_rows = FFN_STEP_TILES * MOE_TILE
    assert r % step_rows == 0
    last_used = lambda nu: (nu[0] - 1) // FFN_STEP_TILES
    return pl.pallas_call(
        _ffn_kernel,
        grid_spec=pltpu.PrefetchScalarGridSpec(
            num_scalar_prefetch=2,
            grid=(r // step_rows,),
            in_specs=[pl.BlockSpec((step_rows, HP_WIDTH), lambda i, plan, nu: (jnp.minimum(i, last_used(nu)), 0)),
                      pl.BlockSpec(memory_space=pl.ANY), pl.BlockSpec(memory_space=pl.ANY),
                      pl.BlockSpec(memory_space=pl.ANY)],
            out_specs=pl.BlockSpec((step_rows, D_MODEL), lambda i, plan, nu: (i, 0)),
            scratch_shapes=[pltpu.VMEM((2, EXPERTS_PER_GROUP, D_MODEL, D_EXPERT), BF16),
                            pltpu.VMEM((2, EXPERTS_PER_GROUP, D_MODEL, D_EXPERT), BF16),
                            pltpu.VMEM((2, EXPERTS_PER_GROUP, D_EXPERT, D_MODEL), BF16),
                            pltpu.SemaphoreType.DMA((2,))],
        ),
        out_shape=jax.ShapeDtypeStruct((r, D_MODEL), F32),
        compiler_params=_cparams(("arbitrary",)),
        name="expert_pair_ffn",
    )(tile_plan, n_used, xs, wg, wu, wd)


def _combine_kernel(final_norm, pos_ref, x_ref, gain_ref, ys_ref, o_ref, buf, sems):
    rows = o_ref.shape[0]
    i = pl.program_id(0)
    slot = i & 1

    def gather_tile(tile, to_slot):
        def issue(r, carry):
            pltpu.make_async_copy(ys_ref.at[pl.ds(pos_ref[tile * rows + r], 1)],
                                  buf.at[to_slot, pl.ds(r, 1)], sems.at[to_slot]).start()
            return carry

        lax.fori_loop(0, rows, issue, 0, unroll=8)

    @pl.when(i == 0)
    def _():
        gather_tile(0, 0)

    @pl.when(i + 1 < pl.num_programs(0))
    def _():
        gather_tile(i + 1, 1 - slot)

    pltpu.make_async_copy(ys_ref.at[pl.ds(0, rows)], buf.at[slot], sems.at[slot]).wait()
    out = x_ref[...] + buf[slot]
    if final_norm:
        ms = jnp.mean(out * out, axis=-1, keepdims=True)
        out = out * lax.rsqrt(ms + EPS) * gain_ref[...]
    o_ref[...] = out


def _combine(x2, ys, pos, gain, final_norm):
    n = x2.shape[0]
    return pl.pallas_call(
        functools.partial(_combine_kernel, final_norm),
        grid_spec=pltpu.PrefetchScalarGridSpec(
            num_scalar_prefetch=1,
            grid=(n // ROW_TILE,),
            in_specs=[pl.BlockSpec((ROW_TILE, D_MODEL), lambda i, p: (i, 0)),
                      pl.BlockSpec((1, D_MODEL), lambda i, p: (0, 0)),
                      pl.BlockSpec(memory_space=pl.ANY)],
            out_specs=pl.BlockSpec((ROW_TILE, D_MODEL), lambda i, p: (i, 0)),
            scratch_shapes=[pltpu.VMEM((2, ROW_TILE, D_MODEL), F32), pltpu.SemaphoreType.DMA((2,))],
        ),
        out_shape=jax.ShapeDtypeStruct((n, D_MODEL), F32),
        compiler_params=_cparams(("arbitrary",)),
        name="moe_combine",
    )(pos, x2, gain, ys)


def _plan(route, counts, n):
    cls, rank = route[:, 0], route[:, 1]
    counts = counts.astype(jnp.int32)
    tiles_per = (counts + MOE_TILE - 1) // MOE_TILE
    tile_end = jnp.cumsum(tiles_per)
    tile_start = tile_end - tiles_per
    class_ids = jnp.arange(N_CLASSES, dtype=jnp.int32)
    start_of = jnp.sum(jnp.where(cls[:, None] == class_ids[None, :], tile_start[None, :], 0), axis=1)
    pos = start_of * MOE_TILE + rank
    n_tiles = _sorted_rows(n) // MOE_TILE
    n_used = tile_end[-1]
    tile_ids = jnp.minimum(jnp.arange(n_tiles, dtype=jnp.int32), n_used - 1)
    tile_cls = jnp.sum((tile_end[None, :] <= tile_ids[:, None]).astype(jnp.int32), axis=1)
    per_group = EXPERTS_PER_GROUP * EXPERTS_PER_GROUP
    tile_group = tile_cls // per_group
    starts_group = jnp.concatenate([jnp.ones((1,), jnp.int32),
                                    (tile_group[1:] != tile_group[:-1]).astype(jnp.int32)])
    slot = (jnp.cumsum(starts_group) - 1) % 2
    present = jnp.sum(tiles_per.reshape(N_GROUPS, per_group), axis=1) > 0
    nxt = [jnp.int32(-1)] * N_GROUPS
    for g in range(N_GROUPS - 2, -1, -1):
        nxt[g] = jnp.where(present[g + 1], g + 1, nxt[g + 1])
    next_group = jnp.stack(nxt)[tile_group]
    tile_plan = jnp.stack([tile_group, slot, starts_group, next_group,
                           (tile_cls // EXPERTS_PER_GROUP) % EXPERTS_PER_GROUP, tile_cls % EXPERTS_PER_GROUP])
    return pos, tile_plan.astype(jnp.int32), n_used.reshape(1)


def _rope_tables(t):
    inv_freq = ROPE_BASE ** (-jnp.arange(0, HEAD_DIM, 2, dtype=F32) / HEAD_DIM)
    ang = jnp.arange(t, dtype=F32)[:, None] * inv_freq[None, :]
    cos, sin = jnp.cos(ang), jnp.sin(ang)
    cos_tab = jnp.tile(cos, (1, 2 * LANES // HEAD_DIM))
    sin_tab = jnp.tile(jnp.concatenate([-sin, sin], axis=-1), (1, LANES // HEAD_DIM))
    return cos_tab, sin_tab


def _split_w_in(w):
    g0, g1 = 0, SEC_G
    r0, r1 = g1 + 2 * GDN_HEADS, g1 + 2 * GDN_HEADS + SEC_R
    m0, m1 = r1, r1 + SEC_M
    assert m1 + 2 * MLSTM_HEADS == w.shape[1]
    main = jnp.concatenate([w[:, m0:m1], w[:, g0:g1], w[:, r0:r1]], axis=1).astype(BF16)
    gate = jnp.concatenate([w[:, g1:r0], w[:, m1:]], axis=1)
    gate = jnp.pad(gate, ((0, 0), (0, LANES - gate.shape[1]))).astype(BF16)
    return main, gate


def _lane_row(values, start):
    return jnp.zeros((LANES,), F32).at[start:start + values.shape[0]].set(values)


def kernel(x, norm_mix, w_in, gdn_conv, gdn_a_log, gdn_dt_bias, gdn_norm, ret_norm, mlstm_conv, mlstm_i_bias, mlstm_f_bias, mlstm_norm, w_out, norm_ffn, router_group, router_expert, router_bias, expert_gate, expert_up, expert_down, norm_final):
    b, t, d = x.shape
    n = b * t
    depth = w_in.shape[0]
    cos_tab, sin_tab = _rope_tables(t)
    x2 = x.reshape(n, d)
    for l in range(depth):
        w_main, w_gate = _split_w_in(w_in[l])
        proj, gates, (wg, wu, wd) = _inproj(x2, norm_mix[l][None, :], w_main, w_gate,
                                            (expert_gate, expert_up, expert_down), l)
        proj3 = proj.reshape(b, t, D_MAIN)
        gates3 = gates.reshape(b, t, LANES)
        gdn_prm = jnp.zeros((8, LANES), F32).at[0].set(_lane_row(gdn_a_log[l], G_A)).at[1].set(
            _lane_row(gdn_dt_bias[l], G_A))
        mlstm_prm = jnp.zeros((8, LANES), F32).at[0].set(_lane_row(mlstm_i_bias[l], M_I)).at[1].set(
            _lane_row(mlstm_f_bias[l], M_F))
        y_m, zeros = _mlstm_mixer(proj3, gates3, mlstm_conv[l], mlstm_prm, mlstm_norm[l][None, :])
        y_g = _gdn_mixer(proj3, gates3, gdn_conv[l], gdn_prm, gdn_norm[l][None, :])
        y_r = _ret_mixer(proj3, cos_tab, sin_tab, ret_norm[l][None, :])
        wo = w_out[l].astype(BF16)
        w_router = jnp.pad(jnp.concatenate([router_group[l], router_expert[l]], axis=1),
                           ((0, 0), (0, LANES - N_GROUPS - N_EXPERTS))).astype(BF16)
        rb = _lane_row(router_bias[l], N_GROUPS)[None, :]
        x2, h_packed, route, counts = _outproj(
            x2, y_m.reshape(n, MLSTM_W), y_g.reshape(n, GDN_W), y_r.reshape(n, RET_W),
            wo[GDN_W + RET_W:], wo[:GDN_W], wo[GDN_W:GDN_W + RET_W],
            norm_ffn[l][None, :], w_router, rb)
        pos, tile_plan, n_used = _plan(route, counts[0], n)
        xs = _dispatch(h_packed, pos, zeros)
        ys = _expert_ffn(xs, tile_plan, n_used, wg, wu, wd)
        x2 = _combine(x2, ys, pos, norm_final[None, :], l == depth - 1)
    return x2.reshape(b, t, d)
```
